```python
import math
import jax, jax.numpy as jnp
from jax import lax
import numpy as np

D_MODEL = 1024
BATCH = 2
SEQ = 8192
DEPTH = 2

R_HEAD_DIM = 64
R_HEADS = D_MODEL // R_HEAD_DIM
R_WIDTH = R_HEADS * R_HEAD_DIM
DECAY_LORA = 64
ICLR_LORA = 64
VRES_LORA = 32
GN_EPS = 64e-5
A_HEAD_DIM = 64
A_HEADS_PER_GROUP = 8
DILATION_GROUPS = ((128, 1), (512, 4), (2048, 16))
N_GROUPS = 3
A_HEADS = N_GROUPS * A_HEADS_PER_GROUP
A_QK_WIDTH = A_HEADS * A_HEAD_DIM
A_OUT_WIDTH = A_HEADS_PER_GROUP * A_HEAD_DIM
BLK = 128
NUM_BUCKETS = 32
MAX_DISTANCE = 2048
N_BRANCHES = 2
PROJ_WIDTH = 4 * R_WIDTH + 3 * A_QK_WIDTH + A_OUT_WIDTH + N_BRANCHES * D_MODEL
RMS_EPS = 1e-6
NEG_INF = -1e30

kernel_name = "rwkv7_dilated_attn_gated_hybrid"


def rmsnorm(x, g):
    xf = x.astype(jnp.float32)
    y = xf * lax.rsqrt(jnp.mean(xf * xf, axis=-1, keepdims=True) + RMS_EPS)
    return (y * g).astype(x.dtype)


def token_shift(t):
    return jnp.pad(t, ((0, 0), (1, 0), (0, 0)))[:, :-1]


def t5_bucket(dist):
    max_exact = NUM_BUCKETS // 2
    safe = np.maximum(dist, 1).astype(np.float32)
    large = max_exact + (np.log(safe / max_exact) / math.log(MAX_DISTANCE / max_exact)
                         * (NUM_BUCKETS - max_exact)).astype(np.int32)
    large = np.minimum(large, NUM_BUCKETS - 1)
    return np.where(dist < max_exact, dist, large).astype(np.int32)


def dilated_window_attention(q, k, v, bias_table, window, dilation):
    B, S, H, E = q.shape
    span = window // dilation
    L = S // dilation
    nb = -(-L // BLK)
    Lp = nb * BLK

    def to_blocks(t):
        t = t.reshape(B, L, dilation, H, E).transpose(0, 2, 3, 1, 4)
        t = jnp.pad(t, ((0, 0), (0, 0), (0, 0), (0, Lp - L), (0, 0)))
        return t.reshape(B, dilation, H, nb, BLK, E)

    def with_prev(t):
        prev = jnp.pad(t[:, :, :, :-1], ((0, 0), (0, 0), (0, 0), (1, 0), (0, 0), (0, 0)))
        return jnp.concatenate([prev, t], axis=4)

    qb = to_blocks(q)
    kw = with_prev(to_blocks(k))
    vw = with_prev(to_blocks(v))

    qi = np.arange(BLK)[:, None]
    ki = np.arange(2 * BLK)[None, :]
    delta = qi + BLK - ki
    band = (delta >= 0) & (delta <= span)
    not_first = np.arange(nb)[:, None, None] > 0
    mask = band[None] & (not_first | (ki >= BLK)[None])
    bucket = t5_bucket(np.maximum(delta, 0) * dilation)
    bias = jnp.take(bias_table, bucket, axis=0).astype(jnp.float32).transpose(2, 0, 1)

    logits = jnp.einsum('bdhnqe,bdhnke->bdhnqk', qb, kw) / math.sqrt(E) + bias[None, None, :, None]
    logits = jnp.where(mask, logits, NEG_INF)
    m = jnp.max(logits, axis=-1, keepdims=True)
    p = jnp.exp(logits - m)
    den = jnp.sum(p, axis=-1, keepdims=True)
    o = jnp.einsum('bdhnqk,bdhnke->bdhnqe', p, vw) / den
    lse = (m + jnp.log(den))[..., 0]

    def from_blocks(t):
        t = t.reshape((B, dilation, H, Lp) + t.shape[5:])[:, :, :, :L]
        t = jnp.moveaxis(t, 3, 1)
        return t.reshape((B, S, H) + t.shape[4:])

    return from_blocks(o), from_blocks(lse)


def rwkv7_scan(r, decay, k, v, neg_kk, kk_a):
    B, S, H, N = r.shape
    xs = tuple(jnp.moveaxis(t, 1, 0) for t in (r, decay, k, v, neg_kk, kk_a))

    def step(state, inp):
        r_t, w_t, k_t, v_t, a_t, b_t = inp
        sa = jnp.einsum('bhij,bhj->bhi', state, a_t)
        state = (state * w_t[:, :, None, :] + sa[..., None] * b_t[:, :, None, :]
                 + v_t[..., None] * k_t[:, :, None, :])
        y = jnp.einsum('bhij,bhj->bhi', state, r_t)
        return state, y

    _, y = lax.scan(step, jnp.zeros((B, H, N, N), jnp.float32), xs)
    return jnp.moveaxis(y, 0, 1)


def rwkv7_time_mix(h, p_r, p_k, p_v, mu_rkv, mu_wa, w0, w1, w2, a0, a1, a2,
                   k_k, k_a, r_k, ln_g, ln_b, v_first, vres):
    B, S, _ = h.shape
    f32 = jnp.float32
    h = h.astype(f32)

    def lerp(t, mu):
        return t + (token_shift(t) - t) * mu

    r = lerp(p_r.astype(f32), mu_rkv[0])
    k = lerp(p_k.astype(f32), mu_rkv[1])
    v = lerp(p_v.astype(f32), mu_rkv[2])
    xw = lerp(h, mu_wa[0])
    xa = lerp(h, mu_wa[1])
    w = -jax.nn.softplus(-(w0 + jnp.tanh(xw @ w1) @ w2)) - 0.5
    decay = jnp.exp(-jnp.exp(w))
    a = jax.nn.sigmoid(a0 + (xa @ a1) @ a2)
    if vres is None:
        v_first = v
    else:
        mu_v, v0, v1, v2 = vres
        xv = lerp(h, mu_v)
        v = v + (v_first - v) * jax.nn.sigmoid(v0 + (xv @ v1) @ v2)

    def heads(t):
        return t.reshape(B, S, R_HEADS, R_HEAD_DIM)

    kk = heads(k * k_k)
    kk = kk * lax.rsqrt(jnp.maximum(jnp.sum(kk * kk, axis=-1, keepdims=True), 1e-24))
    k = k * (1.0 + (a - 1.0) * k_a)
    r, k, v, decay, a = heads(r), heads(k), heads(v), heads(decay), heads(a)

    y = rwkv7_scan(r, decay, k, v, -kk, kk * a)
    mean = jnp.mean(y, axis=-1, keepdims=True)
    var = jnp.mean(jnp.square(y - mean), axis=-1, keepdims=True)
    y = ((y - mean) * lax.rsqrt(var + GN_EPS)).reshape(B, S, R_WIDTH) * ln_g + ln_b
    bonus = (jnp.sum(r * k * r_k, axis=-1, keepdims=True) * v).reshape(B, S, R_WIDTH)
    return y + bonus, v_first


def setup_inputs(seed: int = 0) -> dict:
    key = jax.random.key(seed)
    ks = iter(jax.random.split(key, 32))

    def nrm(shape, scale):
        return jax.random.normal(next(ks), shape, jnp.float32) * scale

    def uni(shape, lo, hi):
        return jax.random.uniform(next(ks), shape, jnp.float32, lo, hi)

    D, L = D_MODEL, DEPTH
    return {
        "x": nrm((BATCH, SEQ, D), 1.0),
        "c": nrm((BATCH, D), 1.0),
        "norm_g": 1.0 + nrm((L, D), 0.02),
        "ada_w": nrm((L, D, 3 * D), 0.5 * D ** -0.5),
        "ada_b": nrm((L, 3 * D), 0.02),
        "w_in": nrm((L, D, PROJ_WIDTH), D ** -0.5),
        "rwkv_mu_rkv": uni((L, 3, R_WIDTH), 0.0, 1.0),
        "rwkv_mu_wa": uni((L, 2, D), 0.0, 1.0),
        "rwkv_w0": uni((L, R_WIDTH), -6.5, -1.5),
        "rwkv_w1": nrm((L, D, DECAY_LORA), D ** -0.5),
        "rwkv_w2": nrm((L, DECAY_LORA, R_WIDTH), 0.5 * DECAY_LORA ** -0.5),
        "rwkv_a0": nrm((L, R_WIDTH), 0.5),
        "rwkv_a1": nrm((L, D, ICLR_LORA), D ** -0.5),
        "rwkv_a2": nrm((L, ICLR_LORA, R_WIDTH), ICLR_LORA ** -0.5),
        "rwkv_k_k": 0.85 + nrm((L, R_WIDTH), 0.05),
        "rwkv_k_a": 1.0 + nrm((L, R_WIDTH), 0.05),
        "rwkv_r_k": nrm((L, R_HEADS, R_HEAD_DIM), 0.1),
        "rwkv_ln_g": 1.0 + nrm((L, R_WIDTH), 0.02),
        "rwkv_ln_b": nrm((L, R_WIDTH), 0.02),
        "rwkv_mu_v": uni((L - 1, D), 0.0, 1.0),
        "rwkv_v0": nrm((L - 1, R_WIDTH), 0.5),
        "rwkv_v1": nrm((L - 1, D, VRES_LORA), D ** -0.5),
        "rwkv_v2": nrm((L - 1, VRES_LORA, R_WIDTH), VRES_LORA ** -0.5),
        "w_branch_a": nrm((L, R_WIDTH, D), R_WIDTH ** -0.5),
        "w_branch_b": nrm((L, A_OUT_WIDTH, D), A_OUT_WIDTH ** -0.5),
        "w_out": nrm((L, D, D), D ** -0.5),
        "rel_bias": nrm((NUM_BUCKETS, A_HEADS), 0.5),
        "final_g": 1.0 + nrm((D,), 0.02),
    }


def reference(x, c, norm_g, ada_w, ada_b, w_in, rwkv_mu_rkv, rwkv_mu_wa, rwkv_w0, rwkv_w1,
              rwkv_w2, rwkv_a0, rwkv_a1, rwkv_a2, rwkv_k_k, rwkv_k_a, rwkv_r_k, rwkv_ln_g,
              rwkv_ln_b, rwkv_mu_v, rwkv_v0, rwkv_v1, rwkv_v2, w_branch_a, w_branch_b, w_out,
              rel_bias, final_g):
    B, S, _ = x.shape
    f32 = jnp.float32
    widths = [R_WIDTH] * 4 + [A_QK_WIDTH] * 3 + [A_OUT_WIDTH] + [D_MODEL]
    split_points = [int(s) for s in np.cumsum(widths)]
    v_first = None
    for i in range(DEPTH):
        mod = jax.nn.silu(c) @ ada_w[i] + ada_b[i]
        shift, scale, gate = jnp.split(mod, 3, axis=-1)
        h = rmsnorm(x, norm_g[i]) * (1.0 + scale[:, None]) + shift[:, None]

        proj = h @ w_in[i]
        p_r, p_k, p_v, g_a, q, k, v, g_b, m_a, m_b = jnp.split(proj, split_points, axis=-1)

        vres = None if i == 0 else (rwkv_mu_v[i - 1], rwkv_v0[i - 1], rwkv_v1[i - 1], rwkv_v2[i - 1])
        y_a, v_first = rwkv7_time_mix(h, p_r, p_k, p_v, rwkv_mu_rkv[i], rwkv_mu_wa[i], rwkv_w0[i],
                                      rwkv_w1[i], rwkv_w2[i], rwkv_a0[i], rwkv_a1[i], rwkv_a2[i],
                                      rwkv_k_k[i], rwkv_k_a[i], rwkv_r_k[i], rwkv_ln_g[i],
                                      rwkv_ln_b[i], v_first, vres)
        y_a = y_a * jax.nn.silu(g_a.astype(f32))

        qg = q.astype(f32).reshape(B, S, N_GROUPS, A_HEADS_PER_GROUP, A_HEAD_DIM)
        kg = k.astype(f32).reshape(B, S, N_GROUPS, A_HEADS_PER_GROUP, A_HEAD_DIM)
        vg = v.astype(f32).reshape(B, S, N_GROUPS, A_HEADS_PER_GROUP, A_HEAD_DIM)
        outs, lses = [], []
        for g, (win, dil) in enumerate(DILATION_GROUPS):
            o_g, l_g = dilated_window_attention(
                qg[:, :, g], kg[:, :, g], vg[:, :, g],
                rel_bias[:, g * A_HEADS_PER_GROUP:(g + 1) * A_HEADS_PER_GROUP], win, dil)
            outs.append(o_g)
            lses.append(l_g)
        wts = jax.nn.softmax(jnp.stack(lses, axis=0), axis=0)
        y_b = jnp.sum(wts[..., None] * jnp.stack(outs, axis=0), axis=0).reshape(B, S, A_OUT_WIDTH)
        y_b = y_b * jax.nn.silu(g_b.astype(f32))

        merged = (jax.nn.sigmoid(m_a.astype(f32)) * (y_a @ w_branch_a[i])
                  + jax.nn.sigmoid(m_b.astype(f32)) * (y_b @ w_branch_b[i]))
        out = merged @ w_out[i]
        x = (x + gate[:, None] * out).astype(x.dtype)
    return rmsnorm(x, final_g)
```

```python
import functools
import math

import numpy as np
import jax
import jax.numpy as jnp
from jax import lax
from jax.experimental import pallas as pl
from jax.experimental.pallas import tpu as pltpu

F32 = jnp.float32
BF16 = jnp.bfloat16

D_MODEL = 1024
DEPTH = 2
HEAD_DIM = 64
R_WIDTH = 1024
N_GROUPS = 3
HEADS_PER_GROUP = 8
DILATIONS = (1, 4, 16)
BLK = 128
A_QK_WIDTH = 1536
A_OUT_WIDTH = 512
NUM_BUCKETS = 32
MAX_DISTANCE = 2048
PROJ_WIDTH = 4 * R_WIDTH + 3 * A_QK_WIDTH + A_OUT_WIDTH + 2 * D_MODEL
RMS_EPS = 1e-6
GN_EPS = 64e-5
NEG_INF = -1e30

LANES = 128
MXU_DIM = 256
HEADS_PER_TILE = MXU_DIM // HEAD_DIM
N_COLGROUPS = R_WIDTH // MXU_DIM
CHUNK = 64

COL_R, COL_K, COL_V, COL_GA = 0, 1024, 2048, 3072
COL_AQ, COL_AK, COL_AV = 4096, 5632, 7168
COL_GB, COL_MA, COL_MB = 8704, 9216, 10240

VMEM_LIMIT = 56 * 1024 * 1024


def _cparams(sem):
    return pltpu.CompilerParams(dimension_semantics=sem, vmem_limit_bytes=VMEM_LIMIT)


def _sigmoid(z):
    return 1.0 / (1.0 + jnp.exp(-z))


def _silu(z):
    return z * _sigmoid(z)


def _softplus(z):
    return jnp.maximum(z, 0.0) + jnp.log(1.0 + jnp.exp(-jnp.abs(z)))


def _dot(a, b):
    return jnp.dot(a, b, preferred_element_type=F32)


def _dot_nt(a, b):
    return lax.dot_general(a, b, (((1,), (1,)), ((), ())), preferred_element_type=F32)


def _split2(x):
    hi = x.astype(BF16)
    lo = (x - hi.astype(F32)).astype(BF16)
    return hi, lo


def _segsum64(x, ones_bd):
    n = x.shape[0]
    xs = jnp.concatenate([x[:, MXU_DIM * g:MXU_DIM * (g + 1)] for g in range(N_COLGROUPS)], axis=0)
    hi, lo = _split2(xs)
    s = _dot(hi, ones_bd) + _dot(lo, ones_bd)
    return jnp.concatenate([s[n * g:n * (g + 1)] for g in range(N_COLGROUPS)], axis=1)


def _mod_kernel(c_ref, w_ref, b_ref, o_ref):
    s = _silu(c_ref[...])
    o_ref[0] = jnp.dot(s, w_ref[0], preferred_element_type=F32,
                       precision=lax.Precision.HIGHEST) + b_ref[0]


def _adaln_mod(c, ada_w, ada_b):
    L = ada_w.shape[0]
    B = c.shape[0]
    c8 = jnp.pad(c, ((0, 8 - B), (0, 0)))
    nj = 3
    return pl.pallas_call(
        _mod_kernel,
        grid=(L, nj),
        in_specs=[pl.BlockSpec((8, D_MODEL), lambda l, j: (0, 0)),
                  pl.BlockSpec((1, D_MODEL, D_MODEL), lambda l, j: (l, 0, j)),
                  pl.BlockSpec((1, 1, D_MODEL), lambda l, j: (l, 0, j))],
        out_specs=pl.BlockSpec((1, 8, D_MODEL), lambda l, j: (l, 0, j)),
        out_shape=jax.ShapeDtypeStruct((L, 8, 3 * D_MODEL), F32),
        compiler_params=_cparams(("parallel", "parallel")),
        name="adaln_mod",
    )(c8, ada_w, ada_b.reshape(L, 1, 3 * D_MODEL))


def _t5_bucket(dist):
    max_exact = NUM_BUCKETS // 2
    safe = np.maximum(dist, 1).astype(np.float32)
    large = max_exact + (np.log(safe / max_exact) / math.log(MAX_DISTANCE / max_exact)
                         * (NUM_BUCKETS - max_exact)).astype(np.int32)
    large = np.minimum(large, NUM_BUCKETS - 1)
    return np.where(dist < max_exact, dist, large).astype(np.int32)


def _bias_kernel(tab_ref, bucket_ref, o_ref):
    h = pl.program_id(0)
    bk = bucket_ref[0]
    acc = jnp.zeros(bk.shape, F32)
    for b in range(NUM_BUCKETS):
        acc = jnp.where(bk == b, tab_ref[h * NUM_BUCKETS + b], acc)
    o_ref[0] = acc


def _rel_bias(rel_bias):
    n_heads = rel_bias.shape[1]
    qi = np.arange(BLK)[:, None]
    ki = np.arange(2 * BLK)[None, :]
    delta = np.maximum(qi + BLK - ki, 0)
    buckets = np.stack([_t5_bucket(delta * d) for d in DILATIONS]).astype(np.int32)
    table = rel_bias.T.reshape(-1)
    return pl.pallas_call(
        _bias_kernel,
        grid=(n_heads,),
        in_specs=[pl.BlockSpec(memory_space=pltpu.SMEM),
                  pl.BlockSpec((1, BLK, 2 * BLK), lambda h: (h // HEADS_PER_GROUP, 0, 0))],
        out_specs=pl.BlockSpec((1, BLK, 2 * BLK), lambda h: (h, 0, 0)),
        out_shape=jax.ShapeDtypeStruct((n_heads, BLK, 2 * BLK), F32),
        compiler_params=_cparams(("parallel",)),
        name="rel_bias",
    )(table, jnp.asarray(buckets))


def _proj_kernel(x_ref, mod_ref, g_ref, w_ref, proj_ref, h_ref, hb_ref):
    @pl.when(pl.program_id(2) == 0)
    def _():
        x = x_ref[0]
        ms = jnp.mean(x * x, axis=-1, keepdims=True)
        y = x * lax.rsqrt(ms + RMS_EPS) * g_ref[...]
        shift = mod_ref[0, :, 0:D_MODEL]
        scale = mod_ref[0, :, D_MODEL:2 * D_MODEL]
        h = y * (1.0 + scale) + shift
        h_ref[0] = h
        hb_ref[...] = h.astype(BF16)

    proj_ref[0] = _dot(hb_ref[...], w_ref[...])


def _norm_proj(x, mod_l, norm_g, w_in_bf, tm=1024, tn=1024):
    B, S, D = x.shape
    N = w_in_bf.shape[1]
    return pl.pallas_call(
        _proj_kernel,
        grid=(B, S // tm, N // tn),
        in_specs=[pl.BlockSpec((1, tm, D), lambda b, i, j: (b, i, 0)),
                  pl.BlockSpec((1, 1, 3 * D), lambda b, i, j: (b, 0, 0)),
                  pl.BlockSpec((1, D), lambda b, i, j: (0, 0)),
                  pl.BlockSpec((D, tn), lambda b, i, j: (0, j))],
        out_specs=[pl.BlockSpec((1, tm, tn), lambda b, i, j: (b, i, j)),
                   pl.BlockSpec((1, tm, D), lambda b, i, j: (b, i, 0))],
        out_shape=[jax.ShapeDtypeStruct((B, S, N), F32),
                   jax.ShapeDtypeStruct((B, S, D), F32)],
        scratch_shapes=[pltpu.VMEM((tm, D), BF16)],
        compiler_params=_cparams(("parallel", "parallel", "arbitrary")),
        name="norm_proj",
    )(x, mod_l, norm_g.reshape(1, D), w_in_bf)


PV_MU_R, PV_MU_K, PV_MU_V, PV_MU_W, PV_MU_A, PV_W0, PV_A0, PV_KK, PV_KA, PV_MU_VRES, PV_V0 = range(11)
PV_ROWS = 16


def _shift_rows(t, prev_last):
    rolled = pltpu.roll(t, 1, axis=0)
    row = lax.broadcasted_iota(jnp.int32, t.shape, 0)
    return jnp.where(row == 0, prev_last, rolled)


def _rprep_kernel(*refs, has_vres):
    if has_vres:
        (h_ref, hp_ref, pr_ref, prp_ref, pk_ref, pkp_ref, pvv_ref, pvp_ref, vf_ref, pvec_ref,
         w1_ref, w2_ref, a1_ref, a2_ref, v1_ref, v2_ref, ones_ref,
         r_out, lw_out, k_out, v_out, a_out, b_out) = refs
    else:
        (h_ref, hp_ref, pr_ref, prp_ref, pk_ref, pkp_ref, pvv_ref, pvp_ref, pvec_ref,
         w1_ref, w2_ref, a1_ref, a2_ref, ones_ref,
         r_out, lw_out, k_out, v_out, a_out, b_out) = refs

    not_first = (pl.program_id(1) > 0).astype(F32)

    def prm(i):
        return pvec_ref[i:i + 1, :]

    def shifted(cur_ref, prev_ref):
        t = cur_ref[0]
        return t, _shift_rows(t, prev_ref[0, 7:8, :] * not_first)

    def lerp(t, ts, mu):
        return t + (ts - t) * mu

    h, hs = shifted(h_ref, hp_ref)
    pr, prs = shifted(pr_ref, prp_ref)
    pk, pks = shifted(pk_ref, pkp_ref)
    pv, pvs = shifted(pvv_ref, pvp_ref)

    r = lerp(pr, prs, prm(PV_MU_R))
    k = lerp(pk, pks, prm(PV_MU_K))
    v = lerp(pv, pvs, prm(PV_MU_V))
    xw = lerp(h, hs, prm(PV_MU_W)).astype(BF16)
    xa = lerp(h, hs, prm(PV_MU_A)).astype(BF16)

    zw = prm(PV_W0) + _dot(jnp.tanh(_dot(xw, w1_ref[...])).astype(BF16), w2_ref[...])
    w = -_softplus(-zw) - 0.5
    lw_out[0] = -jnp.exp(w)
    a = _sigmoid(prm(PV_A0) + _dot(_dot(xa, a1_ref[...]).astype(BF16), a2_ref[...]))
    if has_vres:
        xv = lerp(h, hs, prm(PV_MU_VRES)).astype(BF16)
        mix = _sigmoid(prm(PV_V0) + _dot(_dot(xv, v1_ref[...]).astype(BF16), v2_ref[...]))
        v = v + (vf_ref[0] - v) * mix

    kk = k * prm(PV_KK)
    ss = _segsum64(kk * kk, ones_ref[...])
    kk = kk * lax.rsqrt(jnp.maximum(ss, 1e-24))
    r_out[0] = r
    k_out[0] = k * (1.0 + (a - 1.0) * prm(PV_KA))
    v_out[0] = v
    a_out[0] = -kk
    b_out[0] = kk * a


def _rwkv_prep(h, proj, v_first, pvec, lora, ones_bd, tr=256):
    B, S, D = h.shape
    has_vres = v_first is not None
    rpb = tr // 8

    def cur(c):
        return pl.BlockSpec((1, tr, R_WIDTH), lambda b, i: (b, i, c))

    def prev(c):
        return pl.BlockSpec((1, 8, R_WIDTH), lambda b, i: (b, jnp.maximum(i * rpb - 1, 0), c))

    def full(arr):
        return pl.BlockSpec(arr.shape, lambda b, i: (0,) * arr.ndim)

    in_specs = [cur(0), prev(0)]
    args = [h, h]
    for c in (COL_R, COL_K, COL_V):
        in_specs += [cur(c // R_WIDTH), prev(c // R_WIDTH)]
        args += [proj, proj]
    if has_vres:
        in_specs.append(cur(0))
        args.append(v_first)
    in_specs.append(full(pvec))
    args.append(pvec)
    for wgt in lora:
        in_specs.append(full(wgt))
        args.append(wgt)
    in_specs.append(full(ones_bd))
    args.append(ones_bd)
    out = jax.ShapeDtypeStruct((B, S, R_WIDTH), F32)
    return pl.pallas_call(
        functools.partial(_rprep_kernel, has_vres=has_vres),
        grid=(B, S // tr),
        in_specs=in_specs,
        out_specs=[cur(0)] * 6,
        out_shape=[out] * 6,
        compiler_params=_cparams(("parallel", "parallel")),
        name="rwkv_prep",
    )(*args)


def _scan_kernel(r_ref, lw_ref, k_ref, v_ref, a_ref, b_ref, ga_ref, vec_ref, tril_ref, ones_ref,
                 y_ref, s_ref, *, nb, tt):
    C = CHUNK

    @pl.when(pl.program_id(0) == 0)
    def _():
        s_ref[...] = jnp.zeros(s_ref.shape, F32)

    row = lax.broadcasted_iota(jnp.int32, (C, MXU_DIM), 0)
    lane = lax.broadcasted_iota(jnp.int32, (C, MXU_DIM), 1)
    col = lane & (HEAD_DIM - 1)
    lhead = lane >> 6
    strict = col < row
    incl = col <= row
    eye = (col == row).astype(F32)
    head_masks = [lhead == hh for hh in range(HEADS_PER_TILE)]

    def bdrows(x):
        return jnp.concatenate([jnp.where(m, x, 0.0) for m in head_masks], axis=0).astype(BF16)

    def diag_blocks(full):
        acc = jnp.where(head_masks[0], full[0:C], 0.0)
        for hh in range(1, HEADS_PER_TILE):
            acc = acc + jnp.where(head_masks[hh], full[C * hh:C * (hh + 1)], 0.0)
        return acc

    tril = tril_ref[...]
    ones_bd = ones_ref[...]
    r_k = vec_ref[0:1, :]
    ln_g = vec_ref[1:2, :]
    ln_b = vec_ref[2:3, :]

    def chunk_body(ci, carry):
        rows = pl.ds(pl.multiple_of(ci * C, C), C)
        for bi in range(nb):
            lw = lw_ref[bi, rows, :]
            r = r_ref[bi, rows, :]
            k = k_ref[bi, rows, :]
            v = v_ref[bi, rows, :]
            a = a_ref[bi, rows, :]
            b = b_ref[bi, rows, :]

            hi = lw.astype(BF16)
            r1 = lw - hi.astype(F32)
            mid = r1.astype(BF16)
            lo = (r1 - mid.astype(F32)).astype(BF16)
            cum = _dot(tril, hi) + _dot(tril, mid) + _dot(tril, lo)
            total = cum[C - 1:C, :]
            p_in = jnp.exp(cum)
            p_inv = jnp.exp(-cum)
            a_t = a * jnp.exp(cum - lw)
            r_t = r * p_in
            b_t = b * p_inv
            k_t = k * p_inv
            p_rest = jnp.exp(total - cum)
            bp = b * p_rest
            kp = k * p_rest
            p_all = jnp.exp(total)

            y_parts = []
            for g in range(N_COLGROUPS):
                sl = slice(MXU_DIM * g, MXU_DIM * (g + 1))
                sidx = bi * N_COLGROUPS + g
                lhs = jnp.concatenate([a_t[:, sl], r_t[:, sl]], axis=0).astype(BF16)
                rhs = jnp.concatenate([bdrows(b_t[:, sl]), bdrows(k_t[:, sl])], axis=0)
                res = _dot_nt(lhs, rhs)
                a_ab = jnp.where(strict, res[0:C, 0:MXU_DIM], 0.0)
                a_ak = jnp.where(strict, res[0:C, MXU_DIM:], 0.0)
                a_rb = jnp.where(incl, res[C:, 0:MXU_DIM], 0.0)
                a_rk = jnp.where(incl, res[C:, MXU_DIM:], 0.0)

                pw = a_ab
                tinv = eye + a_ab
                for _ in range(5):
                    pwb = pw.astype(BF16)
                    pw = _dot(pwb, bdrows(pw))
                    tinv = tinv + _dot(pw.astype(BF16), bdrows(tinv))
                tb = tinv.astype(BF16)
                ta = _dot(tb, bdrows(a_t[:, sl])).astype(BF16)
                tak = _dot(tb, bdrows(a_ak)).astype(BF16)

                st = s_ref[sidx]
                st_bd = bdrows(st)
                v_bd = bdrows(v[:, sl])
                u = _dot_nt(ta, st_bd) + _dot(tak, v_bd)
                u_bd = bdrows(u)
                y = (_dot_nt(r_t[:, sl].astype(BF16), st_bd)
                     + _dot(jnp.concatenate([a_rb, a_rk], axis=1).astype(BF16),
                            jnp.concatenate([u_bd, v_bd], axis=0)))
                uv_t = jnp.concatenate([u, v[:, sl]], axis=0).T.astype(BF16)
                bk = jnp.concatenate([bp[:, sl], kp[:, sl]], axis=0).astype(BF16)
                s_ref[sidx] = st * p_all[:, sl] + diag_blocks(_dot(uv_t, bk))
                y_parts.append(y)

            yc = jnp.concatenate(y_parts, axis=1)
            mean = _segsum64(yc, ones_bd) * (1.0 / HEAD_DIM)
            yd = yc - mean
            var = _segsum64(yd * yd, ones_bd) * (1.0 / HEAD_DIM)
            yn = yd * lax.rsqrt(var + GN_EPS) * ln_g + ln_b
            bonus = _segsum64(r * k * r_k, ones_bd) * v
            y_ref[bi, rows, :] = (yn + bonus) * _silu(ga_ref[bi, rows, :])
        return carry

    lax.fori_loop(0, tt // C, chunk_body, 0)


def _rwkv_scan(r, lw, k, v, a, b, proj, vec, tril, ones_bd, tt=128):
    B, S, W = r.shape
    spec = pl.BlockSpec((B, tt, W), lambda t: (0, t, 0))

    def full(arr):
        return pl.BlockSpec(arr.shape, lambda t: (0,) * arr.ndim)

    return pl.pallas_call(
        functools.partial(_scan_kernel, nb=B, tt=tt),
        grid=(S // tt,),
        in_specs=[spec] * 6 + [pl.BlockSpec((B, tt, W), lambda t: (0, t, COL_GA // W)),
                               full(vec), full(tril), full(ones_bd)],
        out_specs=spec,
        out_shape=jax.ShapeDtypeStruct((B, S, W), F32),
        scratch_shapes=[pltpu.VMEM((B * N_COLGROUPS, HEAD_DIM, MXU_DIM), F32)],
        compiler_params=_cparams(("arbitrary",)),
        name="rwkv_scan",
    )(r, lw, k, v, a, b, proj, vec, tril, ones_bd)


ATT_TILE = 2048
ATT_UNROLL = 2


def _attn_kernel(*refs, tiles_per_seq):
    q_refs = refs[0:3]
    k_refs = refs[3:6]
    v_refs = refs[6:9]
    kp_refs = refs[9:12]
    vp_refs = refs[12:15]
    gb_ref, bias_ref, y_ref = refs[15:18]
    kf_refs = refs[18:21]
    vf_refs = refs[21:24]
    o_refs = refs[24:27]
    l_refs = refs[27:30]

    is_first = (pl.program_id(1) % tiles_per_seq) == 0
    qi = lax.broadcasted_iota(jnp.int32, (BLK, 2 * BLK), 0)
    ki = lax.broadcasted_iota(jnp.int32, (BLK, 2 * BLK), 1)
    delta = qi + BLK - ki
    band = (delta >= 0) & (delta <= BLK)
    in_prev = ki < BLK
    lane_kv = lax.broadcasted_iota(jnp.int32, (2 * BLK, LANES), 1)
    lane_o = lax.broadcasted_iota(jnp.int32, (BLK, LANES), 1)
    scale = 1.0 / math.sqrt(HEAD_DIM)

    for g, d in enumerate(DILATIONS):
        span = BLK * d
        kf_refs[g][0:span, :] = kp_refs[g][...]
        kf_refs[g][span:, :] = k_refs[g][...]
        vf_refs[g][0:span, :] = vp_refs[g][...]
        vf_refs[g][span:, :] = v_refs[g][...]

    for g, d in enumerate(DILATIONS):
        span = BLK * d
        n_blocks = ATT_TILE // BLK
        shift = int(math.log2(d))

        def ds(start, size, d=d):
            return pl.ds(start, size) if d == 1 else pl.ds(start, size, stride=d)

        def body(it, carry, g=g, d=d, span=span, shift=shift, ds=ds):
            for u in range(ATT_UNROLL):
                blk = it * ATT_UNROLL + u
                sub = blk >> shift
                res = blk & (d - 1)
                base = sub * span + res
                q = q_refs[g][ds(base, BLK), :].astype(BF16)
                kw = kf_refs[g][ds(base, 2 * BLK), :]
                vw = vf_refs[g][ds(base, 2 * BLK), :].astype(BF16)
                kill_prev = jnp.logical_and(is_first, sub == 0)
                outs, lses = [], []
                for e in range(2):
                    km = jnp.where((lane_kv >> 6) == e, kw, 0.0).astype(BF16)
                    logits = _dot_nt(q, km) * scale + bias_ref[g, 0, e]
                    logits = jnp.where(band, logits, NEG_INF)
                    logits = jnp.where(jnp.logical_and(in_prev, kill_prev), NEG_INF, logits)
                    m = jnp.max(logits, axis=-1, keepdims=True)
                    p = jnp.exp(logits - m)
                    den = jnp.sum(p, axis=-1, keepdims=True)
                    outs.append(_dot(p.astype(BF16), vw) / den)
                    lses.append(m + jnp.log(den))
                sel = (lane_o >> 6) == 0
                o_refs[g][ds(base, BLK), :] = jnp.where(sel, outs[0], outs[1])
                l_refs[g][ds(base, BLK), :] = jnp.where(sel, lses[0], lses[1])
            return carry

        lax.fori_loop(0, n_blocks // ATT_UNROLL, body, 0)

    l0, l1, l2 = l_refs[0][...], l_refs[1][...], l_refs[2][...]
    m = jnp.maximum(jnp.maximum(l0, l1), l2)
    w0, w1, w2 = jnp.exp(l0 - m), jnp.exp(l1 - m), jnp.exp(l2 - m)
    y = (w0 * o_refs[0][...] + w1 * o_refs[1][...] + w2 * o_refs[2][...]) / (w0 + w1 + w2)
    y_ref[...] = y * _silu(gb_ref[...])


def _dilated_attention(proj2d, bias5, seq_len):
    M = proj2d.shape[0]
    n_tiles = M // ATT_TILE
    n_pairs = HEADS_PER_GROUP // 2
    tiles_per_seq = seq_len // ATT_TILE

    def cur(col0):
        return pl.BlockSpec((ATT_TILE, LANES), lambda hp, t: (t, col0 // LANES + hp))

    def prev(col0, d):
        span = BLK * d
        rb = ATT_TILE // span
        return pl.BlockSpec((span, LANES), lambda hp, t: (jnp.maximum(t * rb - 1, 0), col0 // LANES + hp))

    in_specs = ([cur(COL_AQ + A_OUT_WIDTH * g) for g in range(N_GROUPS)]
                + [cur(COL_AK + A_OUT_WIDTH * g) for g in range(N_GROUPS)]
                + [cur(COL_AV + A_OUT_WIDTH * g) for g in range(N_GROUPS)]
                + [prev(COL_AK + A_OUT_WIDTH * g, d) for g, d in enumerate(DILATIONS)]
                + [prev(COL_AV + A_OUT_WIDTH * g, d) for g, d in enumerate(DILATIONS)]
                + [cur(COL_GB),
                   pl.BlockSpec((N_GROUPS, 1, 2, BLK, 2 * BLK), lambda hp, t: (0, hp, 0, 0, 0))])
    scratch = ([pltpu.VMEM((BLK * d + ATT_TILE, LANES), F32) for d in DILATIONS] * 2
               + [pltpu.VMEM((ATT_TILE, LANES), F32)] * 6)
    return pl.pallas_call(
        functools.partial(_attn_kernel, tiles_per_seq=tiles_per_seq),
        grid=(n_pairs, n_tiles),
        in_specs=in_specs,
        out_specs=pl.BlockSpec((ATT_TILE, LANES), lambda hp, t: (t, hp)),
        out_shape=jax.ShapeDtypeStruct((M, A_OUT_WIDTH), F32),
        scratch_shapes=scratch,
        compiler_params=_cparams(("parallel", "parallel")),
        name="dilated_attn",
    )(*([proj2d] * 15), proj2d, bias5)


def _merge_kernel(ya_ref, yb_ref, ma_ref, mb_ref, x_ref, mod_ref, wa_ref, wb_ref, wo_ref, fg_ref,
                  o_ref, *, final_norm):
    pa = _dot(ya_ref[0].astype(BF16), wa_ref[...])
    pb = _dot(yb_ref[0].astype(BF16), wb_ref[...])
    merged = _sigmoid(ma_ref[0]) * pa + _sigmoid(mb_ref[0]) * pb
    out = _dot(merged.astype(BF16), wo_ref[...])
    gate = mod_ref[0, :, 2 * D_MODEL:3 * D_MODEL]
    xn = x_ref[0] + gate * out
    if final_norm:
        ms = jnp.mean(xn * xn, axis=-1, keepdims=True)
        xn = xn * lax.rsqrt(ms + RMS_EPS) * fg_ref[...]
    o_ref[0] = xn


def _merge(ya, yb, proj, x, mod_l, wa, wb, wo, final_g, final_norm, tm=512):
    B, S, D = x.shape

    def rows(width, c):
        return pl.BlockSpec((1, tm, width), lambda b, i: (b, i, c))

    def full(arr):
        return pl.BlockSpec(arr.shape, lambda b, i: (0,) * arr.ndim)

    return pl.pallas_call(
        functools.partial(_merge_kernel, final_norm=final_norm),
        grid=(B, S // tm),
        in_specs=[rows(R_WIDTH, 0), rows(A_OUT_WIDTH, 0),
                  rows(D, COL_MA // D), rows(D, COL_MB // D), rows(D, 0),
                  pl.BlockSpec((1, 1, 3 * D), lambda b, i: (b, 0, 0)),
                  full(wa), full(wb), full(wo), full(final_g)],
        out_specs=rows(D, 0),
        out_shape=jax.ShapeDtypeStruct((B, S, D), F32),
        compiler_params=_cparams(("parallel", "parallel")),
        name="merge",
    )(ya, yb, proj, proj, x, mod_l, wa, wb, wo, final_g)


def _constants():
    idx = np.arange(MXU_DIM)
    ones_bd = (idx[:, None] // HEAD_DIM == idx[None, :] // HEAD_DIM).astype(np.float32)
    t = np.arange(CHUNK)
    tril = (t[None, :] <= t[:, None]).astype(np.float32)
    return jnp.asarray(ones_bd, BF16), jnp.asarray(tril, BF16)


def kernel(x, c, norm_g, ada_w, ada_b, w_in, rwkv_mu_rkv, rwkv_mu_wa, rwkv_w0, rwkv_w1, rwkv_w2, rwkv_a0, rwkv_a1, rwkv_a2, rwkv_k_k, rwkv_k_a, rwkv_r_k, rwkv_ln_g, rwkv_ln_b, rwkv_mu_v, rwkv_v0, rwkv_v1, rwkv_v2, w_branch_a, w_branch_b, w_out, rel_bias, final_g):
    B, S, D = x.shape
    assert D == D_MODEL and S % ATT_TILE == 0 and w_in.shape[2] == PROJ_WIDTH
    ones_bd, tril = _constants()
    mod = _adaln_mod(c, ada_w, ada_b)
    bias = _rel_bias(rel_bias).reshape(N_GROUPS, HEADS_PER_GROUP // 2, 2, BLK, 2 * BLK)
    zeros_row = jnp.zeros((D,), F32)
    v_first = None
    for i in range(DEPTH):
        mod_l = mod[i, :B].reshape(B, 1, 3 * D)
        proj, h = _norm_proj(x, mod_l, norm_g[i], w_in[i].astype(BF16))
        has_vres = i > 0
        pvec = jnp.stack(
            [rwkv_mu_rkv[i, 0], rwkv_mu_rkv[i, 1], rwkv_mu_rkv[i, 2], rwkv_mu_wa[i, 0], rwkv_mu_wa[i, 1],
             rwkv_w0[i], rwkv_a0[i], rwkv_k_k[i], rwkv_k_a[i],
             rwkv_mu_v[i - 1] if has_vres else zeros_row, rwkv_v0[i - 1] if has_vres else zeros_row]
            + [zeros_row] * (PV_ROWS - 11))
        lora = [rwkv_w1[i].astype(BF16), rwkv_w2[i].astype(BF16),
                rwkv_a1[i].astype(BF16), rwkv_a2[i].astype(BF16)]
        if has_vres:
            lora += [rwkv_v1[i - 1].astype(BF16), rwkv_v2[i - 1].astype(BF16)]
        r, lw, k, v, a, b = _rwkv_prep(h, proj, v_first, pvec, lora, ones_bd)
        if i == 0:
            v_first = v
        vec = jnp.stack([rwkv_r_k[i].reshape(-1), rwkv_ln_g[i], rwkv_ln_b[i]] + [zeros_row] * 5)
        y_a = _rwkv_scan(r, lw, k, v, a, b, proj, vec, tril, ones_bd)
        y_b = _dilated_attention(proj.reshape(B * S, PROJ_WIDTH), bias, S).reshape(B, S, A_OUT_WIDTH)
        x = _merge(y_a, y_b, proj, x, mod_l, w_branch_a[i].astype(BF16), w_branch_b[i].astype(BF16),
                   w_out[i].astype(BF16), final_g.reshape(1, D), final_norm=(i == DEPTH - 1))
    return x
```

```python
import functools
import math

import numpy as np
import jax
import jax.numpy as jnp
from jax import lax
from jax.experimental import pallas as pl
from jax.experimental.pallas import tpu as pltpu

F32 = jnp.float32
BF16 = jnp.bfloat16

D_MODEL = 1024
DEPTH = 2
HEAD_DIM = 64
R_WIDTH = 1024
N_GROUPS = 3
HEADS_PER_GROUP = 8
DILATIONS = (1, 4, 16)
BLK = 128
A_QK_WIDTH = 1536
A_OUT_WIDTH = 512
NUM_BUCKETS = 32
MAX_DISTANCE = 2048
PROJ_WIDTH = 4 * R_WIDTH + 3 * A_QK_WIDTH + A_OUT_WIDTH + 2 * D_MODEL
RMS_EPS = 1e-6
GN_EPS = 64e-5
NEG_INF = -1e30

LANES = 128
MXU_DIM = 256
HEADS_PER_TILE = MXU_DIM // HEAD_DIM
N_COLGROUPS = R_WIDTH // MXU_DIM
CHUNK = 64

COL_R, COL_K, COL_V, COL_GA = 0, 1024, 2048, 3072
COL_AQ, COL_AK, COL_AV = 4096, 5632, 7168
COL_GB, COL_MA, COL_MB = 8704, 9216, 10240

VMEM_LIMIT = 56 * 1024 * 1024


def _cparams(sem):
    return pltpu.CompilerParams(dimension_semantics=sem, vmem_limit_bytes=VMEM_LIMIT)


def _sigmoid(z):
    return 1.0 / (1.0 + jnp.exp(-z))


def _silu(z):
    return z * _sigmoid(z)


def _softplus(z):
    return jnp.maximum(z, 0.0) + jnp.log(1.0 + jnp.exp(-jnp.abs(z)))


def _dot(a, b):
    return jnp.dot(a, b, preferred_element_type=F32)


def _dot_nt(a, b):
    return lax.dot_general(a, b, (((1,), (1,)), ((), ())), preferred_element_type=F32)


def _split2(x):
    hi = x.astype(BF16)
    lo = (x - hi.astype(F32)).astype(BF16)
    return hi, lo


def _segsum64(x, ones_bd):
    n = x.shape[0]
    xs = jnp.concatenate([x[:, MXU_DIM * g:MXU_DIM * (g + 1)] for g in range(N_COLGROUPS)], axis=0)
    hi, lo = _split2(xs)
    s = _dot(hi, ones_bd) + _dot(lo, ones_bd)
    return jnp.concatenate([s[n * g:n * (g + 1)] for g in range(N_COLGROUPS)], axis=1)


def _mod_kernel(c_ref, w_ref, b_ref, o_ref):
    s = _silu(c_ref[...])
    o_ref[0] = jnp.dot(s, w_ref[0], preferred_element_type=F32,
                       precision=lax.Precision.HIGHEST) + b_ref[0]


def _adaln_mod(c, ada_w, ada_b):
    L = ada_w.shape[0]
    B = c.shape[0]
    c8 = jnp.pad(c, ((0, 8 - B), (0, 0)))
    nj = 3
    return pl.pallas_call(
        _mod_kernel,
        grid=(L, nj),
        in_specs=[pl.BlockSpec((8, D_MODEL), lambda l, j: (0, 0)),
                  pl.BlockSpec((1, D_MODEL, D_MODEL), lambda l, j: (l, 0, j)),
                  pl.BlockSpec((1, 1, D_MODEL), lambda l, j: (l, 0, j))],
        out_specs=pl.BlockSpec((1, 8, D_MODEL), lambda l, j: (l, 0, j)),
        out_shape=jax.ShapeDtypeStruct((L, 8, 3 * D_MODEL), F32),
        compiler_params=_cparams(("parallel", "parallel")),
        name="adaln_mod",
    )(c8, ada_w, ada_b.reshape(L, 1, 3 * D_MODEL))


def _t5_bucket(dist):
    max_exact = NUM_BUCKETS // 2
    safe = np.maximum(dist, 1).astype(np.float32)
    large = max_exact + (np.log(safe / max_exact) / math.log(MAX_DISTANCE / max_exact)
                         * (NUM_BUCKETS - max_exact)).astype(np.int32)
    large = np.minimum(large, NUM_BUCKETS - 1)
    return np.where(dist < max_exact, dist, large).astype(np.int32)


def _bias_kernel(tab_ref, bucket_ref, o_ref):
    h = pl.program_id(0)
    bk = bucket_ref[0]
    acc = jnp.zeros(bk.shape, F32)
    for b in range(NUM_BUCKETS):
        acc = jnp.where(bk == b, tab_ref[h * NUM_BUCKETS + b], acc)
    o_ref[0] = acc


def _rel_bias(rel_bias):
    n_heads = rel_bias.shape[1]
    qi = np.arange(BLK)[:, None]
    ki = np.arange(2 * BLK)[None, :]
    delta = np.maximum(qi + BLK - ki, 0)
    buckets = np.stack([_t5_bucket(delta * d) for d in DILATIONS]).astype(np.int32)
    table = rel_bias.T.reshape(-1)
    return pl.pallas_call(
        _bias_kernel,
        grid=(n_heads,),
        in_specs=[pl.BlockSpec(memory_space=pltpu.SMEM),
                  pl.BlockSpec((1, BLK, 2 * BLK), lambda h: (h // HEADS_PER_GROUP, 0, 0))],
        out_specs=pl.BlockSpec((1, BLK, 2 * BLK), lambda h: (h, 0, 0)),
        out_shape=jax.ShapeDtypeStruct((n_heads, BLK, 2 * BLK), F32),
        compiler_params=_cparams(("parallel",)),
        name="rel_bias",
    )(table, jnp.asarray(buckets))


def _proj_kernel(x_ref, mod_ref, g_ref, w_ref, proj_ref, h_ref, hb_ref):
    @pl.when(pl.program_id(2) == 0)
    def _():
        x = x_ref[0]
        ms = jnp.mean(x * x, axis=-1, keepdims=True)
        y = x * lax.rsqrt(ms + RMS_EPS) * g_ref[...]
        shift = mod_ref[0, :, 0:D_MODEL]
        scale = mod_ref[0, :, D_MODEL:2 * D_MODEL]
        h = y * (1.0 + scale) + shift
        h_ref[0] = h
        hb_ref[...] = h.astype(BF16)

    proj_ref[0] = _dot(hb_ref[...], w_ref[...])


def _norm_proj(x, mod_l, norm_g, w_in_bf, tm=1024, tn=1024):
    B, S, D = x.shape
    N = w_in_bf.shape[1]
    return pl.pallas_call(
        _proj_kernel,
        grid=(B, S // tm, N // tn),
        in_specs=[pl.BlockSpec((1, tm, D), lambda b, i, j: (b, i, 0)),
                  pl.BlockSpec((1, 1, 3 * D), lambda b, i, j: (b, 0, 0)),
                  pl.BlockSpec((1, D), lambda b, i, j: (0, 0)),
                  pl.BlockSpec((D, tn), lambda b, i, j: (0, j))],
        out_specs=[pl.BlockSpec((1, tm, tn), lambda b, i, j: (b, i, j)),
                   pl.BlockSpec((1, tm, D), lambda b, i, j: (b, i, 0))],
        out_shape=[jax.ShapeDtypeStruct((B, S, N), F32),
                   jax.ShapeDtypeStruct((B, S, D), F32)],
        scratch_shapes=[pltpu.VMEM((tm, D), BF16)],
        compiler_params=_cparams(("parallel", "parallel", "arbitrary")),
        name="norm_proj",
    )(x, mod_l, norm_g.reshape(1, D), w_in_bf)


PV_MU_R, PV_MU_K, PV_MU_V, PV_MU_W, PV_MU_A, PV_W0, PV_A0, PV_KK, PV_KA, PV_MU_VRES, PV_V0 = range(11)
PV_ROWS = 16


def _shift_rows(t, prev_last):
    rolled = pltpu.roll(t, 1, axis=0)
    row = lax.broadcasted_iota(jnp.int32, t.shape, 0)
    return jnp.where(row == 0, prev_last, rolled)


def _rprep_kernel(*refs, has_vres):
    if has_vres:
        (h_ref, hp_ref, pr_ref, prp_ref, pk_ref, pkp_ref, pvv_ref, pvp_ref, vf_ref, pvec_ref,
         w1_ref, w2_ref, a1_ref, a2_ref, v1_ref, v2_ref, ones_ref,
         r_out, lw_out, k_out, v_out, a_out, b_out) = refs
    else:
        (h_ref, hp_ref, pr_ref, prp_ref, pk_ref, pkp_ref, pvv_ref, pvp_ref, pvec_ref,
         w1_ref, w2_ref, a1_ref, a2_ref, ones_ref,
         r_out, lw_out, k_out, v_out, a_out, b_out) = refs

    not_first = (pl.program_id(1) > 0).astype(F32)

    def prm(i):
        return pvec_ref[i:i + 1, :]

    def shifted(cur_ref, prev_ref):
        t = cur_ref[0]
        return t, _shift_rows(t, prev_ref[0, 7:8, :] * not_first)

    def lerp(t, ts, mu):
        return t + (ts - t) * mu

    h, hs = shifted(h_ref, hp_ref)
    pr, prs = shifted(pr_ref, prp_ref)
    pk, pks = shifted(pk_ref, pkp_ref)
    pv, pvs = shifted(pvv_ref, pvp_ref)

    r = lerp(pr, prs, prm(PV_MU_R))
    k = lerp(pk, pks, prm(PV_MU_K))
    v = lerp(pv, pvs, prm(PV_MU_V))
    xw = lerp(h, hs, prm(PV_MU_W)).astype(BF16)
    xa = lerp(h, hs, prm(PV_MU_A)).astype(BF16)

    zw = prm(PV_W0) + _dot(jnp.tanh(_dot(xw, w1_ref[...])).astype(BF16), w2_ref[...])
    w = -_softplus(-zw) - 0.5
    lw_out[0] = -jnp.exp(w)
    a = _sigmoid(prm(PV_A0) + _dot(_dot(xa, a1_ref[...]).astype(BF16), a2_ref[...]))
    if has_vres:
        xv = lerp(h, hs, prm(PV_MU_VRES)).astype(BF16)
        mix = _sigmoid(prm(PV_V0) + _dot(_dot(xv, v1_ref[...]).astype(BF16), v2_ref[...]))
        v = v + (vf_ref[0] - v) * mix

    kk = k * prm(PV_KK)
    ss = _segsum64(kk * kk, ones_ref[...])
    kk = kk * lax.rsqrt(jnp.maximum(ss, 1e-24))
    r_out[0] = r
    k_out[0] = k * (1.0 + (a - 1.0) * prm(PV_KA))
    v_out[0] = v
    a_out[0] = -kk
    b_out[0] = kk * a


def _rwkv_prep(h, proj, v_first, pvec, lora, ones_bd, tr=256):
    B, S, D = h.shape
    has_vres = v_first is not None
    rpb = tr // 8

    def cur(c):
        return pl.BlockSpec((1, tr, R_WIDTH), lambda b, i: (b, i, c))

    def prev(c):
        return pl.BlockSpec((1, 8, R_WIDTH), lambda b, i: (b, jnp.maximum(i * rpb - 1, 0), c))

    def full(arr):
        return pl.BlockSpec(arr.shape, lambda b, i: (0,) * arr.ndim)

    in_specs = [cur(0), prev(0)]
    args = [h, h]
    for c in (COL_R, COL_K, COL_V):
        in_specs += [cur(c // R_WIDTH), prev(c // R_WIDTH)]
        args += [proj, proj]
    if has_vres:
        in_specs.append(cur(0))
        args.append(v_first)
    in_specs.append(full(pvec))
    args.append(pvec)
    for wgt in lora:
        in_specs.append(full(wgt))
        args.append(wgt)
    in_specs.append(full(ones_bd))
    args.append(ones_bd)
    out = jax.ShapeDtypeStruct((B, S, R_WIDTH), F32)
    return pl.pallas_call(
        functools.partial(_rprep_kernel, has_vres=has_vres),
        grid=(B, S // tr),
        in_specs=in_specs,
        out_specs=[cur(0)] * 6,
        out_shape=[out] * 6,
        compiler_params=_cparams(("parallel", "parallel")),
        name="rwkv_prep",
    )(*args)


def _scan_kernel(r_ref, lw_ref, k_ref, v_ref, a_ref, b_ref, ga_ref, vec_ref, tril_ref, ones_ref,
                 y_ref, s_ref, *, nb, tt):
    C = CHUNK

    @pl.when(pl.program_id(0) == 0)
    def _():
        s_ref[...] = jnp.zeros(s_ref.shape, F32)

    row = lax.broadcasted_iota(jnp.int32, (C, MXU_DIM), 0)
    lane = lax.broadcasted_iota(jnp.int32, (C, MXU_DIM), 1)
    col = lane & (HEAD_DIM - 1)
    lhead = lane >> 6
    strict = col < row
    incl = col <= row
    eye = (col == row).astype(F32)
    head_masks = [lhead == hh for hh in range(HEADS_PER_TILE)]

    def bdrows(x):
        return jnp.concatenate([jnp.where(m, x, 0.0) for m in head_masks], axis=0).astype(BF16)

    def diag_blocks(full):
        acc = jnp.where(head_masks[0], full[0:C], 0.0)
        for hh in range(1, HEADS_PER_TILE):
            acc = acc + jnp.where(head_masks[hh], full[C * hh:C * (hh + 1)], 0.0)
        return acc

    tril = tril_ref[...]
    ones_bd = ones_ref[...]
    r_k = vec_ref[0:1, :]
    ln_g = vec_ref[1:2, :]
    ln_b = vec_ref[2:3, :]

    def chunk_body(ci, carry):
        rows = pl.ds(pl.multiple_of(ci * C, C), C)
        pre = []
        for bi in range(nb):
            lw = lw_ref[bi, rows, :]
            r = r_ref[bi, rows, :]
            k = k_ref[bi, rows, :]
            v = v_ref[bi, rows, :]
            a = a_ref[bi, rows, :]
            b = b_ref[bi, rows, :]
            hi = lw.astype(BF16)
            r1 = lw - hi.astype(F32)
            mid = r1.astype(BF16)
            lo = (r1 - mid.astype(F32)).astype(BF16)
            cum = _dot(tril, hi) + _dot(tril, mid) + _dot(tril, lo)
            total = cum[C - 1:C, :]
            p_in = jnp.exp(cum)
            p_inv = jnp.exp(-cum)
            p_rest = jnp.exp(total - cum)
            pre.append(dict(r=r, k=k, v=v, a_t=a * jnp.exp(cum - lw), r_t=r * p_in, b_t=b * p_inv,
                            k_t=k * p_inv, bp=b * p_rest, kp=k * p_rest, p_all=jnp.exp(total)))

        insts = [(bi, g) for bi in range(nb) for g in range(N_COLGROUPS)]

        def part(name, bi, g):
            return pre[bi][name][:, MXU_DIM * g:MXU_DIM * (g + 1)]

        res = [_dot_nt(jnp.concatenate([part("a_t", *i), part("r_t", *i)], axis=0).astype(BF16),
                       jnp.concatenate([bdrows(part("b_t", *i)), bdrows(part("k_t", *i))], axis=0))
               for i in insts]
        a_ab = [jnp.where(strict, x[0:C, 0:MXU_DIM], 0.0) for x in res]
        a_ak = [jnp.where(strict, x[0:C, MXU_DIM:], 0.0) for x in res]
        a_r = [jnp.concatenate([jnp.where(incl, x[C:, 0:MXU_DIM], 0.0),
                                jnp.where(incl, x[C:, MXU_DIM:], 0.0)], axis=1).astype(BF16) for x in res]

        pw = a_ab
        tinv = [eye + x for x in a_ab]
        for _ in range(5):
            pw = [_dot(x.astype(BF16), bdrows(x)) for x in pw]
            tinv = [t + _dot(x.astype(BF16), bdrows(t)) for x, t in zip(pw, tinv)]
        tax = [_dot(t.astype(BF16), jnp.concatenate([bdrows(part("a_t", *i)), bdrows(x)], axis=1))
               for t, x, i in zip(tinv, a_ak, insts)]

        st = [s_ref[bi * N_COLGROUPS + g] for bi, g in insts]
        st_bd = [bdrows(x) for x in st]
        v_bd = [bdrows(part("v", *i)) for i in insts]
        u = [_dot_nt(x[:, 0:MXU_DIM].astype(BF16), s) + _dot(x[:, MXU_DIM:].astype(BF16), vb)
             for x, s, vb in zip(tax, st_bd, v_bd)]
        y = [_dot_nt(part("r_t", *i).astype(BF16), s)
             + _dot(ar, jnp.concatenate([bdrows(uu), vb], axis=0))
             for i, s, ar, uu, vb in zip(insts, st_bd, a_r, u, v_bd)]
        upd = [_dot(jnp.concatenate([uu, part("v", *i)], axis=0).T.astype(BF16),
                    jnp.concatenate([part("bp", *i), part("kp", *i)], axis=0).astype(BF16))
               for uu, i in zip(u, insts)]
        for (bi, g), s_old, x in zip(insts, st, upd):
            s_ref[bi * N_COLGROUPS + g] = s_old * part("p_all", bi, g) + diag_blocks(x)

        for bi in range(nb):
            yc = jnp.concatenate(y[bi * N_COLGROUPS:(bi + 1) * N_COLGROUPS], axis=1)
            mean = _segsum64(yc, ones_bd) * (1.0 / HEAD_DIM)
            yd = yc - mean
            var = _segsum64(yd * yd, ones_bd) * (1.0 / HEAD_DIM)
            yn = yd * lax.rsqrt(var + GN_EPS) * ln_g + ln_b
            bonus = _segsum64(pre[bi]["r"] * pre[bi]["k"] * r_k, ones_bd) * pre[bi]["v"]
            y_ref[bi, rows, :] = (yn + bonus) * _silu(ga_ref[bi, rows, :])
        return carry

    lax.fori_loop(0, tt // C, chunk_body, 0)


def _rwkv_scan(r, lw, k, v, a, b, proj, vec, tril, ones_bd, tt=128):
    B, S, W = r.shape
    spec = pl.BlockSpec((B, tt, W), lambda t: (0, t, 0))

    def full(arr):
        return pl.BlockSpec(arr.shape, lambda t: (0,) * arr.ndim)

    return pl.pallas_call(
        functools.partial(_scan_kernel, nb=B, tt=tt),
        grid=(S // tt,),
        in_specs=[spec] * 6 + [pl.BlockSpec((B, tt, W), lambda t: (0, t, COL_GA // W)),
                               full(vec), full(tril), full(ones_bd)],
        out_specs=spec,
        out_shape=jax.ShapeDtypeStruct((B, S, W), F32),
        scratch_shapes=[pltpu.VMEM((B * N_COLGROUPS, HEAD_DIM, MXU_DIM), F32)],
        compiler_params=_cparams(("arbitrary",)),
        name="rwkv_scan",
    )(r, lw, k, v, a, b, proj, vec, tril, ones_bd)


ATT_TILE = 2048
ATT_UNROLL = (3, 4, 4)


def _attn_kernel(*refs, tiles_per_seq):
    q_refs = refs[0:3]
    k_refs = refs[3:6]
    v_refs = refs[6:9]
    kp_refs = refs[9:12]
    vp_refs = refs[12:15]
    gb_ref, bias_ref, y_ref = refs[15:18]
    o_refs = refs[18:21]
    l_refs = refs[21:24]

    is_first = (pl.program_id(1) % tiles_per_seq) == 0
    prev_limit = jnp.where(is_first, BLK, 0)
    qi = lax.broadcasted_iota(jnp.int32, (2 * BLK, 2 * BLK), 0) & (BLK - 1)
    ki = lax.broadcasted_iota(jnp.int32, (2 * BLK, 2 * BLK), 1)
    delta = qi + BLK - ki
    band = (delta >= 0) & (delta <= BLK)
    head0 = lax.broadcasted_iota(jnp.int32, (BLK, LANES), 1) < HEAD_DIM
    scale = 1.0 / math.sqrt(HEAD_DIM)

    def process(g, d, blocks, from_prev):
        span = BLK * d

        def ds(start, size):
            return pl.ds(start, size) if d == 1 else pl.ds(start, size, stride=d)

        bias2 = bias_ref[g, 0].reshape(2 * BLK, 2 * BLK)
        bases, q2s, kws, vws = [], [], [], []
        for sub, res in blocks:
            base = res if from_prev else sub * span + res
            q = q_refs[g][ds(base, BLK), :] * scale
            q2s.append(jnp.concatenate([jnp.where(head0, q, 0.0), jnp.where(head0, 0.0, q)],
                                       axis=0).astype(BF16))
            if from_prev:
                kw = jnp.concatenate([kp_refs[g][ds(res, BLK), :], k_refs[g][ds(res, BLK), :]], axis=0)
                vw = jnp.concatenate([vp_refs[g][ds(res, BLK), :], v_refs[g][ds(res, BLK), :]], axis=0)
            else:
                kw = k_refs[g][ds(base - span, 2 * BLK), :]
                vw = v_refs[g][ds(base - span, 2 * BLK), :]
            bases.append(base)
            kws.append(kw.astype(BF16))
            vws.append(vw.astype(BF16))
        logits = [_dot_nt(q2, kw) + bias2 for q2, kw in zip(q2s, kws)]
        logits = [jnp.where(band, x, NEG_INF) for x in logits]
        if from_prev:
            logits = [jnp.where(ki < prev_limit, NEG_INF, x) for x in logits]
        ms = [jnp.max(x, axis=-1, keepdims=True) for x in logits]
        ps = [jnp.exp(x - m) for x, m in zip(logits, ms)]
        dens = [jnp.sum(p, axis=-1, keepdims=True) for p in ps]
        pvs = [_dot(p.astype(BF16), vw) for p, vw in zip(ps, vws)]
        for base, pv, m, den in zip(bases, pvs, ms, dens):
            o = pv / den
            lse = m + jnp.log(den)
            o_refs[g][ds(base, BLK), :] = jnp.where(head0, o[0:BLK], o[BLK:])
            l_refs[g][ds(base, BLK), :] = jnp.where(head0, lse[0:BLK], lse[BLK:])

    for g, d in enumerate(DILATIONS):
        unroll = ATT_UNROLL[g]
        shift = int(math.log2(d))
        n_sub = ATT_TILE // (BLK * d)
        if d <= unroll:
            process(g, d, [(0, res) for res in range(d)], True)
        else:
            def first_body(it, carry, g=g, d=d, unroll=unroll):
                process(g, d, [(0, it * unroll + u) for u in range(unroll)], True)
                return carry
            lax.fori_loop(0, d // unroll, first_body, 0)
        n_rest = (n_sub - 1) * d
        if n_rest:
            def rest_body(it, carry, g=g, d=d, unroll=unroll, shift=shift):
                blks = [it * unroll + u for u in range(unroll)]
                process(g, d, [(1 + (b >> shift), b & (d - 1)) for b in blks], False)
                return carry
            lax.fori_loop(0, n_rest // unroll, rest_body, 0)

    l0, l1, l2 = l_refs[0][...], l_refs[1][...], l_refs[2][...]
    m = jnp.maximum(jnp.maximum(l0, l1), l2)
    w0, w1, w2 = jnp.exp(l0 - m), jnp.exp(l1 - m), jnp.exp(l2 - m)
    y = (w0 * o_refs[0][...] + w1 * o_refs[1][...] + w2 * o_refs[2][...]) / (w0 + w1 + w2)
    y_ref[...] = y * _silu(gb_ref[...])


def _dilated_attention(proj2d, bias5, seq_len):
    M = proj2d.shape[0]
    n_tiles = M // ATT_TILE
    n_pairs = HEADS_PER_GROUP // 2
    tiles_per_seq = seq_len // ATT_TILE

    def cur(col0):
        return pl.BlockSpec((ATT_TILE, LANES), lambda hp, t: (t, col0 // LANES + hp))

    def prev(col0, d):
        span = BLK * d
        rb = ATT_TILE // span
        return pl.BlockSpec((span, LANES), lambda hp, t: (jnp.maximum(t * rb - 1, 0), col0 // LANES + hp))

    in_specs = ([cur(COL_AQ + A_OUT_WIDTH * g) for g in range(N_GROUPS)]
                + [cur(COL_AK + A_OUT_WIDTH * g) for g in range(N_GROUPS)]
                + [cur(COL_AV + A_OUT_WIDTH * g) for g in range(N_GROUPS)]
                + [prev(COL_AK + A_OUT_WIDTH * g, d) for g, d in enumerate(DILATIONS)]
                + [prev(COL_AV + A_OUT_WIDTH * g, d) for g, d in enumerate(DILATIONS)]
                + [cur(COL_GB),
                   pl.BlockSpec((N_GROUPS, 1, 2, BLK, 2 * BLK), lambda hp, t: (0, hp, 0, 0, 0))])
    scratch = [pltpu.VMEM((ATT_TILE, LANES), F32)] * 6
    return pl.pallas_call(
        functools.partial(_attn_kernel, tiles_per_seq=tiles_per_seq),
        grid=(n_pairs, n_tiles),
        in_specs=in_specs,
        out_specs=pl.BlockSpec((ATT_TILE, LANES), lambda hp, t: (t, hp)),
        out_shape=jax.ShapeDtypeStruct((M, A_OUT_WIDTH), F32),
        scratch_shapes=scratch,
        compiler_params=_cparams(("parallel", "parallel")),
        name="dilated_attn",
    )(*([proj2d] * 15), proj2d, bias5)


def _merge_kernel(ya_ref, yb_ref, ma_ref, mb_ref, x_ref, mod_ref, wa_ref, wb_ref, wo_ref, fg_ref,
                  o_ref, *, final_norm):
    pa = _dot(ya_ref[0].astype(BF16), wa_ref[...])
    pb = _dot(yb_ref[0].astype(BF16), wb_ref[...])
    merged = _sigmoid(ma_ref[0]) * pa + _sigmoid(mb_ref[0]) * pb
    out = _dot(merged.astype(BF16), wo_ref[...])
    gate = mod_ref[0, :, 2 * D_MODEL:3 * D_MODEL]
    xn = x_ref[0] + gate * out
    if final_norm:
        ms = jnp.mean(xn * xn, axis=-1, keepdims=True)
        xn = xn * lax.rsqrt(ms + RMS_EPS) * fg_ref[...]
    o_ref[0] = xn


def _merge(ya, yb, proj, x, mod_l, wa, wb, wo, final_g, final_norm, tm=512):
    B, S, D = x.shape

    def rows(width, c):
        return pl.BlockSpec((1, tm, width), lambda b, i: (b, i, c))

    def full(arr):
        return pl.BlockSpec(arr.shape, lambda b, i: (0,) * arr.ndim)

    return pl.pallas_call(
        functools.partial(_merge_kernel, final_norm=final_norm),
        grid=(B, S // tm),
        in_specs=[rows(R_WIDTH, 0), rows(A_OUT_WIDTH, 0),
                  rows(D, COL_MA // D), rows(D, COL_MB // D), rows(D, 0),
                  pl.BlockSpec((1, 1, 3 * D), lambda b, i: (b, 0, 0)),
                  full(wa), full(wb), full(wo), full(final_g)],
        out_specs=rows(D, 0),
        out_shape=jax.ShapeDtypeStruct((B, S, D), F32),
        compiler_params=_cparams(("parallel", "parallel")),
        name="merge",
    )(ya, yb, proj, proj, x, mod_l, wa, wb, wo, final_g)


def _constants():
    idx = np.arange(MXU_DIM)
    ones_bd = (idx[:, None] // HEAD_DIM == idx[None, :] // HEAD_DIM).astype(np.float32)
    t = np.arange(CHUNK)
    tril = (t[None, :] <= t[:, None]).astype(np.float32)
    return jnp.asarray(ones_bd, BF16), jnp.asarray(tril, BF16)


def kernel(x, c, norm_g, ada_w, ada_b, w_in, rwkv_mu_rkv, rwkv_mu_wa, rwkv_w0, rwkv_w1, rwkv_w2, rwkv_a0, rwkv_a1, rwkv_a2, rwkv_k_k, rwkv_k_a, rwkv_r_k, rwkv_ln_g, rwkv_ln_b, rwkv_mu_v, rwkv_v0, rwkv_v1, rwkv_v2, w_branch_a, w_branch_b, w_out, rel_bias, final_g):
    B, S, D = x.shape
    assert D == D_MODEL and S % ATT_TILE == 0 and w_in.shape[2] == PROJ_WIDTH
    ones_bd, tril = _constants()
    mod = _adaln_mod(c, ada_w, ada_b)
    bias = _rel_bias(rel_bias).reshape(N_GROUPS, HEADS_PER_GROUP // 2, 2, BLK, 2 * BLK)
    zeros_row = jnp.zeros((D,), F32)
    v_first = None
    for i in range(DEPTH):
        mod_l = mod[i, :B].reshape(B, 1, 3 * D)
        proj, h = _norm_proj(x, mod_l, norm_g[i], w_in[i].astype(BF16))
        has_vres = i > 0
        pvec = jnp.stack(
            [rwkv_mu_rkv[i, 0], rwkv_mu_rkv[i, 1], rwkv_mu_rkv[i, 2], rwkv_mu_wa[i, 0], rwkv_mu_wa[i, 1],
             rwkv_w0[i], rwkv_a0[i], rwkv_k_k[i], rwkv_k_a[i],
             rwkv_mu_v[i - 1] if has_vres else zeros_row, rwkv_v0[i - 1] if has_vres else zeros_row]
            + [zeros_row] * (PV_ROWS - 11))
        lora = [rwkv_w1[i].astype(BF16), rwkv_w2[i].astype(BF16),
                rwkv_a1[i].astype(BF16), rwkv_a2[i].astype(BF16)]
        if has_vres:
            lora += [rwkv_v1[i - 1].astype(BF16), rwkv_v2[i - 1].astype(BF16)]
        r, lw, k, v, a, b = _rwkv_prep(h, proj, v_first, pvec, lora, ones_bd)
        if i == 0:
            v_first = v
        vec = jnp.stack([rwkv_r_k[i].reshape(-1), rwkv_ln_g[i], rwkv_ln_b[i]] + [zeros_row] * 5)
        y_a = _rwkv_scan(r, lw, k, v, a, b, proj, vec, tril, ones_bd)
        y_b = _dilated_attention(proj.reshape(B * S, PROJ_WIDTH), bias, S).reshape(B, S, A_OUT_WIDTH)
        x = _merge(y_a, y_b, proj, x, mod_l, w_branch_a[i].astype(BF16), w_branch_b[i].astype(BF16),
                   w_out[i].astype(BF16), final_g.reshape(1, D), final_norm=(i == DEPTH - 1))
    return x
```

```python
import functools
import math

import numpy as np
import jax
import jax.numpy as jnp
from jax import lax
from jax.experimental import pallas as pl
from jax.experimental.pallas import tpu as pltpu

F32 = jnp.float32
BF16 = jnp.bfloat16

D_MODEL = 1024
DEPTH = 2
HEAD_DIM = 64
R_WIDTH = 1024
N_GROUPS = 3
HEADS_PER_GROUP = 8
DILATIONS = (1, 4, 16)
BLK = 128
A_QK_WIDTH = 1536
A_OUT_WIDTH = 512
NUM_BUCKETS = 32
MAX_DISTANCE = 2048
PROJ_WIDTH = 4 * R_WIDTH + 3 * A_QK_WIDTH + A_OUT_WIDTH + 2 * D_MODEL
RMS_EPS = 1e-6
GN_EPS = 64e-5
NEG_INF = -1e30

LANES = 128
MXU_DIM = 256
HEADS_PER_TILE = MXU_DIM // HEAD_DIM
N_COLGROUPS = R_WIDTH // MXU_DIM
CHUNK = 64

W_R, W_K, W_V, W_GA = 0, 1024, 2048, 3072
W_AQ, W_AK, W_AV = 4096, 5632, 7168
W_GB, W_MA, W_MB = 8704, 9216, 10240
COL_R, COL_K, COL_V, COL_GA, COL_MA, COL_MB, COL_GB, COL_A0 = 0, 1024, 2048, 3072, 4096, 5120, 6144, 6656
MAIN_WIDTH = 8192
GROUP_WIDTH = 3 * A_OUT_WIDTH

VMEM_LIMIT = 56 * 1024 * 1024


def _cparams(sem):
    return pltpu.CompilerParams(dimension_semantics=sem, vmem_limit_bytes=VMEM_LIMIT)


def _sigmoid(z):
    return 1.0 / (1.0 + jnp.exp(-z))


def _silu(z):
    return z * _sigmoid(z)


def _softplus(z):
    return jnp.maximum(z, 0.0) + jnp.log(1.0 + jnp.exp(-jnp.abs(z)))


def _dot(a, b):
    return jnp.dot(a, b, preferred_element_type=F32)


def _dot_nt(a, b):
    return lax.dot_general(a, b, (((1,), (1,)), ((), ())), preferred_element_type=F32)


def _split2(x):
    hi = x.astype(BF16)
    lo = (x - hi.astype(F32)).astype(BF16)
    return hi, lo


def _segsum64(x, ones_bd):
    n = x.shape[0]
    xs = jnp.concatenate([x[:, MXU_DIM * g:MXU_DIM * (g + 1)] for g in range(N_COLGROUPS)], axis=0)
    hi, lo = _split2(xs)
    s = _dot(hi, ones_bd) + _dot(lo, ones_bd)
    return jnp.concatenate([s[n * g:n * (g + 1)] for g in range(N_COLGROUPS)], axis=1)


def _mod_kernel(c_ref, w_ref, b_ref, o_ref):
    s = _silu(c_ref[...])
    o_ref[0] = jnp.dot(s, w_ref[0], preferred_element_type=F32,
                       precision=lax.Precision.HIGHEST) + b_ref[0]


def _adaln_mod(c, ada_w, ada_b):
    L = ada_w.shape[0]
    B = c.shape[0]
    c8 = jnp.pad(c, ((0, 8 - B), (0, 0)))
    nj = 3
    return pl.pallas_call(
        _mod_kernel,
        grid=(L, nj),
        in_specs=[pl.BlockSpec((8, D_MODEL), lambda l, j: (0, 0)),
                  pl.BlockSpec((1, D_MODEL, D_MODEL), lambda l, j: (l, 0, j)),
                  pl.BlockSpec((1, 1, D_MODEL), lambda l, j: (l, 0, j))],
        out_specs=pl.BlockSpec((1, 8, D_MODEL), lambda l, j: (l, 0, j)),
        out_shape=jax.ShapeDtypeStruct((L, 8, 3 * D_MODEL), F32),
        compiler_params=_cparams(("parallel", "parallel")),
        name="adaln_mod",
    )(c8, ada_w, ada_b.reshape(L, 1, 3 * D_MODEL))


def _t5_bucket(dist):
    max_exact = NUM_BUCKETS // 2
    safe = np.maximum(dist, 1).astype(np.float32)
    large = max_exact + (np.log(safe / max_exact) / math.log(MAX_DISTANCE / max_exact)
                         * (NUM_BUCKETS - max_exact)).astype(np.int32)
    large = np.minimum(large, NUM_BUCKETS - 1)
    return np.where(dist < max_exact, dist, large).astype(np.int32)


def _bias_kernel(tab_ref, bucket_ref, o_ref):
    h = pl.program_id(0)
    bk = bucket_ref[0]
    acc = jnp.zeros(bk.shape, F32)
    for b in range(NUM_BUCKETS):
        acc = jnp.where(bk == b, tab_ref[h * NUM_BUCKETS + b], acc)
    o_ref[0] = acc


def _rel_bias(rel_bias):
    n_heads = rel_bias.shape[1]
    qi = np.arange(BLK)[:, None]
    ki = np.arange(2 * BLK)[None, :]
    delta = np.maximum(qi + BLK - ki, 0)
    buckets = np.stack([_t5_bucket(delta * d) for d in DILATIONS]).astype(np.int32)
    table = rel_bias.T.reshape(-1)
    return pl.pallas_call(
        _bias_kernel,
        grid=(n_heads,),
        in_specs=[pl.BlockSpec(memory_space=pltpu.SMEM),
                  pl.BlockSpec((1, BLK, 2 * BLK), lambda h: (h // HEADS_PER_GROUP, 0, 0))],
        out_specs=pl.BlockSpec((1, BLK, 2 * BLK), lambda h: (h, 0, 0)),
        out_shape=jax.ShapeDtypeStruct((n_heads, BLK, 2 * BLK), F32),
        compiler_params=_cparams(("parallel",)),
        name="rel_bias",
    )(table, jnp.asarray(buckets))


def _proj_kernel(x_ref, mod_ref, g_ref, w_ref, proj_ref, h_ref):
    @pl.when(pl.program_id(2) == 0)
    def _():
        x = x_ref[0]
        ms = jnp.mean(x * x, axis=-1, keepdims=True)
        y = x * lax.rsqrt(ms + RMS_EPS) * g_ref[...]
        shift = mod_ref[0, :, 0:D_MODEL]
        scale = mod_ref[0, :, D_MODEL:2 * D_MODEL]
        h_ref[0] = (y * (1.0 + scale) + shift).astype(BF16)

    proj_ref[0] = _dot(h_ref[0], w_ref[...]).astype(BF16)


def _norm_proj(x, mod_l, norm_g, w_main, tm=1024, tn=1024):
    B, S, D = x.shape
    N = w_main.shape[1]
    return pl.pallas_call(
        _proj_kernel,
        grid=(B, S // tm, N // tn),
        in_specs=[pl.BlockSpec((1, tm, D), lambda b, i, j: (b, i, 0)),
                  pl.BlockSpec((1, 1, 3 * D), lambda b, i, j: (b, 0, 0)),
                  pl.BlockSpec((1, D), lambda b, i, j: (0, 0)),
                  pl.BlockSpec((D, tn), lambda b, i, j: (0, j))],
        out_specs=[pl.BlockSpec((1, tm, tn), lambda b, i, j: (b, i, j)),
                   pl.BlockSpec((1, tm, D), lambda b, i, j: (b, i, 0))],
        out_shape=[jax.ShapeDtypeStruct((B, S, N), BF16),
                   jax.ShapeDtypeStruct((B, S, D), BF16)],
        compiler_params=_cparams(("parallel", "parallel", "arbitrary")),
        name="norm_proj",
    )(x, mod_l, norm_g.reshape(1, D), w_main)


def _group_proj_kernel(h_ref, w_ref, o_ref, stage_ref, *, d):
    res = _dot(h_ref[0], w_ref[...])
    tm, tn = res.shape
    for c in range(tn // LANES):
        stage_ref[c] = res[:, LANES * c:LANES * (c + 1)]
    for r in range(d):
        for c in range(tn // LANES):
            o_ref[0, r, :, LANES * c:LANES * (c + 1)] = (
                stage_ref[c, pl.ds(r, tm // d, stride=d), :].astype(BF16))


def _group_proj(h, w_group, d, tm=1024, tn=512):
    B, S, D = h.shape
    N = w_group.shape[1]
    return pl.pallas_call(
        functools.partial(_group_proj_kernel, d=d),
        grid=(B, S // tm, N // tn),
        in_specs=[pl.BlockSpec((1, tm, D), lambda b, i, j: (b, i, 0)),
                  pl.BlockSpec((D, tn), lambda b, i, j: (0, j))],
        out_specs=pl.BlockSpec((1, d, tm // d, tn), lambda b, i, j: (b, 0, i, j)),
        out_shape=jax.ShapeDtypeStruct((B, d, S // d, N), BF16),
        scratch_shapes=[pltpu.VMEM((tn // LANES, tm, LANES), F32)],
        compiler_params=_cparams(("parallel", "parallel", "parallel")),
        name=f"group_proj_d{d}",
    )(h, w_group)


PV_MU_R, PV_MU_K, PV_MU_V, PV_MU_W, PV_MU_A, PV_W0, PV_A0, PV_KK, PV_KA, PV_MU_VRES, PV_V0 = range(11)
PV_ROWS = 16
PREV_ROWS = 16


def _shift_rows(t, prev_last):
    rolled = pltpu.roll(t, 1, axis=0)
    row = lax.broadcasted_iota(jnp.int32, t.shape, 0)
    return jnp.where(row == 0, prev_last, rolled)


def _rprep_kernel(*refs, has_vres):
    if has_vres:
        (h_ref, hp_ref, pr_ref, prp_ref, pk_ref, pkp_ref, pvv_ref, pvp_ref, vf_ref, pvec_ref,
         w1_ref, w2_ref, a1_ref, a2_ref, v1_ref, v2_ref, ones_ref,
         r_out, lw_out, k_out, v_out, a_out, b_out) = refs
    else:
        (h_ref, hp_ref, pr_ref, prp_ref, pk_ref, pkp_ref, pvv_ref, pvp_ref, pvec_ref,
         w1_ref, w2_ref, a1_ref, a2_ref, ones_ref,
         r_out, lw_out, k_out, v_out, a_out, b_out) = refs

    not_first = (pl.program_id(1) > 0).astype(F32)

    def prm(i):
        return pvec_ref[i:i + 1, :]

    def shifted(cur_ref, prev_ref):
        t = cur_ref[0].astype(F32)
        last = prev_ref[0, PREV_ROWS - 1:PREV_ROWS, :].astype(F32)
        return t, _shift_rows(t, last * not_first)

    def lerp(t, ts, mu):
        return t + (ts - t) * mu

    h, hs = shifted(h_ref, hp_ref)
    pr, prs = shifted(pr_ref, prp_ref)
    pk, pks = shifted(pk_ref, pkp_ref)
    pv, pvs = shifted(pvv_ref, pvp_ref)

    r = lerp(pr, prs, prm(PV_MU_R))
    k = lerp(pk, pks, prm(PV_MU_K))
    v = lerp(pv, pvs, prm(PV_MU_V))
    xw = lerp(h, hs, prm(PV_MU_W)).astype(BF16)
    xa = lerp(h, hs, prm(PV_MU_A)).astype(BF16)

    zw = prm(PV_W0) + _dot(jnp.tanh(_dot(xw, w1_ref[...])).astype(BF16), w2_ref[...])
    w = -_softplus(-zw) - 0.5
    lw_out[0] = -jnp.exp(w)
    a = _sigmoid(prm(PV_A0) + _dot(_dot(xa, a1_ref[...]).astype(BF16), a2_ref[...]))
    if has_vres:
        xv = lerp(h, hs, prm(PV_MU_VRES)).astype(BF16)
        mix = _sigmoid(prm(PV_V0) + _dot(_dot(xv, v1_ref[...]).astype(BF16), v2_ref[...]))
        v = v + (vf_ref[0] - v) * mix

    kk = k * prm(PV_KK)
    ss = _segsum64(kk * kk, ones_ref[...])
    kk = kk * lax.rsqrt(jnp.maximum(ss, 1e-24))
    r_out[0] = r
    k_out[0] = k * (1.0 + (a - 1.0) * prm(PV_KA))
    v_out[0] = v
    a_out[0] = -kk
    b_out[0] = kk * a


def _rwkv_prep(h, proj, v_first, pvec, lora, ones_bd, tr=256):
    B, S, D = h.shape
    has_vres = v_first is not None
    rpb = tr // PREV_ROWS

    def cur(c):
        return pl.BlockSpec((1, tr, R_WIDTH), lambda b, i: (b, i, c))

    def prev(c):
        return pl.BlockSpec((1, PREV_ROWS, R_WIDTH), lambda b, i: (b, jnp.maximum(i * rpb - 1, 0), c))

    def full(arr):
        return pl.BlockSpec(arr.shape, lambda b, i: (0,) * arr.ndim)

    in_specs = [cur(0), prev(0)]
    args = [h, h]
    for c in (COL_R, COL_K, COL_V):
        in_specs += [cur(c // R_WIDTH), prev(c // R_WIDTH)]
        args += [proj, proj]
    if has_vres:
        in_specs.append(cur(0))
        args.append(v_first)
    in_specs.append(full(pvec))
    args.append(pvec)
    for wgt in lora:
        in_specs.append(full(wgt))
        args.append(wgt)
    in_specs.append(full(ones_bd))
    args.append(ones_bd)
    out = jax.ShapeDtypeStruct((B, S, R_WIDTH), F32)
    return pl.pallas_call(
        functools.partial(_rprep_kernel, has_vres=has_vres),
        grid=(B, S // tr),
        in_specs=in_specs,
        out_specs=[cur(0)] * 6,
        out_shape=[out] * 6,
        compiler_params=_cparams(("parallel", "parallel")),
        name="rwkv_prep",
    )(*args)


def _scan_kernel(r_ref, lw_ref, k_ref, v_ref, a_ref, b_ref, ga_ref, vec_ref, tril_ref, ones_ref,
                 y_ref, s_ref, *, nb, tt):
    C = CHUNK

    @pl.when(pl.program_id(0) == 0)
    def _():
        s_ref[...] = jnp.zeros(s_ref.shape, F32)

    row = lax.broadcasted_iota(jnp.int32, (C, MXU_DIM), 0)
    lane = lax.broadcasted_iota(jnp.int32, (C, MXU_DIM), 1)
    col = lane & (HEAD_DIM - 1)
    lhead = lane >> 6
    strict = col < row
    incl = col <= row
    eye = (col == row).astype(F32)
    head_masks = [lhead == hh for hh in range(HEADS_PER_TILE)]

    def bdrows(x):
        return jnp.concatenate([jnp.where(m, x, 0.0) for m in head_masks], axis=0).astype(BF16)

    def diag_blocks(full):
        acc = jnp.where(head_masks[0], full[0:C], 0.0)
        for hh in range(1, HEADS_PER_TILE):
            acc = acc + jnp.where(head_masks[hh], full[C * hh:C * (hh + 1)], 0.0)
        return acc

    tril = tril_ref[...]
    ones_bd = ones_ref[...]
    r_k = vec_ref[0:1, :]
    ln_g = vec_ref[1:2, :]
    ln_b = vec_ref[2:3, :]

    def chunk_body(ci, carry):
        rows = pl.ds(pl.multiple_of(ci * C, C), C)
        pre = []
        for bi in range(nb):
            lw = lw_ref[bi, rows, :]
            r = r_ref[bi, rows, :]
            k = k_ref[bi, rows, :]
            v = v_ref[bi, rows, :]
            a = a_ref[bi, rows, :]
            b = b_ref[bi, rows, :]
            hi = lw.astype(BF16)
            r1 = lw - hi.astype(F32)
            mid = r1.astype(BF16)
            lo = (r1 - mid.astype(F32)).astype(BF16)
            cum = _dot(tril, hi) + _dot(tril, mid) + _dot(tril, lo)
            total = cum[C - 1:C, :]
            p_in = jnp.exp(cum)
            p_inv = jnp.exp(-cum)
            p_rest = jnp.exp(total - cum)
            pre.append(dict(r=r, k=k, v=v, a_t=a * jnp.exp(cum - lw), r_t=r * p_in, b_t=b * p_inv,
                            k_t=k * p_inv, bp=b * p_rest, kp=k * p_rest, p_all=jnp.exp(total)))

        insts = [(bi, g) for bi in range(nb) for g in range(N_COLGROUPS)]

        def part(name, bi, g):
            return pre[bi][name][:, MXU_DIM * g:MXU_DIM * (g + 1)]

        res = [_dot_nt(jnp.concatenate([part("a_t", *i), part("r_t", *i)], axis=0).astype(BF16),
                       jnp.concatenate([bdrows(part("b_t", *i)), bdrows(part("k_t", *i))], axis=0))
               for i in insts]
        a_ab = [jnp.where(strict, x[0:C, 0:MXU_DIM], 0.0) for x in res]
        a_ak = [jnp.where(strict, x[0:C, MXU_DIM:], 0.0) for x in res]
        a_r = [jnp.concatenate([jnp.where(incl, x[C:, 0:MXU_DIM], 0.0),
                                jnp.where(incl, x[C:, MXU_DIM:], 0.0)], axis=1).astype(BF16) for x in res]

        pw = a_ab
        tinv = [eye + x for x in a_ab]
        for _ in range(5):
            pw = [_dot(x.astype(BF16), bdrows(x)) for x in pw]
            tinv = [t + _dot(x.astype(BF16), bdrows(t)) for x, t in zip(pw, tinv)]
        tax = [_dot(t.astype(BF16), jnp.concatenate([bdrows(part("a_t", *i)), bdrows(x)], axis=1))
               for t, x, i in zip(tinv, a_ak, insts)]

        st = [s_ref[bi * N_COLGROUPS + g] for bi, g in insts]
        st_bd = [bdrows(x) for x in st]
        v_bd = [bdrows(part("v", *i)) for i in insts]
        u = [_dot_nt(x[:, 0:MXU_DIM].astype(BF16), s) + _dot(x[:, MXU_DIM:].astype(BF16), vb)
             for x, s, vb in zip(tax, st_bd, v_bd)]
        y = [_dot_nt(part("r_t", *i).astype(BF16), s)
             + _dot(ar, jnp.concatenate([bdrows(uu), vb], axis=0))
             for i, s, ar, uu, vb in zip(insts, st_bd, a_r, u, v_bd)]
        upd = [_dot(jnp.concatenate([uu, part("v", *i)], axis=0).T.astype(BF16),
                    jnp.concatenate([part("bp", *i), part("kp", *i)], axis=0).astype(BF16))
               for uu, i in zip(u, insts)]
        for (bi, g), s_old, x in zip(insts, st, upd):
            s_ref[bi * N_COLGROUPS + g] = s_old * part("p_all", bi, g) + diag_blocks(x)

        for bi in range(nb):
            yc = jnp.concatenate(y[bi * N_COLGROUPS:(bi + 1) * N_COLGROUPS], axis=1)
            mean = _segsum64(yc, ones_bd) * (1.0 / HEAD_DIM)
            yd = yc - mean
            var = _segsum64(yd * yd, ones_bd) * (1.0 / HEAD_DIM)
            yn = yd * lax.rsqrt(var + GN_EPS) * ln_g + ln_b
            bonus = _segsum64(pre[bi]["r"] * pre[bi]["k"] * r_k, ones_bd) * pre[bi]["v"]
            y_ref[bi, rows, :] = (yn + bonus) * _silu(ga_ref[bi, rows, :].astype(F32))
        return carry

    lax.fori_loop(0, tt // C, chunk_body, 0)


def _rwkv_scan(r, lw, k, v, a, b, proj, vec, tril, ones_bd, tt=128):
    B, S, W = r.shape
    spec = pl.BlockSpec((B, tt, W), lambda t: (0, t, 0))

    def full(arr):
        return pl.BlockSpec(arr.shape, lambda t: (0,) * arr.ndim)

    return pl.pallas_call(
        functools.partial(_scan_kernel, nb=B, tt=tt),
        grid=(S // tt,),
        in_specs=[spec] * 6 + [pl.BlockSpec((B, tt, W), lambda t: (0, t, COL_GA // W)),
                               full(vec), full(tril), full(ones_bd)],
        out_specs=spec,
        out_shape=jax.ShapeDtypeStruct((B, S, W), F32),
        scratch_shapes=[pltpu.VMEM((B * N_COLGROUPS, HEAD_DIM, MXU_DIM), F32)],
        compiler_params=_cparams(("arbitrary",)),
        name="rwkv_scan",
    )(r, lw, k, v, a, b, proj, vec, tril, ones_bd)


ATT_TILE = 2048
ATT_UNROLL = (3, 4, 4)


def _attn_kernel(*refs, tiles_per_seq):
    q_refs = refs[0:3]
    k_refs = refs[3:6]
    v_refs = refs[6:9]
    kp_refs = refs[9:12]
    vp_refs = refs[12:15]
    gb_ref, bias_ref, y_ref = refs[15:18]
    o_refs = refs[18:21]
    l_refs = refs[21:24]

    is_first = (pl.program_id(1) % tiles_per_seq) == 0
    prev_limit = jnp.where(is_first, BLK, 0)
    qi = lax.broadcasted_iota(jnp.int32, (2 * BLK, 2 * BLK), 0) & (BLK - 1)
    ki = lax.broadcasted_iota(jnp.int32, (2 * BLK, 2 * BLK), 1)
    delta = qi + BLK - ki
    band = (delta >= 0) & (delta <= BLK)
    head0 = lax.broadcasted_iota(jnp.int32, (BLK, LANES), 1) < HEAD_DIM
    scale = 1.0 / math.sqrt(HEAD_DIM)

    def process(g, d, blocks, from_prev):
        span = BLK * d

        def ds(start, size):
            return pl.ds(start, size) if d == 1 else pl.ds(start, size, stride=d)

        bias2 = bias_ref[g, 0].reshape(2 * BLK, 2 * BLK)
        zero = jnp.zeros((BLK, LANES), BF16)
        bases, q2s, kws, vws = [], [], [], []
        for sub, res in blocks:
            base = res if from_prev else sub * span + res
            row0 = 0 if from_prev else pl.multiple_of(sub * BLK, BLK)
            q = q_refs[g][0, res, pl.ds(row0, BLK), :] * scale
            q2s.append(jnp.concatenate([jnp.where(head0, q, zero), jnp.where(head0, zero, q)], axis=0))
            if from_prev:
                kw = jnp.concatenate([kp_refs[g][0, res], k_refs[g][0, res, 0:BLK, :]], axis=0)
                vw = jnp.concatenate([vp_refs[g][0, res], v_refs[g][0, res, 0:BLK, :]], axis=0)
            else:
                window = pl.ds(pl.multiple_of((sub - 1) * BLK, BLK), 2 * BLK)
                kw = k_refs[g][0, res, window, :]
                vw = v_refs[g][0, res, window, :]
            bases.append(base)
            kws.append(kw)
            vws.append(vw)
        logits =[_dot_nt(q2, kw) + bias2 for q2, kw in zip(q2s, kws)]
        logits = [jnp.where(band, x, NEG_INF) for x in logits]
        if from_prev:
            logits = [jnp.where(ki < prev_limit, NEG_INF, x) for x in logits]
        ms = [jnp.max(x, axis=-1, keepdims=True) for x in logits]
        ps = [jnp.exp(x - m) for x, m in zip(logits, ms)]
        dens = [jnp.sum(p, axis=-1, keepdims=True) for p in ps]
        pvs = [_dot(p.astype(BF16), vw) for p, vw in zip(ps, vws)]
        for base, pv, m, den in zip(bases, pvs, ms, dens):
            o = pv / den
            lse = m + jnp.log(den)
            o_refs[g][ds(base, BLK), :] = jnp.where(head0, o[0:BLK], o[BLK:])
            l_refs[g][ds(base, BLK), :] = jnp.where(head0, lse[0:BLK], lse[BLK:])

    for g, d in enumerate(DILATIONS):
        unroll = ATT_UNROLL[g]
        shift = int(math.log2(d))
        n_sub = ATT_TILE // (BLK * d)
        if d <= unroll:
            process(g, d, [(0, res) for res in range(d)], True)
        else:
            def first_body(it, carry, g=g, d=d, unroll=unroll):
                process(g, d, [(0, it * unroll + u) for u in range(unroll)], True)
                return carry
            lax.fori_loop(0, d // unroll, first_body, 0)
        n_rest = (n_sub - 1) * d
        if n_rest:
            def rest_body(it, carry, g=g, d=d, unroll=unroll, shift=shift):
                blks = [it * unroll + u for u in range(unroll)]
                process(g, d, [(1 + (b >> shift), b & (d - 1)) for b in blks], False)
                return carry
            lax.fori_loop(0, n_rest // unroll, rest_body, 0)

    l0, l1, l2 = l_refs[0][...], l_refs[1][...], l_refs[2][...]
    m = jnp.maximum(jnp.maximum(l0, l1), l2)
    w0, w1, w2 = jnp.exp(l0 - m), jnp.exp(l1 - m), jnp.exp(l2 - m)
    y = (w0 * o_refs[0][...] + w1 * o_refs[1][...] + w2 * o_refs[2][...]) / (w0 + w1 + w2)
    y_ref[0] = y * _silu(gb_ref[0].astype(F32))


def _dilated_attention(main, groups, bias5):
    B, S, _ = main.shape
    n_pairs = HEADS_PER_GROUP // 2
    tiles_per_seq = S // ATT_TILE
    arrays = [main.reshape(B, 1, S, MAIN_WIDTH)] + list(groups)
    col_base = [COL_A0 // LANES, 0, 0]

    def cur(g, part):
        d = DILATIONS[g]
        c0 = col_base[g] + part * (A_OUT_WIDTH // LANES)
        return pl.BlockSpec((1, d, ATT_TILE // d, LANES),
                            lambda hp, t: (t // tiles_per_seq, 0, t % tiles_per_seq, c0 + hp))

    def prev(g, part):
        d = DILATIONS[g]
        c0 = col_base[g] + part * (A_OUT_WIDTH // LANES)
        rb = ATT_TILE // (BLK * d)
        return pl.BlockSpec((1, d, BLK, LANES),
                            lambda hp, t: (t // tiles_per_seq, 0,
                                           jnp.maximum((t % tiles_per_seq) * rb - 1, 0), c0 + hp))

    def tile(col0):
        return pl.BlockSpec((1, ATT_TILE, LANES),
                            lambda hp, t: (t // tiles_per_seq, t % tiles_per_seq, col0 // LANES + hp))

    in_specs = ([cur(g, 0) for g in range(N_GROUPS)] + [cur(g, 1) for g in range(N_GROUPS)]
                + [cur(g, 2) for g in range(N_GROUPS)]
                + [prev(g, 1) for g in range(N_GROUPS)] + [prev(g, 2) for g in range(N_GROUPS)]
                + [tile(COL_GB),
                   pl.BlockSpec((N_GROUPS, 1, 2, BLK, 2 * BLK), lambda hp, t: (0, hp, 0, 0, 0))])
    scratch = [pltpu.VMEM((ATT_TILE, LANES), F32)] * 6
    return pl.pallas_call(
        functools.partial(_attn_kernel, tiles_per_seq=tiles_per_seq),
        grid=(n_pairs, B * tiles_per_seq),
        in_specs=in_specs,
        out_specs=tile(0),
        out_shape=jax.ShapeDtypeStruct((B, S, A_OUT_WIDTH), F32),
        scratch_shapes=scratch,
        compiler_params=_cparams(("parallel", "parallel")),
        name="dilated_attn",
    )(*(arrays * 5), main, bias5)


def _merge_kernel(ya_ref, yb_ref, ma_ref, mb_ref, x_ref, mod_ref, wa_ref, wb_ref, wo_ref, fg_ref,
                  o_ref, *, final_norm):
    pa = _dot(ya_ref[0].astype(BF16), wa_ref[...])
    pb = _dot(yb_ref[0].astype(BF16), wb_ref[...])
    merged = _sigmoid(ma_ref[0].astype(F32)) * pa + _sigmoid(mb_ref[0].astype(F32)) * pb
    out = _dot(merged.astype(BF16), wo_ref[...])
    gate = mod_ref[0, :, 2 * D_MODEL:3 * D_MODEL]
    xn = x_ref[0] + gate * out
    if final_norm:
        ms = jnp.mean(xn * xn, axis=-1, keepdims=True)
        xn = xn * lax.rsqrt(ms + RMS_EPS) * fg_ref[...]
    o_ref[0] = xn


def _merge(ya, yb, proj, x, mod_l, wa, wb, wo, final_g, final_norm, tm=512):
    B, S, D = x.shape

    def rows(width, c):
        return pl.BlockSpec((1, tm, width), lambda b, i: (b, i, c))

    def full(arr):
        return pl.BlockSpec(arr.shape, lambda b, i: (0,) * arr.ndim)

    return pl.pallas_call(
        functools.partial(_merge_kernel, final_norm=final_norm),
        grid=(B, S // tm),
        in_specs=[rows(R_WIDTH, 0), rows(A_OUT_WIDTH, 0),
                  rows(D, COL_MA // D), rows(D, COL_MB // D), rows(D, 0),
                  pl.BlockSpec((1, 1, 3 * D), lambda b, i: (b, 0, 0)),
                  full(wa), full(wb), full(wo), full(final_g)],
        out_specs=rows(D, 0),
        out_shape=jax.ShapeDtypeStruct((B, S, D), F32),
        compiler_params=_cparams(("parallel", "parallel")),
        name="merge",
    )(ya, yb, proj, proj, x, mod_l, wa, wb, wo, final_g)


def _constants():
    idx = np.arange(MXU_DIM)
    ones_bd = (idx[:, None] // HEAD_DIM == idx[None, :] // HEAD_DIM).astype(np.float32)
    t = np.arange(CHUNK)
    tril = (t[None, :] <= t[:, None]).astype(np.float32)
    return jnp.asarray(ones_bd, BF16), jnp.asarray(tril, BF16)


def kernel(x, c, norm_g, ada_w, ada_b, w_in, rwkv_mu_rkv, rwkv_mu_wa, rwkv_w0, rwkv_w1, rwkv_w2, rwkv_a0, rwkv_a1, rwkv_a2, rwkv_k_k, rwkv_k_a, rwkv_r_k, rwkv_ln_g, rwkv_ln_b, rwkv_mu_v, rwkv_v0, rwkv_v1, rwkv_v2, w_branch_a, w_branch_b, w_out, rel_bias, final_g):
    B, S, D = x.shape
    assert D == D_MODEL and S % ATT_TILE == 0 and w_in.shape[2] == PROJ_WIDTH
    ones_bd, tril = _constants()
    mod = _adaln_mod(c, ada_w, ada_b)
    bias = _rel_bias(rel_bias).reshape(N_GROUPS, HEADS_PER_GROUP // 2, 2, BLK, 2 * BLK)
    zeros_row = jnp.zeros((D,), F32)
    v_first = None
    for i in range(DEPTH):
        mod_l = mod[i, :B].reshape(B, 1, 3 * D)
        w = w_in[i]

        def cols(start, width, w=w):
            return w[:, start:start + width]

        def group_cols(g, w=w, cols=cols):
            return [cols(base + A_OUT_WIDTH * g, A_OUT_WIDTH) for base in (W_AQ, W_AK, W_AV)]

        w_main = jnp.concatenate(
            [cols(W_R, 4 * R_WIDTH), cols(W_MA, 2 * D_MODEL), cols(W_GB, A_OUT_WIDTH)] + group_cols(0),
            axis=1).astype(BF16)
        proj, h = _norm_proj(x, mod_l, norm_g[i], w_main)
        groups = [_group_proj(h, jnp.concatenate(group_cols(g), axis=1).astype(BF16), DILATIONS[g])
                  for g in range(1, N_GROUPS)]
        has_vres = i > 0
        pvec = jnp.stack(
            [rwkv_mu_rkv[i, 0], rwkv_mu_rkv[i, 1], rwkv_mu_rkv[i, 2], rwkv_mu_wa[i, 0], rwkv_mu_wa[i, 1],
             rwkv_w0[i], rwkv_a0[i], rwkv_k_k[i], rwkv_k_a[i],
             rwkv_mu_v[i - 1] if has_vres else zeros_row, rwkv_v0[i - 1] if has_vres else zeros_row]
            + [zeros_row] * (PV_ROWS - 11))
        lora = [rwkv_w1[i].astype(BF16), rwkv_w2[i].astype(BF16),
                rwkv_a1[i].astype(BF16), rwkv_a2[i].astype(BF16)]
        if has_vres:
            lora += [rwkv_v1[i - 1].astype(BF16), rwkv_v2[i - 1].astype(BF16)]
        r, lw, k, v, a, b = _rwkv_prep(h, proj, v_first, pvec, lora, ones_bd)
        if i == 0:
            v_first = v
        vec = jnp.stack([rwkv_r_k[i].reshape(-1), rwkv_ln_g[i], rwkv_ln_b[i]] + [zeros_row] * 5)
        y_a = _rwkv_scan(r, lw, k, v, a, b, proj, vec, tril, ones_bd)
        y_b = _dilated_attention(proj, groups, bias)
        x = _merge(y_a, y_b, proj, x, mod_l, w_branch_a[i].astype(BF16), w_branch_b[i].astype(BF16),
                   w_out[i].astype(BF16), final_g.reshape(1, D), final_norm=(i == DEPTH - 1))
    return x
```

```python
import functools
import math

import numpy as np
import jax
import jax.numpy as jnp
from jax import lax
from jax.experimental import pallas as pl
from jax.experimental.pallas import tpu as pltpu

F32 = jnp.float32
BF16 = jnp.bfloat16

D_MODEL = 1024
DEPTH = 2
HEAD_DIM = 64
R_WIDTH = 1024
N_GROUPS = 3
HEADS_PER_GROUP = 8
DILATIONS = (1, 4, 16)
BLK = 128
A_QK_WIDTH = 1536
A_OUT_WIDTH = 512
NUM_BUCKETS = 32
MAX_DISTANCE = 2048
PROJ_WIDTH = 4 * R_WIDTH + 3 * A_QK_WIDTH + A_OUT_WIDTH + 2 * D_MODEL
RMS_EPS = 1e-6
GN_EPS = 64e-5
NEG_INF = -1e30

LANES = 128
MXU_DIM = 256
HEADS_PER_TILE = MXU_DIM // HEAD_DIM
N_COLGROUPS = R_WIDTH // MXU_DIM
CHUNK = 64

W_R, W_K, W_V, W_GA = 0, 1024, 2048, 3072
W_AQ, W_AK, W_AV = 4096, 5632, 7168
W_GB, W_MA, W_MB = 8704, 9216, 10240
COL_R, COL_K, COL_V, COL_GA, COL_MA, COL_MB, COL_GB, COL_A0 = 0, 1024, 2048, 3072, 4096, 5120, 6144, 6656
MAIN_WIDTH = 8192
GROUP_WIDTH = 3 * A_OUT_WIDTH

VMEM_LIMIT = 56 * 1024 * 1024


def _cparams(sem):
    return pltpu.CompilerParams(dimension_semantics=sem, vmem_limit_bytes=VMEM_LIMIT)


def _sigmoid(z):
    return 1.0 / (1.0 + jnp.exp(-z))


def _silu(z):
    return z * _sigmoid(z)


def _softplus(z):
    return jnp.maximum(z, 0.0) + jnp.log(1.0 + jnp.exp(-jnp.abs(z)))


def _dot(a, b):
    return jnp.dot(a, b, preferred_element_type=F32)


def _dot_nt(a, b):
    return lax.dot_general(a, b, (((1,), (1,)), ((), ())), preferred_element_type=F32)


def _split2(x):
    hi = x.astype(BF16)
    lo = (x - hi.astype(F32)).astype(BF16)
    return hi, lo


def _segsum64(x, ones_bd, split):
    n = x.shape[0]
    xs = jnp.concatenate([x[:, MXU_DIM * g:MXU_DIM * (g + 1)] for g in range(N_COLGROUPS)], axis=0)
    if split:
        hi, lo = _split2(xs)
        s = _dot(hi, ones_bd) + _dot(lo, ones_bd)
    else:
        s = _dot(xs.astype(BF16), ones_bd)
    return jnp.concatenate([s[n * g:n * (g + 1)] for g in range(N_COLGROUPS)], axis=1)


def _mod_kernel(c_ref, w_ref, b_ref, o_ref):
    s = _silu(c_ref[...])
    o_ref[0] = jnp.dot(s, w_ref[0], preferred_element_type=F32,
                       precision=lax.Precision.HIGHEST) + b_ref[0]


def _adaln_mod(c, ada_w, ada_b):
    L = ada_w.shape[0]
    B = c.shape[0]
    c8 = jnp.pad(c, ((0, 8 - B), (0, 0)))
    nj = 3
    return pl.pallas_call(
        _mod_kernel,
        grid=(L, nj),
        in_specs=[pl.BlockSpec((8, D_MODEL), lambda l, j: (0, 0)),
                  pl.BlockSpec((1, D_MODEL, D_MODEL), lambda l, j: (l, 0, j)),
                  pl.BlockSpec((1, 1, D_MODEL), lambda l, j: (l, 0, j))],
        out_specs=pl.BlockSpec((1, 8, D_MODEL), lambda l, j: (l, 0, j)),
        out_shape=jax.ShapeDtypeStruct((L, 8, 3 * D_MODEL), F32),
        compiler_params=_cparams(("parallel", "parallel")),
        name="adaln_mod",
    )(c8, ada_w, ada_b.reshape(L, 1, 3 * D_MODEL))


def _t5_bucket(dist):
    max_exact = NUM_BUCKETS // 2
    safe = np.maximum(dist, 1).astype(np.float32)
    large = max_exact + (np.log(safe / max_exact) / math.log(MAX_DISTANCE / max_exact)
                         * (NUM_BUCKETS - max_exact)).astype(np.int32)
    large = np.minimum(large, NUM_BUCKETS - 1)
    return np.where(dist < max_exact, dist, large).astype(np.int32)


def _bias_kernel(tab_ref, bucket_ref, o_ref):
    h = pl.program_id(0)
    bk = bucket_ref[0]
    acc = jnp.zeros(bk.shape, F32)
    for b in range(NUM_BUCKETS):
        acc = jnp.where(bk == b, tab_ref[h * NUM_BUCKETS + b], acc)
    o_ref[0] = acc


def _rel_bias(rel_bias):
    n_heads = rel_bias.shape[1]
    qi = np.arange(BLK)[:, None]
    ki = np.arange(2 * BLK)[None, :]
    delta = np.maximum(qi + BLK - ki, 0)
    buckets = np.stack([_t5_bucket(delta * d) for d in DILATIONS]).astype(np.int32)
    table = rel_bias.T.reshape(-1)
    return pl.pallas_call(
        _bias_kernel,
        grid=(n_heads,),
        in_specs=[pl.BlockSpec(memory_space=pltpu.SMEM),
                  pl.BlockSpec((1, BLK, 2 * BLK), lambda h: (h // HEADS_PER_GROUP, 0, 0))],
        out_specs=pl.BlockSpec((1, BLK, 2 * BLK), lambda h: (h, 0, 0)),
        out_shape=jax.ShapeDtypeStruct((n_heads, BLK, 2 * BLK), F32),
        compiler_params=_cparams(("parallel",)),
        name="rel_bias",
    )(table, jnp.asarray(buckets))


def _proj_kernel(x_ref, mod_ref, g_ref, w_ref, proj_ref, h_ref):
    @pl.when(pl.program_id(2) == 0)
    def _():
        x = x_ref[0]
        ms = jnp.mean(x * x, axis=-1, keepdims=True)
        y = x * lax.rsqrt(ms + RMS_EPS) * g_ref[...]
        shift = mod_ref[0, :, 0:D_MODEL]
        scale = mod_ref[0, :, D_MODEL:2 * D_MODEL]
        h_ref[0] = (y * (1.0 + scale) + shift).astype(BF16)

    proj_ref[0] = _dot(h_ref[0], w_ref[...]).astype(BF16)


def _norm_proj(x, mod_l, norm_g, w_main, tm=1024, tn=1024):
    B, S, D = x.shape
    N = w_main.shape[1]
    return pl.pallas_call(
        _proj_kernel,
        grid=(B, S // tm, N // tn),
        in_specs=[pl.BlockSpec((1, tm, D), lambda b, i, j: (b, i, 0)),
                  pl.BlockSpec((1, 1, 3 * D), lambda b, i, j: (b, 0, 0)),
                  pl.BlockSpec((1, D), lambda b, i, j: (0, 0)),
                  pl.BlockSpec((D, tn), lambda b, i, j: (0, j))],
        out_specs=[pl.BlockSpec((1, tm, tn), lambda b, i, j: (b, i, j)),
                   pl.BlockSpec((1, tm, D), lambda b, i, j: (b, i, 0))],
        out_shape=[jax.ShapeDtypeStruct((B, S, N), BF16),
                   jax.ShapeDtypeStruct((B, S, D), BF16)],
        compiler_params=_cparams(("parallel", "parallel", "arbitrary")),
        name="norm_proj",
    )(x, mod_l, norm_g.reshape(1, D), w_main)


GATHER_ROWS = 256


def _group_proj_kernel(h_ref, w_ref, o_ref, stage_a, stage_b, *, d):
    tm, tn = h_ref.shape[1], w_ref.shape[1]
    per_res = GATHER_ROWS // d
    for ck in range(tm // GATHER_ROWS):
        stage = stage_b if ck % 2 else stage_a
        r0 = ck * GATHER_ROWS
        res = _dot(h_ref[0, r0:r0 + GATHER_ROWS, :], w_ref[...])
        for c in range(tn // LANES):
            stage[c] = res[:, LANES * c:LANES * (c + 1)]
        for r in range(d):
            for c in range(tn // LANES):
                o_ref[0, r, ck * per_res:(ck + 1) * per_res, LANES * c:LANES * (c + 1)] = (
                    stage[c, pl.ds(r, per_res, stride=d), :].astype(BF16))


def _group_proj(h, w_group, d, tm=1024, tn=512):
    B, S, D = h.shape
    N = w_group.shape[1]
    return pl.pallas_call(
        functools.partial(_group_proj_kernel, d=d),
        grid=(B, S // tm, N // tn),
        in_specs=[pl.BlockSpec((1, tm, D), lambda b, i, j: (b, i, 0)),
                  pl.BlockSpec((D, tn), lambda b, i, j: (0, j))],
        out_specs=pl.BlockSpec((1, d, tm // d, tn), lambda b, i, j: (b, 0, i, j)),
        out_shape=jax.ShapeDtypeStruct((B, d, S // d, N), BF16),
        scratch_shapes=[pltpu.VMEM((tn // LANES, GATHER_ROWS, LANES), F32)] * 2,
        compiler_params=_cparams(("parallel", "parallel", "parallel")),
        name=f"group_proj_d{d}",
    )(h, w_group)


PV_MU_R, PV_MU_K, PV_MU_V, PV_MU_W, PV_MU_A, PV_W0, PV_A0, PV_KK, PV_KA, PV_MU_VRES, PV_V0 = range(11)
PV_ROWS = 16
PREV_ROWS = 16


def _shift_rows(t, prev_last):
    rolled = pltpu.roll(t, 1, axis=0)
    row = lax.broadcasted_iota(jnp.int32, t.shape, 0)
    return jnp.where(row == 0, prev_last, rolled)


def _rprep_kernel(*refs, has_vres):
    if has_vres:
        (h_ref, hp_ref, pr_ref, prp_ref, pk_ref, pkp_ref, pvv_ref, pvp_ref, vf_ref, pvec_ref,
         w1_ref, w2_ref, a1_ref, a2_ref, v1_ref, v2_ref, ones_ref, tril_ref,
         r_out, cum_out, k_out, v_out, a_out, b_out) = refs
    else:
        (h_ref, hp_ref, pr_ref, prp_ref, pk_ref, pkp_ref, pvv_ref, pvp_ref, pvec_ref,
         w1_ref, w2_ref, a1_ref, a2_ref, ones_ref, tril_ref,
         r_out, cum_out, k_out, v_out, a_out, b_out) = refs

    not_first = (pl.program_id(1) > 0).astype(F32)

    def prm(i):
        return pvec_ref[i:i + 1, :]

    def shifted(cur_ref, prev_ref):
        t = cur_ref[0].astype(F32)
        last = prev_ref[0, PREV_ROWS - 1:PREV_ROWS, :].astype(F32)
        return t, _shift_rows(t, last * not_first)

    def lerp(t, ts, mu):
        return t + (ts - t) * mu

    h, hs = shifted(h_ref, hp_ref)
    pr, prs = shifted(pr_ref, prp_ref)
    pk, pks = shifted(pk_ref, pkp_ref)
    pv, pvs = shifted(pvv_ref, pvp_ref)

    r = lerp(pr, prs, prm(PV_MU_R))
    k = lerp(pk, pks, prm(PV_MU_K))
    v = lerp(pv, pvs, prm(PV_MU_V))
    xw = lerp(h, hs, prm(PV_MU_W)).astype(BF16)
    xa = lerp(h, hs, prm(PV_MU_A)).astype(BF16)

    zw = prm(PV_W0) + _dot(jnp.tanh(_dot(xw, w1_ref[...])).astype(BF16), w2_ref[...])
    w = -_softplus(-zw) - 0.5
    lw = -jnp.exp(w)
    hi = lw.astype(BF16)
    rest = lw - hi.astype(F32)
    mid = rest.astype(BF16)
    lo = (rest - mid.astype(F32)).astype(BF16)
    tril = tril_ref[...]
    cum_out[0] = _dot(tril, hi) + _dot(tril, mid) + _dot(tril, lo)
    a = _sigmoid(prm(PV_A0) + _dot(_dot(xa, a1_ref[...]).astype(BF16), a2_ref[...]))
    if has_vres:
        xv = lerp(h, hs, prm(PV_MU_VRES)).astype(BF16)
        mix = _sigmoid(prm(PV_V0) + _dot(_dot(xv, v1_ref[...]).astype(BF16), v2_ref[...]))
        v = v + (vf_ref[0] - v) * mix

    kk = k * prm(PV_KK)
    ss = _segsum64(kk * kk, ones_ref[...], split=False)
    kk = kk * lax.rsqrt(jnp.maximum(ss, 1e-24))
    r_out[0] = r
    k_out[0] = k * (1.0 + (a - 1.0) * prm(PV_KA))
    v_out[0] = v
    a_out[0] = -kk
    b_out[0] = kk * a


def _rwkv_prep(h, proj, v_first, pvec, lora, ones_bd, tr=256):
    B, S, D = h.shape
    has_vres = v_first is not None
    rpb = tr // PREV_ROWS
    t = np.arange(tr)
    tril_bd = jnp.asarray((t[None, :] <= t[:, None]) & (t[None, :] // CHUNK == t[:, None] // CHUNK), BF16)

    def cur(c):
        return pl.BlockSpec((1, tr, R_WIDTH), lambda b, i: (b, i, c))

    def prev(c):
        return pl.BlockSpec((1, PREV_ROWS, R_WIDTH), lambda b, i: (b, jnp.maximum(i * rpb - 1, 0), c))

    def full(arr):
        return pl.BlockSpec(arr.shape, lambda b, i: (0,) * arr.ndim)

    in_specs = [cur(0), prev(0)]
    args = [h, h]
    for c in (COL_R, COL_K, COL_V):
        in_specs += [cur(c // R_WIDTH), prev(c // R_WIDTH)]
        args += [proj, proj]
    if has_vres:
        in_specs.append(cur(0))
        args.append(v_first)
    in_specs.append(full(pvec))
    args.append(pvec)
    for wgt in lora:
        in_specs.append(full(wgt))
        args.append(wgt)
    for const in (ones_bd, tril_bd):
        in_specs.append(full(const))
        args.append(const)
    out = jax.ShapeDtypeStruct((B, S, R_WIDTH), F32)
    return pl.pallas_call(
        functools.partial(_rprep_kernel, has_vres=has_vres),
        grid=(B, S // tr),
        in_specs=in_specs,
        out_specs=[cur(0)] * 6,
        out_shape=[out] * 6,
        compiler_params=_cparams(("parallel", "parallel")),
        name="rwkv_prep",
    )(*args)


def _scan_kernel(r_ref, cum_ref, k_ref, v_ref, a_ref, b_ref, ga_ref, vec_ref, ones_ref,
                 y_ref, s_ref, *, nb, tt):
    C = CHUNK

    @pl.when(pl.program_id(0) == 0)
    def _():
        s_ref[...] = jnp.zeros(s_ref.shape, F32)

    row = lax.broadcasted_iota(jnp.int32, (C, MXU_DIM), 0)
    lane = lax.broadcasted_iota(jnp.int32, (C, MXU_DIM), 1)
    col = lane & (HEAD_DIM - 1)
    lhead = lane >> 6
    strict = col < row
    incl = col <= row
    eye = (col == row).astype(F32)
    head_masks = [lhead == hh for hh in range(HEADS_PER_TILE)]

    def bdrows(x):
        return jnp.concatenate([jnp.where(m, x, 0.0) for m in head_masks], axis=0).astype(BF16)

    def diag_blocks(full):
        acc = jnp.where(head_masks[0], full[0:C], 0.0)
        for hh in range(1, HEADS_PER_TILE):
            acc = acc + jnp.where(head_masks[hh], full[C * hh:C * (hh + 1)], 0.0)
        return acc

    row_full = lax.broadcasted_iota(jnp.int32, (C, R_WIDTH), 0)
    ones_bd = ones_ref[...]
    r_k = vec_ref[0:1, :]
    ln_g = vec_ref[1:2, :]
    ln_b = vec_ref[2:3, :]

    def chunk_body(ci, carry):
        rows = pl.ds(pl.multiple_of(ci * C, C), C)
        pre = []
        for bi in range(nb):
            cum = cum_ref[bi, rows, :]
            r = r_ref[bi, rows, :]
            k = k_ref[bi, rows, :]
            v = v_ref[bi, rows, :]
            a = a_ref[bi, rows, :]
            b = b_ref[bi, rows, :]
            total = cum[C - 1:C, :]
            p_in = jnp.exp(cum)
            p_inv = jnp.exp(-cum)
            p_rest = jnp.exp(total - cum)
            p_before = jnp.where(row_full == 0, 1.0, pltpu.roll(p_in, 1, axis=0))
            pre.append(dict(r=r, k=k, v=v, a_t=a * p_before, r_t=r * p_in, b_t=b * p_inv,
                            k_t=k * p_inv, bp=b * p_rest, kp=k * p_rest, p_all=jnp.exp(total)))

        insts = [(bi, g) for bi in range(nb) for g in range(N_COLGROUPS)]

        def part(name, bi, g):
            return pre[bi][name][:, MXU_DIM * g:MXU_DIM * (g + 1)]

        res = [_dot_nt(jnp.concatenate([part("a_t", *i), part("r_t", *i)], axis=0).astype(BF16),
                       jnp.concatenate([bdrows(part("b_t", *i)), bdrows(part("k_t", *i))], axis=0))
               for i in insts]
        a_ab = [jnp.where(strict, x[0:C, 0:MXU_DIM], 0.0) for x in res]
        a_ak = [jnp.where(strict, x[0:C, MXU_DIM:], 0.0) for x in res]
        a_rb = [jnp.where(incl, x[C:, 0:MXU_DIM], 0.0).astype(BF16) for x in res]
        a_rk = [jnp.where(incl, x[C:, MXU_DIM:], 0.0) for x in res]

        pw = [_dot(x.astype(BF16), bdrows(x)) for x in a_ab]
        tinv = [eye + x for x in a_ab]
        for _ in range(4):
            both = [_dot(jnp.concatenate([p, t], axis=0).astype(BF16), bdrows(p)) for p, t in zip(pw, tinv)]
            tinv = [t + x[C:] for t, x in zip(tinv, both)]
            pw = [x[0:C] for x in both]
        tinv = [t + _dot(t.astype(BF16), bdrows(p)) for p, t in zip(pw, tinv)]
        tax = [_dot(t.astype(BF16), jnp.concatenate([bdrows(part("a_t", *i)), bdrows(x)], axis=1))
               for t, x, i in zip(tinv, a_ak, insts)]

        st = [s_ref[bi * N_COLGROUPS + g] for bi, g in insts]
        from_state = [_dot_nt(jnp.concatenate([x[:, 0:MXU_DIM], part("r_t", *i)], axis=0).astype(BF16),
                              bdrows(s)) for x, i, s in zip(tax, insts, st)]
        from_v = [_dot(jnp.concatenate([x[:, MXU_DIM:], ark], axis=0).astype(BF16), bdrows(part("v", *i)))
                  for x, ark, i in zip(tax, a_rk, insts)]
        u = [x[0:C] + z[0:C] for x, z in zip(from_state, from_v)]
        y = [x[C:] + z[C:] + _dot(arb, bdrows(uu))
             for x, z, arb, uu in zip(from_state, from_v, a_rb, u)]
        upd =[_dot(jnp.concatenate([uu, part("v", *i)], axis=0).T.astype(BF16),
                    jnp.concatenate([part("bp", *i), part("kp", *i)], axis=0).astype(BF16))
               for uu, i in zip(u, insts)]
        for (bi, g), s_old, x in zip(insts, st, upd):
            s_ref[bi * N_COLGROUPS + g] = s_old * part("p_all", bi, g) + diag_blocks(x)

        for bi in range(nb):
            yc = jnp.concatenate(y[bi * N_COLGROUPS:(bi + 1) * N_COLGROUPS], axis=1)
            mean = _segsum64(yc, ones_bd, split=True) * (1.0 / HEAD_DIM)
            yd = yc - mean
            var = _segsum64(yd * yd, ones_bd, split=False) * (1.0 / HEAD_DIM)
            yn = yd * lax.rsqrt(var + GN_EPS) * ln_g + ln_b
            bonus = _segsum64(pre[bi]["r"] * pre[bi]["k"] * r_k, ones_bd, split=False) * pre[bi]["v"]
            y_ref[bi, rows, :] = (yn + bonus) * _silu(ga_ref[bi, rows, :].astype(F32))
        return carry

    lax.fori_loop(0, tt // C, chunk_body, 0)


def _rwkv_scan(r, cum, k, v, a, b, proj, vec, ones_bd, tt=128):
    B, S, W = r.shape
    spec = pl.BlockSpec((B, tt, W), lambda t: (0, t, 0))

    def full(arr):
        return pl.BlockSpec(arr.shape, lambda t: (0,) * arr.ndim)

    return pl.pallas_call(
        functools.partial(_scan_kernel, nb=B, tt=tt),
        grid=(S // tt,),
        in_specs=[spec] * 6 + [pl.BlockSpec((B, tt, W), lambda t: (0, t, COL_GA // W)),
                               full(vec), full(ones_bd)],
        out_specs=spec,
        out_shape=jax.ShapeDtypeStruct((B, S, W), F32),
        scratch_shapes=[pltpu.VMEM((B * N_COLGROUPS, HEAD_DIM, MXU_DIM), F32)],
        compiler_params=_cparams(("arbitrary",)),
        name="rwkv_scan",
    )(r, cum, k, v, a, b, proj, vec, ones_bd)


ATT_TILE = 2048
ATT_UNROLL = (3, 4, 4)


def _attn_kernel(*refs, tiles_per_seq):
    q_refs = refs[0:3]
    k_refs = refs[3:6]
    v_refs = refs[6:9]
    kp_refs = refs[9:12]
    vp_refs = refs[12:15]
    gb_ref, bias_ref, y_ref = refs[15:18]
    o_refs = refs[18:21]
    l_refs = refs[21:24]

    is_first = (pl.program_id(1) % tiles_per_seq) == 0
    prev_limit = jnp.where(is_first, BLK, 0)
    qi = lax.broadcasted_iota(jnp.int32, (2 * BLK, 2 * BLK), 0) & (BLK - 1)
    ki = lax.broadcasted_iota(jnp.int32, (2 * BLK, 2 * BLK), 1)
    delta = qi + BLK - ki
    band = (delta >= 0) & (delta <= BLK)
    head0 = lax.broadcasted_iota(jnp.int32, (BLK, LANES), 1) < HEAD_DIM
    scale = 1.0 / math.sqrt(HEAD_DIM)

    def process(g, d, blocks, from_prev):
        span = BLK * d

        def ds(start, size):
            return pl.ds(start, size) if d == 1 else pl.ds(start, size, stride=d)

        bias2 = bias_ref[g, 0].reshape(2 * BLK, 2 * BLK)
        zero = jnp.zeros((BLK, LANES), BF16)
        bases, q2s, kws, vws = [], [], [], []
        for sub, res in blocks:
            base = res if from_prev else sub * span + res
            row0 = 0 if from_prev else pl.multiple_of(sub * BLK, BLK)
            q = q_refs[g][0, res, pl.ds(row0, BLK), :] * scale
            q2s.append(jnp.concatenate([jnp.where(head0, q, zero), jnp.where(head0, zero, q)], axis=0))
            if from_prev:
                kw = jnp.concatenate([kp_refs[g][0, res], k_refs[g][0, res, 0:BLK, :]], axis=0)
                vw = jnp.concatenate([vp_refs[g][0, res], v_refs[g][0, res, 0:BLK, :]], axis=0)
            else:
                window = pl.ds(pl.multiple_of((sub - 1) * BLK, BLK), 2 * BLK)
                kw = k_refs[g][0, res, window, :]
                vw = v_refs[g][0, res, window, :]
            bases.append(base)
            kws.append(kw)
            vws.append(vw)
        logits =[_dot_nt(q2, kw) + bias2 for q2, kw in zip(q2s, kws)]
        logits = [jnp.where(band, x, NEG_INF) for x in logits]
        if from_prev:
            logits = [jnp.where(ki < prev_limit, NEG_INF, x) for x in logits]
        ms = [jnp.max(x, axis=-1, keepdims=True) for x in logits]
        ps = [jnp.exp(x - m) for x, m in zip(logits, ms)]
        dens = [jnp.sum(p, axis=-1, keepdims=True) for p in ps]
        pvs = [_dot(p.astype(BF16), vw) for p, vw in zip(ps, vws)]
        for base, pv, m, den in zip(bases, pvs, ms, dens):
            o = pv / den
            lse = m + jnp.log(den)
            o_refs[g][ds(base, BLK), :] = jnp.where(head0, o[0:BLK], o[BLK:])
            l_refs[g][ds(base, BLK), :] = jnp.where(head0, lse[0:BLK], lse[BLK:])

    for g, d in enumerate(DILATIONS):
        unroll = ATT_UNROLL[g]
        shift = int(math.log2(d))
        n_sub = ATT_TILE // (BLK * d)
        if d <= unroll:
            process(g, d, [(0, res) for res in range(d)], True)
        else:
            def first_body(it, carry, g=g, d=d, unroll=unroll):
                process(g, d, [(0, it * unroll + u) for u in range(unroll)], True)
                return carry
            lax.fori_loop(0, d // unroll, first_body, 0)
        n_rest = (n_sub - 1) * d
        if n_rest:
            def rest_body(it, carry, g=g, d=d, unroll=unroll, shift=shift):
                blks = [it * unroll + u for u in range(unroll)]
                process(g, d, [(1 + (b >> shift), b & (d - 1)) for b in blks], False)
                return carry
            lax.fori_loop(0, n_rest // unroll, rest_body, 0)

    l0, l1, l2 = l_refs[0][...], l_refs[1][...], l_refs[2][...]
    m = jnp.maximum(jnp.maximum(l0, l1), l2)
    w0, w1, w2 = jnp.exp(l0 - m), jnp.exp(l1 - m), jnp.exp(l2 - m)
    y = (w0 * o_refs[0][...] + w1 * o_refs[1][...] + w2 * o_refs[2][...]) / (w0 + w1 + w2)
    y_ref[0] = y * _silu(gb_ref[0].astype(F32))


def _dilated_attention(main, groups, bias5):
    B, S, _ = main.shape
    n_pairs = HEADS_PER_GROUP // 2
    tiles_per_seq = S // ATT_TILE
    arrays = [main.reshape(B, 1, S, MAIN_WIDTH)] + list(groups)
    col_base = [COL_A0 // LANES, 0, 0]

    def cur(g, part):
        d = DILATIONS[g]
        c0 = col_base[g] + part * (A_OUT_WIDTH // LANES)
        return pl.BlockSpec((1, d, ATT_TILE // d, LANES),
                            lambda hp, t: (t // tiles_per_seq, 0, t % tiles_per_seq, c0 + hp))

    def prev(g, part):
        d = DILATIONS[g]
        c0 = col_base[g] + part * (A_OUT_WIDTH // LANES)
        rb = ATT_TILE // (BLK * d)
        return pl.BlockSpec((1, d, BLK, LANES),
                            lambda hp, t: (t // tiles_per_seq, 0,
                                           jnp.maximum((t % tiles_per_seq) * rb - 1, 0), c0 + hp))

    def tile(col0):
        return pl.BlockSpec((1, ATT_TILE, LANES),
                            lambda hp, t: (t // tiles_per_seq, t % tiles_per_seq, col0 // LANES + hp))

    in_specs = ([cur(g, 0) for g in range(N_GROUPS)] + [cur(g, 1) for g in range(N_GROUPS)]
                + [cur(g, 2) for g in range(N_GROUPS)]
                + [prev(g, 1) for g in range(N_GROUPS)] + [prev(g, 2) for g in range(N_GROUPS)]
                + [tile(COL_GB),
                   pl.BlockSpec((N_GROUPS, 1, 2, BLK, 2 * BLK), lambda hp, t: (0, hp, 0, 0, 0))])
    scratch = [pltpu.VMEM((ATT_TILE, LANES), F32)] * 6
    return pl.pallas_call(
        functools.partial(_attn_kernel, tiles_per_seq=tiles_per_seq),
        grid=(n_pairs, B * tiles_per_seq),
        in_specs=in_specs,
        out_specs=tile(0),
        out_shape=jax.ShapeDtypeStruct((B, S, A_OUT_WIDTH), F32),
        scratch_shapes=scratch,
        compiler_params=_cparams(("parallel", "parallel")),
        name="dilated_attn",
    )(*(arrays * 5), main, bias5)


def _merge_kernel(ya_ref, yb_ref, ma_ref, mb_ref, x_ref, mod_ref, wa_ref, wb_ref, wo_ref, fg_ref,
                  o_ref, *, final_norm):
    pa = _dot(ya_ref[0].astype(BF16), wa_ref[...])
    pb = _dot(yb_ref[0].astype(BF16), wb_ref[...])
    merged = _sigmoid(ma_ref[0].astype(F32)) * pa + _sigmoid(mb_ref[0].astype(F32)) * pb
    out = _dot(merged.astype(BF16), wo_ref[...])
    gate = mod_ref[0, :, 2 * D_MODEL:3 * D_MODEL]
    xn = x_ref[0] + gate * out
    if final_norm:
        ms = jnp.mean(xn * xn, axis=-1, keepdims=True)
        xn = xn * lax.rsqrt(ms + RMS_EPS) * fg_ref[...]
    o_ref[0] = xn


def _merge(ya, yb, proj, x, mod_l, wa, wb, wo, final_g, final_norm, tm=512):
    B, S, D = x.shape

    def rows(width, c):
        return pl.BlockSpec((1, tm, width), lambda b, i: (b, i, c))

    def full(arr):
        return pl.BlockSpec(arr.shape, lambda b, i: (0,) * arr.ndim)

    return pl.pallas_call(
        functools.partial(_merge_kernel, final_norm=final_norm),
        grid=(B, S // tm),
        in_specs=[rows(R_WIDTH, 0), rows(A_OUT_WIDTH, 0),
                  rows(D, COL_MA // D), rows(D, COL_MB // D), rows(D, 0),
                  pl.BlockSpec((1, 1, 3 * D), lambda b, i: (b, 0, 0)),
                  full(wa), full(wb), full(wo), full(final_g)],
        out_specs=rows(D, 0),
        out_shape=jax.ShapeDtypeStruct((B, S, D), F32),
        compiler_params=_cparams(("parallel", "parallel")),
        name="merge",
    )(ya, yb, proj, proj, x, mod_l, wa, wb, wo, final_g)


def _segment_ones():
    idx = np.arange(MXU_DIM)
    return jnp.asarray(idx[:, None] // HEAD_DIM == idx[None, :] // HEAD_DIM, BF16)


def kernel(x, c, norm_g, ada_w, ada_b, w_in, rwkv_mu_rkv, rwkv_mu_wa, rwkv_w0, rwkv_w1, rwkv_w2, rwkv_a0, rwkv_a1, rwkv_a2, rwkv_k_k, rwkv_k_a, rwkv_r_k, rwkv_ln_g, rwkv_ln_b, rwkv_mu_v, rwkv_v0, rwkv_v1, rwkv_v2, w_branch_a, w_branch_b, w_out, rel_bias, final_g):
    B, S, D = x.shape
    assert D == D_MODEL and S % ATT_TILE == 0 and w_in.shape[2] == PROJ_WIDTH
    ones_bd = _segment_ones()
    mod = _adaln_mod(c, ada_w, ada_b)
    bias = _rel_bias(rel_bias).reshape(N_GROUPS, HEADS_PER_GROUP // 2, 2, BLK, 2 * BLK)
    zeros_row = jnp.zeros((D,), F32)
    v_first = None
    for i in range(DEPTH):
        mod_l = mod[i, :B].reshape(B, 1, 3 * D)
        w = w_in[i]

        def cols(start, width, w=w):
            return w[:, start:start + width]

        def group_cols(g, w=w, cols=cols):
            return [cols(base + A_OUT_WIDTH * g, A_OUT_WIDTH) for base in (W_AQ, W_AK, W_AV)]

        w_main = jnp.concatenate(
            [cols(W_R, 4 * R_WIDTH), cols(W_MA, 2 * D_MODEL), cols(W_GB, A_OUT_WIDTH)] + group_cols(0),
            axis=1).astype(BF16)
        proj, h = _norm_proj(x, mod_l, norm_g[i], w_main)
        groups = [_group_proj(h, jnp.concatenate(group_cols(g), axis=1).astype(BF16), DILATIONS[g])
                  for g in range(1, N_GROUPS)]
        has_vres = i > 0
        pvec = jnp.stack(
            [rwkv_mu_rkv[i, 0], rwkv_mu_rkv[i, 1], rwkv_mu_rkv[i, 2], rwkv_mu_wa[i, 0], rwkv_mu_wa[i, 1],
             rwkv_w0[i], rwkv_a0[i], rwkv_k_k[i], rwkv_k_a[i],
             rwkv_mu_v[i - 1] if has_vres else zeros_row, rwkv_v0[i - 1] if has_vres else zeros_row]
            + [zeros_row] * (PV_ROWS - 11))
        lora = [rwkv_w1[i].astype(BF16), rwkv_w2[i].astype(BF16),
                rwkv_a1[i].astype(BF16), rwkv_a2[i].astype(BF16)]
        if has_vres:
            lora += [rwkv_v1[i - 1].astype(BF16), rwkv_v2[i - 1].astype(BF16)]
        r, cum, k, v, a, b = _rwkv_prep(h, proj, v_first, pvec, lora, ones_bd)
        if i == 0:
            v_first = v
        vec = jnp.stack([rwkv_r_k[i].reshape(-1), rwkv_ln_g[i], rwkv_ln_b[i]] + [zeros_row] * 5)
        y_a = _rwkv_scan(r, cum, k, v, a, b, proj, vec, ones_bd)
        y_b = _dilated_attention(proj, groups, bias)
        x = _merge(y_a, y_b, proj, x, mod_l, w_branch_a[i].astype(BF16), w_branch_b[i].astype(BF16),
                   w_out[i].astype(BF16), final_g.reshape(1, D), final_norm=(i == DEPTH - 1))
    return x
```

```python
import functools
import math

import numpy as np
import jax
import jax.numpy as jnp
from jax import lax
from jax.experimental import pallas as pl
from jax.experimental.pallas import tpu as pltpu

F32 = jnp.float32
BF16 = jnp.bfloat16

D_MODEL = 1024
DEPTH = 2
HEAD_DIM = 64
R_WIDTH = 1024
N_GROUPS = 3
HEADS_PER_GROUP = 8
DILATIONS = (1, 4, 16)
BLK = 128
A_QK_WIDTH = 1536
A_OUT_WIDTH = 512
NUM_BUCKETS = 32
MAX_DISTANCE = 2048
PROJ_WIDTH = 4 * R_WIDTH + 3 * A_QK_WIDTH + A_OUT_WIDTH + 2 * D_MODEL
RMS_EPS = 1e-6
GN_EPS = 64e-5
NEG_INF = -1e30

LANES = 128
MXU_DIM = 256
HEADS_PER_TILE = MXU_DIM // HEAD_DIM
N_COLGROUPS = R_WIDTH // MXU_DIM
CHUNK = 64
SCAN_CHUNKS = 2

W_R, W_K, W_V, W_GA = 0, 1024, 2048, 3072
W_AQ, W_AK, W_AV = 4096, 5632, 7168
W_GB, W_MA, W_MB = 8704, 9216, 10240
COL_R, COL_K, COL_V, COL_GA, COL_MA, COL_MB, COL_GB, COL_A0 = 0, 1024, 2048, 3072, 4096, 5120, 6144, 6656
MAIN_WIDTH = 8192
GROUP_WIDTH = 3 * A_OUT_WIDTH

VMEM_LIMIT = 56 * 1024 * 1024


def _cparams(sem):
    return pltpu.CompilerParams(dimension_semantics=sem, vmem_limit_bytes=VMEM_LIMIT)


def _sigmoid(z):
    return 1.0 / (1.0 + jnp.exp(-z))


def _silu(z):
    return z * _sigmoid(z)


def _softplus(z):
    return jnp.maximum(z, 0.0) + jnp.log(1.0 + jnp.exp(-jnp.abs(z)))


def _dot(a, b):
    return jnp.dot(a, b, preferred_element_type=F32)


def _dot_nt(a, b):
    return lax.dot_general(a, b, (((1,), (1,)), ((), ())), preferred_element_type=F32)


def _split2(x):
    hi = x.astype(BF16)
    lo = (x - hi.astype(F32)).astype(BF16)
    return hi, lo


def _segsum64(x, ones_bd, split):
    n = x.shape[0]
    xs = jnp.concatenate([x[:, MXU_DIM * g:MXU_DIM * (g + 1)] for g in range(N_COLGROUPS)], axis=0)
    if split:
        hi, lo = _split2(xs)
        s = _dot(hi, ones_bd) + _dot(lo, ones_bd)
    else:
        s = _dot(xs.astype(BF16), ones_bd)
    return jnp.concatenate([s[n * g:n * (g + 1)] for g in range(N_COLGROUPS)], axis=1)


def _mod_kernel(c_ref, w_ref, b_ref, o_ref):
    s = _silu(c_ref[...])
    o_ref[0] = jnp.dot(s, w_ref[0], preferred_element_type=F32,
                       precision=lax.Precision.HIGHEST) + b_ref[0]


def _adaln_mod(c, ada_w, ada_b):
    L = ada_w.shape[0]
    B = c.shape[0]
    c8 = jnp.pad(c, ((0, 8 - B), (0, 0)))
    nj = 3
    return pl.pallas_call(
        _mod_kernel,
        grid=(L, nj),
        in_specs=[pl.BlockSpec((8, D_MODEL), lambda l, j: (0, 0)),
                  pl.BlockSpec((1, D_MODEL, D_MODEL), lambda l, j: (l, 0, j)),
                  pl.BlockSpec((1, 1, D_MODEL), lambda l, j: (l, 0, j))],
        out_specs=pl.BlockSpec((1, 8, D_MODEL), lambda l, j: (l, 0, j)),
        out_shape=jax.ShapeDtypeStruct((L, 8, 3 * D_MODEL), F32),
        compiler_params=_cparams(("parallel", "parallel")),
        name="adaln_mod",
    )(c8, ada_w, ada_b.reshape(L, 1, 3 * D_MODEL))


def _t5_bucket(dist):
    max_exact = NUM_BUCKETS // 2
    safe = np.maximum(dist, 1).astype(np.float32)
    large = max_exact + (np.log(safe / max_exact) / math.log(MAX_DISTANCE / max_exact)
                         * (NUM_BUCKETS - max_exact)).astype(np.int32)
    large = np.minimum(large, NUM_BUCKETS - 1)
    return np.where(dist < max_exact, dist, large).astype(np.int32)


def _bias_kernel(tab_ref, bucket_ref, o_ref):
    h = pl.program_id(0)
    bk = bucket_ref[0]
    acc = jnp.zeros(bk.shape, F32)
    for b in range(NUM_BUCKETS):
        acc = jnp.where(bk == b, tab_ref[h * NUM_BUCKETS + b], acc)
    o_ref[0] = acc


def _rel_bias(rel_bias):
    n_heads = rel_bias.shape[1]
    qi = np.arange(BLK)[:, None]
    ki = np.arange(2 * BLK)[None, :]
    delta = np.maximum(qi + BLK - ki, 0)
    buckets = np.stack([_t5_bucket(delta * d) for d in DILATIONS]).astype(np.int32)
    table = rel_bias.T.reshape(-1)
    return pl.pallas_call(
        _bias_kernel,
        grid=(n_heads,),
        in_specs=[pl.BlockSpec(memory_space=pltpu.SMEM),
                  pl.BlockSpec((1, BLK, 2 * BLK), lambda h: (h // HEADS_PER_GROUP, 0, 0))],
        out_specs=pl.BlockSpec((1, BLK, 2 * BLK), lambda h: (h, 0, 0)),
        out_shape=jax.ShapeDtypeStruct((n_heads, BLK, 2 * BLK), F32),
        compiler_params=_cparams(("parallel",)),
        name="rel_bias",
    )(table, jnp.asarray(buckets))


def _proj_kernel(x_ref, mod_ref, g_ref, w_ref, proj_ref, h_ref):
    @pl.when(pl.program_id(2) == 0)
    def _():
        x = x_ref[0]
        ms = jnp.mean(x * x, axis=-1, keepdims=True)
        y = x * lax.rsqrt(ms + RMS_EPS) * g_ref[...]
        shift = mod_ref[0, :, 0:D_MODEL]
        scale = mod_ref[0, :, D_MODEL:2 * D_MODEL]
        h_ref[0] = (y * (1.0 + scale) + shift).astype(BF16)

    proj_ref[0] = _dot(h_ref[0], w_ref[...]).astype(BF16)


def _norm_proj(x, mod_l, norm_g, w_main, tm=1024, tn=1024):
    B, S, D = x.shape
    N = w_main.shape[1]
    return pl.pallas_call(
        _proj_kernel,
        grid=(B, S // tm, N // tn),
        in_specs=[pl.BlockSpec((1, tm, D), lambda b, i, j: (b, i, 0)),
                  pl.BlockSpec((1, 1, 3 * D), lambda b, i, j: (b, 0, 0)),
                  pl.BlockSpec((1, D), lambda b, i, j: (0, 0)),
                  pl.BlockSpec((D, tn), lambda b, i, j: (0, j))],
        out_specs=[pl.BlockSpec((1, tm, tn), lambda b, i, j: (b, i, j)),
                   pl.BlockSpec((1, tm, D), lambda b, i, j: (b, i, 0))],
        out_shape=[jax.ShapeDtypeStruct((B, S, N), BF16),
                   jax.ShapeDtypeStruct((B, S, D), BF16)],
        compiler_params=_cparams(("parallel", "parallel", "arbitrary")),
        name="norm_proj",
    )(x, mod_l, norm_g.reshape(1, D), w_main)


GATHER_ROWS = 256


def _group_proj_kernel(h_ref, w_ref, o_ref, stage_a, stage_b, *, d):
    tm, tn = h_ref.shape[1], w_ref.shape[1]
    per_res = GATHER_ROWS // d
    for ck in range(tm // GATHER_ROWS):
        stage = stage_b if ck % 2 else stage_a
        r0 = ck * GATHER_ROWS
        res = _dot(h_ref[0, r0:r0 + GATHER_ROWS, :], w_ref[...])
        for c in range(tn // LANES):
            stage[c] = res[:, LANES * c:LANES * (c + 1)]
        for r in range(d):
            for c in range(tn // LANES):
                o_ref[0, r, ck * per_res:(ck + 1) * per_res, LANES * c:LANES * (c + 1)] = (
                    stage[c, pl.ds(r, per_res, stride=d), :].astype(BF16))


def _group_proj(h, w_group, d, tm=1024, tn=512):
    B, S, D = h.shape
    N = w_group.shape[1]
    return pl.pallas_call(
        functools.partial(_group_proj_kernel, d=d),
        grid=(B, S // tm, N // tn),
        in_specs=[pl.BlockSpec((1, tm, D), lambda b, i, j: (b, i, 0)),
                  pl.BlockSpec((D, tn), lambda b, i, j: (0, j))],
        out_specs=pl.BlockSpec((1, d, tm // d, tn), lambda b, i, j: (b, 0, i, j)),
        out_shape=jax.ShapeDtypeStruct((B, d, S // d, N), BF16),
        scratch_shapes=[pltpu.VMEM((tn // LANES, GATHER_ROWS, LANES), F32)] * 2,
        compiler_params=_cparams(("parallel", "parallel", "parallel")),
        name=f"group_proj_d{d}",
    )(h, w_group)


PV_MU_R, PV_MU_K, PV_MU_V, PV_MU_W, PV_MU_A, PV_W0, PV_A0, PV_KK, PV_KA, PV_MU_VRES, PV_V0 = range(11)
PV_ROWS = 16
PREV_ROWS = 16


def _shift_rows(t, prev_last):
    rolled = pltpu.roll(t, 1, axis=0)
    row = lax.broadcasted_iota(jnp.int32, t.shape, 0)
    return jnp.where(row == 0, prev_last, rolled)


def _rprep_kernel(*refs, has_vres):
    if has_vres:
        (h_ref, hp_ref, pr_ref, prp_ref, pk_ref, pkp_ref, pvv_ref, pvp_ref, vf_ref, pvec_ref,
         w1_ref, w2_ref, a1_ref, a2_ref, v1_ref, v2_ref, ones_ref, tril_ref,
         r_out, cum_out, k_out, v_out, a_out, b_out) = refs
    else:
        (h_ref, hp_ref, pr_ref, prp_ref, pk_ref, pkp_ref, pvv_ref, pvp_ref, pvec_ref,
         w1_ref, w2_ref, a1_ref, a2_ref, ones_ref, tril_ref,
         r_out, cum_out, k_out, v_out, a_out, b_out) = refs

    not_first = (pl.program_id(1) > 0).astype(F32)

    def prm(i):
        return pvec_ref[i:i + 1, :]

    def shifted(cur_ref, prev_ref):
        t = cur_ref[0].astype(F32)
        last = prev_ref[0, PREV_ROWS - 1:PREV_ROWS, :].astype(F32)
        return t, _shift_rows(t, last * not_first)

    def lerp(t, ts, mu):
        return t + (ts - t) * mu

    h, hs = shifted(h_ref, hp_ref)
    pr, prs = shifted(pr_ref, prp_ref)
    pk, pks = shifted(pk_ref, pkp_ref)
    pv, pvs = shifted(pvv_ref, pvp_ref)

    r = lerp(pr, prs, prm(PV_MU_R))
    k = lerp(pk, pks, prm(PV_MU_K))
    v = lerp(pv, pvs, prm(PV_MU_V))
    xw = lerp(h, hs, prm(PV_MU_W)).astype(BF16)
    xa = lerp(h, hs, prm(PV_MU_A)).astype(BF16)

    zw = prm(PV_W0) + _dot(jnp.tanh(_dot(xw, w1_ref[...])).astype(BF16), w2_ref[...])
    w = -_softplus(-zw) - 0.5
    lw = -jnp.exp(w)
    hi = lw.astype(BF16)
    rest = lw - hi.astype(F32)
    mid = rest.astype(BF16)
    lo = (rest - mid.astype(F32)).astype(BF16)
    tril = tril_ref[...]
    cum_out[0] = _dot(tril, hi) + _dot(tril, mid) + _dot(tril, lo)
    a = _sigmoid(prm(PV_A0) + _dot(_dot(xa, a1_ref[...]).astype(BF16), a2_ref[...]))
    if has_vres:
        xv = lerp(h, hs, prm(PV_MU_VRES)).astype(BF16)
        mix = _sigmoid(prm(PV_V0) + _dot(_dot(xv, v1_ref[...]).astype(BF16), v2_ref[...]))
        v = v + (vf_ref[0] - v) * mix

    kk = k * prm(PV_KK)
    ss = _segsum64(kk * kk, ones_ref[...], split=False)
    kk = kk * lax.rsqrt(jnp.maximum(ss, 1e-24))
    r_out[0] = r
    k_out[0] = k * (1.0 + (a - 1.0) * prm(PV_KA))
    v_out[0] = v
    a_out[0] = -kk
    b_out[0] = kk * a


def _rwkv_prep(h, proj, v_first, pvec, lora, ones_bd, tr=256):
    B, S, D = h.shape
    has_vres = v_first is not None
    rpb = tr // PREV_ROWS
    t = np.arange(tr)
    tril_bd = jnp.asarray((t[None, :] <= t[:, None]) & (t[None, :] // CHUNK == t[:, None] // CHUNK), BF16)

    def cur(c):
        return pl.BlockSpec((1, tr, R_WIDTH), lambda b, i: (b, i, c))

    def prev(c):
        return pl.BlockSpec((1, PREV_ROWS, R_WIDTH), lambda b, i: (b, jnp.maximum(i * rpb - 1, 0), c))

    def full(arr):
        return pl.BlockSpec(arr.shape, lambda b, i: (0,) * arr.ndim)

    in_specs = [cur(0), prev(0)]
    args = [h, h]
    for c in (COL_R, COL_K, COL_V):
        in_specs += [cur(c // R_WIDTH), prev(c // R_WIDTH)]
        args += [proj, proj]
    if has_vres:
        in_specs.append(cur(0))
        args.append(v_first)
    in_specs.append(full(pvec))
    args.append(pvec)
    for wgt in lora:
        in_specs.append(full(wgt))
        args.append(wgt)
    for const in (ones_bd, tril_bd):
        in_specs.append(full(const))
        args.append(const)
    out = jax.ShapeDtypeStruct((B, S, R_WIDTH), F32)
    return pl.pallas_call(
        functools.partial(_rprep_kernel, has_vres=has_vres),
        grid=(B, S // tr),
        in_specs=in_specs,
        out_specs=[cur(0)] * 6,
        out_shape=[out] * 6,
        compiler_params=_cparams(("parallel", "parallel")),
        name="rwkv_prep",
    )(*args)


def _scan_kernel(r_ref, cum_ref, k_ref, v_ref, a_ref, b_ref, ga_ref, vec_ref, ones_ref,
                 y_ref, s_ref, *, nb, tt):
    C = CHUNK

    @pl.when(pl.program_id(0) == 0)
    def _():
        s_ref[...] = jnp.zeros(s_ref.shape, F32)

    row = lax.broadcasted_iota(jnp.int32, (C, MXU_DIM), 0)
    lane = lax.broadcasted_iota(jnp.int32, (C, MXU_DIM), 1)
    col = lane & (HEAD_DIM - 1)
    lhead = lane >> 6
    strict = col < row
    incl = col <= row
    eye = (col == row).astype(F32)
    head_masks = [lhead == hh for hh in range(HEADS_PER_TILE)]

    def bdrows(x):
        return jnp.concatenate([jnp.where(m, x, 0.0) for m in head_masks], axis=0).astype(BF16)

    def diag_blocks(full):
        acc = jnp.where(head_masks[0], full[0:C], 0.0)
        for hh in range(1, HEADS_PER_TILE):
            acc = acc + jnp.where(head_masks[hh], full[C * hh:C * (hh + 1)], 0.0)
        return acc

    row_full = lax.broadcasted_iota(jnp.int32, (C, R_WIDTH), 0)
    ones_bd = ones_ref[...]
    r_k = vec_ref[0:1, :]
    ln_g = vec_ref[1:2, :]
    ln_b = vec_ref[2:3, :]

    def body(it, carry):
        rows = [pl.ds(pl.multiple_of((it * SCAN_CHUNKS + ck) * C, C), C) for ck in range(SCAN_CHUNKS)]
        pre = {}
        for ck in range(SCAN_CHUNKS):
            for bi in range(nb):
                cum = cum_ref[bi, rows[ck], :]
                r = r_ref[bi, rows[ck], :]
                k = k_ref[bi, rows[ck], :]
                v = v_ref[bi, rows[ck], :]
                a = a_ref[bi, rows[ck], :]
                b = b_ref[bi, rows[ck], :]
                total = cum[C - 1:C, :]
                p_in = jnp.exp(cum)
                p_inv = jnp.exp(-cum)
                p_rest = jnp.exp(total - cum)
                p_before = jnp.where(row_full == 0, 1.0, pltpu.roll(p_in, 1, axis=0))
                pre[ck, bi] = dict(r=r, k=k, v=v, a_t=a * p_before, r_t=r * p_in, b_t=b * p_inv,
                                   k_t=k * p_inv, bp=b * p_rest, kp=k * p_rest, p_all=jnp.exp(total))

        chains = [(bi, g) for bi in range(nb) for g in range(N_COLGROUPS)]
        insts = [(ck, bi, g) for ck in range(SCAN_CHUNKS) for bi, g in chains]

        def part(name, ck, bi, g):
            return pre[ck, bi][name][:, MXU_DIM * g:MXU_DIM * (g + 1)]

        res = [_dot_nt(jnp.concatenate([part("a_t", *i), part("r_t", *i)], axis=0).astype(BF16),
                       jnp.concatenate([bdrows(part("b_t", *i)), bdrows(part("k_t", *i))], axis=0))
               for i in insts]
        a_ab = [jnp.where(strict, x[0:C, 0:MXU_DIM], 0.0) for x in res]
        a_ak = [jnp.where(strict, x[0:C, MXU_DIM:], 0.0) for x in res]
        a_rb = [jnp.where(incl, x[C:, 0:MXU_DIM], 0.0).astype(BF16) for x in res]
        a_rk = [jnp.where(incl, x[C:, MXU_DIM:], 0.0) for x in res]

        pw = [_dot(x.astype(BF16), bdrows(x)) for x in a_ab]
        tinv = [eye + x for x in a_ab]
        for _ in range(4):
            both = [_dot(jnp.concatenate([p, t], axis=0).astype(BF16), bdrows(p)) for p, t in zip(pw, tinv)]
            tinv = [t + x[C:] for t, x in zip(tinv, both)]
            pw = [x[0:C] for x in both]
        tinv = [t + _dot(t.astype(BF16), bdrows(p)) for p, t in zip(pw, tinv)]
        tax = [_dot(t.astype(BF16), jnp.concatenate([bdrows(part("a_t", *i)), bdrows(x)], axis=1))
               for t, x, i in zip(tinv, a_ak, insts)]
        from_v = [_dot(jnp.concatenate([x[:, MXU_DIM:], ark], axis=0).astype(BF16), bdrows(part("v", *i)))
                  for x, ark, i in zip(tax, a_rk, insts)]

        st = [s_ref[bi * N_COLGROUPS + g] for bi, g in chains]
        y = {}
        for ck in range(SCAN_CHUNKS):
            sel = range(ck * len(chains), (ck + 1) * len(chains))
            from_state = [_dot_nt(jnp.concatenate([tax[n][:, 0:MXU_DIM], part("r_t", *insts[n])],
                                                  axis=0).astype(BF16), bdrows(s))
                          for n, s in zip(sel, st)]
            u = [x[0:C] + from_v[n][0:C] for x, n in zip(from_state, sel)]
            for x, n, uu in zip(from_state, sel, u):
                y[insts[n]] = x[C:] + from_v[n][C:] + _dot(a_rb[n], bdrows(uu))
            upd = [_dot(jnp.concatenate([uu, part("v", *insts[n])], axis=0).T.astype(BF16),
                        jnp.concatenate([part("bp", *insts[n]), part("kp", *insts[n])], axis=0).astype(BF16))
                   for uu, n in zip(u, sel)]
            st = [s_old * part("p_all", *insts[n]) + diag_blocks(x) for s_old, x, n in zip(st, upd, sel)]
        for (bi, g), s_new in zip(chains, st):
            s_ref[bi * N_COLGROUPS + g] = s_new

        for ck in range(SCAN_CHUNKS):
            for bi in range(nb):
                p = pre[ck, bi]
                yc = jnp.concatenate([y[ck, bi, g] for g in range(N_COLGROUPS)], axis=1)
                mean = _segsum64(yc, ones_bd, split=True) * (1.0 / HEAD_DIM)
                yd = yc - mean
                var = _segsum64(yd * yd, ones_bd, split=False) * (1.0 / HEAD_DIM)
                yn = yd * lax.rsqrt(var + GN_EPS) * ln_g + ln_b
                bonus = _segsum64(p["r"] * p["k"] * r_k, ones_bd, split=False) * p["v"]
                y_ref[bi, rows[ck], :] = (yn + bonus) * _silu(ga_ref[bi, rows[ck], :].astype(F32))
        return carry

    n_iter = tt // (C * SCAN_CHUNKS)
    if n_iter == 1:
        body(0, 0)
    else:
        lax.fori_loop(0, n_iter, body, 0)


def _rwkv_scan(r, cum, k, v, a, b, proj, vec, ones_bd, tt=128):
    B, S, W = r.shape
    spec = pl.BlockSpec((B, tt, W), lambda t: (0, t, 0))

    def full(arr):
        return pl.BlockSpec(arr.shape, lambda t: (0,) * arr.ndim)

    return pl.pallas_call(
        functools.partial(_scan_kernel, nb=B, tt=tt),
        grid=(S // tt,),
        in_specs=[spec] * 6 + [pl.BlockSpec((B, tt, W), lambda t: (0, t, COL_GA // W)),
                               full(vec), full(ones_bd)],
        out_specs=spec,
        out_shape=jax.ShapeDtypeStruct((B, S, W), F32),
        scratch_shapes=[pltpu.VMEM((B * N_COLGROUPS, HEAD_DIM, MXU_DIM), F32)],
        compiler_params=_cparams(("arbitrary",)),
        name="rwkv_scan",
    )(r, cum, k, v, a, b, proj, vec, ones_bd)


ATT_TILE = 2048
ATT_UNROLL = (3, 4, 4)


def _attn_kernel(*refs, tiles_per_seq):
    q_refs = refs[0:3]
    k_refs = refs[3:6]
    v_refs = refs[6:9]
    kp_refs = refs[9:12]
    vp_refs = refs[12:15]
    gb_ref, bias_ref, y_ref = refs[15:18]
    o_refs = refs[18:21]
    l_refs = refs[21:24]

    is_first = (pl.program_id(1) % tiles_per_seq) == 0
    prev_limit = jnp.where(is_first, BLK, 0)
    qi = lax.broadcasted_iota(jnp.int32, (2 * BLK, 2 * BLK), 0) & (BLK - 1)
    ki = lax.broadcasted_iota(jnp.int32, (2 * BLK, 2 * BLK), 1)
    delta = qi + BLK - ki
    band = (delta >= 0) & (delta <= BLK)
    head0 = lax.broadcasted_iota(jnp.int32, (BLK, LANES), 1) < HEAD_DIM
    scale = 1.0 / math.sqrt(HEAD_DIM)

    def process(g, d, blocks, from_prev):
        span = BLK * d

        def ds(start, size):
            return pl.ds(start, size) if d == 1 else pl.ds(start, size, stride=d)

        bias2 = bias_ref[g, 0].reshape(2 * BLK, 2 * BLK)
        zero = jnp.zeros((BLK, LANES), BF16)
        bases, q2s, kws, vws = [], [], [], []
        for sub, res in blocks:
            base = res if from_prev else sub * span + res
            row0 = 0 if from_prev else pl.multiple_of(sub * BLK, BLK)
            q = q_refs[g][0, res, pl.ds(row0, BLK), :] * scale
            q2s.append(jnp.concatenate([jnp.where(head0, q, zero), jnp.where(head0, zero, q)], axis=0))
            if from_prev:
                kw = jnp.concatenate([kp_refs[g][0, res], k_refs[g][0, res, 0:BLK, :]], axis=0)
                vw = jnp.concatenate([vp_refs[g][0, res], v_refs[g][0, res, 0:BLK, :]], axis=0)
            else:
                window = pl.ds(pl.multiple_of((sub - 1) * BLK, BLK), 2 * BLK)
                kw = k_refs[g][0, res, window, :]
                vw = v_refs[g][0, res, window, :]
            bases.append(base)
            kws.append(kw)
            vws.append(vw)
        logits =[_dot_nt(q2, kw) + bias2 for q2, kw in zip(q2s, kws)]
        logits = [jnp.where(band, x, NEG_INF) for x in logits]
        if from_prev:
            logits = [jnp.where(ki < prev_limit, NEG_INF, x) for x in logits]
        ms = [jnp.max(x, axis=-1, keepdims=True) for x in logits]
        ps = [jnp.exp(x - m) for x, m in zip(logits, ms)]
        dens = [jnp.sum(p, axis=-1, keepdims=True) for p in ps]
        pvs = [_dot(p.astype(BF16), vw) for p, vw in zip(ps, vws)]
        for base, pv, m, den in zip(bases, pvs, ms, dens):
            o = pv / den
            lse = m + jnp.log(den)
            o_refs[g][ds(base, BLK), :] = jnp.where(head0, o[0:BLK], o[BLK:])
            l_refs[g][ds(base, BLK), :] = jnp.where(head0, lse[0:BLK], lse[BLK:])

    for g, d in enumerate(DILATIONS):
        unroll = ATT_UNROLL[g]
        shift = int(math.log2(d))
        n_sub = ATT_TILE // (BLK * d)
        if d <= unroll:
            process(g, d, [(0, res) for res in range(d)], True)
        else:
            def first_body(it, carry, g=g, d=d, unroll=unroll):
                process(g, d, [(0, it * unroll + u) for u in range(unroll)], True)
                return carry
            lax.fori_loop(0, d // unroll, first_body, 0)
        n_rest = (n_sub - 1) * d
        if n_rest:
            def rest_body(it, carry, g=g, d=d, unroll=unroll, shift=shift):
                blks = [it * unroll + u for u in range(unroll)]
                process(g, d, [(1 + (b >> shift), b & (d - 1)) for b in blks], False)
                return carry
            lax.fori_loop(0, n_rest // unroll, rest_body, 0)

    l0, l1, l2 = l_refs[0][...], l_refs[1][...], l_refs[2][...]
    m = jnp.maximum(jnp.maximum(l0, l1), l2)
    w0, w1, w2 = jnp.exp(l0 - m), jnp.exp(l1 - m), jnp.exp(l2 - m)
    y = (w0 * o_refs[0][...] + w1 * o_refs[1][...] + w2 * o_refs[2][...]) / (w0 + w1 + w2)
    y_ref[0] = y * _silu(gb_ref[0].astype(F32))


def _dilated_attention(main, groups, bias5):
    B, S, _ = main.shape
    n_pairs = HEADS_PER_GROUP // 2
    tiles_per_seq = S // ATT_TILE
    arrays = [main.reshape(B, 1, S, MAIN_WIDTH)] + list(groups)
    col_base = [COL_A0 // LANES, 0, 0]

    def cur(g, part):
        d = DILATIONS[g]
        c0 = col_base[g] + part * (A_OUT_WIDTH // LANES)
        return pl.BlockSpec((1, d, ATT_TILE // d, LANES),
                            lambda hp, t: (t // tiles_per_seq, 0, t % tiles_per_seq, c0 + hp))

    def prev(g, part):
        d = DILATIONS[g]
        c0 = col_base[g] + part * (A_OUT_WIDTH // LANES)
        rb = ATT_TILE // (BLK * d)
        return pl.BlockSpec((1, d, BLK, LANES),
                            lambda hp, t: (t // tiles_per_seq, 0,
                                           jnp.maximum((t % tiles_per_seq) * rb - 1, 0), c0 + hp))

    def tile(col0):
        return pl.BlockSpec((1, ATT_TILE, LANES),
                            lambda hp, t: (t // tiles_per_seq, t % tiles_per_seq, col0 // LANES + hp))

    in_specs = ([cur(g, 0) for g in range(N_GROUPS)] + [cur(g, 1) for g in range(N_GROUPS)]
                + [cur(g, 2) for g in range(N_GROUPS)]
                + [prev(g, 1) for g in range(N_GROUPS)] + [prev(g, 2) for g in range(N_GROUPS)]
                + [tile(COL_GB),
                   pl.BlockSpec((N_GROUPS, 1, 2, BLK, 2 * BLK), lambda hp, t: (0, hp, 0, 0, 0))])
    scratch = [pltpu.VMEM((ATT_TILE, LANES), F32)] * 6
    return pl.pallas_call(
        functools.partial(_attn_kernel, tiles_per_seq=tiles_per_seq),
        grid=(n_pairs, B * tiles_per_seq),
        in_specs=in_specs,
        out_specs=tile(0),
        out_shape=jax.ShapeDtypeStruct((B, S, A_OUT_WIDTH), F32),
        scratch_shapes=scratch,
        compiler_params=_cparams(("parallel", "parallel")),
        name="dilated_attn",
    )(*(arrays * 5), main, bias5)


def _merge_kernel(ya_ref, yb_ref, ma_ref, mb_ref, x_ref, mod_ref, wa_ref, wb_ref, wo_ref, fg_ref,
                  o_ref, *, final_norm):
    pa = _dot(ya_ref[0].astype(BF16), wa_ref[...])
    pb = _dot(yb_ref[0].astype(BF16), wb_ref[...])
    merged = _sigmoid(ma_ref[0].astype(F32)) * pa + _sigmoid(mb_ref[0].astype(F32)) * pb
    out = _dot(merged.astype(BF16), wo_ref[...])
    gate = mod_ref[0, :, 2 * D_MODEL:3 * D_MODEL]
    xn = x_ref[0] + gate * out
    if final_norm:
        ms = jnp.mean(xn * xn, axis=-1, keepdims=True)
        xn = xn * lax.rsqrt(ms + RMS_EPS) * fg_ref[...]
    o_ref[0] = xn


def _merge(ya, yb, proj, x, mod_l, wa, wb, wo, final_g, final_norm, tm=512):
    B, S, D = x.shape

    def rows(width, c):
        return pl.BlockSpec((1, tm, width), lambda b, i: (b, i, c))

    def full(arr):
        return pl.BlockSpec(arr.shape, lambda b, i: (0,) * arr.ndim)

    return pl.pallas_call(
        functools.partial(_merge_kernel, final_norm=final_norm),
        grid=(B, S // tm),
        in_specs=[rows(R_WIDTH, 0), rows(A_OUT_WIDTH, 0),
                  rows(D, COL_MA // D), rows(D, COL_MB // D), rows(D, 0),
                  pl.BlockSpec((1, 1, 3 * D), lambda b, i: (b, 0, 0)),
                  full(wa), full(wb), full(wo), full(final_g)],
        out_specs=rows(D, 0),
        out_shape=jax.ShapeDtypeStruct((B, S, D), F32),
        compiler_params=_cparams(("parallel", "parallel")),
        name="merge",
    )(ya, yb, proj, proj, x, mod_l, wa, wb, wo, final_g)


def _segment_ones():
    idx = np.arange(MXU_DIM)
    return jnp.asarray(idx[:, None] // HEAD_DIM == idx[None, :] // HEAD_DIM, BF16)


def kernel(x, c, norm_g, ada_w, ada_b, w_in, rwkv_mu_rkv, rwkv_mu_wa, rwkv_w0, rwkv_w1, rwkv_w2, rwkv_a0, rwkv_a1, rwkv_a2, rwkv_k_k, rwkv_k_a, rwkv_r_k, rwkv_ln_g, rwkv_ln_b, rwkv_mu_v, rwkv_v0, rwkv_v1, rwkv_v2, w_branch_a, w_branch_b, w_out, rel_bias, final_g):
    B, S, D = x.shape
    assert D == D_MODEL and S % ATT_TILE == 0 and w_in.shape[2] == PROJ_WIDTH
    ones_bd = _segment_ones()
    mod = _adaln_mod(c, ada_w, ada_b)
    bias = _rel_bias(rel_bias).reshape(N_GROUPS, HEADS_PER_GROUP // 2, 2, BLK, 2 * BLK)
    zeros_row = jnp.zeros((D,), F32)
    v_first = None
    for i in range(DEPTH):
        mod_l = mod[i, :B].reshape(B, 1, 3 * D)
        w = w_in[i]

        def cols(start, width, w=w):
            return w[:, start:start + width]

        def group_cols(g, w=w, cols=cols):
            return [cols(base + A_OUT_WIDTH * g, A_OUT_WIDTH) for base in (W_AQ, W_AK, W_AV)]

        w_main = jnp.concatenate(
            [cols(W_R, 4 * R_WIDTH), cols(W_MA, 2 * D_MODEL), cols(W_GB, A_OUT_WIDTH)] + group_cols(0),
            axis=1).astype(BF16)
        proj, h = _norm_proj(x, mod_l, norm_g[i], w_main)
        groups = [_group_proj(h, jnp.concatenate(group_cols(g), axis=1).astype(BF16), DILATIONS[g])
                  for g in range(1, N_GROUPS)]
        has_vres = i > 0
        pvec = jnp.stack(
            [rwkv_mu_rkv[i, 0], rwkv_mu_rkv[i, 1], rwkv_mu_rkv[i, 2], rwkv_mu_wa[i, 0], rwkv_mu_wa[i, 1],
             rwkv_w0[i], rwkv_a0[i], rwkv_k_k[i], rwkv_k_a[i],
             rwkv_mu_v[i - 1] if has_vres else zeros_row, rwkv_v0[i - 1] if has_vres else zeros_row]
            + [zeros_row] * (PV_ROWS - 11))
        lora = [rwkv_w1[i].astype(BF16), rwkv_w2[i].astype(BF16),
                rwkv_a1[i].astype(BF16), rwkv_a2[i].astype(BF16)]
        if has_vres:
            lora += [rwkv_v1[i - 1].astype(BF16), rwkv_v2[i - 1].astype(BF16)]
        r, cum, k, v, a, b = _rwkv_prep(h, proj, v_first, pvec, lora, ones_bd)
        if i == 0:
            v_first = v
        vec = jnp.stack([rwkv_r_k[i].reshape(-1), rwkv_ln_g[i], rwkv_ln_b[i]] + [zeros_row] * 5)
        y_a = _rwkv_scan(r, cum, k, v, a, b, proj, vec, ones_bd)
        y_b = _dilated_attention(proj, groups, bias)
        x = _merge(y_a, y_b, proj, x, mod_l, w_branch_a[i].astype(BF16), w_branch_b[i].astype(BF16),
                   w_out[i].astype(BF16), final_g.reshape(1, D), final_norm=(i == DEPTH - 1))
    return x
```

```python
import functools
import math

import numpy as np
import jax
import jax.numpy as jnp
from jax import lax
from jax.experimental import pallas as pl
from jax.experimental.pallas import tpu as pltpu

F32 = jnp.float32
BF16 = jnp.bfloat16

D_MODEL = 1024
DEPTH = 2
HEAD_DIM = 64
R_WIDTH = 1024
N_GROUPS = 3
HEADS_PER_GROUP = 8
DILATIONS = (1, 4, 16)
BLK = 128
A_QK_WIDTH = 1536
A_OUT_WIDTH = 512
NUM_BUCKETS = 32
MAX_DISTANCE = 2048
PROJ_WIDTH = 4 * R_WIDTH + 3 * A_QK_WIDTH + A_OUT_WIDTH + 2 * D_MODEL
RMS_EPS = 1e-6
GN_EPS = 64e-5
NEG_INF = -1e30

LANES = 128
MXU_DIM = 256
HEADS_PER_TILE = MXU_DIM // HEAD_DIM
N_COLGROUPS = R_WIDTH // MXU_DIM
CHUNK = 64
SCAN_CHUNKS = 2

W_R, W_K, W_V, W_GA = 0, 1024, 2048, 3072
W_AQ, W_AK, W_AV = 4096, 5632, 7168
W_GB, W_MA, W_MB = 8704, 9216, 10240
COL_R, COL_K, COL_V, COL_GA, COL_MA, COL_MB, COL_GB, COL_A0 = 0, 1024, 2048, 3072, 4096, 5120, 6144, 6656
MAIN_WIDTH = 8192
GROUP_WIDTH = 3 * A_OUT_WIDTH

VMEM_LIMIT = 56 * 1024 * 1024


def _cparams(sem):
    return pltpu.CompilerParams(dimension_semantics=sem, vmem_limit_bytes=VMEM_LIMIT)


def _sigmoid(z):
    return 1.0 / (1.0 + jnp.exp(-z))


def _silu(z):
    return z * _sigmoid(z)


def _softplus(z):
    return jnp.maximum(z, 0.0) + jnp.log(1.0 + jnp.exp(-jnp.abs(z)))


def _dot(a, b):
    return jnp.dot(a, b, preferred_element_type=F32)


def _dot_nt(a, b):
    return lax.dot_general(a, b, (((1,), (1,)), ((), ())), preferred_element_type=F32)


def _split2(x):
    hi = x.astype(BF16)
    lo = (x - hi.astype(F32)).astype(BF16)
    return hi, lo


def _segsum64(x, ones_bd, split):
    n = x.shape[0]
    xs = jnp.concatenate([x[:, MXU_DIM * g:MXU_DIM * (g + 1)] for g in range(N_COLGROUPS)], axis=0)
    if split:
        hi, lo = _split2(xs)
        s = _dot(hi, ones_bd) + _dot(lo, ones_bd)
    else:
        s = _dot(xs.astype(BF16), ones_bd)
    return jnp.concatenate([s[n * g:n * (g + 1)] for g in range(N_COLGROUPS)], axis=1)


def _mod_kernel(c_ref, w_ref, b_ref, o_ref):
    s = _silu(c_ref[...])
    o_ref[0] = jnp.dot(s, w_ref[0], preferred_element_type=F32,
                       precision=lax.Precision.HIGHEST) + b_ref[0]


def _adaln_mod(c, ada_w, ada_b):
    L = ada_w.shape[0]
    B = c.shape[0]
    c8 = jnp.pad(c, ((0, 8 - B), (0, 0)))
    nj = 3
    return pl.pallas_call(
        _mod_kernel,
        grid=(L, nj),
        in_specs=[pl.BlockSpec((8, D_MODEL), lambda l, j: (0, 0)),
                  pl.BlockSpec((1, D_MODEL, D_MODEL), lambda l, j: (l, 0, j)),
                  pl.BlockSpec((1, 1, D_MODEL), lambda l, j: (l, 0, j))],
        out_specs=pl.BlockSpec((1, 8, D_MODEL), lambda l, j: (l, 0, j)),
        out_shape=jax.ShapeDtypeStruct((L, 8, 3 * D_MODEL), F32),
        compiler_params=_cparams(("parallel", "parallel")),
        name="adaln_mod",
    )(c8, ada_w, ada_b.reshape(L, 1, 3 * D_MODEL))


def _t5_bucket(dist):
    max_exact = NUM_BUCKETS // 2
    safe = np.maximum(dist, 1).astype(np.float32)
    large = max_exact + (np.log(safe / max_exact) / math.log(MAX_DISTANCE / max_exact)
                         * (NUM_BUCKETS - max_exact)).astype(np.int32)
    large = np.minimum(large, NUM_BUCKETS - 1)
    return np.where(dist < max_exact, dist, large).astype(np.int32)


def _bias_kernel(tab_ref, bucket_ref, o_ref):
    h = pl.program_id(0)
    bk = bucket_ref[0]
    acc = jnp.zeros(bk.shape, F32)
    for b in range(NUM_BUCKETS):
        acc = jnp.where(bk == b, tab_ref[h * NUM_BUCKETS + b], acc)
    o_ref[0] = acc


def _rel_bias(rel_bias):
    n_heads = rel_bias.shape[1]
    qi = np.arange(BLK)[:, None]
    ki = np.arange(2 * BLK)[None, :]
    delta = np.maximum(qi + BLK - ki, 0)
    buckets = np.stack([_t5_bucket(delta * d) for d in DILATIONS]).astype(np.int32)
    table = rel_bias.T.reshape(-1)
    return pl.pallas_call(
        _bias_kernel,
        grid=(n_heads,),
        in_specs=[pl.BlockSpec(memory_space=pltpu.SMEM),
                  pl.BlockSpec((1, BLK, 2 * BLK), lambda h: (h // HEADS_PER_GROUP, 0, 0))],
        out_specs=pl.BlockSpec((1, BLK, 2 * BLK), lambda h: (h, 0, 0)),
        out_shape=jax.ShapeDtypeStruct((n_heads, BLK, 2 * BLK), F32),
        compiler_params=_cparams(("parallel",)),
        name="rel_bias",
    )(table, jnp.asarray(buckets))


def _proj_kernel(x_ref, mod_ref, g_ref, w_ref, proj_ref, h_ref):
    @pl.when(pl.program_id(2) == 0)
    def _():
        x = x_ref[0]
        ms = jnp.mean(x * x, axis=-1, keepdims=True)
        y = x * lax.rsqrt(ms + RMS_EPS) * g_ref[...]
        shift = mod_ref[0, :, 0:D_MODEL]
        scale = mod_ref[0, :, D_MODEL:2 * D_MODEL]
        h_ref[0] = (y * (1.0 + scale) + shift).astype(BF16)

    proj_ref[0] = _dot(h_ref[0], w_ref[...]).astype(BF16)


def _norm_proj(x, mod_l, norm_g, w_main, tm=1024, tn=2048):
    B, S, D = x.shape
    N = w_main.shape[1]
    return pl.pallas_call(
        _proj_kernel,
        grid=(B, S // tm, N // tn),
        in_specs=[pl.BlockSpec((1, tm, D), lambda b, i, j: (b, i, 0)),
                  pl.BlockSpec((1, 1, 3 * D), lambda b, i, j: (b, 0, 0)),
                  pl.BlockSpec((1, D), lambda b, i, j: (0, 0)),
                  pl.BlockSpec((D, tn), lambda b, i, j: (0, j))],
        out_specs=[pl.BlockSpec((1, tm, tn), lambda b, i, j: (b, i, j)),
                   pl.BlockSpec((1, tm, D), lambda b, i, j: (b, i, 0))],
        out_shape=[jax.ShapeDtypeStruct((B, S, N), BF16),
                   jax.ShapeDtypeStruct((B, S, D), BF16)],
        compiler_params=_cparams(("parallel", "parallel", "arbitrary")),
        name="norm_proj",
    )(x, mod_l, norm_g.reshape(1, D), w_main)


GATHER_ROWS = 256


def _group_proj_kernel(h_ref, w_ref, o_ref, stage_a, stage_b, *, d):
    tm, tn = h_ref.shape[1], w_ref.shape[1]
    per_res = GATHER_ROWS // d
    for ck in range(tm // GATHER_ROWS):
        stage = stage_b if ck % 2 else stage_a
        r0 = ck * GATHER_ROWS
        res = _dot(h_ref[0, r0:r0 + GATHER_ROWS, :], w_ref[...])
        for c in range(tn // LANES):
            stage[c] = res[:, LANES * c:LANES * (c + 1)]
        for r in range(d):
            for c in range(tn // LANES):
                o_ref[0, r, ck * per_res:(ck + 1) * per_res, LANES * c:LANES * (c + 1)] = (
                    stage[c, pl.ds(r, per_res, stride=d), :].astype(BF16))


def _group_proj(h, w_group, d, tm=1024, tn=GROUP_WIDTH):
    B, S, D = h.shape
    N = w_group.shape[1]
    return pl.pallas_call(
        functools.partial(_group_proj_kernel, d=d),
        grid=(B, S // tm, N // tn),
        in_specs=[pl.BlockSpec((1, tm, D), lambda b, i, j: (b, i, 0)),
                  pl.BlockSpec((D, tn), lambda b, i, j: (0, j))],
        out_specs=pl.BlockSpec((1, d, tm // d, tn), lambda b, i, j: (b, 0, i, j)),
        out_shape=jax.ShapeDtypeStruct((B, d, S // d, N), BF16),
        scratch_shapes=[pltpu.VMEM((tn // LANES, GATHER_ROWS, LANES), F32)] * 2,
        compiler_params=_cparams(("parallel", "parallel", "parallel")),
        name=f"group_proj_d{d}",
    )(h, w_group)


PV_MU_R, PV_MU_K, PV_MU_V, PV_MU_W, PV_MU_A, PV_W0, PV_A0, PV_KK, PV_KA, PV_MU_VRES, PV_V0 = range(11)
PV_ROWS = 16
PREV_ROWS = 16


def _shift_rows(t, prev_last):
    rolled = pltpu.roll(t, 1, axis=0)
    row = lax.broadcasted_iota(jnp.int32, t.shape, 0)
    return jnp.where(row == 0, prev_last, rolled)


def _rprep_kernel(*refs, has_vres):
    if has_vres:
        (h_ref, hp_ref, pr_ref, prp_ref, pk_ref, pkp_ref, pvv_ref, pvp_ref, vf_ref, pvec_ref,
         w1_ref, w2_ref, a1_ref, a2_ref, v1_ref, v2_ref, ones_ref, tril_ref,
         r_out, cum_out, k_out, v_out, a_out, b_out) = refs
    else:
        (h_ref, hp_ref, pr_ref, prp_ref, pk_ref, pkp_ref, pvv_ref, pvp_ref, pvec_ref,
         w1_ref, w2_ref, a1_ref, a2_ref, ones_ref, tril_ref,
         r_out, cum_out, k_out, v_out, a_out, b_out) = refs

    not_first = (pl.program_id(1) > 0).astype(F32)

    def prm(i):
        return pvec_ref[i:i + 1, :]

    def shifted(cur_ref, prev_ref):
        t = cur_ref[0].astype(F32)
        last = prev_ref[0, PREV_ROWS - 1:PREV_ROWS, :].astype(F32)
        return t, _shift_rows(t, last * not_first)

    def lerp(t, ts, mu):
        return t + (ts - t) * mu

    h, hs = shifted(h_ref, hp_ref)
    pr, prs = shifted(pr_ref, prp_ref)
    pk, pks = shifted(pk_ref, pkp_ref)
    pv, pvs = shifted(pvv_ref, pvp_ref)

    r = lerp(pr, prs, prm(PV_MU_R))
    k = lerp(pk, pks, prm(PV_MU_K))
    v = lerp(pv, pvs, prm(PV_MU_V))
    xw = lerp(h, hs, prm(PV_MU_W)).astype(BF16)
    xa = lerp(h, hs, prm(PV_MU_A)).astype(BF16)

    zw = prm(PV_W0) + _dot(jnp.tanh(_dot(xw, w1_ref[...])).astype(BF16), w2_ref[...])
    w = -_softplus(-zw) - 0.5
    lw = -jnp.exp(w)
    hi = lw.astype(BF16)
    rest = lw - hi.astype(F32)
    mid = rest.astype(BF16)
    lo = (rest - mid.astype(F32)).astype(BF16)
    tril = tril_ref[...]
    for i in range(lw.shape[0] // MXU_DIM):
        blk = slice(MXU_DIM * i, MXU_DIM * (i + 1))
        cum_out[0, blk, :] = _dot(tril, hi[blk]) + _dot(tril, mid[blk]) + _dot(tril, lo[blk])
    a = _sigmoid(prm(PV_A0) + _dot(_dot(xa, a1_ref[...]).astype(BF16), a2_ref[...]))
    if has_vres:
        xv = lerp(h, hs, prm(PV_MU_VRES)).astype(BF16)
        mix = _sigmoid(prm(PV_V0) + _dot(_dot(xv, v1_ref[...]).astype(BF16), v2_ref[...]))
        v = v + (vf_ref[0] - v) * mix

    kk = k * prm(PV_KK)
    ss = _segsum64(kk * kk, ones_ref[...], split=False)
    kk = kk * lax.rsqrt(jnp.maximum(ss, 1e-24))
    r_out[0] = r
    k_out[0] = k * (1.0 + (a - 1.0) * prm(PV_KA))
    v_out[0] = v
    a_out[0] = -kk
    b_out[0] = kk * a


def _rwkv_prep(h, proj, v_first, pvec, lora, ones_bd, tr=512):
    B, S, D = h.shape
    has_vres = v_first is not None
    rpb = tr // PREV_ROWS
    t = np.arange(MXU_DIM)
    tril_bd =jnp.asarray((t[None, :] <= t[:, None]) & (t[None, :] // CHUNK == t[:, None] // CHUNK), BF16)

    def cur(c):
        return pl.BlockSpec((1, tr, R_WIDTH), lambda b, i: (b, i, c))

    def prev(c):
        return pl.BlockSpec((1, PREV_ROWS, R_WIDTH), lambda b, i: (b, jnp.maximum(i * rpb - 1, 0), c))

    def full(arr):
        return pl.BlockSpec(arr.shape, lambda b, i: (0,) * arr.ndim)

    in_specs = [cur(0), prev(0)]
    args = [h, h]
    for c in (COL_R, COL_K, COL_V):
        in_specs += [cur(c // R_WIDTH), prev(c // R_WIDTH)]
        args += [proj, proj]
    if has_vres:
        in_specs.append(cur(0))
        args.append(v_first)
    in_specs.append(full(pvec))
    args.append(pvec)
    for wgt in lora:
        in_specs.append(full(wgt))
        args.append(wgt)
    for const in (ones_bd, tril_bd):
        in_specs.append(full(const))
        args.append(const)
    out = jax.ShapeDtypeStruct((B, S, R_WIDTH), F32)
    return pl.pallas_call(
        functools.partial(_rprep_kernel, has_vres=has_vres),
        grid=(B, S // tr),
        in_specs=in_specs,
        out_specs=[cur(0)] * 6,
        out_shape=[out] * 6,
        compiler_params=_cparams(("parallel", "parallel")),
        name="rwkv_prep",
    )(*args)


def _scan_kernel(r_ref, cum_ref, k_ref, v_ref, a_ref, b_ref, ga_ref, vec_ref, ones_ref,
                 y_ref, s_ref, *, nb, tt):
    C = CHUNK

    @pl.when(pl.program_id(0) == 0)
    def _():
        s_ref[...] = jnp.zeros(s_ref.shape, F32)

    row = lax.broadcasted_iota(jnp.int32, (C, MXU_DIM), 0)
    lane = lax.broadcasted_iota(jnp.int32, (C, MXU_DIM), 1)
    col = lane & (HEAD_DIM - 1)
    lhead = lane >> 6
    strict = col < row
    incl = col <= row
    eye = (col == row).astype(F32)
    head_masks = [lhead == hh for hh in range(HEADS_PER_TILE)]

    def bdrows(x):
        return jnp.concatenate([jnp.where(m, x, 0.0) for m in head_masks], axis=0).astype(BF16)

    def diag_blocks(full):
        acc = jnp.where(head_masks[0], full[0:C], 0.0)
        for hh in range(1, HEADS_PER_TILE):
            acc = acc + jnp.where(head_masks[hh], full[C * hh:C * (hh + 1)], 0.0)
        return acc

    row_full = lax.broadcasted_iota(jnp.int32, (C, R_WIDTH), 0)
    ones_bd = ones_ref[...]
    r_k = vec_ref[0:1, :]
    ln_g = vec_ref[1:2, :]
    ln_b = vec_ref[2:3, :]

    def body(it, carry):
        rows = [pl.ds(pl.multiple_of((it * SCAN_CHUNKS + ck) * C, C), C) for ck in range(SCAN_CHUNKS)]
        pre = {}
        for ck in range(SCAN_CHUNKS):
            for bi in range(nb):
                cum = cum_ref[bi, rows[ck], :]
                r = r_ref[bi, rows[ck], :]
                k = k_ref[bi, rows[ck], :]
                v = v_ref[bi, rows[ck], :]
                a = a_ref[bi, rows[ck], :]
                b = b_ref[bi, rows[ck], :]
                total = cum[C - 1:C, :]
                p_in = jnp.exp(cum)
                p_inv = jnp.exp(-cum)
                p_rest = jnp.exp(total - cum)
                p_before = jnp.where(row_full == 0, 1.0, pltpu.roll(p_in, 1, axis=0))
                pre[ck, bi] = dict(r=r, k=k, v=v, a_t=a * p_before, r_t=r * p_in, b_t=b * p_inv,
                                   k_t=k * p_inv, bp=b * p_rest, kp=k * p_rest, p_all=jnp.exp(total))

        chains = [(bi, g) for bi in range(nb) for g in range(N_COLGROUPS)]
        insts = [(ck, bi, g) for ck in range(SCAN_CHUNKS) for bi, g in chains]

        def part(name, ck, bi, g):
            return pre[ck, bi][name][:, MXU_DIM * g:MXU_DIM * (g + 1)]

        res = [_dot_nt(jnp.concatenate([part("a_t", *i), part("r_t", *i)], axis=0).astype(BF16),
                       jnp.concatenate([bdrows(part("b_t", *i)), bdrows(part("k_t", *i))], axis=0))
               for i in insts]
        a_ab = [jnp.where(strict, x[0:C, 0:MXU_DIM], 0.0) for x in res]
        a_ak = [jnp.where(strict, x[0:C, MXU_DIM:], 0.0) for x in res]
        a_rb = [jnp.where(incl, x[C:, 0:MXU_DIM], 0.0).astype(BF16) for x in res]
        a_rk = [jnp.where(incl, x[C:, MXU_DIM:], 0.0) for x in res]

        pw = [_dot(x.astype(BF16), bdrows(x)) for x in a_ab]
        tinv = [eye + x for x in a_ab]
        for _ in range(4):
            both = [_dot(jnp.concatenate([p, t], axis=0).astype(BF16), bdrows(p)) for p, t in zip(pw, tinv)]
            tinv = [t + x[C:] for t, x in zip(tinv, both)]
            pw = [x[0:C] for x in both]
        tinv = [t + _dot(t.astype(BF16), bdrows(p)) for p, t in zip(pw, tinv)]
        tax = [_dot(t.astype(BF16), jnp.concatenate([bdrows(part("a_t", *i)), bdrows(x)], axis=1))
               for t, x, i in zip(tinv, a_ak, insts)]
        from_v = [_dot(jnp.concatenate([x[:, MXU_DIM:], ark], axis=0).astype(BF16), bdrows(part("v", *i)))
                  for x, ark, i in zip(tax, a_rk, insts)]

        st = [s_ref[bi * N_COLGROUPS + g] for bi, g in chains]
        y = {}
        for ck in range(SCAN_CHUNKS):
            sel = range(ck * len(chains), (ck + 1) * len(chains))
            from_state = [_dot_nt(jnp.concatenate([tax[n][:, 0:MXU_DIM], part("r_t", *insts[n])],
                                                  axis=0).astype(BF16), bdrows(s))
                          for n, s in zip(sel, st)]
            u = [x[0:C] + from_v[n][0:C] for x, n in zip(from_state, sel)]
            for x, n, uu in zip(from_state, sel, u):
                y[insts[n]] = x[C:] + from_v[n][C:] + _dot(a_rb[n], bdrows(uu))
            upd = [_dot(jnp.concatenate([uu, part("v", *insts[n])], axis=0).T.astype(BF16),
                        jnp.concatenate([part("bp", *insts[n]), part("kp", *insts[n])], axis=0).astype(BF16))
                   for uu, n in zip(u, sel)]
            st = [s_old * part("p_all", *insts[n]) + diag_blocks(x) for s_old, x, n in zip(st, upd, sel)]
        for (bi, g), s_new in zip(chains, st):
            s_ref[bi * N_COLGROUPS + g] = s_new

        for ck in range(SCAN_CHUNKS):
            for bi in range(nb):
                p = pre[ck, bi]
                yc = jnp.concatenate([y[ck, bi, g] for g in range(N_COLGROUPS)], axis=1)
                mean = _segsum64(yc, ones_bd, split=True) * (1.0 / HEAD_DIM)
                yd = yc - mean
                var = _segsum64(yd * yd, ones_bd, split=False) * (1.0 / HEAD_DIM)
                yn = yd * lax.rsqrt(var + GN_EPS) * ln_g + ln_b
                bonus = _segsum64(p["r"] * p["k"] * r_k, ones_bd, split=False) * p["v"]
                y_ref[bi, rows[ck], :] = (yn + bonus) * _silu(ga_ref[bi, rows[ck], :].astype(F32))
        return carry

    n_iter = tt // (C * SCAN_CHUNKS)
    if n_iter == 1:
        body(0, 0)
    else:
        lax.fori_loop(0, n_iter, body, 0)


def _rwkv_scan(r, cum, k, v, a, b, proj, vec, ones_bd, tt=256):
    B, S, W = r.shape
    spec = pl.BlockSpec((B, tt, W), lambda t: (0, t, 0))

    def full(arr):
        return pl.BlockSpec(arr.shape, lambda t: (0,) * arr.ndim)

    return pl.pallas_call(
        functools.partial(_scan_kernel, nb=B, tt=tt),
        grid=(S // tt,),
        in_specs=[spec] * 6 + [pl.BlockSpec((B, tt, W), lambda t: (0, t, COL_GA // W)),
                               full(vec), full(ones_bd)],
        out_specs=spec,
        out_shape=jax.ShapeDtypeStruct((B, S, W), F32),
        scratch_shapes=[pltpu.VMEM((B * N_COLGROUPS, HEAD_DIM, MXU_DIM), F32)],
        compiler_params=_cparams(("arbitrary",)),
        name="rwkv_scan",
    )(r, cum, k, v, a, b, proj, vec, ones_bd)


ATT_TILE = 2048
ATT_UNROLL = (3, 4, 4)


def _attn_kernel(*refs, tiles_per_seq):
    q_refs = refs[0:3]
    k_refs = refs[3:6]
    v_refs = refs[6:9]
    kp_refs = refs[9:12]
    vp_refs = refs[12:15]
    gb_ref, bias_ref, y_ref = refs[15:18]
    o_refs = refs[18:21]
    l_refs = refs[21:24]

    is_first = (pl.program_id(1) % tiles_per_seq) == 0
    prev_limit = jnp.where(is_first, BLK, 0)
    qi = lax.broadcasted_iota(jnp.int32, (2 * BLK, 2 * BLK), 0) & (BLK - 1)
    ki = lax.broadcasted_iota(jnp.int32, (2 * BLK, 2 * BLK), 1)
    delta = qi + BLK - ki
    band = (delta >= 0) & (delta <= BLK)
    head0 = lax.broadcasted_iota(jnp.int32, (BLK, LANES), 1) < HEAD_DIM
    scale = 1.0 / math.sqrt(HEAD_DIM)

    def process(g, d, blocks, from_prev):
        span = BLK * d

        def ds(start, size):
            return pl.ds(start, size) if d == 1 else pl.ds(start, size, stride=d)

        bias2 = bias_ref[g, 0].reshape(2 * BLK, 2 * BLK)
        zero = jnp.zeros((BLK, LANES), BF16)
        bases, q2s, kws, vws = [], [], [], []
        for sub, res in blocks:
            base = res if from_prev else sub * span + res
            row0 = 0 if from_prev else pl.multiple_of(sub * BLK, BLK)
            q = q_refs[g][0, res, pl.ds(row0, BLK), :] * scale
            q2s.append(jnp.concatenate([jnp.where(head0, q, zero), jnp.where(head0, zero, q)], axis=0))
            if from_prev:
                kw = jnp.concatenate([kp_refs[g][0, res], k_refs[g][0, res, 0:BLK, :]], axis=0)
                vw = jnp.concatenate([vp_refs[g][0, res], v_refs[g][0, res, 0:BLK, :]], axis=0)
            else:
                window = pl.ds(pl.multiple_of((sub - 1) * BLK, BLK), 2 * BLK)
                kw = k_refs[g][0, res, window, :]
                vw = v_refs[g][0, res, window, :]
            bases.append(base)
            kws.append(kw)
            vws.append(vw)
        logits =[_dot_nt(q2, kw) + bias2 for q2, kw in zip(q2s, kws)]
        logits = [jnp.where(band, x, NEG_INF) for x in logits]
        if from_prev:
            logits = [jnp.where(ki < prev_limit, NEG_INF, x) for x in logits]
        ms = [jnp.max(x, axis=-1, keepdims=True) for x in logits]
        ps = [jnp.exp(x - m) for x, m in zip(logits, ms)]
        dens = [jnp.sum(p, axis=-1, keepdims=True) for p in ps]
        pvs = [_dot(p.astype(BF16), vw) for p, vw in zip(ps, vws)]
        for base, pv, m, den in zip(bases, pvs, ms, dens):
            o = pv / den
            lse = m + jnp.log(den)
            o_refs[g][ds(base, BLK), :] = jnp.where(head0, o[0:BLK], o[BLK:])
            l_refs[g][ds(base, BLK), :] = jnp.where(head0, lse[0:BLK], lse[BLK:])

    for g, d in enumerate(DILATIONS):
        unroll = ATT_UNROLL[g]
        shift = int(math.log2(d))
        n_sub = ATT_TILE // (BLK * d)
        if d <= unroll:
            process(g, d, [(0, res) for res in range(d)], True)
        else:
            def first_body(it, carry, g=g, d=d, unroll=unroll):
                process(g, d, [(0, it * unroll + u) for u in range(unroll)], True)
                return carry
            lax.fori_loop(0, d // unroll, first_body, 0)
        n_rest = (n_sub - 1) * d
        if n_rest:
            def rest_body(it, carry, g=g, d=d, unroll=unroll, shift=shift):
                blks = [it * unroll + u for u in range(unroll)]
                process(g, d, [(1 + (b >> shift), b & (d - 1)) for b in blks], False)
                return carry
            lax.fori_loop(0, n_rest // unroll, rest_body, 0)

    l0, l1, l2 = l_refs[0][...], l_refs[1][...], l_refs[2][...]
    m = jnp.maximum(jnp.maximum(l0, l1), l2)
    w0, w1, w2 = jnp.exp(l0 - m), jnp.exp(l1 - m), jnp.exp(l2 - m)
    y = (w0 * o_refs[0][...] + w1 * o_refs[1][...] + w2 * o_refs[2][...]) / (w0 + w1 + w2)
    y_ref[0] = y * _silu(gb_ref[0].astype(F32))


def _dilated_attention(main, groups, bias5):
    B, S, _ = main.shape
    n_pairs = HEADS_PER_GROUP // 2
    tiles_per_seq = S // ATT_TILE
    arrays = [main.reshape(B, 1, S, MAIN_WIDTH)] + list(groups)
    col_base = [COL_A0 // LANES, 0, 0]

    def cur(g, part):
        d = DILATIONS[g]
        c0 = col_base[g] + part * (A_OUT_WIDTH // LANES)
        return pl.BlockSpec((1, d, ATT_TILE // d, LANES),
                            lambda hp, t: (t // tiles_per_seq, 0, t % tiles_per_seq, c0 + hp))

    def prev(g, part):
        d = DILATIONS[g]
        c0 = col_base[g] + part * (A_OUT_WIDTH // LANES)
        rb = ATT_TILE // (BLK * d)
        return pl.BlockSpec((1, d, BLK, LANES),
                            lambda hp, t: (t // tiles_per_seq, 0,
                                           jnp.maximum((t % tiles_per_seq) * rb - 1, 0), c0 + hp))

    def tile(col0):
        return pl.BlockSpec((1, ATT_TILE, LANES),
                            lambda hp, t: (t // tiles_per_seq, t % tiles_per_seq, col0 // LANES + hp))

    in_specs = ([cur(g, 0) for g in range(N_GROUPS)] + [cur(g, 1) for g in range(N_GROUPS)]
                + [cur(g, 2) for g in range(N_GROUPS)]
                + [prev(g, 1) for g in range(N_GROUPS)] + [prev(g, 2) for g in range(N_GROUPS)]
                + [tile(COL_GB),
                   pl.BlockSpec((N_GROUPS, 1, 2, BLK, 2 * BLK), lambda hp, t: (0, hp, 0, 0, 0))])
    scratch = [pltpu.VMEM((ATT_TILE, LANES), F32)] * 6
    return pl.pallas_call(
        functools.partial(_attn_kernel, tiles_per_seq=tiles_per_seq),
        grid=(n_pairs, B * tiles_per_seq),
        in_specs=in_specs,
        out_specs=tile(0),
        out_shape=jax.ShapeDtypeStruct((B, S, A_OUT_WIDTH), F32),
        scratch_shapes=scratch,
        compiler_params=_cparams(("parallel", "parallel")),
        name="dilated_attn",
    )(*(arrays * 5), main, bias5)


def _merge_kernel(ya_ref, yb_ref, ma_ref, mb_ref, x_ref, mod_ref, wa_ref, wb_ref, wo_ref, fg_ref,
                  o_ref, *, final_norm):
    pa = _dot(ya_ref[0].astype(BF16), wa_ref[...])
    pb = _dot(yb_ref[0].astype(BF16), wb_ref[...])
    merged = _sigmoid(ma_ref[0].astype(F32)) * pa + _sigmoid(mb_ref[0].astype(F32)) * pb
    out = _dot(merged.astype(BF16), wo_ref[...])
    gate = mod_ref[0, :, 2 * D_MODEL:3 * D_MODEL]
    xn = x_ref[0] + gate * out
    if final_norm:
        ms = jnp.mean(xn * xn, axis=-1, keepdims=True)
        xn = xn * lax.rsqrt(ms + RMS_EPS) * fg_ref[...]
    o_ref[0] = xn


def _merge(ya, yb, proj, x, mod_l, wa, wb, wo, final_g, final_norm, tm=512):
    B, S, D = x.shape

    def rows(width, c):
        return pl.BlockSpec((1, tm, width), lambda b, i: (b, i, c))

    def full(arr):
        return pl.BlockSpec(arr.shape, lambda b, i: (0,) * arr.ndim)

    return pl.pallas_call(
        functools.partial(_merge_kernel, final_norm=final_norm),
        grid=(B, S // tm),
        in_specs=[rows(R_WIDTH, 0), rows(A_OUT_WIDTH, 0),
                  rows(D, COL_MA // D), rows(D, COL_MB // D), rows(D, 0),
                  pl.BlockSpec((1, 1, 3 * D), lambda b, i: (b, 0, 0)),
                  full(wa), full(wb), full(wo), full(final_g)],
        out_specs=rows(D, 0),
        out_shape=jax.ShapeDtypeStruct((B, S, D), F32),
        compiler_params=_cparams(("parallel", "parallel")),
        name="merge",
    )(ya, yb, proj, proj, x, mod_l, wa, wb, wo, final_g)


def _segment_ones():
    idx = np.arange(MXU_DIM)
    return jnp.asarray(idx[:, None] // HEAD_DIM == idx[None, :] // HEAD_DIM, BF16)


def kernel(x, c, norm_g, ada_w, ada_b, w_in, rwkv_mu_rkv, rwkv_mu_wa, rwkv_w0, rwkv_w1, rwkv_w2, rwkv_a0, rwkv_a1, rwkv_a2, rwkv_k_k, rwkv_k_a, rwkv_r_k, rwkv_ln_g, rwkv_ln_b, rwkv_mu_v, rwkv_v0, rwkv_v1, rwkv_v2, w_branch_a, w_branch_b, w_out, rel_bias, final_g):
    B, S, D = x.shape
    assert D == D_MODEL and S % ATT_TILE == 0 and w_in.shape[2] == PROJ_WIDTH
    ones_bd = _segment_ones()
    mod = _adaln_mod(c, ada_w, ada_b)
    bias = _rel_bias(rel_bias).reshape(N_GROUPS, HEADS_PER_GROUP // 2, 2, BLK, 2 * BLK)
    zeros_row = jnp.zeros((D,), F32)
    v_first = None
    for i in range(DEPTH):
        mod_l = mod[i, :B].reshape(B, 1, 3 * D)
        w = w_in[i]

        def cols(start, width, w=w):
            return w[:, start:start + width]

        def group_cols(g, w=w, cols=cols):
            return [cols(base + A_OUT_WIDTH * g, A_OUT_WIDTH) for base in (W_AQ, W_AK, W_AV)]

        w_main = jnp.concatenate(
            [cols(W_R, 4 * R_WIDTH), cols(W_MA, 2 * D_MODEL), cols(W_GB, A_OUT_WIDTH)] + group_cols(0),
            axis=1).astype(BF16)
        proj, h = _norm_proj(x, mod_l, norm_g[i], w_main)
        groups = [_group_proj(h, jnp.concatenate(group_cols(g), axis=1).astype(BF16), DILATIONS[g])
                  for g in range(1, N_GROUPS)]
        has_vres = i > 0
        pvec = jnp.stack(
            [rwkv_mu_rkv[i, 0], rwkv_mu_rkv[i, 1], rwkv_mu_rkv[i, 2], rwkv_mu_wa[i, 0], rwkv_mu_wa[i, 1],
             rwkv_w0[i], rwkv_a0[i], rwkv_k_k[i], rwkv_k_a[i],
             rwkv_mu_v[i - 1] if has_vres else zeros_row, rwkv_v0[i - 1] if has_vres else zeros_row]
            + [zeros_row] * (PV_ROWS - 11))
        lora = [rwkv_w1[i].astype(BF16), rwkv_w2[i].astype(BF16),
                rwkv_a1[i].astype(BF16), rwkv_a2[i].astype(BF16)]
        if has_vres:
            lora += [rwkv_v1[i - 1].astype(BF16), rwkv_v2[i - 1].astype(BF16)]
        r, cum, k, v, a, b = _rwkv_prep(h, proj, v_first, pvec, lora, ones_bd)
        if i == 0:
            v_first = v
        vec = jnp.stack([rwkv_r_k[i].reshape(-1), rwkv_ln_g[i], rwkv_ln_b[i]] + [zeros_row] * 5)
        y_a = _rwkv_scan(r, cum, k, v, a, b, proj, vec, ones_bd)
        y_b = _dilated_attention(proj, groups, bias)
        x = _merge(y_a, y_b, proj, x, mod_l, w_branch_a[i].astype(BF16), w_branch_b[i].astype(BF16),
                   w_out[i].astype(BF16), final_g.reshape(1, D), final_norm=(i == DEPTH - 1))
    return x
```

```python
import functools
import math

import numpy as np
import jax
import jax.numpy as jnp
from jax import lax
from jax.experimental import pallas as pl
from jax.experimental.pallas import tpu as pltpu

F32 = jnp.float32
BF16 = jnp.bfloat16

D_MODEL = 1024
DEPTH = 2
HEAD_DIM = 64
R_WIDTH = 1024
N_GROUPS = 3
HEADS_PER_GROUP = 8
DILATIONS = (1, 4, 16)
BLK = 128
A_QK_WIDTH = 1536
A_OUT_WIDTH = 512
NUM_BUCKETS = 32
MAX_DISTANCE = 2048
PROJ_WIDTH = 4 * R_WIDTH + 3 * A_QK_WIDTH + A_OUT_WIDTH + 2 * D_MODEL
RMS_EPS = 1e-6
GN_EPS = 64e-5
NEG_INF = -1e30
LOG2E = math.log2(math.e)

LANES = 128
MXU_DIM = 256
HEADS_PER_TILE = MXU_DIM // HEAD_DIM
N_COLGROUPS = R_WIDTH // MXU_DIM
CHUNK = 64
SCAN_CHUNKS = 2

W_R, W_K, W_V, W_GA = 0, 1024, 2048, 3072
W_AQ, W_AK, W_AV = 4096, 5632, 7168
W_GB, W_MA, W_MB = 8704, 9216, 10240
COL_R, COL_K, COL_V, COL_GA, COL_MA, COL_MB, COL_GB, COL_A0 = 0, 1024, 2048, 3072, 4096, 5120, 6144, 6656
MAIN_WIDTH = 8192
GROUP_WIDTH = 3 * A_OUT_WIDTH

VMEM_LIMIT = 56 * 1024 * 1024


def _cparams(sem):
    return pltpu.CompilerParams(dimension_semantics=sem, vmem_limit_bytes=VMEM_LIMIT)


def _sigmoid(z):
    return 1.0 / (1.0 + jnp.exp(-z))


def _silu(z):
    return z * _sigmoid(z)


def _softplus(z):
    return jnp.maximum(z, 0.0) + jnp.log(1.0 + jnp.exp(-jnp.abs(z)))


def _dot(a, b):
    return jnp.dot(a, b, preferred_element_type=F32)


def _dot_nt(a, b):
    return lax.dot_general(a, b, (((1,), (1,)), ((), ())), preferred_element_type=F32)


def _split2(x):
    hi = x.astype(BF16)
    lo = (x - hi.astype(F32)).astype(BF16)
    return hi, lo


def _segsum64(x, ones_bd, split):
    n = x.shape[0]
    xs = jnp.concatenate([x[:, MXU_DIM * g:MXU_DIM * (g + 1)] for g in range(N_COLGROUPS)], axis=0)
    if split:
        hi, lo = _split2(xs)
        s = _dot(hi, ones_bd) + _dot(lo, ones_bd)
    else:
        s = _dot(xs.astype(BF16), ones_bd)
    return jnp.concatenate([s[n * g:n * (g + 1)] for g in range(N_COLGROUPS)], axis=1)


def _mod_kernel(c_ref, w_ref, b_ref, o_ref):
    s = _silu(c_ref[...])
    o_ref[0] = jnp.dot(s, w_ref[0], preferred_element_type=F32,
                       precision=lax.Precision.HIGHEST) + b_ref[0]


def _adaln_mod(c, ada_w, ada_b):
    L = ada_w.shape[0]
    B = c.shape[0]
    c8 = jnp.pad(c, ((0, 8 - B), (0, 0)))
    nj = 3
    return pl.pallas_call(
        _mod_kernel,
        grid=(L, nj),
        in_specs=[pl.BlockSpec((8, D_MODEL), lambda l, j: (0, 0)),
                  pl.BlockSpec((1, D_MODEL, D_MODEL), lambda l, j: (l, 0, j)),
                  pl.BlockSpec((1, 1, D_MODEL), lambda l, j: (l, 0, j))],
        out_specs=pl.BlockSpec((1, 8, D_MODEL), lambda l, j: (l, 0, j)),
        out_shape=jax.ShapeDtypeStruct((L, 8, 3 * D_MODEL), F32),
        compiler_params=_cparams(("parallel", "parallel")),
        name="adaln_mod",
    )(c8, ada_w, ada_b.reshape(L, 1, 3 * D_MODEL))


def _t5_bucket(dist):
    max_exact = NUM_BUCKETS // 2
    safe = np.maximum(dist, 1).astype(np.float32)
    large = max_exact + (np.log(safe / max_exact) / math.log(MAX_DISTANCE / max_exact)
                         * (NUM_BUCKETS - max_exact)).astype(np.int32)
    large = np.minimum(large, NUM_BUCKETS - 1)
    return np.where(dist < max_exact, dist, large).astype(np.int32)


def _bias_kernel(tab_ref, bucket_ref, o_ref):
    h = pl.program_id(0)
    bk = bucket_ref[0]
    acc = jnp.zeros(bk.shape, F32)
    for b in range(NUM_BUCKETS):
        acc = jnp.where(bk == b, tab_ref[h * NUM_BUCKETS + b], acc)
    o_ref[0] = jnp.where(bk >= 0, acc * LOG2E, NEG_INF)


def _rel_bias(rel_bias):
    n_heads = rel_bias.shape[1]
    qi = np.arange(BLK)[:, None]
    ki = np.arange(2 * BLK)[None, :]
    delta = qi + BLK - ki
    band = (delta >= 0) & (delta <= BLK)
    buckets = np.stack([np.where(band, _t5_bucket(np.maximum(delta, 0) * d), -1)
                        for d in DILATIONS]).astype(np.int32)
    table = rel_bias.T.reshape(-1)
    return pl.pallas_call(
        _bias_kernel,
        grid=(n_heads,),
        in_specs=[pl.BlockSpec(memory_space=pltpu.SMEM),
                  pl.BlockSpec((1, BLK, 2 * BLK), lambda h: (h // HEADS_PER_GROUP, 0, 0))],
        out_specs=pl.BlockSpec((1, BLK, 2 * BLK), lambda h: (h, 0, 0)),
        out_shape=jax.ShapeDtypeStruct((n_heads, BLK, 2 * BLK), F32),
        compiler_params=_cparams(("parallel",)),
        name="rel_bias",
    )(table, jnp.asarray(buckets))


def _proj_kernel(x_ref, mod_ref, g_ref, w_ref, proj_ref, h_ref):
    @pl.when(pl.program_id(2) == 0)
    def _():
        x = x_ref[0]
        ms = jnp.mean(x * x, axis=-1, keepdims=True)
        y = x * lax.rsqrt(ms + RMS_EPS) * g_ref[...]
        shift = mod_ref[0, :, 0:D_MODEL]
        scale = mod_ref[0, :, D_MODEL:2 * D_MODEL]
        h_ref[0] = (y * (1.0 + scale) + shift).astype(BF16)

    proj_ref[0] = _dot(h_ref[0], w_ref[...]).astype(BF16)


def _norm_proj(x, mod_l, norm_g, w_main, tm=1024, tn=2048):
    B, S, D = x.shape
    N = w_main.shape[1]
    return pl.pallas_call(
        _proj_kernel,
        grid=(B, S // tm, N // tn),
        in_specs=[pl.BlockSpec((1, tm, D), lambda b, i, j: (b, i, 0)),
                  pl.BlockSpec((1, 1, 3 * D), lambda b, i, j: (b, 0, 0)),
                  pl.BlockSpec((1, D), lambda b, i, j: (0, 0)),
                  pl.BlockSpec((D, tn), lambda b, i, j: (0, j))],
        out_specs=[pl.BlockSpec((1, tm, tn), lambda b, i, j: (b, i, j)),
                   pl.BlockSpec((1, tm, D), lambda b, i, j: (b, i, 0))],
        out_shape=[jax.ShapeDtypeStruct((B, S, N), BF16),
                   jax.ShapeDtypeStruct((B, S, D), BF16)],
        compiler_params=_cparams(("parallel", "parallel", "arbitrary")),
        name="norm_proj",
    )(x, mod_l, norm_g.reshape(1, D), w_main)


GATHER_ROWS = 256


def _group_proj_kernel(h_ref, w_ref, o_ref, stage_a, stage_b, *, d):
    tm, tn = h_ref.shape[1], w_ref.shape[1]
    per_res = GATHER_ROWS // d
    for ck in range(tm // GATHER_ROWS):
        stage = stage_b if ck % 2 else stage_a
        r0 = ck * GATHER_ROWS
        res = _dot(h_ref[0, r0:r0 + GATHER_ROWS, :], w_ref[...])
        for c in range(tn // LANES):
            stage[c] = res[:, LANES * c:LANES * (c + 1)]
        for r in range(d):
            for c in range(tn // LANES):
                o_ref[0, r, ck * per_res:(ck + 1) * per_res, LANES * c:LANES * (c + 1)] = (
                    stage[c, pl.ds(r, per_res, stride=d), :].astype(BF16))


def _group_proj(h, w_group, d, tm=1024, tn=GROUP_WIDTH):
    B, S, D = h.shape
    N = w_group.shape[1]
    return pl.pallas_call(
        functools.partial(_group_proj_kernel, d=d),
        grid=(B, S // tm, N // tn),
        in_specs=[pl.BlockSpec((1, tm, D), lambda b, i, j: (b, i, 0)),
                  pl.BlockSpec((D, tn), lambda b, i, j: (0, j))],
        out_specs=pl.BlockSpec((1, d, tm // d, tn), lambda b, i, j: (b, 0, i, j)),
        out_shape=jax.ShapeDtypeStruct((B, d, S // d, N), BF16),
        scratch_shapes=[pltpu.VMEM((tn // LANES, GATHER_ROWS, LANES), F32)] * 2,
        compiler_params=_cparams(("parallel", "parallel", "parallel")),
        name=f"group_proj_d{d}",
    )(h, w_group)


PV_MU_R, PV_MU_K, PV_MU_V, PV_W0, PV_A0, PV_KK, PV_KA, PV_V0 = range(8)
LORA_LANES = 256


def _pack_lora(paths):
    d_model = paths[0][1].shape[0]
    keep = jnp.zeros((d_model, LORA_LANES), F32)
    shifted = jnp.zeros((d_model, LORA_LANES), F32)
    ups = []
    lane = 0
    for mu, down, up in paths:
        rank = down.shape[1]
        keep = keep.at[:, lane:lane + rank].set((1.0 - mu)[:, None] * down)
        shifted = shifted.at[:, lane:lane + rank].set(mu[:, None] * down)
        ups.append(jnp.zeros((LORA_LANES, up.shape[1]), F32).at[lane:lane + rank].set(up).astype(BF16))
        lane += rank
    return [jnp.concatenate([keep, shifted], axis=1).astype(BF16)] + ups
PV_ROWS = 16
PREV_ROWS = 16


def _shift_rows(t, prev_last):
    rolled = pltpu.roll(t, 1, axis=0)
    row = lax.broadcasted_iota(jnp.int32, t.shape, 0)
    return jnp.where(row == 0, prev_last, rolled)


def _rprep_kernel(*refs, has_vres):
    if has_vres:
        (h_ref, hp_ref, pr_ref, prp_ref, pk_ref, pkp_ref, pvv_ref, pvp_ref, vf_ref, pvec_ref,
         wd_ref, uw_ref, ua_ref, uv_ref, ones_ref, tril_ref,
         r_out, cum_out, k_out, v_out, a_out, b_out) = refs
    else:
        (h_ref, hp_ref, pr_ref, prp_ref, pk_ref, pkp_ref, pvv_ref, pvp_ref, pvec_ref,
         wd_ref, uw_ref, ua_ref, ones_ref, tril_ref,
         r_out, cum_out, k_out, v_out, a_out, b_out) = refs

    not_first = (pl.program_id(1) > 0).astype(F32)

    def prm(i):
        return pvec_ref[i:i + 1, :]

    def lerp_shift(cur_ref, prev_ref, mu):
        t = cur_ref[0].astype(F32)
        last = prev_ref[0, PREV_ROWS - 1:PREV_ROWS, :].astype(F32)
        return t + (_shift_rows(t, last * not_first) - t) * mu

    r = lerp_shift(pr_ref, prp_ref, prm(PV_MU_R))
    k = lerp_shift(pk_ref, pkp_ref, prm(PV_MU_K))
    v = lerp_shift(pvv_ref, pvp_ref, prm(PV_MU_V))

    wd = wd_ref[...]
    z2 = _dot(h_ref[0], wd)
    z_prev = _dot(hp_ref[0], wd[:, LORA_LANES:])[PREV_ROWS - 1:PREV_ROWS, :]
    z = z2[:, 0:LORA_LANES] + _shift_rows(z2[:, LORA_LANES:], z_prev * not_first)
    zb = z.astype(BF16)

    zw = prm(PV_W0) + _dot(jnp.tanh(z).astype(BF16), uw_ref[...])
    w = -_softplus(-zw) - 0.5
    lw = -jnp.exp(w)
    hi = lw.astype(BF16)
    rest = lw - hi.astype(F32)
    mid = rest.astype(BF16)
    lo = (rest - mid.astype(F32)).astype(BF16)
    tril = tril_ref[...]
    for i in range(lw.shape[0] // MXU_DIM):
        blk = slice(MXU_DIM * i, MXU_DIM * (i + 1))
        cum_out[0, blk, :] = _dot(tril, hi[blk]) + _dot(tril, mid[blk]) + _dot(tril, lo[blk])
    a = _sigmoid(prm(PV_A0) + _dot(zb, ua_ref[...]))
    if has_vres:
        mix = _sigmoid(prm(PV_V0) + _dot(zb, uv_ref[...]))
        v = v + (vf_ref[0] - v) * mix

    kk = k * prm(PV_KK)
    ss = _segsum64(kk * kk, ones_ref[...], split=False)
    kk = kk * lax.rsqrt(jnp.maximum(ss, 1e-24))
    r_out[0] = r
    k_out[0] = k * (1.0 + (a - 1.0) * prm(PV_KA))
    v_out[0] = v
    a_out[0] = -kk
    b_out[0] = kk * a


def _rwkv_prep(h, proj, v_first, pvec, lora, ones_bd, tr=512):
    B, S, D = h.shape
    has_vres = v_first is not None
    rpb = tr // PREV_ROWS
    t = np.arange(MXU_DIM)
    tril_bd =jnp.asarray((t[None, :] <= t[:, None]) & (t[None, :] // CHUNK == t[:, None] // CHUNK), BF16)

    def cur(c):
        return pl.BlockSpec((1, tr, R_WIDTH), lambda b, i: (b, i, c))

    def prev(c):
        return pl.BlockSpec((1, PREV_ROWS, R_WIDTH), lambda b, i: (b, jnp.maximum(i * rpb - 1, 0), c))

    def full(arr):
        return pl.BlockSpec(arr.shape, lambda b, i: (0,) * arr.ndim)

    in_specs = [cur(0), prev(0)]
    args = [h, h]
    for c in (COL_R, COL_K, COL_V):
        in_specs += [cur(c // R_WIDTH), prev(c // R_WIDTH)]
        args += [proj, proj]
    if has_vres:
        in_specs.append(cur(0))
        args.append(v_first)
    in_specs.append(full(pvec))
    args.append(pvec)
    for wgt in lora:
        in_specs.append(full(wgt))
        args.append(wgt)
    for const in (ones_bd, tril_bd):
        in_specs.append(full(const))
        args.append(const)
    out = jax.ShapeDtypeStruct((B, S, R_WIDTH), F32)
    return pl.pallas_call(
        functools.partial(_rprep_kernel, has_vres=has_vres),
        grid=(B, S // tr),
        in_specs=in_specs,
        out_specs=[cur(0)] * 6,
        out_shape=[out] * 6,
        compiler_params=_cparams(("parallel", "parallel")),
        name="rwkv_prep",
    )(*args)


def _scan_kernel(r_ref, cum_ref, k_ref, v_ref, a_ref, b_ref, ga_ref, vec_ref, ones_ref,
                 y_ref, s_ref, *, nb, tt):
    C = CHUNK

    @pl.when(pl.program_id(0) == 0)
    def _():
        s_ref[...] = jnp.zeros(s_ref.shape, F32)

    row = lax.broadcasted_iota(jnp.int32, (C, MXU_DIM), 0)
    lane = lax.broadcasted_iota(jnp.int32, (C, MXU_DIM), 1)
    col = lane & (HEAD_DIM - 1)
    lhead = lane >> 6
    strict = col < row
    incl = col <= row
    eye = (col == row).astype(F32)
    head_masks = [lhead == hh for hh in range(HEADS_PER_TILE)]

    def bdrows(x):
        return jnp.concatenate([jnp.where(m, x, 0.0) for m in head_masks], axis=0).astype(BF16)

    def diag_blocks(full):
        acc = jnp.where(head_masks[0], full[0:C], 0.0)
        for hh in range(1, HEADS_PER_TILE):
            acc = acc + jnp.where(head_masks[hh], full[C * hh:C * (hh + 1)], 0.0)
        return acc

    row_full = lax.broadcasted_iota(jnp.int32, (C, R_WIDTH), 0)
    ones_bd = ones_ref[...]
    r_k = vec_ref[0:1, :]
    ln_g = vec_ref[1:2, :]
    ln_b = vec_ref[2:3, :]

    def body(it, carry):
        rows = [pl.ds(pl.multiple_of((it * SCAN_CHUNKS + ck) * C, C), C) for ck in range(SCAN_CHUNKS)]
        pre = {}
        for ck in range(SCAN_CHUNKS):
            for bi in range(nb):
                cum = cum_ref[bi, rows[ck], :]
                r = r_ref[bi, rows[ck], :]
                k = k_ref[bi, rows[ck], :]
                v = v_ref[bi, rows[ck], :]
                a = a_ref[bi, rows[ck], :]
                b = b_ref[bi, rows[ck], :]
                total = cum[C - 1:C, :]
                p_in = jnp.exp(cum)
                p_inv = jnp.exp(-cum)
                p_rest = jnp.exp(total - cum)
                p_before = jnp.where(row_full == 0, 1.0, pltpu.roll(p_in, 1, axis=0))
                pre[ck, bi] = dict(r=r, k=k, v=v, a_t=a * p_before, r_t=r * p_in, b_t=b * p_inv,
                                   k_t=k * p_inv, bp=b * p_rest, kp=k * p_rest, p_all=jnp.exp(total))

        chains = [(bi, g) for bi in range(nb) for g in range(N_COLGROUPS)]
        insts = [(ck, bi, g) for ck in range(SCAN_CHUNKS) for bi, g in chains]

        def part(name, ck, bi, g):
            return pre[ck, bi][name][:, MXU_DIM * g:MXU_DIM * (g + 1)]

        res = [_dot_nt(jnp.concatenate([part("a_t", *i), part("r_t", *i)], axis=0).astype(BF16),
                       jnp.concatenate([bdrows(part("b_t", *i)), bdrows(part("k_t", *i))], axis=0))
               for i in insts]
        a_ab = [jnp.where(strict, x[0:C, 0:MXU_DIM], 0.0) for x in res]
        a_ak = [jnp.where(strict, x[0:C, MXU_DIM:], 0.0) for x in res]
        a_rb = [jnp.where(incl, x[C:, 0:MXU_DIM], 0.0).astype(BF16) for x in res]
        a_rk = [jnp.where(incl, x[C:, MXU_DIM:], 0.0) for x in res]

        pw = [_dot(x.astype(BF16), bdrows(x)) for x in a_ab]
        tinv = [eye + x for x in a_ab]
        for _ in range(4):
            both = [_dot(jnp.concatenate([p, t], axis=0).astype(BF16), bdrows(p)) for p, t in zip(pw, tinv)]
            tinv = [t + x[C:] for t, x in zip(tinv, both)]
            pw = [x[0:C] for x in both]
        tinv = [t + _dot(t.astype(BF16), bdrows(p)) for p, t in zip(pw, tinv)]
        tax = [_dot(t.astype(BF16), jnp.concatenate([bdrows(part("a_t", *i)), bdrows(x)], axis=1))
               for t, x, i in zip(tinv, a_ak, insts)]
        from_v = [_dot(jnp.concatenate([x[:, MXU_DIM:], ark], axis=0).astype(BF16), bdrows(part("v", *i)))
                  for x, ark, i in zip(tax, a_rk, insts)]

        st = [s_ref[bi * N_COLGROUPS + g] for bi, g in chains]
        y = {}
        for ck in range(SCAN_CHUNKS):
            sel = range(ck * len(chains), (ck + 1) * len(chains))
            from_state = [_dot_nt(jnp.concatenate([tax[n][:, 0:MXU_DIM], part("r_t", *insts[n])],
                                                  axis=0).astype(BF16), bdrows(s))
                          for n, s in zip(sel, st)]
            u = [x[0:C] + from_v[n][0:C] for x, n in zip(from_state, sel)]
            for x, n, uu in zip(from_state, sel, u):
                y[insts[n]] = x[C:] + from_v[n][C:] + _dot(a_rb[n], bdrows(uu))
            upd = [_dot(jnp.concatenate([uu, part("v", *insts[n])], axis=0).T.astype(BF16),
                        jnp.concatenate([part("bp", *insts[n]), part("kp", *insts[n])], axis=0).astype(BF16))
                   for uu, n in zip(u, sel)]
            st = [s_old * part("p_all", *insts[n]) + diag_blocks(x) for s_old, x, n in zip(st, upd, sel)]
        for (bi, g), s_new in zip(chains, st):
            s_ref[bi * N_COLGROUPS + g] = s_new

        for ck in range(SCAN_CHUNKS):
            for bi in range(nb):
                p = pre[ck, bi]
                yc = jnp.concatenate([y[ck, bi, g] for g in range(N_COLGROUPS)], axis=1)
                mean = _segsum64(yc, ones_bd, split=True) * (1.0 / HEAD_DIM)
                yd = yc - mean
                var = _segsum64(yd * yd, ones_bd, split=False) * (1.0 / HEAD_DIM)
                yn = yd * lax.rsqrt(var + GN_EPS) * ln_g + ln_b
                bonus = _segsum64(p["r"] * p["k"] * r_k, ones_bd, split=False) * p["v"]
                y_ref[bi, rows[ck], :] = (yn + bonus) * _silu(ga_ref[bi, rows[ck], :].astype(F32))
        return carry

    n_iter = tt // (C * SCAN_CHUNKS)
    if n_iter == 1:
        body(0, 0)
    else:
        lax.fori_loop(0, n_iter, body, 0)


def _rwkv_scan(r, cum, k, v, a, b, proj, vec, ones_bd, tt=256):
    B, S, W = r.shape
    spec = pl.BlockSpec((B, tt, W), lambda t: (0, t, 0))

    def full(arr):
        return pl.BlockSpec(arr.shape, lambda t: (0,) * arr.ndim)

    return pl.pallas_call(
        functools.partial(_scan_kernel, nb=B, tt=tt),
        grid=(S // tt,),
        in_specs=[spec] * 6 + [pl.BlockSpec((B, tt, W), lambda t: (0, t, COL_GA // W)),
                               full(vec), full(ones_bd)],
        out_specs=spec,
        out_shape=jax.ShapeDtypeStruct((B, S, W), F32),
        scratch_shapes=[pltpu.VMEM((B * N_COLGROUPS, HEAD_DIM, MXU_DIM), F32)],
        compiler_params=_cparams(("arbitrary",)),
        name="rwkv_scan",
    )(r, cum, k, v, a, b, proj, vec, ones_bd)


ATT_TILE = 2048
ATT_UNROLL = (5, 6, 8)
ATT_STATIC_FIRST = 4


def _attn_kernel(*refs, tiles_per_seq):
    q_refs = refs[0:3]
    k_refs = refs[3:6]
    v_refs = refs[6:9]
    kp_refs = refs[9:12]
    vp_refs = refs[12:15]
    gb_ref, bias_ref, y_ref = refs[15:18]
    o_refs = refs[18:21]
    l_refs = refs[21:24]

    is_first = (pl.program_id(1) % tiles_per_seq) == 0
    prev_limit = jnp.where(is_first, BLK, 0)
    ki = lax.broadcasted_iota(jnp.int32, (2 * BLK, 2 * BLK), 1)
    head0 = lax.broadcasted_iota(jnp.int32, (BLK, LANES), 1) < HEAD_DIM
    ones_cols = jnp.ones((2 * BLK, LANES), BF16)

    def process(blocks):
        zero = jnp.zeros((BLK, LANES), BF16)
        q2s, kws, vws, bias2s, stores = [], [], [], [], []
        for g, sub, res, from_prev in blocks:
            d = DILATIONS[g]
            base = res if from_prev else sub * (BLK * d) + res
            row0 = 0 if from_prev else pl.multiple_of(sub * BLK, BLK)
            q = q_refs[g][0, res, pl.ds(row0, BLK), :]
            q2s.append(jnp.concatenate([jnp.where(head0, q, zero), jnp.where(head0, zero, q)], axis=0))
            if from_prev:
                kw = jnp.concatenate([kp_refs[g][0, res], k_refs[g][0, res, 0:BLK, :]], axis=0)
                vw = jnp.concatenate([vp_refs[g][0, res], v_refs[g][0, res, 0:BLK, :]], axis=0)
            else:
                window = pl.ds(pl.multiple_of((sub - 1) * BLK, BLK), 2 * BLK)
                kw = k_refs[g][0, res, window, :]
                vw = v_refs[g][0, res, window, :]
            kws.append(kw)
            vws.append(jnp.concatenate([vw, ones_cols], axis=1))
            bias2s.append(bias_ref[g, 0].reshape(2 * BLK, 2 * BLK))
            stores.append((g, pl.ds(base, BLK) if d == 1 else pl.ds(base, BLK, stride=d)))
        logits = [jnp.where(bias2 > 0.5 * NEG_INF, _dot_nt(q2, kw) + bias2, NEG_INF)
                  for q2, kw, bias2 in zip(q2s, kws, bias2s)]
        logits = [jnp.where(ki < prev_limit, NEG_INF, x) if blk[3] else x for x, blk in zip(logits, blocks)]
        ms = [jnp.max(x, axis=-1, keepdims=True) for x in logits]
        ps = [jnp.exp2(x - m).astype(BF16) for x, m in zip(logits, ms)]
        pvs = [_dot(p, vw) for p, vw in zip(ps, vws)]
        for (g, rows), pv, m in zip(stores, pvs, ms):
            num = jnp.where(head0, pv[0:BLK, 0:LANES], pv[BLK:, 0:LANES])
            den = jnp.where(head0, pv[0:BLK, LANES:], pv[BLK:, LANES:])
            o_refs[g][rows, :] = num / den
            l_refs[g][rows, :] = jnp.where(head0, m[0:BLK], m[BLK:]) + jnp.log2(den)

    process([(g, 0, res, True) for g, d in enumerate(DILATIONS) if d <= ATT_STATIC_FIRST for res in range(d)])
    for g, d in enumerate(DILATIONS):
        unroll = ATT_UNROLL[g]
        shift = int(math.log2(d))
        n_sub = ATT_TILE // (BLK * d)
        if d > ATT_STATIC_FIRST:
            def first_body(it, carry, g=g, unroll=unroll):
                process([(g, 0, it * unroll + u, True) for u in range(unroll)])
                return carry
            lax.fori_loop(0, d // unroll, first_body, 0)
        n_rest = (n_sub - 1) * d
        if n_rest:
            def rest_body(it, carry, g=g, d=d, unroll=unroll, shift=shift):
                blks = [it * unroll + u for u in range(unroll)]
                process([(g, 1 + (b >> shift), b & (d - 1), False) for b in blks])
                return carry
            lax.fori_loop(0, n_rest // unroll, rest_body, 0)

    l0, l1, l2 = l_refs[0][...], l_refs[1][...], l_refs[2][...]
    m = jnp.maximum(jnp.maximum(l0, l1), l2)
    w0, w1, w2 = jnp.exp2(l0 - m), jnp.exp2(l1 - m), jnp.exp2(l2 - m)
    y = (w0 * o_refs[0][...] + w1 * o_refs[1][...] + w2 * o_refs[2][...]) / (w0 + w1 + w2)
    y_ref[0] = y * _silu(gb_ref[0].astype(F32))


def _dilated_attention(main, groups, bias5):
    B, S, _ = main.shape
    n_pairs = HEADS_PER_GROUP // 2
    tiles_per_seq = S // ATT_TILE
    arrays = [main.reshape(B, 1, S, MAIN_WIDTH)] + list(groups)
    col_base = [COL_A0 // LANES, 0, 0]

    def cur(g, part):
        d = DILATIONS[g]
        c0 = col_base[g] + part * (A_OUT_WIDTH // LANES)
        return pl.BlockSpec((1, d, ATT_TILE // d, LANES),
                            lambda hp, t: (t // tiles_per_seq, 0, t % tiles_per_seq, c0 + hp))

    def prev(g, part):
        d = DILATIONS[g]
        c0 = col_base[g] + part * (A_OUT_WIDTH // LANES)
        rb = ATT_TILE // (BLK * d)
        return pl.BlockSpec((1, d, BLK, LANES),
                            lambda hp, t: (t // tiles_per_seq, 0,
                                           jnp.maximum((t % tiles_per_seq) * rb - 1, 0), c0 + hp))

    def tile(col0):
        return pl.BlockSpec((1, ATT_TILE, LANES),
                            lambda hp, t: (t // tiles_per_seq, t % tiles_per_seq, col0 // LANES + hp))

    in_specs = ([cur(g, 0) for g in range(N_GROUPS)] + [cur(g, 1) for g in range(N_GROUPS)]
                + [cur(g, 2) for g in range(N_GROUPS)]
                + [prev(g, 1) for g in range(N_GROUPS)] + [prev(g, 2) for g in range(N_GROUPS)]
                + [tile(COL_GB),
                   pl.BlockSpec((N_GROUPS, 1, 2, BLK, 2 * BLK), lambda hp, t: (0, hp, 0, 0, 0))])
    scratch = [pltpu.VMEM((ATT_TILE, LANES), F32)] * 6
    return pl.pallas_call(
        functools.partial(_attn_kernel, tiles_per_seq=tiles_per_seq),
        grid=(n_pairs, B * tiles_per_seq),
        in_specs=in_specs,
        out_specs=tile(0),
        out_shape=jax.ShapeDtypeStruct((B, S, A_OUT_WIDTH), F32),
        scratch_shapes=scratch,
        compiler_params=_cparams(("parallel", "parallel")),
        name="dilated_attn",
    )(*(arrays * 5), main, bias5)


def _merge_kernel(ya_ref, yb_ref, ma_ref, mb_ref, x_ref, mod_ref, wa_ref, wb_ref, wo_ref, fg_ref,
                  o_ref, *, final_norm):
    pa = _dot(ya_ref[0].astype(BF16), wa_ref[...])
    pb = _dot(yb_ref[0].astype(BF16), wb_ref[...])
    merged = _sigmoid(ma_ref[0].astype(F32)) * pa + _sigmoid(mb_ref[0].astype(F32)) * pb
    out = _dot(merged.astype(BF16), wo_ref[...])
    gate = mod_ref[0, :, 2 * D_MODEL:3 * D_MODEL]
    xn = x_ref[0] + gate * out
    if final_norm:
        ms = jnp.mean(xn * xn, axis=-1, keepdims=True)
        xn = xn * lax.rsqrt(ms + RMS_EPS) * fg_ref[...]
    o_ref[0] = xn


def _merge(ya, yb, proj, x, mod_l, wa, wb, wo, final_g, final_norm, tm=512):
    B, S, D = x.shape

    def rows(width, c):
        return pl.BlockSpec((1, tm, width), lambda b, i: (b, i, c))

    def full(arr):
        return pl.BlockSpec(arr.shape, lambda b, i: (0,) * arr.ndim)

    return pl.pallas_call(
        functools.partial(_merge_kernel, final_norm=final_norm),
        grid=(B, S // tm),
        in_specs=[rows(R_WIDTH, 0), rows(A_OUT_WIDTH, 0),
                  rows(D, COL_MA // D), rows(D, COL_MB // D), rows(D, 0),
                  pl.BlockSpec((1, 1, 3 * D), lambda b, i: (b, 0, 0)),
                  full(wa), full(wb), full(wo), full(final_g)],
        out_specs=rows(D, 0),
        out_shape=jax.ShapeDtypeStruct((B, S, D), F32),
        compiler_params=_cparams(("parallel", "parallel")),
        name="merge",
    )(ya, yb, proj, proj, x, mod_l, wa, wb, wo, final_g)


def _segment_ones():
    idx = np.arange(MXU_DIM)
    return jnp.asarray(idx[:, None] // HEAD_DIM == idx[None, :] // HEAD_DIM, BF16)


def kernel(x, c, norm_g, ada_w, ada_b, w_in, rwkv_mu_rkv, rwkv_mu_wa, rwkv_w0, rwkv_w1, rwkv_w2, rwkv_a0, rwkv_a1, rwkv_a2, rwkv_k_k, rwkv_k_a, rwkv_r_k, rwkv_ln_g, rwkv_ln_b, rwkv_mu_v, rwkv_v0, rwkv_v1, rwkv_v2, w_branch_a, w_branch_b, w_out, rel_bias, final_g):
    B, S, D = x.shape
    assert D == D_MODEL and S % ATT_TILE == 0 and w_in.shape[2] == PROJ_WIDTH
    ones_bd = _segment_ones()
    mod = _adaln_mod(c, ada_w, ada_b)
    bias = _rel_bias(rel_bias).reshape(N_GROUPS, HEADS_PER_GROUP // 2, 2, BLK, 2 * BLK)
    zeros_row = jnp.zeros((D,), F32)
    v_first = None
    for i in range(DEPTH):
        mod_l = mod[i, :B].reshape(B, 1, 3 * D)
        w = w_in[i]

        def cols(start, width, w=w):
            return w[:, start:start + width]

        def group_cols(g, w=w, cols=cols):
            return [cols(W_AQ + A_OUT_WIDTH * g, A_OUT_WIDTH) * (LOG2E / math.sqrt(HEAD_DIM)),
                    cols(W_AK + A_OUT_WIDTH * g, A_OUT_WIDTH), cols(W_AV + A_OUT_WIDTH * g, A_OUT_WIDTH)]

        w_main = jnp.concatenate(
            [cols(W_R, 4 * R_WIDTH), cols(W_MA, 2 * D_MODEL), cols(W_GB, A_OUT_WIDTH)] + group_cols(0),
            axis=1).astype(BF16)
        proj, h = _norm_proj(x, mod_l, norm_g[i], w_main)
        groups = [_group_proj(h, jnp.concatenate(group_cols(g), axis=1).astype(BF16), DILATIONS[g])
                  for g in range(1, N_GROUPS)]
        has_vres = i > 0
        pvec = jnp.stack(
            [rwkv_mu_rkv[i, 0], rwkv_mu_rkv[i, 1], rwkv_mu_rkv[i, 2],
             rwkv_w0[i], rwkv_a0[i], rwkv_k_k[i], rwkv_k_a[i], rwkv_v0[i - 1] if has_vres else zeros_row]
            + [zeros_row] * (PV_ROWS - 8))
        paths = [(rwkv_mu_wa[i, 0], rwkv_w1[i], rwkv_w2[i]), (rwkv_mu_wa[i, 1], rwkv_a1[i], rwkv_a2[i])]
        if has_vres:
            paths.append((rwkv_mu_v[i - 1], rwkv_v1[i - 1], rwkv_v2[i - 1]))
        r, cum, k, v, a, b = _rwkv_prep(h, proj, v_first, pvec, _pack_lora(paths), ones_bd)
        if i == 0:
            v_first = v
        vec = jnp.stack([rwkv_r_k[i].reshape(-1), rwkv_ln_g[i], rwkv_ln_b[i]] + [zeros_row] * 5)
        y_a = _rwkv_scan(r, cum, k, v, a, b, proj, vec, ones_bd)
        y_b = _dilated_attention(proj, groups, bias)
        x = _merge(y_a, y_b, proj, x, mod_l, w_branch_a[i].astype(BF16), w_branch_b[i].astype(BF16),
                   w_out[i].astype(BF16), final_g.reshape(1, D), final_norm=(i == DEPTH - 1))
    return x
```

```python
import functools
import math

import numpy as np
import jax
import jax.numpy as jnp
from jax import lax
from jax.experimental import pallas as pl
from jax.experimental.pallas import tpu as pltpu

F32 = jnp.float32
BF16 = jnp.bfloat16

D_MODEL = 1024
DEPTH = 2
HEAD_DIM = 64
R_WIDTH = 1024
N_GROUPS = 3
HEADS_PER_GROUP = 8
DILATIONS = (1, 4, 16)
BLK = 128
A_QK_WIDTH = 1536
A_OUT_WIDTH = 512
NUM_BUCKETS = 32
MAX_DISTANCE = 2048
PROJ_WIDTH = 4 * R_WIDTH + 3 * A_QK_WIDTH + A_OUT_WIDTH + 2 * D_MODEL
RMS_EPS = 1e-6
GN_EPS = 64e-5
NEG_INF = -1e30
LOG2E = math.log2(math.e)

LANES = 128
MXU_DIM = 256
HEADS_PER_TILE = MXU_DIM // HEAD_DIM
N_COLGROUPS = R_WIDTH // MXU_DIM
CHUNK = 64
SCAN_CHUNKS = 2

W_R, W_K, W_V, W_GA = 0, 1024, 2048, 3072
W_AQ, W_AK, W_AV = 4096, 5632, 7168
W_GB, W_MA, W_MB = 8704, 9216, 10240
COL_R, COL_K, COL_V, COL_GA, COL_MA, COL_MB, COL_GB, COL_A0 = 0, 1024, 2048, 3072, 4096, 5120, 6144, 6656
MAIN_WIDTH = 8192
GROUP_WIDTH = 3 * A_OUT_WIDTH

VMEM_LIMIT = 56 * 1024 * 1024
MOD_ROWS = 8


def _cparams(sem):
    return pltpu.CompilerParams(dimension_semantics=sem, vmem_limit_bytes=VMEM_LIMIT)


def _sigmoid(z):
    return 1.0 / (1.0 + jnp.exp(-z))


def _silu(z):
    return z * _sigmoid(z)


def _softplus(z):
    return jnp.maximum(z, 0.0) + jnp.log(1.0 + jnp.exp(-jnp.abs(z)))


def _dot(a, b):
    return jnp.dot(a, b, preferred_element_type=F32)


def _dot_nt(a, b):
    return lax.dot_general(a, b, (((1,), (1,)), ((), ())), preferred_element_type=F32)


def _split2(x):
    hi = x.astype(BF16)
    lo = (x - hi.astype(F32)).astype(BF16)
    return hi, lo


def _segsum64(x, ones_bd, split):
    n = x.shape[0]
    xs = jnp.concatenate([x[:, MXU_DIM * g:MXU_DIM * (g + 1)] for g in range(N_COLGROUPS)], axis=0)
    if split:
        hi, lo = _split2(xs)
        s = _dot(hi, ones_bd) + _dot(lo, ones_bd)
    else:
        s = _dot(xs.astype(BF16), ones_bd)
    return jnp.concatenate([s[n * g:n * (g + 1)] for g in range(N_COLGROUPS)], axis=1)


def _mod_kernel(c_ref, w_ref, b_ref, o_ref):
    s = _silu(c_ref[...])
    o_ref[0] = jnp.dot(s, w_ref[0], preferred_element_type=F32,
                       precision=lax.Precision.HIGHEST) + b_ref[0]


def _adaln_mod(c, ada_w, ada_b):
    L = ada_w.shape[0]
    B = c.shape[0]
    c_rows = jnp.pad(c, ((0, MOD_ROWS - B), (0, 0)))
    nj = 3
    return pl.pallas_call(
        _mod_kernel,
        grid=(L, nj),
        in_specs=[pl.BlockSpec((MOD_ROWS, D_MODEL), lambda l, j: (0, 0)),
                  pl.BlockSpec((1, D_MODEL, D_MODEL), lambda l, j: (l, 0, j)),
                  pl.BlockSpec((1, 1, D_MODEL), lambda l, j: (l, 0, j))],
        out_specs=pl.BlockSpec((1, MOD_ROWS, D_MODEL), lambda l, j: (l, 0, j)),
        out_shape=jax.ShapeDtypeStruct((L, MOD_ROWS, 3 * D_MODEL), F32),
        compiler_params=_cparams(("parallel", "parallel")),
        name="adaln_mod",
    )(c_rows, ada_w, ada_b.reshape(L, 1, 3 * D_MODEL))


def _t5_bucket(dist):
    max_exact = NUM_BUCKETS // 2
    safe = np.maximum(dist, 1).astype(np.float32)
    large = max_exact + (np.log(safe / max_exact) / math.log(MAX_DISTANCE / max_exact)
                         * (NUM_BUCKETS - max_exact)).astype(np.int32)
    large = np.minimum(large, NUM_BUCKETS - 1)
    return np.where(dist < max_exact, dist, large).astype(np.int32)


def _bias_kernel(tab_ref, bucket_ref, o_ref):
    h = pl.program_id(0)
    bk = bucket_ref[0]
    acc = jnp.zeros(bk.shape, F32)
    for b in range(NUM_BUCKETS):
        acc = jnp.where(bk == b, tab_ref[h * NUM_BUCKETS + b], acc)
    o_ref[0] = jnp.where(bk >= 0, acc * LOG2E, NEG_INF)


def _rel_bias(rel_bias):
    n_heads = rel_bias.shape[1]
    qi = np.arange(BLK)[:, None]
    ki = np.arange(2 * BLK)[None, :]
    delta = qi + BLK - ki
    band = (delta >= 0) & (delta <= BLK)
    buckets = np.stack([np.where(band, _t5_bucket(np.maximum(delta, 0) * d), -1)
                        for d in DILATIONS]).astype(np.int32)
    table = rel_bias.T.reshape(-1)
    return pl.pallas_call(
        _bias_kernel,
        grid=(n_heads,),
        in_specs=[pl.BlockSpec(memory_space=pltpu.SMEM),
                  pl.BlockSpec((1, BLK, 2 * BLK), lambda h: (h // HEADS_PER_GROUP, 0, 0))],
        out_specs=pl.BlockSpec((1, BLK, 2 * BLK), lambda h: (h, 0, 0)),
        out_shape=jax.ShapeDtypeStruct((n_heads, BLK, 2 * BLK), F32),
        compiler_params=_cparams(("parallel",)),
        name="rel_bias",
    )(table, jnp.asarray(buckets))


def _proj_kernel(x_ref, mod_ref, g_ref, w_ref, proj_ref, h_ref):
    @pl.when(pl.program_id(2) == 0)
    def _():
        x = x_ref[0]
        ms = jnp.mean(x * x, axis=-1, keepdims=True)
        y = x * lax.rsqrt(ms + RMS_EPS) * g_ref[...]
        shift = mod_ref[0, :, 0:D_MODEL]
        scale = mod_ref[0, :, D_MODEL:2 * D_MODEL]
        h_ref[0] = (y * (1.0 + scale) + shift).astype(BF16)

    proj_ref[0] = _dot(h_ref[0], w_ref[...]).astype(BF16)


def _layer_block(arr, layer):
    tail = (0,) * (arr.ndim - 1)
    return pl.BlockSpec((None,) + arr.shape[1:], lambda *_: (layer,) + tail)


def _norm_proj(x, mod, norm_g, w_main, layer, tm=1024, tn=2048):
    B, S, D = x.shape
    N = w_main.shape[2]
    return pl.pallas_call(
        _proj_kernel,
        grid=(B, S // tm, N // tn),
        in_specs=[pl.BlockSpec((1, tm, D), lambda b, i, j: (b, i, 0)),
                  pl.BlockSpec((1, 1, 3 * D), lambda b, i, j: (layer * MOD_ROWS + b, 0, 0)),
                  _layer_block(norm_g, layer),
                  pl.BlockSpec((None, D, tn), lambda b, i, j: (layer, 0, j))],
        out_specs=[pl.BlockSpec((1, tm, tn), lambda b, i, j: (b, i, j)),
                   pl.BlockSpec((1, tm, D), lambda b, i, j: (b, i, 0))],
        out_shape=[jax.ShapeDtypeStruct((B, S, N), BF16),
                   jax.ShapeDtypeStruct((B, S, D), BF16)],
        compiler_params=_cparams(("parallel", "parallel", "arbitrary")),
        name="norm_proj",
    )(x, mod, norm_g, w_main)


GATHER_ROWS = MXU_DIM


def _group_proj_kernel(h_ref, perm_ref, w_ref, o_ref, *, d):
    tm = h_ref.shape[1]
    per_res = GATHER_ROWS // d
    perm = perm_ref[...]
    for ck in range(tm // GATHER_ROWS):
        r0 = ck * GATHER_ROWS
        hp = _dot(perm, h_ref[0, r0:r0 + GATHER_ROWS, :]).astype(BF16)
        res = _dot(hp, w_ref[...]).astype(BF16)
        for r in range(d):
            o_ref[0, r, ck * per_res:(ck + 1) * per_res, :] = res[r * per_res:(r + 1) * per_res]


def _group_proj(h, w_group, d, layer, tm=1024, tn=GROUP_WIDTH):
    B, S, D = h.shape
    N = w_group.shape[2]
    dst = np.arange(GATHER_ROWS)
    src = (dst % (GATHER_ROWS // d)) * d + dst // (GATHER_ROWS // d)
    perm = jnp.asarray(src[:, None] == np.arange(GATHER_ROWS)[None, :], BF16)
    return pl.pallas_call(
        functools.partial(_group_proj_kernel, d=d),
        grid=(B, S // tm, N // tn),
        in_specs=[pl.BlockSpec((1, tm, D), lambda b, i, j: (b, i, 0)),
                  pl.BlockSpec((GATHER_ROWS, GATHER_ROWS), lambda b, i, j: (0, 0)),
                  pl.BlockSpec((None, D, tn), lambda b, i, j: (layer, 0, j))],
        out_specs=pl.BlockSpec((1, d, tm // d, tn), lambda b, i, j: (b, 0, i, j)),
        out_shape=jax.ShapeDtypeStruct((B, d, S // d, N), BF16),
        compiler_params=_cparams(("parallel", "parallel", "parallel")),
        name=f"group_proj_d{d}",
    )(h, perm, w_group)


PV_MU_R, PV_MU_K, PV_MU_V, PV_W0, PV_A0, PV_KK, PV_KA, PV_V0 = range(8)
LORA_LANES = 256


def _pack_lora(paths):
    n_layers, d_model, _ = paths[0][1].shape
    used = sum(down.shape[2] for _, down, _ in paths)
    pad = jnp.zeros((n_layers, d_model, LORA_LANES - used), F32)
    keep = jnp.concatenate([(1.0 - mu)[:, :, None] * down for mu, down, _ in paths] + [pad], axis=2)
    shifted = jnp.concatenate([mu[:, :, None] * down for mu, down, _ in paths] + [pad], axis=2)
    ups, lane = [], 0
    for _, down, up in paths:
        rank = down.shape[2]
        ups.append(jnp.pad(up, ((0, 0), (lane, LORA_LANES - lane - rank), (0, 0))).astype(BF16))
        lane += rank
    return [jnp.concatenate([keep, shifted], axis=2).astype(BF16)] + ups


def _first_layer_blank(arr):
    return jnp.concatenate([jnp.zeros((1,) + arr.shape[1:], arr.dtype), arr], axis=0)
PV_ROWS = 16
PREV_ROWS = 16


def _shift_rows(t, prev_last):
    rolled = pltpu.roll(t, 1, axis=0)
    row = lax.broadcasted_iota(jnp.int32, t.shape, 0)
    return jnp.where(row == 0, prev_last, rolled)


def _rprep_kernel(*refs, has_vres):
    if has_vres:
        (h_ref, hp_ref, pr_ref, prp_ref, pk_ref, pkp_ref, pvv_ref, pvp_ref, vf_ref, pvec_ref,
         wd_ref, uw_ref, ua_ref, uv_ref, ones_ref, tril_ref,
         r_out, cum_out, k_out, v_out, a_out, b_out) = refs
    else:
        (h_ref, hp_ref, pr_ref, prp_ref, pk_ref, pkp_ref, pvv_ref, pvp_ref, pvec_ref,
         wd_ref, uw_ref, ua_ref, ones_ref, tril_ref,
         r_out, cum_out, k_out, v_out, a_out, b_out) = refs

    not_first = (pl.program_id(1) > 0).astype(F32)

    def prm(i):
        return pvec_ref[i:i + 1, :]

    def lerp_shift(cur_ref, prev_ref, mu):
        t = cur_ref[0].astype(F32)
        last = prev_ref[0, PREV_ROWS - 1:PREV_ROWS, :].astype(F32)
        return t + (_shift_rows(t, last * not_first) - t) * mu

    r = lerp_shift(pr_ref, prp_ref, prm(PV_MU_R))
    k = lerp_shift(pk_ref, pkp_ref, prm(PV_MU_K))
    v = lerp_shift(pvv_ref, pvp_ref, prm(PV_MU_V))

    wd = wd_ref[...]
    z2 = _dot(h_ref[0], wd)
    z_prev = _dot(hp_ref[0], wd[:, LORA_LANES:])[PREV_ROWS - 1:PREV_ROWS, :]
    z = z2[:, 0:LORA_LANES] + _shift_rows(z2[:, LORA_LANES:], z_prev * not_first)
    zb = z.astype(BF16)

    zw = prm(PV_W0) + _dot(jnp.tanh(z).astype(BF16), uw_ref[...])
    w = -_softplus(-zw) - 0.5
    lw = -jnp.exp(w)
    hi = lw.astype(BF16)
    rest = lw - hi.astype(F32)
    mid = rest.astype(BF16)
    lo = (rest - mid.astype(F32)).astype(BF16)
    tril = tril_ref[...]
    for i in range(lw.shape[0] // MXU_DIM):
        blk = slice(MXU_DIM * i, MXU_DIM * (i + 1))
        cum_out[0, blk, :] = _dot(tril, hi[blk]) + _dot(tril, mid[blk]) + _dot(tril, lo[blk])
    a = _sigmoid(prm(PV_A0) + _dot(zb, ua_ref[...]))
    if has_vres:
        mix = _sigmoid(prm(PV_V0) + _dot(zb, uv_ref[...]))
        v = v + (vf_ref[0] - v) * mix

    kk = k * prm(PV_KK)
    ss = _segsum64(kk * kk, ones_ref[...], split=False)
    kk = kk * lax.rsqrt(jnp.maximum(ss, 1e-24))
    r_out[0] = r
    k_out[0] = k * (1.0 + (a - 1.0) * prm(PV_KA))
    v_out[0] = v
    a_out[0] = -kk
    b_out[0] = kk * a


def _rwkv_prep(h, proj, v_first, pvec, lora, ones_bd, layer, tr=512):
    B, S, D = h.shape
    has_vres = v_first is not None
    rpb = tr // PREV_ROWS
    t = np.arange(MXU_DIM)
    tril_bd =jnp.asarray((t[None, :] <= t[:, None]) & (t[None, :] // CHUNK == t[:, None] // CHUNK), BF16)

    def cur(c):
        return pl.BlockSpec((1, tr, R_WIDTH), lambda b, i: (b, i, c))

    def prev(c):
        return pl.BlockSpec((1, PREV_ROWS, R_WIDTH), lambda b, i: (b, jnp.maximum(i * rpb - 1, 0), c))

    def full(arr):
        return pl.BlockSpec(arr.shape, lambda b, i: (0,) * arr.ndim)

    in_specs = [cur(0), prev(0)]
    args = [h, h]
    for c in (COL_R, COL_K, COL_V):
        in_specs += [cur(c // R_WIDTH), prev(c // R_WIDTH)]
        args += [proj, proj]
    if has_vres:
        in_specs.append(cur(0))
        args.append(v_first)
    for per_layer in [pvec] + list(lora):
        in_specs.append(_layer_block(per_layer, layer))
        args.append(per_layer)
    for const in (ones_bd, tril_bd):
        in_specs.append(full(const))
        args.append(const)
    out = jax.ShapeDtypeStruct((B, S, R_WIDTH), F32)
    return pl.pallas_call(
        functools.partial(_rprep_kernel, has_vres=has_vres),
        grid=(B, S // tr),
        in_specs=in_specs,
        out_specs=[cur(0)] * 6,
        out_shape=[out] * 6,
        compiler_params=_cparams(("parallel", "parallel")),
        name="rwkv_prep",
    )(*args)


def _scan_kernel(r_ref, cum_ref, k_ref, v_ref, a_ref, b_ref, ga_ref, vec_ref, ones_ref,
                 y_ref, s_ref, *, nb, tt):
    C = CHUNK

    @pl.when(pl.program_id(0) == 0)
    def _():
        s_ref[...] = jnp.zeros(s_ref.shape, F32)

    row = lax.broadcasted_iota(jnp.int32, (C, MXU_DIM), 0)
    lane = lax.broadcasted_iota(jnp.int32, (C, MXU_DIM), 1)
    col = lane & (HEAD_DIM - 1)
    lhead = lane >> 6
    strict = col < row
    incl = col <= row
    eye = (col == row).astype(F32)
    head_masks = [lhead == hh for hh in range(HEADS_PER_TILE)]

    def bdrows(x):
        return jnp.concatenate([jnp.where(m, x, 0.0) for m in head_masks], axis=0).astype(BF16)

    def diag_blocks(full):
        acc = jnp.where(head_masks[0], full[0:C], 0.0)
        for hh in range(1, HEADS_PER_TILE):
            acc = acc + jnp.where(head_masks[hh], full[C * hh:C * (hh + 1)], 0.0)
        return acc

    row_full = lax.broadcasted_iota(jnp.int32, (C, R_WIDTH), 0)
    ones_bd = ones_ref[...]
    r_k = vec_ref[0:1, :]
    ln_g = vec_ref[1:2, :]
    ln_b = vec_ref[2:3, :]

    def body(it, carry):
        rows = [pl.ds(pl.multiple_of((it * SCAN_CHUNKS + ck) * C, C), C) for ck in range(SCAN_CHUNKS)]
        pre = {}
        for ck in range(SCAN_CHUNKS):
            for bi in range(nb):
                cum = cum_ref[bi, rows[ck], :]
                r = r_ref[bi, rows[ck], :]
                k = k_ref[bi, rows[ck], :]
                v = v_ref[bi, rows[ck], :]
                a = a_ref[bi, rows[ck], :]
                b = b_ref[bi, rows[ck], :]
                total = cum[C - 1:C, :]
                p_in = jnp.exp(cum)
                p_inv = jnp.exp(-cum)
                p_rest = jnp.exp(total - cum)
                p_before = jnp.where(row_full == 0, 1.0, pltpu.roll(p_in, 1, axis=0))
                pre[ck, bi] = dict(r=r, k=k, v=v, a_t=a * p_before, r_t=r * p_in, b_t=b * p_inv,
                                   k_t=k * p_inv, bp=b * p_rest, kp=k * p_rest, p_all=jnp.exp(total))

        chains = [(bi, g) for bi in range(nb) for g in range(N_COLGROUPS)]
        insts = [(ck, bi, g) for ck in range(SCAN_CHUNKS) for bi, g in chains]

        def part(name, ck, bi, g):
            return pre[ck, bi][name][:, MXU_DIM * g:MXU_DIM * (g + 1)]

        res = [_dot_nt(jnp.concatenate([part("a_t", *i), part("r_t", *i)], axis=0).astype(BF16),
                       jnp.concatenate([bdrows(part("b_t", *i)), bdrows(part("k_t", *i))], axis=0))
               for i in insts]
        a_ab = [jnp.where(strict, x[0:C, 0:MXU_DIM], 0.0) for x in res]
        a_ak = [jnp.where(strict, x[0:C, MXU_DIM:], 0.0) for x in res]
        a_rb = [jnp.where(incl, x[C:, 0:MXU_DIM], 0.0).astype(BF16) for x in res]
        a_rk = [jnp.where(incl, x[C:, MXU_DIM:], 0.0) for x in res]

        pw = [_dot(x.astype(BF16), bdrows(x)) for x in a_ab]
        tinv = [eye + x for x in a_ab]
        for _ in range(4):
            both = [_dot(jnp.concatenate([p, t], axis=0).astype(BF16), bdrows(p)) for p, t in zip(pw, tinv)]
            tinv = [t + x[C:] for t, x in zip(tinv, both)]
            pw = [x[0:C] for x in both]
        tinv = [t + _dot(t.astype(BF16), bdrows(p)) for p, t in zip(pw, tinv)]
        tax = [_dot(t.astype(BF16), jnp.concatenate([bdrows(part("a_t", *i)), bdrows(x)], axis=1))
               for t, x, i in zip(tinv, a_ak, insts)]
        from_v = [_dot(jnp.concatenate([x[:, MXU_DIM:], ark], axis=0).astype(BF16), bdrows(part("v", *i)))
                  for x, ark, i in zip(tax, a_rk, insts)]

        st = [s_ref[bi * N_COLGROUPS + g] for bi, g in chains]
        y = {}
        for ck in range(SCAN_CHUNKS):
            sel = range(ck * len(chains), (ck + 1) * len(chains))
            from_state = [_dot_nt(jnp.concatenate([tax[n][:, 0:MXU_DIM], part("r_t", *insts[n])],
                                                  axis=0).astype(BF16), bdrows(s))
                          for n, s in zip(sel, st)]
            u = [x[0:C] + from_v[n][0:C] for x, n in zip(from_state, sel)]
            for x, n, uu in zip(from_state, sel, u):
                y[insts[n]] = x[C:] + from_v[n][C:] + _dot(a_rb[n], bdrows(uu))
            upd = [_dot(jnp.concatenate([uu, part("v", *insts[n])], axis=0).T.astype(BF16),
                        jnp.concatenate([part("bp", *insts[n]), part("kp", *insts[n])], axis=0).astype(BF16))
                   for uu, n in zip(u, sel)]
            st = [s_old * part("p_all", *insts[n]) + diag_blocks(x) for s_old, x, n in zip(st, upd, sel)]
        for (bi, g), s_new in zip(chains, st):
            s_ref[bi * N_COLGROUPS + g] = s_new

        for ck in range(SCAN_CHUNKS):
            for bi in range(nb):
                p = pre[ck, bi]
                yc = jnp.concatenate([y[ck, bi, g] for g in range(N_COLGROUPS)], axis=1)
                mean = _segsum64(yc, ones_bd, split=True) * (1.0 / HEAD_DIM)
                yd = yc - mean
                var = _segsum64(yd * yd, ones_bd, split=False) * (1.0 / HEAD_DIM)
                yn = yd * lax.rsqrt(var + GN_EPS) * ln_g + ln_b
                bonus = _segsum64(p["r"] * p["k"] * r_k, ones_bd, split=False) * p["v"]
                y_ref[bi, rows[ck], :] = (yn + bonus) * _silu(ga_ref[bi, rows[ck], :].astype(F32))
        return carry

    n_iter = tt // (C * SCAN_CHUNKS)
    if n_iter == 1:
        body(0, 0)
    else:
        lax.fori_loop(0, n_iter, body, 0)


def _rwkv_scan(r, cum, k, v, a, b, proj, vec, ones_bd, layer, tt=256):
    B, S, W = r.shape
    spec = pl.BlockSpec((B, tt, W), lambda t: (0, t, 0))

    def full(arr):
        return pl.BlockSpec(arr.shape, lambda t: (0,) * arr.ndim)

    return pl.pallas_call(
        functools.partial(_scan_kernel, nb=B, tt=tt),
        grid=(S // tt,),
        in_specs=[spec] * 6 + [pl.BlockSpec((B, tt, W), lambda t: (0, t, COL_GA // W)),
                               _layer_block(vec, layer), full(ones_bd)],
        out_specs=spec,
        out_shape=jax.ShapeDtypeStruct((B, S, W), F32),
        scratch_shapes=[pltpu.VMEM((B * N_COLGROUPS, HEAD_DIM, MXU_DIM), F32)],
        compiler_params=_cparams(("arbitrary",)),
        name="rwkv_scan",
    )(r, cum, k, v, a, b, proj, vec, ones_bd)


ATT_TILE = 2048
ATT_UNROLL = (5, 6, 8)
ATT_STATIC_FIRST = 4


def _attn_kernel(*refs, tiles_per_seq):
    q_refs = refs[0:3]
    k_refs = refs[3:6]
    v_refs = refs[6:9]
    kp_refs = refs[9:12]
    vp_refs = refs[12:15]
    gb_ref, bias_ref, y_ref = refs[15:18]
    o_refs = refs[18:21]
    l_refs = refs[21:24]

    is_first = (pl.program_id(1) % tiles_per_seq) == 0
    prev_limit = jnp.where(is_first, BLK, 0)
    ki = lax.broadcasted_iota(jnp.int32, (2 * BLK, 2 * BLK), 1)
    head0 = lax.broadcasted_iota(jnp.int32, (BLK, LANES), 1) < HEAD_DIM
    ones_cols = jnp.ones((2 * BLK, LANES), BF16)

    def process(blocks):
        zero = jnp.zeros((BLK, LANES), BF16)
        q2s, kws, vws, bias2s, stores = [], [], [], [], []
        for g, sub, res, from_prev in blocks:
            d = DILATIONS[g]
            base = res if from_prev else sub * (BLK * d) + res
            row0 = 0 if from_prev else pl.multiple_of(sub * BLK, BLK)
            q = q_refs[g][0, res, pl.ds(row0, BLK), :]
            q2s.append(jnp.concatenate([jnp.where(head0, q, zero), jnp.where(head0, zero, q)], axis=0))
            if from_prev:
                kw = jnp.concatenate([kp_refs[g][0, res], k_refs[g][0, res, 0:BLK, :]], axis=0)
                vw = jnp.concatenate([vp_refs[g][0, res], v_refs[g][0, res, 0:BLK, :]], axis=0)
            else:
                window = pl.ds(pl.multiple_of((sub - 1) * BLK, BLK), 2 * BLK)
                kw = k_refs[g][0, res, window, :]
                vw = v_refs[g][0, res, window, :]
            kws.append(kw)
            vws.append(jnp.concatenate([vw, ones_cols], axis=1))
            bias2s.append(bias_ref[g, 0].reshape(2 * BLK, 2 * BLK))
            stores.append((g, pl.ds(base, BLK) if d == 1 else pl.ds(base, BLK, stride=d)))
        logits = [jnp.where(bias2 > 0.5 * NEG_INF, _dot_nt(q2, kw) + bias2, NEG_INF)
                  for q2, kw, bias2 in zip(q2s, kws, bias2s)]
        logits = [jnp.where(ki < prev_limit, NEG_INF, x) if blk[3] else x for x, blk in zip(logits, blocks)]
        ms = [jnp.max(x, axis=-1, keepdims=True) for x in logits]
        ps = [jnp.exp2(x - m).astype(BF16) for x, m in zip(logits, ms)]
        pvs = [_dot(p, vw) for p, vw in zip(ps, vws)]
        for (g, rows), pv, m in zip(stores, pvs, ms):
            num = jnp.where(head0, pv[0:BLK, 0:LANES], pv[BLK:, 0:LANES])
            den = jnp.where(head0, pv[0:BLK, LANES:], pv[BLK:, LANES:])
            o_refs[g][rows, :] = num / den
            l_refs[g][rows, :] = jnp.where(head0, m[0:BLK], m[BLK:]) + jnp.log2(den)

    process([(g, 0, res, True) for g, d in enumerate(DILATIONS) if d <= ATT_STATIC_FIRST for res in range(d)])
    for g, d in enumerate(DILATIONS):
        unroll = ATT_UNROLL[g]
        shift = int(math.log2(d))
        n_sub = ATT_TILE // (BLK * d)
        if d > ATT_STATIC_FIRST:
            def first_body(it, carry, g=g, unroll=unroll):
                process([(g, 0, it * unroll + u, True) for u in range(unroll)])
                return carry
            lax.fori_loop(0, d // unroll, first_body, 0)
        n_rest = (n_sub - 1) * d
        if n_rest:
            def rest_body(it, carry, g=g, d=d, unroll=unroll, shift=shift):
                blks = [it * unroll + u for u in range(unroll)]
                process([(g, 1 + (b >> shift), b & (d - 1), False) for b in blks])
                return carry
            lax.fori_loop(0, n_rest // unroll, rest_body, 0)

    l0, l1, l2 = l_refs[0][...], l_refs[1][...], l_refs[2][...]
    m = jnp.maximum(jnp.maximum(l0, l1), l2)
    w0, w1, w2 = jnp.exp2(l0 - m), jnp.exp2(l1 - m), jnp.exp2(l2 - m)
    y = (w0 * o_refs[0][...] + w1 * o_refs[1][...] + w2 * o_refs[2][...]) / (w0 + w1 + w2)
    y_ref[0] = y * _silu(gb_ref[0].astype(F32))


def _dilated_attention(main, groups, bias5):
    B, S, _ = main.shape
    n_pairs = HEADS_PER_GROUP // 2
    tiles_per_seq = S // ATT_TILE
    arrays = [main.reshape(B, 1, S, MAIN_WIDTH)] + list(groups)
    col_base = [COL_A0 // LANES, 0, 0]

    def cur(g, part):
        d = DILATIONS[g]
        c0 = col_base[g] + part * (A_OUT_WIDTH // LANES)
        return pl.BlockSpec((1, d, ATT_TILE // d, LANES),
                            lambda hp, t: (t // tiles_per_seq, 0, t % tiles_per_seq, c0 + hp))

    def prev(g, part):
        d = DILATIONS[g]
        c0 = col_base[g] + part * (A_OUT_WIDTH // LANES)
        rb = ATT_TILE // (BLK * d)
        return pl.BlockSpec((1, d, BLK, LANES),
                            lambda hp, t: (t // tiles_per_seq, 0,
                                           jnp.maximum((t % tiles_per_seq) * rb - 1, 0), c0 + hp))

    def tile(col0):
        return pl.BlockSpec((1, ATT_TILE, LANES),
                            lambda hp, t: (t // tiles_per_seq, t % tiles_per_seq, col0 // LANES + hp))

    in_specs = ([cur(g, 0) for g in range(N_GROUPS)] + [cur(g, 1) for g in range(N_GROUPS)]
                + [cur(g, 2) for g in range(N_GROUPS)]
                + [prev(g, 1) for g in range(N_GROUPS)] + [prev(g, 2) for g in range(N_GROUPS)]
                + [tile(COL_GB),
                   pl.BlockSpec((N_GROUPS, 1, 2, BLK, 2 * BLK), lambda hp, t: (0, hp, 0, 0, 0))])
    scratch = [pltpu.VMEM((ATT_TILE, LANES), F32)] * 6
    return pl.pallas_call(
        functools.partial(_attn_kernel, tiles_per_seq=tiles_per_seq),
        grid=(n_pairs, B * tiles_per_seq),
        in_specs=in_specs,
        out_specs=tile(0),
        out_shape=jax.ShapeDtypeStruct((B, S, A_OUT_WIDTH), F32),
        scratch_shapes=scratch,
        compiler_params=_cparams(("parallel", "parallel")),
        name="dilated_attn",
    )(*(arrays * 5), main, bias5)


def _merge_kernel(ya_ref, yb_ref, ma_ref, mb_ref, x_ref, mod_ref, wa_ref, wb_ref, wo_ref, fg_ref,
                  o_ref, *, final_norm):
    pa = _dot(ya_ref[0].astype(BF16), wa_ref[...])
    pb = _dot(yb_ref[0].astype(BF16), wb_ref[...])
    merged = _sigmoid(ma_ref[0].astype(F32)) * pa + _sigmoid(mb_ref[0].astype(F32)) * pb
    out = _dot(merged.astype(BF16), wo_ref[...])
    gate = mod_ref[0, :, 2 * D_MODEL:3 * D_MODEL]
    xn = x_ref[0] + gate * out
    if final_norm:
        ms = jnp.mean(xn * xn, axis=-1, keepdims=True)
        xn = xn * lax.rsqrt(ms + RMS_EPS) * fg_ref[...]
    o_ref[0] = xn


def _merge(ya, yb, proj, x, mod, wa, wb, wo, final_g, final_norm, layer, tm=512):
    B, S, D = x.shape

    def rows(width, c):
        return pl.BlockSpec((1, tm, width), lambda b, i: (b, i, c))

    def full(arr):
        return pl.BlockSpec(arr.shape, lambda b, i: (0,) * arr.ndim)

    return pl.pallas_call(
        functools.partial(_merge_kernel, final_norm=final_norm),
        grid=(B, S // tm),
        in_specs=[rows(R_WIDTH, 0), rows(A_OUT_WIDTH, 0),
                  rows(D, COL_MA // D), rows(D, COL_MB // D), rows(D, 0),
                  pl.BlockSpec((1, 1, 3 * D), lambda b, i: (layer * MOD_ROWS + b, 0, 0)),
                  _layer_block(wa, layer), _layer_block(wb, layer), _layer_block(wo, layer), full(final_g)],
        out_specs=rows(D, 0),
        out_shape=jax.ShapeDtypeStruct((B, S, D), F32),
        compiler_params=_cparams(("parallel", "parallel")),
        name="merge",
    )(ya, yb, proj, proj, x, mod, wa, wb, wo, final_g)


def _segment_ones():
    idx = np.arange(MXU_DIM)
    return jnp.asarray(idx[:, None] // HEAD_DIM == idx[None, :] // HEAD_DIM, BF16)


def kernel(x, c, norm_g, ada_w, ada_b, w_in, rwkv_mu_rkv, rwkv_mu_wa, rwkv_w0, rwkv_w1, rwkv_w2, rwkv_a0, rwkv_a1, rwkv_a2, rwkv_k_k, rwkv_k_a, rwkv_r_k, rwkv_ln_g, rwkv_ln_b, rwkv_mu_v, rwkv_v0, rwkv_v1, rwkv_v2, w_branch_a, w_branch_b, w_out, rel_bias, final_g):
    B, S, D = x.shape
    assert D == D_MODEL and S % ATT_TILE == 0 and w_in.shape[2] == PROJ_WIDTH
    ones_bd = _segment_ones()
    mod = _adaln_mod(c, ada_w, ada_b).reshape(DEPTH * MOD_ROWS, 1, 3 * D)
    bias = _rel_bias(rel_bias).reshape(N_GROUPS, HEADS_PER_GROUP // 2, 2, BLK, 2 * BLK)

    def cols(start, width):
        return w_in[:, :, start:start + width]

    def group_cols(g):
        return [cols(W_AQ + A_OUT_WIDTH * g, A_OUT_WIDTH) * (LOG2E / math.sqrt(HEAD_DIM)),
                cols(W_AK + A_OUT_WIDTH * g, A_OUT_WIDTH), cols(W_AV + A_OUT_WIDTH * g, A_OUT_WIDTH)]

    w_main = jnp.concatenate(
        [cols(W_R, 4 * R_WIDTH), cols(W_MA, 2 * D_MODEL), cols(W_GB, A_OUT_WIDTH)] + group_cols(0),
        axis=2).astype(BF16)
    w_groups = [jnp.concatenate(group_cols(g), axis=2).astype(BF16) for g in range(1, N_GROUPS)]
    zeros_rows = jnp.zeros((DEPTH, D), F32)
    pvec = jnp.stack([rwkv_mu_rkv[:, 0], rwkv_mu_rkv[:, 1], rwkv_mu_rkv[:, 2], rwkv_w0, rwkv_a0, rwkv_k_k,
                      rwkv_k_a, _first_layer_blank(rwkv_v0)] + [zeros_rows] * (PV_ROWS - 8), axis=1)
    lora = _pack_lora([(rwkv_mu_wa[:, 0], rwkv_w1, rwkv_w2), (rwkv_mu_wa[:, 1], rwkv_a1, rwkv_a2),
                       (_first_layer_blank(rwkv_mu_v), _first_layer_blank(rwkv_v1),
                        _first_layer_blank(rwkv_v2))])
    vec = jnp.stack([rwkv_r_k.reshape(DEPTH, -1), rwkv_ln_g, rwkv_ln_b] + [zeros_rows] * 5, axis=1)
    norm_g3 = norm_g.reshape(DEPTH, 1, D)
    wa, wb, wo = w_branch_a.astype(BF16), w_branch_b.astype(BF16), w_out.astype(BF16)

    v_first = None
    for i in range(DEPTH):
        proj, h = _norm_proj(x, mod, norm_g3, w_main, i)
        groups = [_group_proj(h, w_groups[g - 1], DILATIONS[g], i) for g in range(1, N_GROUPS)]
        r, cum, k, v, a, b = _rwkv_prep(h, proj, v_first, pvec, lora if i > 0 else lora[:3], ones_bd, i)
        if i == 0:
            v_first = v
        y_a = _rwkv_scan(r, cum, k, v, a, b, proj, vec, ones_bd, i)
        y_b = _dilated_attention(proj, groups, bias)
        x = _merge(y_a, y_b, proj, x, mod, wa, wb, wo, final_g.reshape(1, D),
                   final_norm=(i == DEPTH - 1), layer=i)
    return x
```

```python
import functools
import math

import numpy as np
import jax
import jax.numpy as jnp
from jax import lax
from jax.experimental import pallas as pl
from jax.experimental.pallas import tpu as pltpu

F32 = jnp.float32
BF16 = jnp.bfloat16

D_MODEL = 1024
DEPTH = 2
HEAD_DIM = 64
R_WIDTH = 1024
N_GROUPS = 3
HEADS_PER_GROUP = 8
DILATIONS = (1, 4, 16)
BLK = 128
A_QK_WIDTH = 1536
A_OUT_WIDTH = 512
NUM_BUCKETS = 32
MAX_DISTANCE = 2048
PROJ_WIDTH = 4 * R_WIDTH + 3 * A_QK_WIDTH + A_OUT_WIDTH + 2 * D_MODEL
RMS_EPS = 1e-6
GN_EPS = 64e-5
NEG_INF = -1e30
LOG2E = math.log2(math.e)

LANES = 128
MXU_DIM = 256
HEADS_PER_TILE = MXU_DIM // HEAD_DIM
N_COLGROUPS = R_WIDTH // MXU_DIM
CHUNK = 64
SCAN_CHUNKS = 2

W_R, W_K, W_V, W_GA = 0, 1024, 2048, 3072
W_AQ, W_AK, W_AV = 4096, 5632, 7168
W_GB, W_MA, W_MB = 8704, 9216, 10240
COL_R, COL_K, COL_V, COL_GA, COL_MA, COL_MB, COL_GB, COL_A0 = 0, 1024, 2048, 3072, 4096, 5120, 6144, 6656
MAIN_WIDTH = 8192
GROUP_WIDTH = 3 * A_OUT_WIDTH

VMEM_LIMIT = 56 * 1024 * 1024
MOD_ROWS = 8


def _cparams(sem):
    return pltpu.CompilerParams(dimension_semantics=sem, vmem_limit_bytes=VMEM_LIMIT)


def _sigmoid(z):
    return 1.0 / (1.0 + jnp.exp(-z))


def _silu(z):
    return z * _sigmoid(z)


def _softplus(z):
    return jnp.maximum(z, 0.0) + jnp.log(1.0 + jnp.exp(-jnp.abs(z)))


def _dot(a, b):
    return jnp.dot(a, b, preferred_element_type=F32)


def _dot_nt(a, b):
    return lax.dot_general(a, b, (((1,), (1,)), ((), ())), preferred_element_type=F32)


def _split2(x):
    hi = x.astype(BF16)
    lo = (x - hi.astype(F32)).astype(BF16)
    return hi, lo


def _segsum64(x, ones_bd, split):
    n = x.shape[0]
    xs = jnp.concatenate([x[:, MXU_DIM * g:MXU_DIM * (g + 1)] for g in range(N_COLGROUPS)], axis=0)
    if split:
        hi, lo = _split2(xs)
        s = _dot(hi, ones_bd) + _dot(lo, ones_bd)
    else:
        s = _dot(xs.astype(BF16), ones_bd)
    return jnp.concatenate([s[n * g:n * (g + 1)] for g in range(N_COLGROUPS)], axis=1)


def _mod_kernel(c_ref, w_ref, b_ref, o_ref):
    s = _silu(c_ref[...])
    o_ref[0] = jnp.dot(s, w_ref[0], preferred_element_type=F32,
                       precision=lax.Precision.HIGHEST) + b_ref[0]


def _adaln_mod(c, ada_w, ada_b):
    L = ada_w.shape[0]
    B = c.shape[0]
    c_rows = jnp.pad(c, ((0, MOD_ROWS - B), (0, 0)))
    nj = 3
    return pl.pallas_call(
        _mod_kernel,
        grid=(L, nj),
        in_specs=[pl.BlockSpec((MOD_ROWS, D_MODEL), lambda l, j: (0, 0)),
                  pl.BlockSpec((1, D_MODEL, D_MODEL), lambda l, j: (l, 0, j)),
                  pl.BlockSpec((1, 1, D_MODEL), lambda l, j: (l, 0, j))],
        out_specs=pl.BlockSpec((1, MOD_ROWS, D_MODEL), lambda l, j: (l, 0, j)),
        out_shape=jax.ShapeDtypeStruct((L, MOD_ROWS, 3 * D_MODEL), F32),
        compiler_params=_cparams(("parallel", "parallel")),
        name="adaln_mod",
    )(c_rows, ada_w, ada_b.reshape(L, 1, 3 * D_MODEL))


def _t5_bucket(dist):
    max_exact = NUM_BUCKETS // 2
    safe = np.maximum(dist, 1).astype(np.float32)
    large = max_exact + (np.log(safe / max_exact) / math.log(MAX_DISTANCE / max_exact)
                         * (NUM_BUCKETS - max_exact)).astype(np.int32)
    large = np.minimum(large, NUM_BUCKETS - 1)
    return np.where(dist < max_exact, dist, large).astype(np.int32)


def _bias_kernel(tab_ref, bucket_ref, o_ref):
    h = pl.program_id(0)
    bk = bucket_ref[0]
    acc = jnp.zeros(bk.shape, F32)
    for b in range(NUM_BUCKETS):
        acc = jnp.where(bk == b, tab_ref[h * NUM_BUCKETS + b], acc)
    o_ref[0] = jnp.where(bk >= 0, acc * LOG2E, NEG_INF)


def _rel_bias(rel_bias):
    n_heads = rel_bias.shape[1]
    qi = np.arange(BLK)[:, None]
    ki = np.arange(2 * BLK)[None, :]
    delta = qi + BLK - ki
    band = (delta >= 0) & (delta <= BLK)
    buckets = np.stack([np.where(band, _t5_bucket(np.maximum(delta, 0) * d), -1)
                        for d in DILATIONS]).astype(np.int32)
    table = rel_bias.T.reshape(-1)
    return pl.pallas_call(
        _bias_kernel,
        grid=(n_heads,),
        in_specs=[pl.BlockSpec(memory_space=pltpu.SMEM),
                  pl.BlockSpec((1, BLK, 2 * BLK), lambda h: (h // HEADS_PER_GROUP, 0, 0))],
        out_specs=pl.BlockSpec((1, BLK, 2 * BLK), lambda h: (h, 0, 0)),
        out_shape=jax.ShapeDtypeStruct((n_heads, BLK, 2 * BLK), F32),
        compiler_params=_cparams(("parallel",)),
        name="rel_bias",
    )(table, jnp.asarray(buckets))


def _proj_kernel(x_ref, mod_ref, g_ref, w_ref, proj_ref, h_ref):
    @pl.when(pl.program_id(2) == 0)
    def _():
        x = x_ref[0]
        ms = jnp.mean(x * x, axis=-1, keepdims=True)
        y = x * lax.rsqrt(ms + RMS_EPS) * g_ref[...]
        shift = mod_ref[0, :, 0:D_MODEL]
        scale = mod_ref[0, :, D_MODEL:2 * D_MODEL]
        h_ref[0] = (y * (1.0 + scale) + shift).astype(BF16)

    proj_ref[0] = _dot(h_ref[0], w_ref[...]).astype(BF16)


def _layer_block(arr, layer):
    tail = (0,) * (arr.ndim - 1)
    return pl.BlockSpec((None,) + arr.shape[1:], lambda *_: (layer,) + tail)


def _norm_proj(x, mod, norm_g, w_main, layer, tm=1024, tn=2048):
    B, S, D = x.shape
    N = w_main.shape[2]
    return pl.pallas_call(
        _proj_kernel,
        grid=(B, S // tm, N // tn),
        in_specs=[pl.BlockSpec((1, tm, D), lambda b, i, j: (b, i, 0)),
                  pl.BlockSpec((1, 1, 3 * D), lambda b, i, j: (layer * MOD_ROWS + b, 0, 0)),
                  _layer_block(norm_g, layer),
                  pl.BlockSpec((None, D, tn), lambda b, i, j: (layer, 0, j))],
        out_specs=[pl.BlockSpec((1, tm, tn), lambda b, i, j: (b, i, j)),
                   pl.BlockSpec((1, tm, D), lambda b, i, j: (b, i, 0))],
        out_shape=[jax.ShapeDtypeStruct((B, S, N), BF16),
                   jax.ShapeDtypeStruct((B, S, D), BF16)],
        compiler_params=_cparams(("parallel", "parallel", "arbitrary")),
        name="norm_proj",
    )(x, mod, norm_g, w_main)


GATHER_ROWS = MXU_DIM


def _group_proj_kernel(h_ref, perm_ref, w_ref, o_ref, *, d):
    tm = h_ref.shape[1]
    per_res = GATHER_ROWS // d
    perm = perm_ref[...]
    for ck in range(tm // GATHER_ROWS):
        r0 = ck * GATHER_ROWS
        hp = _dot(perm, h_ref[0, r0:r0 + GATHER_ROWS, :]).astype(BF16)
        res = _dot(hp, w_ref[...]).astype(BF16)
        for r in range(d):
            o_ref[0, r, ck * per_res:(ck + 1) * per_res, :] = res[r * per_res:(r + 1) * per_res]


def _group_proj(h, w_group, d, layer, tm=1024, tn=GROUP_WIDTH):
    B, S, D = h.shape
    N = w_group.shape[2]
    dst = np.arange(GATHER_ROWS)
    src = (dst % (GATHER_ROWS // d)) * d + dst // (GATHER_ROWS // d)
    perm = jnp.asarray(src[:, None] == np.arange(GATHER_ROWS)[None, :], BF16)
    return pl.pallas_call(
        functools.partial(_group_proj_kernel, d=d),
        grid=(B, S // tm, N // tn),
        in_specs=[pl.BlockSpec((1, tm, D), lambda b, i, j: (b, i, 0)),
                  pl.BlockSpec((GATHER_ROWS, GATHER_ROWS), lambda b, i, j: (0, 0)),
                  pl.BlockSpec((None, D, tn), lambda b, i, j: (layer, 0, j))],
        out_specs=pl.BlockSpec((1, d, tm // d, tn), lambda b, i, j: (b, 0, i, j)),
        out_shape=jax.ShapeDtypeStruct((B, d, S // d, N), BF16),
        compiler_params=_cparams(("parallel", "parallel", "parallel")),
        name=f"group_proj_d{d}",
    )(h, perm, w_group)


PV_MU_R, PV_MU_K, PV_MU_V, PV_W0, PV_A0, PV_KK, PV_KA, PV_V0 = range(8)
LORA_LANES = 256


def _pack_lora(paths):
    n_layers, d_model, _ = paths[0][1].shape
    used = sum(down.shape[2] for _, down, _ in paths)
    pad = jnp.zeros((n_layers, d_model, LORA_LANES - used), F32)
    keep = jnp.concatenate([(1.0 - mu)[:, :, None] * down for mu, down, _ in paths] + [pad], axis=2)
    shifted = jnp.concatenate([mu[:, :, None] * down for mu, down, _ in paths] + [pad], axis=2)
    ups, lane = [], 0
    for _, down, up in paths:
        rank = down.shape[2]
        ups.append(jnp.pad(up, ((0, 0), (lane, LORA_LANES - lane - rank), (0, 0))).astype(BF16))
        lane += rank
    return [jnp.concatenate([keep, shifted], axis=2).astype(BF16)] + ups


def _first_layer_blank(arr):
    return jnp.concatenate([jnp.zeros((1,) + arr.shape[1:], arr.dtype), arr], axis=0)
PV_ROWS = 16
PREV_ROWS = 16


def _shift_rows(t, prev_last):
    rolled = pltpu.roll(t, 1, axis=0)
    row = lax.broadcasted_iota(jnp.int32, t.shape, 0)
    return jnp.where(row == 0, prev_last, rolled)


def _rprep_kernel(*refs, has_vres):
    if has_vres:
        (h_ref, hp_ref, pr_ref, prp_ref, pk_ref, pkp_ref, pvv_ref, pvp_ref, vf_ref, pvec_ref,
         wd_ref, uw_ref, ua_ref, uv_ref, ones_ref, tril_ref,
         r_out, cum_out, k_out, v_out, a_out, b_out) = refs
    else:
        (h_ref, hp_ref, pr_ref, prp_ref, pk_ref, pkp_ref, pvv_ref, pvp_ref, pvec_ref,
         wd_ref, uw_ref, ua_ref, ones_ref, tril_ref,
         r_out, cum_out, k_out, v_out, a_out, b_out) = refs

    not_first = (pl.program_id(1) > 0).astype(F32)

    def prm(i):
        return pvec_ref[i:i + 1, :]

    def lerp_shift(cur_ref, prev_ref, mu):
        t = cur_ref[0].astype(F32)
        last = prev_ref[0, PREV_ROWS - 1:PREV_ROWS, :].astype(F32)
        return t + (_shift_rows(t, last * not_first) - t) * mu

    r = lerp_shift(pr_ref, prp_ref, prm(PV_MU_R))
    k = lerp_shift(pk_ref, pkp_ref, prm(PV_MU_K))
    v = lerp_shift(pvv_ref, pvp_ref, prm(PV_MU_V))

    wd = wd_ref[...]
    z2 = _dot(h_ref[0], wd)
    z_prev = _dot(hp_ref[0], wd[:, LORA_LANES:])[PREV_ROWS - 1:PREV_ROWS, :]
    z = z2[:, 0:LORA_LANES] + _shift_rows(z2[:, LORA_LANES:], z_prev * not_first)
    zb = z.astype(BF16)

    zw = prm(PV_W0) + _dot(jnp.tanh(z).astype(BF16), uw_ref[...])
    w = -_softplus(-zw) - 0.5
    lw = -jnp.exp(w)
    hi = lw.astype(BF16)
    rest = lw - hi.astype(F32)
    mid = rest.astype(BF16)
    lo = (rest - mid.astype(F32)).astype(BF16)
    tril = tril_ref[...]
    for i in range(lw.shape[0] // MXU_DIM):
        blk = slice(MXU_DIM * i, MXU_DIM * (i + 1))
        cum_out[0, blk, :] = _dot(tril, hi[blk]) + _dot(tril, mid[blk]) + _dot(tril, lo[blk])
    a = _sigmoid(prm(PV_A0) + _dot(zb, ua_ref[...]))
    if has_vres:
        mix = _sigmoid(prm(PV_V0) + _dot(zb, uv_ref[...]))
        v = v + (vf_ref[0] - v) * mix

    kk = k * prm(PV_KK)
    ss = _segsum64(kk * kk, ones_ref[...], split=False)
    kk = kk * lax.rsqrt(jnp.maximum(ss, 1e-24))
    r_out[0] = r
    k_out[0] = k * (1.0 + (a - 1.0) * prm(PV_KA))
    v_out[0] = v
    a_out[0] = -kk
    b_out[0] = kk * a


def _rwkv_prep(h, proj, v_first, pvec, lora, ones_bd, layer, tr=512):
    B, S, D = h.shape
    has_vres = v_first is not None
    rpb = tr // PREV_ROWS
    t = np.arange(MXU_DIM)
    tril_bd =jnp.asarray((t[None, :] <= t[:, None]) & (t[None, :] // CHUNK == t[:, None] // CHUNK), BF16)

    def cur(c):
        return pl.BlockSpec((1, tr, R_WIDTH), lambda b, i: (b, i, c))

    def prev(c):
        return pl.BlockSpec((1, PREV_ROWS, R_WIDTH), lambda b, i: (b, jnp.maximum(i * rpb - 1, 0), c))

    def full(arr):
        return pl.BlockSpec(arr.shape, lambda b, i: (0,) * arr.ndim)

    in_specs = [cur(0), prev(0)]
    args = [h, h]
    for c in (COL_R, COL_K, COL_V):
        in_specs += [cur(c // R_WIDTH), prev(c // R_WIDTH)]
        args += [proj, proj]
    if has_vres:
        in_specs.append(cur(0))
        args.append(v_first)
    for per_layer in [pvec] + list(lora):
        in_specs.append(_layer_block(per_layer, layer))
        args.append(per_layer)
    for const in (ones_bd, tril_bd):
        in_specs.append(full(const))
        args.append(const)
    out = jax.ShapeDtypeStruct((B, S, R_WIDTH), F32)
    return pl.pallas_call(
        functools.partial(_rprep_kernel, has_vres=has_vres),
        grid=(B, S // tr),
        in_specs=in_specs,
        out_specs=[cur(0)] * 6,
        out_shape=[out] * 6,
        compiler_params=_cparams(("parallel", "parallel")),
        name="rwkv_prep",
    )(*args)


def _scan_kernel(r_ref, cum_ref, k_ref, v_ref, a_ref, b_ref, ga_ref, vec_ref, ones_ref,
                 y_ref, s_ref, *, nb, tt):
    C = CHUNK

    @pl.when(pl.program_id(0) == 0)
    def _():
        s_ref[...] = jnp.zeros(s_ref.shape, F32)

    row = lax.broadcasted_iota(jnp.int32, (C, MXU_DIM), 0)
    lane = lax.broadcasted_iota(jnp.int32, (C, MXU_DIM), 1)
    col = lane & (HEAD_DIM - 1)
    lhead = lane >> 6
    strict = col < row
    incl = col <= row
    eye = (col == row).astype(F32)
    head_masks = [lhead == hh for hh in range(HEADS_PER_TILE)]

    def bdrows(x):
        return jnp.concatenate([jnp.where(m, x, 0.0) for m in head_masks], axis=0).astype(BF16)

    def diag_blocks(full):
        acc = jnp.where(head_masks[0], full[0:C], 0.0)
        for hh in range(1, HEADS_PER_TILE):
            acc = acc + jnp.where(head_masks[hh], full[C * hh:C * (hh + 1)], 0.0)
        return acc

    row_full = lax.broadcasted_iota(jnp.int32, (C, R_WIDTH), 0)
    ones_bd = ones_ref[...]
    r_k = vec_ref[0:1, :]
    ln_g = vec_ref[1:2, :]
    ln_b = vec_ref[2:3, :]

    chains = [(bi, g) for bi in range(nb) for g in range(N_COLGROUPS)]
    insts = [(ck, bi, g) for ck in range(SCAN_CHUNKS) for bi, g in chains]


    def load(gi):
        rows = [slice((gi * SCAN_CHUNKS + ck) * C, (gi * SCAN_CHUNKS + ck + 1) * C) for ck in range(SCAN_CHUNKS)]
        ops = {}
        for ck in range(SCAN_CHUNKS):
            for bi in range(nb):
                cum = cum_ref[bi, rows[ck], :]
                r = r_ref[bi, rows[ck], :]
                k = k_ref[bi, rows[ck], :]
                v = v_ref[bi, rows[ck], :]
                a = a_ref[bi, rows[ck], :]
                b = b_ref[bi, rows[ck], :]
                total = cum[C - 1:C, :]
                p_in = jnp.exp(cum)
                p_inv = jnp.exp(-cum)
                p_rest = jnp.exp(total - cum)
                p_before = jnp.where(row_full == 0, 1.0, pltpu.roll(p_in, 1, axis=0))
                ops[ck, bi] = dict(r=r, k=k, v=v, a_t=a * p_before, r_t=r * p_in, b_t=b * p_inv,
                                   k_t=k * p_inv, bp=b * p_rest, kp=k * p_rest, p_all=jnp.exp(total))
        return dict(rows=rows, ops=ops)

    def part(ctx, name, ck, bi, g):
        return ctx["ops"][ck, bi][name][:, MXU_DIM * g:MXU_DIM * (g + 1)]

    def independent(ctx):
        res = [_dot_nt(jnp.concatenate([part(ctx, "a_t", *i), part(ctx, "r_t", *i)], axis=0).astype(BF16),
                       jnp.concatenate([bdrows(part(ctx, "b_t", *i)), bdrows(part(ctx, "k_t", *i))], axis=0))
               for i in insts]
        yield
        a_ab = [jnp.where(strict, x[0:C, 0:MXU_DIM], 0.0) for x in res]
        a_ak = [jnp.where(strict, x[0:C, MXU_DIM:], 0.0) for x in res]
        ctx["a_rb"] = [jnp.where(incl, x[C:, 0:MXU_DIM], 0.0).astype(BF16) for x in res]
        a_rk = [jnp.where(incl, x[C:, MXU_DIM:], 0.0) for x in res]

        pw = [_dot(x.astype(BF16), bdrows(x)) for x in a_ab]
        tinv = [eye + x for x in a_ab]
        yield
        for _ in range(4):
            both = [_dot(jnp.concatenate([p, t], axis=0).astype(BF16), bdrows(p)) for p, t in zip(pw, tinv)]
            tinv = [t + x[C:] for t, x in zip(tinv, both)]
            pw = [x[0:C] for x in both]
            yield
        tinv = [t + _dot(t.astype(BF16), bdrows(p)) for p, t in zip(pw, tinv)]
        yield
        tax = [_dot(t.astype(BF16), jnp.concatenate([bdrows(part(ctx, "a_t", *i)), bdrows(x)], axis=1))
               for t, x, i in zip(tinv, a_ak, insts)]
        yield
        ctx["from_v"] = [_dot(jnp.concatenate([x[:, MXU_DIM:], ark], axis=0).astype(BF16),
                              bdrows(part(ctx, "v", *i))) for x, ark, i in zip(tax, a_rk, insts)]
        ctx["tax"] = tax
        yield

    def dependent(ctx, carried):
        tax, from_v, a_rb = ctx["tax"], ctx["from_v"], ctx["a_rb"]
        st = carried["st"]
        y = {}
        for ck in range(SCAN_CHUNKS):
            sel = range(ck * len(chains), (ck + 1) * len(chains))
            from_state = [_dot_nt(jnp.concatenate([tax[n][:, 0:MXU_DIM], part(ctx, "r_t", *insts[n])],
                                                  axis=0).astype(BF16), bdrows(s))
                          for n, s in zip(sel, st)]
            yield
            u = [x[0:C] + from_v[n][0:C] for x, n in zip(from_state, sel)]
            for x, n, uu in zip(from_state, sel, u):
                y[insts[n]] = x[C:] + from_v[n][C:] + _dot(a_rb[n], bdrows(uu))
            upd = [_dot(jnp.concatenate([uu, part(ctx, "v", *insts[n])], axis=0).T.astype(BF16),
                        jnp.concatenate([part(ctx, "bp", *insts[n]), part(ctx, "kp", *insts[n])],
                                        axis=0).astype(BF16))
                   for uu, n in zip(u, sel)]
            yield
            st = [s_old * part(ctx, "p_all", *insts[n]) + diag_blocks(x) for s_old, x, n in zip(st, upd, sel)]
        carried["st"] = st

        for ck in range(SCAN_CHUNKS):
            for bi in range(nb):
                p = ctx["ops"][ck, bi]
                yc = jnp.concatenate([y[ck, bi, g] for g in range(N_COLGROUPS)], axis=1)
                mean = _segsum64(yc, ones_bd, split=True) * (1.0 / HEAD_DIM)
                yield
                yd = yc - mean
                var = _segsum64(yd * yd, ones_bd, split=False) * (1.0 / HEAD_DIM)
                yn = yd * lax.rsqrt(var + GN_EPS) * ln_g + ln_b
                bonus = _segsum64(p["r"] * p["k"] * r_k, ones_bd, split=False) * p["v"]
                rows = ctx["rows"][ck]
                y_ref[bi, rows, :] = (yn + bonus) * _silu(ga_ref[bi, rows, :].astype(F32))
                yield

    def run(*gens):
        live = list(gens)
        while live:
            for gen in list(live):
                if next(gen, StopIteration) is StopIteration:
                    live.remove(gen)

    carried = {"st": [s_ref[bi * N_COLGROUPS + g] for bi, g in chains]}
    n_groups = tt // (C * SCAN_CHUNKS)
    ctx = load(0)
    run(independent(ctx))
    for gi in range(1, n_groups):
        nxt = load(gi)
        run(dependent(ctx, carried), independent(nxt))
        ctx = nxt
    run(dependent(ctx, carried))
    for (bi, g), s_new in zip(chains, carried["st"]):
        s_ref[bi * N_COLGROUPS + g] = s_new


def _rwkv_scan(r, cum, k, v, a, b, proj, vec, ones_bd, layer, tt=256):
    B, S, W = r.shape
    spec = pl.BlockSpec((B, tt, W), lambda t: (0, t, 0))

    def full(arr):
        return pl.BlockSpec(arr.shape, lambda t: (0,) * arr.ndim)

    return pl.pallas_call(
        functools.partial(_scan_kernel, nb=B, tt=tt),
        grid=(S // tt,),
        in_specs=[spec] * 6 + [pl.BlockSpec((B, tt, W), lambda t: (0, t, COL_GA // W)),
                               _layer_block(vec, layer), full(ones_bd)],
        out_specs=spec,
        out_shape=jax.ShapeDtypeStruct((B, S, W), F32),
        scratch_shapes=[pltpu.VMEM((B * N_COLGROUPS, HEAD_DIM, MXU_DIM), F32)],
        compiler_params=_cparams(("arbitrary",)),
        name="rwkv_scan",
    )(r, cum, k, v, a, b, proj, vec, ones_bd)


ATT_TILE = 2048
ATT_UNROLL = (5, 6, 8)
ATT_STATIC_FIRST = 4


def _attn_kernel(*refs, tiles_per_seq):
    q_refs = refs[0:3]
    k_refs = refs[3:6]
    v_refs = refs[6:9]
    kp_refs = refs[9:12]
    vp_refs = refs[12:15]
    gb_ref, bias_ref, y_ref = refs[15:18]
    o_refs = refs[18:21]
    l_refs = refs[21:24]

    is_first = (pl.program_id(1) % tiles_per_seq) == 0
    prev_limit = jnp.where(is_first, BLK, 0)
    ki = lax.broadcasted_iota(jnp.int32, (2 * BLK, 2 * BLK), 1)
    head0 = lax.broadcasted_iota(jnp.int32, (BLK, LANES), 1) < HEAD_DIM
    ones_cols = jnp.ones((2 * BLK, LANES), BF16)

    def process(blocks):
        zero = jnp.zeros((BLK, LANES), BF16)
        q2s, kws, vws, bias2s, stores = [], [], [], [], []
        for g, sub, res, from_prev in blocks:
            d = DILATIONS[g]
            base = res if from_prev else sub * (BLK * d) + res
            row0 = 0 if from_prev else pl.multiple_of(sub * BLK, BLK)
            q = q_refs[g][0, res, pl.ds(row0, BLK), :]
            q2s.append(jnp.concatenate([jnp.where(head0, q, zero), jnp.where(head0, zero, q)], axis=0))
            if from_prev:
                kw = jnp.concatenate([kp_refs[g][0, res], k_refs[g][0, res, 0:BLK, :]], axis=0)
                vw = jnp.concatenate([vp_refs[g][0, res], v_refs[g][0, res, 0:BLK, :]], axis=0)
            else:
                window = pl.ds(pl.multiple_of((sub - 1) * BLK, BLK), 2 * BLK)
                kw = k_refs[g][0, res, window, :]
                vw = v_refs[g][0, res, window, :]
            kws.append(kw)
            vws.append(jnp.concatenate([vw, ones_cols], axis=1))
            bias2s.append(bias_ref[g, 0].reshape(2 * BLK, 2 * BLK))
            stores.append((g, pl.ds(base, BLK) if d == 1 else pl.ds(base, BLK, stride=d)))
        logits = [jnp.where(bias2 > 0.5 * NEG_INF, _dot_nt(q2, kw) + bias2, NEG_INF)
                  for q2, kw, bias2 in zip(q2s, kws, bias2s)]
        logits = [jnp.where(ki < prev_limit, NEG_INF, x) if blk[3] else x for x, blk in zip(logits, blocks)]
        ms = [jnp.max(x, axis=-1, keepdims=True) for x in logits]
        ps = [jnp.exp2(x - m).astype(BF16) for x, m in zip(logits, ms)]
        pvs = [_dot(p, vw) for p, vw in zip(ps, vws)]
        for (g, rows), pv, m in zip(stores, pvs, ms):
            num = jnp.where(head0, pv[0:BLK, 0:LANES], pv[BLK:, 0:LANES])
            den = jnp.where(head0, pv[0:BLK, LANES:], pv[BLK:, LANES:])
            o_refs[g][rows, :] = num / den
            l_refs[g][rows, :] = jnp.where(head0, m[0:BLK], m[BLK:]) + jnp.log2(den)

    process([(g, 0, res, True) for g, d in enumerate(DILATIONS) if d <= ATT_STATIC_FIRST for res in range(d)])
    for g, d in enumerate(DILATIONS):
        unroll = ATT_UNROLL[g]
        shift = int(math.log2(d))
        n_sub = ATT_TILE // (BLK * d)
        if d > ATT_STATIC_FIRST:
            def first_body(it, carry, g=g, unroll=unroll):
                process([(g, 0, it * unroll + u, True) for u in range(unroll)])
                return carry
            lax.fori_loop(0, d // unroll, first_body, 0)
        n_rest = (n_sub - 1) * d
        if n_rest:
            def rest_body(it, carry, g=g, d=d, unroll=unroll, shift=shift):
                blks = [it * unroll + u for u in range(unroll)]
                process([(g, 1 + (b >> shift), b & (d - 1), False) for b in blks])
                return carry
            lax.fori_loop(0, n_rest // unroll, rest_body, 0)

    l0, l1, l2 = l_refs[0][...], l_refs[1][...], l_refs[2][...]
    m = jnp.maximum(jnp.maximum(l0, l1), l2)
    w0, w1, w2 = jnp.exp2(l0 - m), jnp.exp2(l1 - m), jnp.exp2(l2 - m)
    y = (w0 * o_refs[0][...] + w1 * o_refs[1][...] + w2 * o_refs[2][...]) / (w0 + w1 + w2)
    y_ref[0] = y * _silu(gb_ref[0].astype(F32))


def _dilated_attention(main, groups, bias5):
    B, S, _ = main.shape
    n_pairs = HEADS_PER_GROUP // 2
    tiles_per_seq = S // ATT_TILE
    arrays = [main.reshape(B, 1, S, MAIN_WIDTH)] + list(groups)
    col_base = [COL_A0 // LANES, 0, 0]

    def cur(g, part):
        d = DILATIONS[g]
        c0 = col_base[g] + part * (A_OUT_WIDTH // LANES)
        return pl.BlockSpec((1, d, ATT_TILE // d, LANES),
                            lambda hp, t: (t // tiles_per_seq, 0, t % tiles_per_seq, c0 + hp))

    def prev(g, part):
        d = DILATIONS[g]
        c0 = col_base[g] + part * (A_OUT_WIDTH // LANES)
        rb = ATT_TILE // (BLK * d)
        return pl.BlockSpec((1, d, BLK, LANES),
                            lambda hp, t: (t // tiles_per_seq, 0,
                                           jnp.maximum((t % tiles_per_seq) * rb - 1, 0), c0 + hp))

    def tile(col0):
        return pl.BlockSpec((1, ATT_TILE, LANES),
                            lambda hp, t: (t // tiles_per_seq, t % tiles_per_seq, col0 // LANES + hp))

    in_specs = ([cur(g, 0) for g in range(N_GROUPS)] + [cur(g, 1) for g in range(N_GROUPS)]
                + [cur(g, 2) for g in range(N_GROUPS)]
                + [prev(g, 1) for g in range(N_GROUPS)] + [prev(g, 2) for g in range(N_GROUPS)]
                + [tile(COL_GB),
                   pl.BlockSpec((N_GROUPS, 1, 2, BLK, 2 * BLK), lambda hp, t: (0, hp, 0, 0, 0))])
    scratch = [pltpu.VMEM((ATT_TILE, LANES), F32)] * 6
    return pl.pallas_call(
        functools.partial(_attn_kernel, tiles_per_seq=tiles_per_seq),
        grid=(n_pairs, B * tiles_per_seq),
        in_specs=in_specs,
        out_specs=tile(0),
        out_shape=jax.ShapeDtypeStruct((B, S, A_OUT_WIDTH), F32),
        scratch_shapes=scratch,
        compiler_params=_cparams(("parallel", "parallel")),
        name="dilated_attn",
    )(*(arrays * 5), main, bias5)


def _merge_kernel(ya_ref, yb_ref, ma_ref, mb_ref, x_ref, mod_ref, wa_ref, wb_ref, wo_ref, fg_ref,
                  o_ref, *, final_norm):
    pa = _dot(ya_ref[0].astype(BF16), wa_ref[...])
    pb = _dot(yb_ref[0].astype(BF16), wb_ref[...])
    merged = _sigmoid(ma_ref[0].astype(F32)) * pa + _sigmoid(mb_ref[0].astype(F32)) * pb
    out = _dot(merged.astype(BF16), wo_ref[...])
    gate = mod_ref[0, :, 2 * D_MODEL:3 * D_MODEL]
    xn = x_ref[0] + gate * out
    if final_norm:
        ms = jnp.mean(xn * xn, axis=-1, keepdims=True)
        xn = xn * lax.rsqrt(ms + RMS_EPS) * fg_ref[...]
    o_ref[0] = xn


def _merge(ya, yb, proj, x, mod, wa, wb, wo, final_g, final_norm, layer, tm=512):
    B, S, D = x.shape

    def rows(width, c):
        return pl.BlockSpec((1, tm, width), lambda b, i: (b, i, c))

    def full(arr):
        return pl.BlockSpec(arr.shape, lambda b, i: (0,) * arr.ndim)

    return pl.pallas_call(
        functools.partial(_merge_kernel, final_norm=final_norm),
        grid=(B, S // tm),
        in_specs=[rows(R_WIDTH, 0), rows(A_OUT_WIDTH, 0),
                  rows(D, COL_MA // D), rows(D, COL_MB // D), rows(D, 0),
                  pl.BlockSpec((1, 1, 3 * D), lambda b, i: (layer * MOD_ROWS + b, 0, 0)),
                  _layer_block(wa, layer), _layer_block(wb, layer), _layer_block(wo, layer), full(final_g)],
        out_specs=rows(D, 0),
        out_shape=jax.ShapeDtypeStruct((B, S, D), F32),
        compiler_params=_cparams(("parallel", "parallel")),
        name="merge",
    )(ya, yb, proj, proj, x, mod, wa, wb, wo, final_g)


def _segment_ones():
    idx = np.arange(MXU_DIM)
    return jnp.asarray(idx[:, None] // HEAD_DIM == idx[None, :] // HEAD_DIM, BF16)


def kernel(x, c, norm_g, ada_w, ada_b, w_in, rwkv_mu_rkv, rwkv_mu_wa, rwkv_w0, rwkv_w1, rwkv_w2, rwkv_a0, rwkv_a1, rwkv_a2, rwkv_k_k, rwkv_k_a, rwkv_r_k, rwkv_ln_g, rwkv_ln_b, rwkv_mu_v, rwkv_v0, rwkv_v1, rwkv_v2, w_branch_a, w_branch_b, w_out, rel_bias, final_g):
    B, S, D = x.shape
    assert D == D_MODEL and S % ATT_TILE == 0 and w_in.shape[2] == PROJ_WIDTH
    ones_bd = _segment_ones()
    mod = _adaln_mod(c, ada_w, ada_b).reshape(DEPTH * MOD_ROWS, 1, 3 * D)
    bias = _rel_bias(rel_bias).reshape(N_GROUPS, HEADS_PER_GROUP // 2, 2, BLK, 2 * BLK)

    def cols(start, width):
        return w_in[:, :, start:start + width]

    def group_cols(g):
        return [cols(W_AQ + A_OUT_WIDTH * g, A_OUT_WIDTH) * (LOG2E / math.sqrt(HEAD_DIM)),
                cols(W_AK + A_OUT_WIDTH * g, A_OUT_WIDTH), cols(W_AV + A_OUT_WIDTH * g, A_OUT_WIDTH)]

    w_main = jnp.concatenate(
        [cols(W_R, 4 * R_WIDTH), cols(W_MA, 2 * D_MODEL), cols(W_GB, A_OUT_WIDTH)] + group_cols(0),
        axis=2).astype(BF16)
    w_groups = [jnp.concatenate(group_cols(g), axis=2).astype(BF16) for g in range(1, N_GROUPS)]
    zeros_rows = jnp.zeros((DEPTH, D), F32)
    pvec = jnp.stack([rwkv_mu_rkv[:, 0], rwkv_mu_rkv[:, 1], rwkv_mu_rkv[:, 2], rwkv_w0, rwkv_a0, rwkv_k_k,
                      rwkv_k_a, _first_layer_blank(rwkv_v0)] + [zeros_rows] * (PV_ROWS - 8), axis=1)
    lora = _pack_lora([(rwkv_mu_wa[:, 0], rwkv_w1, rwkv_w2), (rwkv_mu_wa[:, 1], rwkv_a1, rwkv_a2),
                       (_first_layer_blank(rwkv_mu_v), _first_layer_blank(rwkv_v1),
                        _first_layer_blank(rwkv_v2))])
    vec = jnp.stack([rwkv_r_k.reshape(DEPTH, -1), rwkv_ln_g, rwkv_ln_b] + [zeros_rows] * 5, axis=1)
    norm_g3 = norm_g.reshape(DEPTH, 1, D)
    wa, wb, wo = w_branch_a.astype(BF16), w_branch_b.astype(BF16), w_out.astype(BF16)

    v_first = None
    for i in range(DEPTH):
        proj, h = _norm_proj(x, mod, norm_g3, w_main, i)
        groups = [_group_proj(h, w_groups[g - 1], DILATIONS[g], i) for g in range(1, N_GROUPS)]
        r, cum, k, v, a, b = _rwkv_prep(h, proj, v_first, pvec, lora if i > 0 else lora[:3], ones_bd, i)
        if i == 0:
            v_first = v
        y_a = _rwkv_scan(r, cum, k, v, a, b, proj, vec, ones_bd, i)
        y_b = _dilated_attention(proj, groups, bias)
        x = _merge(y_a, y_b, proj, x, mod, wa, wb, wo, final_g.reshape(1, D),
                   final_norm=(i == DEPTH - 1), layer=i)
    return x
```

```python
import functools
import math

import numpy as np
import jax
import jax.numpy as jnp
from jax import lax
from jax.experimental import pallas as pl
from jax.experimental.pallas import tpu as pltpu

F32 = jnp.float32
BF16 = jnp.bfloat16

D_MODEL = 1024
DEPTH = 2
HEAD_DIM = 64
R_WIDTH = 1024
N_GROUPS = 3
HEADS_PER_GROUP = 8
DILATIONS = (1, 4, 16)
BLK = 128
A_QK_WIDTH = 1536
A_OUT_WIDTH = 512
NUM_BUCKETS = 32
MAX_DISTANCE = 2048
PROJ_WIDTH = 4 * R_WIDTH + 3 * A_QK_WIDTH + A_OUT_WIDTH + 2 * D_MODEL
RMS_EPS = 1e-6
GN_EPS = 64e-5
NEG_INF = -1e30
LOG2E = math.log2(math.e)

LANES = 128
MXU_DIM = 256
HEADS_PER_TILE = MXU_DIM // HEAD_DIM
N_COLGROUPS = R_WIDTH // MXU_DIM
CHUNK = 64
SCAN_CHUNKS = 2

W_R, W_K, W_V, W_GA = 0, 1024, 2048, 3072
W_AQ, W_AK, W_AV = 4096, 5632, 7168
W_GB, W_MA, W_MB = 8704, 9216, 10240
COL_R, COL_K, COL_V, COL_GA, COL_MA, COL_MB, COL_GB, COL_A0 = 0, 1024, 2048, 3072, 4096, 5120, 6144, 6656
MAIN_WIDTH = 8192
GROUP_WIDTH = 3 * A_OUT_WIDTH

VMEM_LIMIT = 56 * 1024 * 1024
MOD_ROWS = 8


def _cparams(sem, vmem_limit=VMEM_LIMIT):
    return pltpu.CompilerParams(dimension_semantics=sem, vmem_limit_bytes=vmem_limit)


def _sigmoid(z):
    return 1.0 / (1.0 + jnp.exp(-z))


def _silu(z):
    return z * _sigmoid(z)


def _softplus(z):
    return jnp.maximum(z, 0.0) + jnp.log(1.0 + jnp.exp(-jnp.abs(z)))


def _dot(a, b):
    return jnp.dot(a, b, preferred_element_type=F32)


def _dot_nt(a, b):
    return lax.dot_general(a, b, (((1,), (1,)), ((), ())), preferred_element_type=F32)


def _split2(x):
    hi = x.astype(BF16)
    lo = (x - hi.astype(F32)).astype(BF16)
    return hi, lo


def _segsum64(x, ones_bd, split):
    n = x.shape[0]
    xs = jnp.concatenate([x[:, MXU_DIM * g:MXU_DIM * (g + 1)] for g in range(N_COLGROUPS)], axis=0)
    if split:
        hi, lo = _split2(xs)
        s = _dot(hi, ones_bd) + _dot(lo, ones_bd)
    else:
        s = _dot(xs.astype(BF16), ones_bd)
    return jnp.concatenate([s[n * g:n * (g + 1)] for g in range(N_COLGROUPS)], axis=1)


def _mod_kernel(c_ref, w_ref, b_ref, o_ref):
    s = _silu(c_ref[...])
    o_ref[0] = jnp.dot(s, w_ref[0], preferred_element_type=F32,
                       precision=lax.Precision.HIGHEST) + b_ref[0]


def _adaln_mod(c, ada_w, ada_b):
    L = ada_w.shape[0]
    B = c.shape[0]
    c_rows = jnp.pad(c, ((0, MOD_ROWS - B), (0, 0)))
    nj = 3
    return pl.pallas_call(
        _mod_kernel,
        grid=(L, nj),
        in_specs=[pl.BlockSpec((MOD_ROWS, D_MODEL), lambda l, j: (0, 0)),
                  pl.BlockSpec((1, D_MODEL, D_MODEL), lambda l, j: (l, 0, j)),
                  pl.BlockSpec((1, 1, D_MODEL), lambda l, j: (l, 0, j))],
        out_specs=pl.BlockSpec((1, MOD_ROWS, D_MODEL), lambda l, j: (l, 0, j)),
        out_shape=jax.ShapeDtypeStruct((L, MOD_ROWS, 3 * D_MODEL), F32),
        compiler_params=_cparams(("parallel", "parallel")),
        name="adaln_mod",
    )(c_rows, ada_w, ada_b.reshape(L, 1, 3 * D_MODEL))


def _t5_bucket(dist):
    max_exact = NUM_BUCKETS // 2
    safe = np.maximum(dist, 1).astype(np.float32)
    large = max_exact + (np.log(safe / max_exact) / math.log(MAX_DISTANCE / max_exact)
                         * (NUM_BUCKETS - max_exact)).astype(np.int32)
    large = np.minimum(large, NUM_BUCKETS - 1)
    return np.where(dist < max_exact, dist, large).astype(np.int32)


def _bias_kernel(tab_ref, bucket_ref, o_ref):
    h = pl.program_id(0)
    bk = bucket_ref[0]
    acc = jnp.zeros(bk.shape, F32)
    for b in range(NUM_BUCKETS):
        acc = jnp.where(bk == b, tab_ref[h * NUM_BUCKETS + b], acc)
    o_ref[0] = jnp.where(bk >= 0, acc * LOG2E, NEG_INF)


def _rel_bias(rel_bias):
    n_heads = rel_bias.shape[1]
    qi = np.arange(BLK)[:, None]
    ki = np.arange(2 * BLK)[None, :]
    delta = qi + BLK - ki
    band = (delta >= 0) & (delta <= BLK)
    buckets = np.stack([np.where(band, _t5_bucket(np.maximum(delta, 0) * d), -1)
                        for d in DILATIONS]).astype(np.int32)
    table = rel_bias.T.reshape(-1)
    return pl.pallas_call(
        _bias_kernel,
        grid=(n_heads,),
        in_specs=[pl.BlockSpec(memory_space=pltpu.SMEM),
                  pl.BlockSpec((1, BLK, 2 * BLK), lambda h: (h // HEADS_PER_GROUP, 0, 0))],
        out_specs=pl.BlockSpec((1, BLK, 2 * BLK), lambda h: (h, 0, 0)),
        out_shape=jax.ShapeDtypeStruct((n_heads, BLK, 2 * BLK), F32),
        compiler_params=_cparams(("parallel",)),
        name="rel_bias",
    )(table, jnp.asarray(buckets))


def _proj_kernel(x_ref, mod_ref, g_ref, w_ref, proj_ref, h_ref):
    @pl.when(pl.program_id(2) == 0)
    def _():
        x = x_ref[0]
        ms = jnp.mean(x * x, axis=-1, keepdims=True)
        y = x * lax.rsqrt(ms + RMS_EPS) * g_ref[...]
        shift = mod_ref[0, :, 0:D_MODEL]
        scale = mod_ref[0, :, D_MODEL:2 * D_MODEL]
        h_ref[0] = (y * (1.0 + scale) + shift).astype(BF16)

    proj_ref[0] = _dot(h_ref[0], w_ref[...]).astype(BF16)


def _layer_block(arr, layer):
    tail = (0,) * (arr.ndim - 1)
    return pl.BlockSpec((None,) + arr.shape[1:], lambda *_: (layer,) + tail)


def _norm_proj(x, mod, norm_g, w_main, layer, tm=1024, tn=2048):
    B, S, D = x.shape
    N = w_main.shape[2]
    return pl.pallas_call(
        _proj_kernel,
        grid=(B, S // tm, N // tn),
        in_specs=[pl.BlockSpec((1, tm, D), lambda b, i, j: (b, i, 0)),
                  pl.BlockSpec((1, 1, 3 * D), lambda b, i, j: (layer * MOD_ROWS + b, 0, 0)),
                  _layer_block(norm_g, layer),
                  pl.BlockSpec((None, D, tn), lambda b, i, j: (layer, 0, j))],
        out_specs=[pl.BlockSpec((1, tm, tn), lambda b, i, j: (b, i, j)),
                   pl.BlockSpec((1, tm, D), lambda b, i, j: (b, i, 0))],
        out_shape=[jax.ShapeDtypeStruct((B, S, N), BF16),
                   jax.ShapeDtypeStruct((B, S, D), BF16)],
        compiler_params=_cparams(("parallel", "parallel", "arbitrary")),
        name="norm_proj",
    )(x, mod, norm_g, w_main)


GATHER_ROWS = MXU_DIM


def _group_proj_kernel(h_ref, perm_ref, w_ref, o_ref, *, d):
    tm = h_ref.shape[1]
    per_res = GATHER_ROWS // d
    perm = perm_ref[...]
    for ck in range(tm // GATHER_ROWS):
        r0 = ck * GATHER_ROWS
        hp = _dot(perm, h_ref[0, r0:r0 + GATHER_ROWS, :]).astype(BF16)
        res = _dot(hp, w_ref[...]).astype(BF16)
        for r in range(d):
            o_ref[0, r, ck * per_res:(ck + 1) * per_res, :] = res[r * per_res:(r + 1) * per_res]


def _group_proj(h, w_group, d, layer, tm=1024, tn=GROUP_WIDTH):
    B, S, D = h.shape
    N = w_group.shape[2]
    dst = np.arange(GATHER_ROWS)
    src = (dst % (GATHER_ROWS // d)) * d + dst // (GATHER_ROWS // d)
    perm = jnp.asarray(src[:, None] == np.arange(GATHER_ROWS)[None, :], BF16)
    return pl.pallas_call(
        functools.partial(_group_proj_kernel, d=d),
        grid=(B, S // tm, N // tn),
        in_specs=[pl.BlockSpec((1, tm, D), lambda b, i, j: (b, i, 0)),
                  pl.BlockSpec((GATHER_ROWS, GATHER_ROWS), lambda b, i, j: (0, 0)),
                  pl.BlockSpec((None, D, tn), lambda b, i, j: (layer, 0, j))],
        out_specs=pl.BlockSpec((1, d, tm // d, tn), lambda b, i, j: (b, 0, i, j)),
        out_shape=jax.ShapeDtypeStruct((B, d, S // d, N), BF16),
        compiler_params=_cparams(("parallel", "parallel", "parallel")),
        name=f"group_proj_d{d}",
    )(h, perm, w_group)


PV_MU_R, PV_MU_K, PV_MU_V, PV_W0, PV_A0, PV_KK, PV_KA, PV_V0 = range(8)
LORA_LANES = 256


def _pack_lora(paths):
    n_layers, d_model, _ = paths[0][1].shape
    used = sum(down.shape[2] for _, down, _ in paths)
    pad = jnp.zeros((n_layers, d_model, LORA_LANES - used), F32)
    keep = jnp.concatenate([(1.0 - mu)[:, :, None] * down for mu, down, _ in paths] + [pad], axis=2)
    shifted = jnp.concatenate([mu[:, :, None] * down for mu, down, _ in paths] + [pad], axis=2)
    ups, lane = [], 0
    for _, down, up in paths:
        rank = down.shape[2]
        ups.append(jnp.pad(up, ((0, 0), (lane, LORA_LANES - lane - rank), (0, 0))).astype(BF16))
        lane += rank
    return [jnp.concatenate([keep, shifted], axis=2).astype(BF16)] + ups


def _first_layer_blank(arr):
    return jnp.concatenate([jnp.zeros((1,) + arr.shape[1:], arr.dtype), arr], axis=0)
PV_ROWS = 16
PREV_ROWS = 16


def _shift_rows(t, prev_last):
    rolled = pltpu.roll(t, 1, axis=0)
    row = lax.broadcasted_iota(jnp.int32, t.shape, 0)
    return jnp.where(row == 0, prev_last, rolled)


def _rprep_kernel(*refs, has_vres):
    if has_vres:
        (h_ref, hp_ref, pr_ref, prp_ref, pk_ref, pkp_ref, pvv_ref, pvp_ref, vf_ref, pvec_ref,
         wd_ref, uw_ref, ua_ref, uv_ref, ones_ref, tril_ref,
         r_out, cum_out, k_out, v_out, a_out, b_out) = refs
    else:
        (h_ref, hp_ref, pr_ref, prp_ref, pk_ref, pkp_ref, pvv_ref, pvp_ref, pvec_ref,
         wd_ref, uw_ref, ua_ref, ones_ref, tril_ref,
         r_out, cum_out, k_out, v_out, a_out, b_out) = refs

    not_first = (pl.program_id(1) > 0).astype(F32)

    def prm(i):
        return pvec_ref[i:i + 1, :]

    def lerp_shift(cur_ref, prev_ref, mu):
        t = cur_ref[0].astype(F32)
        last = prev_ref[0, PREV_ROWS - 1:PREV_ROWS, :].astype(F32)
        return t + (_shift_rows(t, last * not_first) - t) * mu

    r = lerp_shift(pr_ref, prp_ref, prm(PV_MU_R))
    k = lerp_shift(pk_ref, pkp_ref, prm(PV_MU_K))
    v = lerp_shift(pvv_ref, pvp_ref, prm(PV_MU_V))

    wd = wd_ref[...]
    z2 = _dot(h_ref[0], wd)
    z_prev = _dot(hp_ref[0], wd[:, LORA_LANES:])[PREV_ROWS - 1:PREV_ROWS, :]
    z = z2[:, 0:LORA_LANES] + _shift_rows(z2[:, LORA_LANES:], z_prev * not_first)
    zb = z.astype(BF16)

    zw = prm(PV_W0) + _dot(jnp.tanh(z).astype(BF16), uw_ref[...])
    w = -_softplus(-zw) - 0.5
    lw = -jnp.exp(w)
    hi = lw.astype(BF16)
    rest = lw - hi.astype(F32)
    mid = rest.astype(BF16)
    lo = (rest - mid.astype(F32)).astype(BF16)
    tril = tril_ref[...]
    for i in range(lw.shape[0] // MXU_DIM):
        blk = slice(MXU_DIM * i, MXU_DIM * (i + 1))
        cum_out[0, blk, :] = _dot(tril, hi[blk]) + _dot(tril, mid[blk]) + _dot(tril, lo[blk])
    a = _sigmoid(prm(PV_A0) + _dot(zb, ua_ref[...]))
    if has_vres:
        mix = _sigmoid(prm(PV_V0) + _dot(zb, uv_ref[...]))
        v = v + (vf_ref[0].astype(F32) - v) * mix

    kk = k * prm(PV_KK)
    ss = _segsum64(kk * kk, ones_ref[...], split=False)
    kk = kk * lax.rsqrt(jnp.maximum(ss, 1e-24))
    r_out[0] = r.astype(BF16)
    k_out[0] = (k * (1.0 + (a - 1.0) * prm(PV_KA))).astype(BF16)
    v_out[0] = v.astype(BF16)
    a_out[0] = (-kk).astype(BF16)
    b_out[0] = (kk * a).astype(BF16)


def _rwkv_prep(h, proj, v_first, pvec, lora, ones_bd, layer, tr=512):
    B, S, D = h.shape
    has_vres = v_first is not None
    rpb = tr // PREV_ROWS
    t = np.arange(MXU_DIM)
    tril_bd =jnp.asarray((t[None, :] <= t[:, None]) & (t[None, :] // CHUNK == t[:, None] // CHUNK), BF16)

    def cur(c):
        return pl.BlockSpec((1, tr, R_WIDTH), lambda b, i: (b, i, c))

    def prev(c):
        return pl.BlockSpec((1, PREV_ROWS, R_WIDTH), lambda b, i: (b, jnp.maximum(i * rpb - 1, 0), c))

    def full(arr):
        return pl.BlockSpec(arr.shape, lambda b, i: (0,) * arr.ndim)

    in_specs = [cur(0), prev(0)]
    args = [h, h]
    for c in (COL_R, COL_K, COL_V):
        in_specs += [cur(c // R_WIDTH), prev(c // R_WIDTH)]
        args += [proj, proj]
    if has_vres:
        in_specs.append(cur(0))
        args.append(v_first)
    for per_layer in [pvec] + list(lora):
        in_specs.append(_layer_block(per_layer, layer))
        args.append(per_layer)
    for const in (ones_bd, tril_bd):
        in_specs.append(full(const))
        args.append(const)
    out = [jax.ShapeDtypeStruct((B, S, R_WIDTH), F32 if n == 1 else BF16) for n in range(6)]
    return pl.pallas_call(
        functools.partial(_rprep_kernel, has_vres=has_vres),
        grid=(B, S // tr),
        in_specs=in_specs,
        out_specs=[cur(0)] * 6,
        out_shape=out,
        compiler_params=_cparams(("parallel", "parallel")),
        name="rwkv_prep",
    )(*args)


def _interleave(*gens):
    live = list(gens)
    while live:
        for gen in list(live):
            if next(gen, StopIteration) is StopIteration:
                live.remove(gen)
        yield


def _scan_stages(r_ref, cum_ref, k_ref, v_ref, a_ref, b_ref, ga_ref, vec_ref, ones_ref,
                 y_ref, s_ref, *, nb, tt):
    C = CHUNK

    @pl.when(pl.program_id(0) == 0)
    def _():
        s_ref[...] = jnp.zeros(s_ref.shape, F32)

    row = lax.broadcasted_iota(jnp.int32, (C, MXU_DIM), 0)
    lane = lax.broadcasted_iota(jnp.int32, (C, MXU_DIM), 1)
    col = lane & (HEAD_DIM - 1)
    lhead = lane >> 6
    strict = col < row
    incl = col <= row
    eye = (col == row).astype(F32)
    head_masks = [lhead == hh for hh in range(HEADS_PER_TILE)]

    def bdrows(x):
        return jnp.concatenate([jnp.where(m, x, 0.0) for m in head_masks], axis=0).astype(BF16)

    def diag_blocks(full):
        acc = jnp.where(head_masks[0], full[0:C], 0.0)
        for hh in range(1, HEADS_PER_TILE):
            acc = acc + jnp.where(head_masks[hh], full[C * hh:C * (hh + 1)], 0.0)
        return acc

    row_full = lax.broadcasted_iota(jnp.int32, (C, R_WIDTH), 0)
    ones_bd = ones_ref[...]
    r_k = vec_ref[0:1, :]
    ln_g = vec_ref[1:2, :]
    ln_b = vec_ref[2:3, :]

    chains = [(bi, g) for bi in range(nb) for g in range(N_COLGROUPS)]
    insts = [(ck, bi, g) for ck in range(SCAN_CHUNKS) for bi, g in chains]


    def load(gi):
        rows = [slice((gi * SCAN_CHUNKS + ck) * C, (gi * SCAN_CHUNKS + ck + 1) * C) for ck in range(SCAN_CHUNKS)]
        ops = {}
        for ck in range(SCAN_CHUNKS):
            for bi in range(nb):
                cum = cum_ref[bi, rows[ck], :]
                r = r_ref[bi, rows[ck], :].astype(F32)
                k = k_ref[bi, rows[ck], :].astype(F32)
                v = v_ref[bi, rows[ck], :].astype(F32)
                a = a_ref[bi, rows[ck], :].astype(F32)
                b = b_ref[bi, rows[ck], :].astype(F32)
                total = cum[C - 1:C, :]
                p_in = jnp.exp(cum)
                p_inv = jnp.exp(-cum)
                p_rest = jnp.exp(total - cum)
                p_before = jnp.where(row_full == 0, 1.0, pltpu.roll(p_in, 1, axis=0))
                ops[ck, bi] = dict(r=r, k=k, v=v, a_t=a * p_before, r_t=r * p_in, b_t=b * p_inv,
                                   k_t=k * p_inv, bp=b * p_rest, kp=k * p_rest, p_all=jnp.exp(total))
        return dict(rows=rows, ops=ops)

    def part(ctx, name, ck, bi, g):
        return ctx["ops"][ck, bi][name][:, MXU_DIM * g:MXU_DIM * (g + 1)]

    def independent(ctx):
        res = [_dot_nt(jnp.concatenate([part(ctx, "a_t", *i), part(ctx, "r_t", *i)], axis=0).astype(BF16),
                       jnp.concatenate([bdrows(part(ctx, "b_t", *i)), bdrows(part(ctx, "k_t", *i))], axis=0))
               for i in insts]
        yield
        a_ab = [jnp.where(strict, x[0:C, 0:MXU_DIM], 0.0) for x in res]
        a_ak = [jnp.where(strict, x[0:C, MXU_DIM:], 0.0) for x in res]
        ctx["a_rb"] = [jnp.where(incl, x[C:, 0:MXU_DIM], 0.0).astype(BF16) for x in res]
        a_rk = [jnp.where(incl, x[C:, MXU_DIM:], 0.0) for x in res]

        pw = [_dot(x.astype(BF16), bdrows(x)) for x in a_ab]
        tinv = [eye + x for x in a_ab]
        yield
        for _ in range(4):
            both = [_dot(jnp.concatenate([p, t], axis=0).astype(BF16), bdrows(p)) for p, t in zip(pw, tinv)]
            tinv = [t + x[C:] for t, x in zip(tinv, both)]
            pw = [x[0:C] for x in both]
            yield
        tinv = [t + _dot(t.astype(BF16), bdrows(p)) for p, t in zip(pw, tinv)]
        yield
        tax = [_dot(t.astype(BF16), jnp.concatenate([bdrows(part(ctx, "a_t", *i)), bdrows(x)], axis=1))
               for t, x, i in zip(tinv, a_ak, insts)]
        yield
        ctx["from_v"] = [_dot(jnp.concatenate([x[:, MXU_DIM:], ark], axis=0).astype(BF16),
                              bdrows(part(ctx, "v", *i))) for x, ark, i in zip(tax, a_rk, insts)]
        ctx["tax"] = tax
        yield

    def dependent(ctx, carried):
        tax, from_v, a_rb = ctx["tax"], ctx["from_v"], ctx["a_rb"]
        st = carried["st"]
        y = {}
        for ck in range(SCAN_CHUNKS):
            sel = range(ck * len(chains), (ck + 1) * len(chains))
            from_state = [_dot_nt(jnp.concatenate([tax[n][:, 0:MXU_DIM], part(ctx, "r_t", *insts[n])],
                                                  axis=0).astype(BF16), bdrows(s))
                          for n, s in zip(sel, st)]
            yield
            u = [x[0:C] + from_v[n][0:C] for x, n in zip(from_state, sel)]
            for x, n, uu in zip(from_state, sel, u):
                y[insts[n]] = x[C:] + from_v[n][C:] + _dot(a_rb[n], bdrows(uu))
            upd = [_dot(jnp.concatenate([uu, part(ctx, "v", *insts[n])], axis=0).T.astype(BF16),
                        jnp.concatenate([part(ctx, "bp", *insts[n]), part(ctx, "kp", *insts[n])],
                                        axis=0).astype(BF16))
                   for uu, n in zip(u, sel)]
            yield
            st = [s_old * part(ctx, "p_all", *insts[n]) + diag_blocks(x) for s_old, x, n in zip(st, upd, sel)]
        carried["st"] = st

        for ck in range(SCAN_CHUNKS):
            for bi in range(nb):
                p = ctx["ops"][ck, bi]
                yc = jnp.concatenate([y[ck, bi, g] for g in range(N_COLGROUPS)], axis=1)
                mean = _segsum64(yc, ones_bd, split=True) * (1.0 / HEAD_DIM)
                yield
                yd = yc - mean
                var = _segsum64(yd * yd, ones_bd, split=False) * (1.0 / HEAD_DIM)
                yn = yd * lax.rsqrt(var + GN_EPS) * ln_g + ln_b
                bonus = _segsum64(p["r"] * p["k"] * r_k, ones_bd, split=False) * p["v"]
                rows = ctx["rows"][ck]
                y_ref[bi, rows, :] = ((yn + bonus) * _silu(ga_ref[bi, rows, :].astype(F32))).astype(BF16)
                yield

    carried = {"st": [s_ref[bi * N_COLGROUPS + g] for bi, g in chains]}
    n_groups = tt // (C * SCAN_CHUNKS)
    ctx = load(0)
    yield from independent(ctx)
    for gi in range(1, n_groups):
        nxt = load(gi)
        yield from _interleave(dependent(ctx, carried), independent(nxt))
        ctx = nxt
    yield from dependent(ctx, carried)
    for (bi, g), s_new in zip(chains, carried["st"]):
        s_ref[bi * N_COLGROUPS + g] = s_new


ATT_TILE = 2048
ATT_UNROLL = (5, 6, 8)


def _attn_stages(q_refs, k_refs, v_refs, kp_refs, vp_refs, gb_ref, bias_ref, y_ref, o_refs, l_refs,
                 is_first):
    prev_limit = jnp.where(is_first, BLK, 0)
    ki = lax.broadcasted_iota(jnp.int32, (2 * BLK, 2 * BLK), 1)
    head0 = lax.broadcasted_iota(jnp.int32, (BLK, LANES), 1) < HEAD_DIM
    ones_cols = jnp.ones((2 * BLK, LANES), BF16)
    zero = jnp.zeros((BLK, LANES), BF16)

    def process(blocks):
        q2s, kws, vws, bias2s, stores = [], [], [], [], []
        for g, sub, res in blocks:
            d = DILATIONS[g]
            base = sub * (BLK * d) + res
            q = q_refs[g][0, res, sub * BLK:(sub + 1) * BLK, :]
            q2s.append(jnp.concatenate([jnp.where(head0, q, zero), jnp.where(head0, zero, q)], axis=0))
            if sub == 0:
                kw = jnp.concatenate([kp_refs[g][0, res], k_refs[g][0, res, 0:BLK, :]], axis=0)
                vw = jnp.concatenate([vp_refs[g][0, res], v_refs[g][0, res, 0:BLK, :]], axis=0)
            else:
                kw = k_refs[g][0, res, (sub - 1) * BLK:(sub + 1) * BLK, :]
                vw = v_refs[g][0, res, (sub - 1) * BLK:(sub + 1) * BLK, :]
            kws.append(kw)
            vws.append(jnp.concatenate([vw, ones_cols], axis=1))
            bias2s.append(bias_ref[g, 0].reshape(2 * BLK, 2 * BLK))
            stores.append((g, pl.ds(base, BLK) if d == 1 else pl.ds(base, BLK, stride=d)))
        logits = [jnp.where(bias2 > 0.5 * NEG_INF, _dot_nt(q2, kw) + bias2, NEG_INF)
                  for q2, kw, bias2 in zip(q2s, kws, bias2s)]
        logits = [jnp.where(ki < prev_limit, NEG_INF, x) if blk[1] == 0 else x
                  for x, blk in zip(logits, blocks)]
        yield
        ms = [jnp.max(x, axis=-1, keepdims=True) for x in logits]
        ps = [jnp.exp2(x - m).astype(BF16) for x, m in zip(logits, ms)]
        pvs = [_dot(p, vw) for p, vw in zip(ps, vws)]
        yield
        for (g, rows), pv, m in zip(stores, pvs, ms):
            num = jnp.where(head0, pv[0:BLK, 0:LANES], pv[BLK:, 0:LANES])
            den = jnp.where(head0, pv[0:BLK, LANES:], pv[BLK:, LANES:])
            o_refs[g][rows, :] = num / den
            l_refs[g][rows, :] = jnp.where(head0, m[0:BLK], m[BLK:]) + jnp.log2(den)

    for g, d in enumerate(DILATIONS):
        blocks = [(g, sub, res) for sub in range(ATT_TILE // (BLK * d)) for res in range(d)]
        for n in range(0, len(blocks), ATT_UNROLL[g]):
            yield from process(blocks[n:n + ATT_UNROLL[g]])

    l0, l1, l2 = l_refs[0][...], l_refs[1][...], l_refs[2][...]
    m = jnp.maximum(jnp.maximum(l0, l1), l2)
    w0, w1, w2 = jnp.exp2(l0 - m), jnp.exp2(l1 - m), jnp.exp2(l2 - m)
    y = (w0 * o_refs[0][...] + w1 * o_refs[1][...] + w2 * o_refs[2][...]) / (w0 + w1 + w2)
    y_ref[0] = (y * _silu(gb_ref[0].astype(F32))).astype(BF16)
    yield


N_SCAN_REFS = 9
N_ATTN_REFS = 17
MIXERS_VMEM_LIMIT = 62 * 1024 * 1024


def _mixers_kernel(*refs, nb, tt, tiles_per_seq, n_tiles):
    scan_in = refs[:N_SCAN_REFS]
    attn_in = refs[N_SCAN_REFS:N_SCAN_REFS + N_ATTN_REFS]
    ya_ref, yb_ref, s_ref = refs[N_SCAN_REFS + N_ATTN_REFS:N_SCAN_REFS + N_ATTN_REFS + 3]
    scratch = refs[N_SCAN_REFS + N_ATTN_REFS + 3:]
    is_first = ((pl.program_id(0) % n_tiles) % tiles_per_seq) == 0
    scan = _scan_stages(*scan_in, ya_ref, s_ref, nb=nb, tt=tt)
    attn = _attn_stages(attn_in[0:3], attn_in[3:6], attn_in[6:9], attn_in[9:12], attn_in[12:15],
                        attn_in[15], attn_in[16], yb_ref, scratch[0:3], scratch[3:6], is_first)
    for _ in _interleave(scan, attn):
        pass


def _mixers(r, cum, k, v, a, b, main, groups, vec, ones_bd, bias5, layer):
    B, S, W = r.shape
    n_pairs = HEADS_PER_GROUP // 2
    tiles_per_seq = S // ATT_TILE
    n_tiles = B * tiles_per_seq
    n_steps = n_pairs * n_tiles
    tt = S // n_steps
    assert tt * n_steps == S and tt % (CHUNK * SCAN_CHUNKS) == 0

    scan_spec = pl.BlockSpec((B, tt, W), lambda t: (0, t, 0))

    def full(arr):
        return pl.BlockSpec(arr.shape, lambda t: (0,) * arr.ndim)

    scan_specs = [scan_spec] * 6 + [pl.BlockSpec((B, tt, W), lambda t: (0, t, COL_GA // W)),
                                    _layer_block(vec, layer), full(ones_bd)]

    arrays = [main.reshape(B, 1, S, MAIN_WIDTH)] + list(groups)
    col_base = [COL_A0 // LANES, 0, 0]

    def where(t):
        tile = t % n_tiles
        return t // n_tiles, tile // tiles_per_seq, tile % tiles_per_seq

    def cur(g, part):
        d = DILATIONS[g]
        c0 = col_base[g] + part * (A_OUT_WIDTH // LANES)

        def index(t):
            hp, bi, ti = where(t)
            return bi, 0, ti, c0 + hp
        return pl.BlockSpec((1, d, ATT_TILE // d, LANES), index)

    def prev(g, part):
        d = DILATIONS[g]
        c0 = col_base[g] + part * (A_OUT_WIDTH // LANES)
        rb = ATT_TILE // (BLK * d)

        def index(t):
            hp, bi, ti = where(t)
            return bi, 0, jnp.maximum(ti * rb - 1, 0), c0 + hp
        return pl.BlockSpec((1, d, BLK, LANES), index)

    def tile(col0):
        def index(t):
            hp, bi, ti = where(t)
            return bi, ti, col0 // LANES + hp
        return pl.BlockSpec((1, ATT_TILE, LANES), index)

    attn_specs = ([cur(g, 0) for g in range(N_GROUPS)] + [cur(g, 1) for g in range(N_GROUPS)]
                  + [cur(g, 2) for g in range(N_GROUPS)]
                  + [prev(g, 1) for g in range(N_GROUPS)] + [prev(g, 2) for g in range(N_GROUPS)]
                  + [tile(COL_GB),
                     pl.BlockSpec((N_GROUPS, 1, 2, BLK, 2 * BLK), lambda t: (0, t // n_tiles, 0, 0, 0))])
    assert len(scan_specs) == N_SCAN_REFS and len(attn_specs) == N_ATTN_REFS
    return pl.pallas_call(
        functools.partial(_mixers_kernel, nb=B, tt=tt, tiles_per_seq=tiles_per_seq, n_tiles=n_tiles),
        grid=(n_steps,),
        in_specs=scan_specs + attn_specs,
        out_specs=[scan_spec, tile(0)],
        out_shape=[jax.ShapeDtypeStruct((B, S, W), BF16), jax.ShapeDtypeStruct((B, S, A_OUT_WIDTH), BF16)],
        scratch_shapes=([pltpu.VMEM((B * N_COLGROUPS, HEAD_DIM, MXU_DIM), F32)]
                        + [pltpu.VMEM((ATT_TILE, LANES), F32)] * 6),
        compiler_params=_cparams(("arbitrary",), MIXERS_VMEM_LIMIT),
        name="mixers",
    )(r, cum, k, v, a, b, main, vec, ones_bd, *(arrays * 5), main, bias5)


def _merge_kernel(ya_ref, yb_ref, ma_ref, mb_ref, x_ref, mod_ref, wa_ref, wb_ref, wo_ref, fg_ref,
                  o_ref, *, final_norm):
    pa = _dot(ya_ref[0], wa_ref[...])
    pb = _dot(yb_ref[0], wb_ref[...])
    merged = _sigmoid(ma_ref[0].astype(F32)) * pa + _sigmoid(mb_ref[0].astype(F32)) * pb
    out = _dot(merged.astype(BF16), wo_ref[...])
    gate = mod_ref[0, :, 2 * D_MODEL:3 * D_MODEL]
    xn = x_ref[0] + gate * out
    if final_norm:
        ms = jnp.mean(xn * xn, axis=-1, keepdims=True)
        xn = xn * lax.rsqrt(ms + RMS_EPS) * fg_ref[...]
    o_ref[0] = xn


def _merge(ya, yb, proj, x, mod, wa, wb, wo, final_g, final_norm, layer, tm=512):
    B, S, D = x.shape

    def rows(width, c):
        return pl.BlockSpec((1, tm, width), lambda b, i: (b, i, c))

    def full(arr):
        return pl.BlockSpec(arr.shape, lambda b, i: (0,) * arr.ndim)

    return pl.pallas_call(
        functools.partial(_merge_kernel, final_norm=final_norm),
        grid=(B, S // tm),
        in_specs=[rows(R_WIDTH, 0), rows(A_OUT_WIDTH, 0),
                  rows(D, COL_MA // D), rows(D, COL_MB // D), rows(D, 0),
                  pl.BlockSpec((1, 1, 3 * D), lambda b, i: (layer * MOD_ROWS + b, 0, 0)),
                  _layer_block(wa, layer), _layer_block(wb, layer), _layer_block(wo, layer), full(final_g)],
        out_specs=rows(D, 0),
        out_shape=jax.ShapeDtypeStruct((B, S, D), F32),
        compiler_params=_cparams(("parallel", "parallel")),
        name="merge",
    )(ya, yb, proj, proj, x, mod, wa, wb, wo, final_g)


def _segment_ones():
    idx = np.arange(MXU_DIM)
    return jnp.asarray(idx[:, None] // HEAD_DIM == idx[None, :] // HEAD_DIM, BF16)


def kernel(x, c, norm_g, ada_w, ada_b, w_in, rwkv_mu_rkv, rwkv_mu_wa, rwkv_w0, rwkv_w1, rwkv_w2, rwkv_a0, rwkv_a1, rwkv_a2, rwkv_k_k, rwkv_k_a, rwkv_r_k, rwkv_ln_g, rwkv_ln_b, rwkv_mu_v, rwkv_v0, rwkv_v1, rwkv_v2, w_branch_a, w_branch_b, w_out, rel_bias, final_g):
    B, S, D = x.shape
    assert D == D_MODEL and S % ATT_TILE == 0 and w_in.shape[2] == PROJ_WIDTH
    ones_bd = _segment_ones()
    mod = _adaln_mod(c, ada_w, ada_b).reshape(DEPTH * MOD_ROWS, 1, 3 * D)
    bias = _rel_bias(rel_bias).reshape(N_GROUPS, HEADS_PER_GROUP // 2, 2, BLK, 2 * BLK)

    def cols(start, width):
        return w_in[:, :, start:start + width]

    def group_cols(g):
        return [cols(W_AQ + A_OUT_WIDTH * g, A_OUT_WIDTH) * (LOG2E / math.sqrt(HEAD_DIM)),
                cols(W_AK + A_OUT_WIDTH * g, A_OUT_WIDTH), cols(W_AV + A_OUT_WIDTH * g, A_OUT_WIDTH)]

    w_main = jnp.concatenate(
        [cols(W_R, 4 * R_WIDTH), cols(W_MA, 2 * D_MODEL), cols(W_GB, A_OUT_WIDTH)] + group_cols(0),
        axis=2).astype(BF16)
    w_groups = [jnp.concatenate(group_cols(g), axis=2).astype(BF16) for g in range(1, N_GROUPS)]
    zeros_rows = jnp.zeros((DEPTH, D), F32)
    pvec = jnp.stack([rwkv_mu_rkv[:, 0], rwkv_mu_rkv[:, 1], rwkv_mu_rkv[:, 2], rwkv_w0, rwkv_a0, rwkv_k_k,
                      rwkv_k_a, _first_layer_blank(rwkv_v0)] + [zeros_rows] * (PV_ROWS - 8), axis=1)
    lora = _pack_lora([(rwkv_mu_wa[:, 0], rwkv_w1, rwkv_w2), (rwkv_mu_wa[:, 1], rwkv_a1, rwkv_a2),
                       (_first_layer_blank(rwkv_mu_v), _first_layer_blank(rwkv_v1),
                        _first_layer_blank(rwkv_v2))])
    vec = jnp.stack([rwkv_r_k.reshape(DEPTH, -1), rwkv_ln_g, rwkv_ln_b] + [zeros_rows] * 5, axis=1)
    norm_g3 = norm_g.reshape(DEPTH, 1, D)
    wa, wb, wo = w_branch_a.astype(BF16), w_branch_b.astype(BF16), w_out.astype(BF16)

    v_first = None
    for i in range(DEPTH):
        proj, h = _norm_proj(x, mod, norm_g3, w_main, i)
        groups = [_group_proj(h, w_groups[g - 1], DILATIONS[g], i) for g in range(1, N_GROUPS)]
        r, cum, k, v, a, b = _rwkv_prep(h, proj, v_first, pvec, lora if i > 0 else lora[:3], ones_bd, i)
        if i == 0:
            v_first = v
        y_a, y_b = _mixers(r, cum, k, v, a, b, proj, groups, vec, ones_bd, bias, i)
        x = _merge(y_a, y_b, proj, x, mod, wa, wb, wo, final_g.reshape(1, D),
                   final_norm=(i == DEPTH - 1), layer=i)
    return x
```

```python
import functools
import math

import numpy as np
import jax
import jax.numpy as jnp
from jax import lax
from jax.experimental import pallas as pl
from jax.experimental.pallas import tpu as pltpu

F32 = jnp.float32
BF16 = jnp.bfloat16

D_MODEL = 1024
DEPTH = 2
HEAD_DIM = 64
R_WIDTH = 1024
N_GROUPS = 3
HEADS_PER_GROUP = 8
DILATIONS = (1, 4, 16)
BLK = 128
A_QK_WIDTH = 1536
A_OUT_WIDTH = 512
NUM_BUCKETS = 32
MAX_DISTANCE = 2048
PROJ_WIDTH = 4 * R_WIDTH + 3 * A_QK_WIDTH + A_OUT_WIDTH + 2 * D_MODEL
RMS_EPS = 1e-6
GN_EPS = 64e-5
NEG_INF = -1e30
LOG2E = math.log2(math.e)

LANES = 128
MXU_DIM = 256
HEADS_PER_TILE = MXU_DIM // HEAD_DIM
N_COLGROUPS = R_WIDTH // MXU_DIM
CHUNK = 64
SCAN_CHUNKS = 2

W_R, W_K, W_V, W_GA = 0, 1024, 2048, 3072
W_AQ, W_AK, W_AV = 4096, 5632, 7168
W_GB, W_MA, W_MB = 8704, 9216, 10240
COL_R, COL_K, COL_V, COL_GA, COL_MA, COL_MB, COL_GB, COL_A0 = 0, 1024, 2048, 3072, 4096, 5120, 6144, 6656
MAIN_WIDTH = 8192
GROUP_WIDTH = 3 * A_OUT_WIDTH

VMEM_LIMIT = 56 * 1024 * 1024
MOD_ROWS = 8


def _cparams(sem, vmem_limit=VMEM_LIMIT):
    return pltpu.CompilerParams(dimension_semantics=sem, vmem_limit_bytes=vmem_limit)


def _sigmoid(z):
    return 1.0 / (1.0 + jnp.exp(-z))


def _silu(z):
    return z * _sigmoid(z)


def _dot(a, b):
    return jnp.dot(a, b, preferred_element_type=F32)


def _dot_nt(a, b):
    return lax.dot_general(a, b, (((1,), (1,)), ((), ())), preferred_element_type=F32)


def _split2(x):
    hi = x.astype(BF16)
    lo = (x - hi.astype(F32)).astype(BF16)
    return hi, lo


def _segsum64(x, ones_bd, split):
    n = x.shape[0]
    xs = jnp.concatenate([x[:, MXU_DIM * g:MXU_DIM * (g + 1)] for g in range(N_COLGROUPS)], axis=0)
    if split:
        hi, lo = _split2(xs)
        s = _dot(hi, ones_bd) + _dot(lo, ones_bd)
    else:
        s = _dot(xs.astype(BF16), ones_bd)
    return jnp.concatenate([s[n * g:n * (g + 1)] for g in range(N_COLGROUPS)], axis=1)


def _mod_kernel(c_ref, w_ref, b_ref, o_ref):
    s = _silu(c_ref[...])
    o_ref[0] = jnp.dot(s, w_ref[0], preferred_element_type=F32,
                       precision=lax.Precision.HIGHEST) + b_ref[0]


def _adaln_mod(c, ada_w, ada_b):
    L = ada_w.shape[0]
    B = c.shape[0]
    c_rows = jnp.pad(c, ((0, MOD_ROWS - B), (0, 0)))
    nj = 3
    return pl.pallas_call(
        _mod_kernel,
        grid=(L, nj),
        in_specs=[pl.BlockSpec((MOD_ROWS, D_MODEL), lambda l, j: (0, 0)),
                  pl.BlockSpec((1, D_MODEL, D_MODEL), lambda l, j: (l, 0, j)),
                  pl.BlockSpec((1, 1, D_MODEL), lambda l, j: (l, 0, j))],
        out_specs=pl.BlockSpec((1, MOD_ROWS, D_MODEL), lambda l, j: (l, 0, j)),
        out_shape=jax.ShapeDtypeStruct((L, MOD_ROWS, 3 * D_MODEL), F32),
        compiler_params=_cparams(("parallel", "parallel")),
        name="adaln_mod",
    )(c_rows, ada_w, ada_b.reshape(L, 1, 3 * D_MODEL))


def _t5_bucket(dist):
    max_exact = NUM_BUCKETS // 2
    safe = np.maximum(dist, 1).astype(np.float32)
    large = max_exact + (np.log(safe / max_exact) / math.log(MAX_DISTANCE / max_exact)
                         * (NUM_BUCKETS - max_exact)).astype(np.int32)
    large = np.minimum(large, NUM_BUCKETS - 1)
    return np.where(dist < max_exact, dist, large).astype(np.int32)


def _bias_kernel(tab_ref, bucket_ref, o_ref):
    h = pl.program_id(0)
    bk = bucket_ref[0]
    acc = jnp.zeros(bk.shape, F32)
    for b in range(NUM_BUCKETS):
        acc = jnp.where(bk == b, tab_ref[h * NUM_BUCKETS + b], acc)
    o_ref[0] = jnp.where(bk >= 0, acc * LOG2E, NEG_INF)


def _rel_bias(rel_bias):
    n_heads = rel_bias.shape[1]
    qi = np.arange(BLK)[:, None]
    ki = np.arange(2 * BLK)[None, :]
    delta = qi + BLK - ki
    band = (delta >= 0) & (delta <= BLK)
    buckets = np.stack([np.where(band, _t5_bucket(np.maximum(delta, 0) * d), -1)
                        for d in DILATIONS]).astype(np.int32)
    table = rel_bias.T.reshape(-1)
    return pl.pallas_call(
        _bias_kernel,
        grid=(n_heads,),
        in_specs=[pl.BlockSpec(memory_space=pltpu.SMEM),
                  pl.BlockSpec((1, BLK, 2 * BLK), lambda h: (h // HEADS_PER_GROUP, 0, 0))],
        out_specs=pl.BlockSpec((1, BLK, 2 * BLK), lambda h: (h, 0, 0)),
        out_shape=jax.ShapeDtypeStruct((n_heads, BLK, 2 * BLK), F32),
        compiler_params=_cparams(("parallel",)),
        name="rel_bias",
    )(table, jnp.asarray(buckets))


def _proj_kernel(x_ref, mod_ref, g_ref, w_ref, proj_ref, h_ref):
    @pl.when(pl.program_id(2) == 0)
    def _():
        x = x_ref[0]
        ms = jnp.mean(x * x, axis=-1, keepdims=True)
        y = x * lax.rsqrt(ms + RMS_EPS) * g_ref[...]
        shift = mod_ref[0, :, 0:D_MODEL]
        scale = mod_ref[0, :, D_MODEL:2 * D_MODEL]
        h_ref[0] = (y * (1.0 + scale) + shift).astype(BF16)

    proj_ref[0] = _dot(h_ref[0], w_ref[...]).astype(BF16)


def _layer_block(arr, layer):
    tail = (0,) * (arr.ndim - 1)
    return pl.BlockSpec((None,) + arr.shape[1:], lambda *_: (layer,) + tail)


def _norm_proj(x, mod, norm_g, w_main, layer, tm=1024, tn=4096):
    B, S, D = x.shape
    N = w_main.shape[2]
    return pl.pallas_call(
        _proj_kernel,
        grid=(B, S // tm, N // tn),
        in_specs=[pl.BlockSpec((1, tm, D), lambda b, i, j: (b, i, 0)),
                  pl.BlockSpec((1, 1, 3 * D), lambda b, i, j: (layer * MOD_ROWS + b, 0, 0)),
                  _layer_block(norm_g, layer),
                  pl.BlockSpec((None, D, tn), lambda b, i, j: (layer, 0, j))],
        out_specs=[pl.BlockSpec((1, tm, tn), lambda b, i, j: (b, i, j)),
                   pl.BlockSpec((1, tm, D), lambda b, i, j: (b, i, 0))],
        out_shape=[jax.ShapeDtypeStruct((B, S, N), BF16),
                   jax.ShapeDtypeStruct((B, S, D), BF16)],
        compiler_params=_cparams(("parallel", "parallel", "arbitrary")),
        name="norm_proj",
    )(x, mod, norm_g, w_main)


GATHER_ROWS = MXU_DIM


def _group_proj_kernel(h_ref, perm_ref, w_ref, o_ref, *, d):
    tm = h_ref.shape[1]
    per_res = GATHER_ROWS // d
    perm = perm_ref[...]
    for ck in range(tm // GATHER_ROWS):
        r0 = ck * GATHER_ROWS
        hp = _dot(perm, h_ref[0, r0:r0 + GATHER_ROWS, :]).astype(BF16)
        res = _dot(hp, w_ref[...]).astype(BF16)
        for r in range(d):
            o_ref[0, r, ck * per_res:(ck + 1) * per_res, :] = res[r * per_res:(r + 1) * per_res]


def _group_proj(h, w_group, d, layer, tm=1024, tn=GROUP_WIDTH):
    B, S, D = h.shape
    N = w_group.shape[2]
    dst = np.arange(GATHER_ROWS)
    src = (dst % (GATHER_ROWS // d)) * d + dst // (GATHER_ROWS // d)
    perm = jnp.asarray(src[:, None] == np.arange(GATHER_ROWS)[None, :], BF16)
    return pl.pallas_call(
        functools.partial(_group_proj_kernel, d=d),
        grid=(B, S // tm, N // tn),
        in_specs=[pl.BlockSpec((1, tm, D), lambda b, i, j: (b, i, 0)),
                  pl.BlockSpec((GATHER_ROWS, GATHER_ROWS), lambda b, i, j: (0, 0)),
                  pl.BlockSpec((None, D, tn), lambda b, i, j: (layer, 0, j))],
        out_specs=pl.BlockSpec((1, d, tm // d, tn), lambda b, i, j: (b, 0, i, j)),
        out_shape=jax.ShapeDtypeStruct((B, d, S // d, N), BF16),
        compiler_params=_cparams(("parallel", "parallel", "parallel")),
        name=f"group_proj_d{d}",
    )(h, perm, w_group)


PV_MU_R, PV_MU_K, PV_MU_V, PV_W0, PV_A0, PV_KK, PV_KA, PV_V0 = range(8)
LORA_LANES = 256


def _pack_lora(paths):
    n_layers, d_model, _ = paths[0][1].shape
    used = sum(down.shape[2] for _, down, _ in paths)
    pad = jnp.zeros((n_layers, d_model, LORA_LANES - used), F32)
    keep = jnp.concatenate([(1.0 - mu)[:, :, None] * down for mu, down, _ in paths] + [pad], axis=2)
    shifted = jnp.concatenate([mu[:, :, None] * down for mu, down, _ in paths] + [pad], axis=2)
    ups, lane = [], 0
    for _, down, up in paths:
        rank = down.shape[2]
        ups.append(jnp.pad(up, ((0, 0), (lane, LORA_LANES - lane - rank), (0, 0))).astype(BF16))
        lane += rank
    return [jnp.concatenate([keep, shifted], axis=2).astype(BF16)] + ups


def _first_layer_blank(arr):
    return jnp.concatenate([jnp.zeros((1,) + arr.shape[1:], arr.dtype), arr], axis=0)
PV_ROWS = 16
PREV_ROWS = 16


def _shift_rows(t, prev_last):
    rolled = pltpu.roll(t, 1, axis=0)
    row = lax.broadcasted_iota(jnp.int32, t.shape, 0)
    return jnp.where(row == 0, prev_last, rolled)


def _rprep_kernel(*refs, has_vres):
    if has_vres:
        (h_ref, hp_ref, pr_ref, prp_ref, pk_ref, pkp_ref, pvv_ref, pvp_ref, vf_ref, pvec_ref,
         wd_ref, uw_ref, ua_ref, uv_ref, ones_ref, tril_ref,
         r_out, cum_out, k_out, v_out, a_out, b_out) = refs
    else:
        (h_ref, hp_ref, pr_ref, prp_ref, pk_ref, pkp_ref, pvv_ref, pvp_ref, pvec_ref,
         wd_ref, uw_ref, ua_ref, ones_ref, tril_ref,
         r_out, cum_out, k_out, v_out, a_out, b_out) = refs

    not_first = (pl.program_id(1) > 0).astype(F32)

    def prm(i):
        return pvec_ref[i:i + 1, :]

    def lerp_shift(cur_ref, prev_ref, mu):
        t = cur_ref[0].astype(F32)
        last = prev_ref[0, PREV_ROWS - 1:PREV_ROWS, :].astype(F32)
        return t + (_shift_rows(t, last * not_first) - t) * mu

    r = lerp_shift(pr_ref, prp_ref, prm(PV_MU_R))
    k = lerp_shift(pk_ref, pkp_ref, prm(PV_MU_K))
    v = lerp_shift(pvv_ref, pvp_ref, prm(PV_MU_V))

    wd = wd_ref[...]
    z2 = _dot(h_ref[0], wd)
    z_prev = _dot(hp_ref[0], wd[:, LORA_LANES:])[PREV_ROWS - 1:PREV_ROWS, :]
    z = z2[:, 0:LORA_LANES] + _shift_rows(z2[:, LORA_LANES:], z_prev * not_first)
    zb = z.astype(BF16)

    zw = prm(PV_W0) + _dot(jnp.tanh(z).astype(BF16), uw_ref[...])
    w = jnp.minimum(zw, 0.0) - jnp.log(1.0 + jnp.exp(-jnp.abs(zw))) - 0.5
    lw = -jnp.exp(w)
    hi, lo = _split2(lw)
    tril = tril_ref[...]
    for i in range(lw.shape[0] // MXU_DIM):
        blk = slice(MXU_DIM * i, MXU_DIM * (i + 1))
        cum_out[0, blk, :] = _dot(tril, hi[blk]) + _dot(tril, lo[blk])
    a = _sigmoid(prm(PV_A0) + _dot(zb, ua_ref[...]))
    if has_vres:
        mix = _sigmoid(prm(PV_V0) + _dot(zb, uv_ref[...]))
        v = v + (vf_ref[0].astype(F32) - v) * mix

    kk = k * prm(PV_KK)
    ss = _segsum64(kk * kk, ones_ref[...], split=False)
    kk = kk * lax.rsqrt(jnp.maximum(ss, 1e-24))
    r_out[0] = r.astype(BF16)
    k_out[0] = (k * (1.0 + (a - 1.0) * prm(PV_KA))).astype(BF16)
    v_out[0] = v.astype(BF16)
    a_out[0] = (-kk).astype(BF16)
    b_out[0] = (kk * a).astype(BF16)


def _rwkv_prep(h, proj, v_first, pvec, lora, ones_bd, layer, tr=512):
    B, S, D = h.shape
    has_vres = v_first is not None
    rpb = tr // PREV_ROWS
    t = np.arange(MXU_DIM)
    tril_bd =jnp.asarray((t[None, :] <= t[:, None]) & (t[None, :] // CHUNK == t[:, None] // CHUNK), BF16)

    def cur(c):
        return pl.BlockSpec((1, tr, R_WIDTH), lambda b, i: (b, i, c))

    def prev(c):
        return pl.BlockSpec((1, PREV_ROWS, R_WIDTH), lambda b, i: (b, jnp.maximum(i * rpb - 1, 0), c))

    def full(arr):
        return pl.BlockSpec(arr.shape, lambda b, i: (0,) * arr.ndim)

    in_specs = [cur(0), prev(0)]
    args = [h, h]
    for c in (COL_R, COL_K, COL_V):
        in_specs += [cur(c // R_WIDTH), prev(c // R_WIDTH)]
        args += [proj, proj]
    if has_vres:
        in_specs.append(cur(0))
        args.append(v_first)
    for per_layer in [pvec] + list(lora):
        in_specs.append(_layer_block(per_layer, layer))
        args.append(per_layer)
    for const in (ones_bd, tril_bd):
        in_specs.append(full(const))
        args.append(const)
    out = [jax.ShapeDtypeStruct((B, S, R_WIDTH), F32 if n == 1 else BF16) for n in range(6)]
    return pl.pallas_call(
        functools.partial(_rprep_kernel, has_vres=has_vres),
        grid=(B, S // tr),
        in_specs=in_specs,
        out_specs=[cur(0)] * 6,
        out_shape=out,
        compiler_params=_cparams(("parallel", "parallel")),
        name="rwkv_prep",
    )(*args)


def _interleave(*gens):
    live = list(gens)
    while live:
        for gen in list(live):
            if next(gen, StopIteration) is StopIteration:
                live.remove(gen)
        yield


def _scan_stages(r_ref, cum_ref, k_ref, v_ref, a_ref, b_ref, ga_ref, vec_ref, ones_ref,
                 y_ref, s_ref, *, nb, tt):
    C = CHUNK

    @pl.when(pl.program_id(0) == 0)
    def _():
        s_ref[...] = jnp.zeros(s_ref.shape, F32)

    row = lax.broadcasted_iota(jnp.int32, (C, MXU_DIM), 0)
    lane = lax.broadcasted_iota(jnp.int32, (C, MXU_DIM), 1)
    col = lane & (HEAD_DIM - 1)
    lhead = lane >> 6
    strict = col < row
    incl = col <= row
    eye = (col == row).astype(F32)
    head_masks = [lhead == hh for hh in range(HEADS_PER_TILE)]

    def bdrows(x):
        return jnp.concatenate([jnp.where(m, x, 0.0) for m in head_masks], axis=0).astype(BF16)

    def diag_blocks(full):
        acc = jnp.where(head_masks[0], full[0:C], 0.0)
        for hh in range(1, HEADS_PER_TILE):
            acc = acc + jnp.where(head_masks[hh], full[C * hh:C * (hh + 1)], 0.0)
        return acc

    row_full = lax.broadcasted_iota(jnp.int32, (C, R_WIDTH), 0)
    ones_bd = ones_ref[...]
    r_k = vec_ref[0:1, :]
    ln_g = vec_ref[1:2, :]
    ln_b = vec_ref[2:3, :]

    chains = [(bi, g) for bi in range(nb) for g in range(N_COLGROUPS)]
    insts = [(ck, bi, g) for ck in range(SCAN_CHUNKS) for bi, g in chains]


    def load(gi):
        rows = [slice((gi * SCAN_CHUNKS + ck) * C, (gi * SCAN_CHUNKS + ck + 1) * C) for ck in range(SCAN_CHUNKS)]
        ops = {}
        for ck in range(SCAN_CHUNKS):
            for bi in range(nb):
                cum = cum_ref[bi, rows[ck], :]
                r = r_ref[bi, rows[ck], :].astype(F32)
                k = k_ref[bi, rows[ck], :].astype(F32)
                v = v_ref[bi, rows[ck], :].astype(F32)
                a = a_ref[bi, rows[ck], :].astype(F32)
                b = b_ref[bi, rows[ck], :].astype(F32)
                total = cum[C - 1:C, :]
                p_in = jnp.exp(cum)
                p_inv = jnp.exp(-cum)
                p_rest = jnp.exp(total - cum)
                p_before = jnp.where(row_full == 0, 1.0, pltpu.roll(p_in, 1, axis=0))
                ops[ck, bi] = dict(r=r, k=k, v=v, a_t=a * p_before, r_t=r * p_in, b_t=b * p_inv,
                                   k_t=k * p_inv, bp=b * p_rest, kp=k * p_rest, p_all=jnp.exp(total))
        return dict(rows=rows, ops=ops)

    def part(ctx, name, ck, bi, g):
        return ctx["ops"][ck, bi][name][:, MXU_DIM * g:MXU_DIM * (g + 1)]

    def independent(ctx):
        res = [_dot_nt(jnp.concatenate([part(ctx, "a_t", *i), part(ctx, "r_t", *i)], axis=0).astype(BF16),
                       jnp.concatenate([bdrows(part(ctx, "b_t", *i)), bdrows(part(ctx, "k_t", *i))], axis=0))
               for i in insts]
        yield
        a_ab = [jnp.where(strict, x[0:C, 0:MXU_DIM], 0.0) for x in res]
        a_ak = [jnp.where(strict, x[0:C, MXU_DIM:], 0.0) for x in res]
        ctx["a_rb"] = [jnp.where(incl, x[C:, 0:MXU_DIM], 0.0).astype(BF16) for x in res]
        a_rk = [jnp.where(incl, x[C:, MXU_DIM:], 0.0) for x in res]

        pw = [_dot(x.astype(BF16), bdrows(x)) for x in a_ab]
        tinv = [eye + x for x in a_ab]
        yield
        for _ in range(4):
            both = [_dot(jnp.concatenate([p, t], axis=0).astype(BF16), bdrows(p)) for p, t in zip(pw, tinv)]
            tinv = [t + x[C:] for t, x in zip(tinv, both)]
            pw = [x[0:C] for x in both]
            yield
        tinv = [t + _dot(t.astype(BF16), bdrows(p)) for p, t in zip(pw, tinv)]
        yield
        tax = [_dot(t.astype(BF16), jnp.concatenate([bdrows(part(ctx, "a_t", *i)), bdrows(x)], axis=1))
               for t, x, i in zip(tinv, a_ak, insts)]
        yield
        ctx["from_v"] = [_dot(jnp.concatenate([x[:, MXU_DIM:], ark], axis=0).astype(BF16),
                              bdrows(part(ctx, "v", *i))) for x, ark, i in zip(tax, a_rk, insts)]
        ctx["tax"] = tax
        yield

    def dependent(ctx, carried):
        tax, from_v, a_rb = ctx["tax"], ctx["from_v"], ctx["a_rb"]
        st = carried["st"]
        y = {}
        for ck in range(SCAN_CHUNKS):
            sel = range(ck * len(chains), (ck + 1) * len(chains))
            from_state = [_dot_nt(jnp.concatenate([tax[n][:, 0:MXU_DIM], part(ctx, "r_t", *insts[n])],
                                                  axis=0).astype(BF16), bdrows(s))
                          for n, s in zip(sel, st)]
            yield
            u = [x[0:C] + from_v[n][0:C] for x, n in zip(from_state, sel)]
            for x, n, uu in zip(from_state, sel, u):
                y[insts[n]] = x[C:] + from_v[n][C:] + _dot(a_rb[n], bdrows(uu))
            upd = [_dot(jnp.concatenate([uu, part(ctx, "v", *insts[n])], axis=0).T.astype(BF16),
                        jnp.concatenate([part(ctx, "bp", *insts[n]), part(ctx, "kp", *insts[n])],
                                        axis=0).astype(BF16))
                   for uu, n in zip(u, sel)]
            yield
            st = [s_old * part(ctx, "p_all", *insts[n]) + diag_blocks(x) for s_old, x, n in zip(st, upd, sel)]
        carried["st"] = st

        for ck in range(SCAN_CHUNKS):
            for bi in range(nb):
                p = ctx["ops"][ck, bi]
                yc = jnp.concatenate([y[ck, bi, g] for g in range(N_COLGROUPS)], axis=1)
                mean = _segsum64(yc, ones_bd, split=True) * (1.0 / HEAD_DIM)
                yield
                yd = yc - mean
                var = _segsum64(yd * yd, ones_bd, split=False) * (1.0 / HEAD_DIM)
                yn = yd * lax.rsqrt(var + GN_EPS) * ln_g + ln_b
                bonus = _segsum64(p["r"] * p["k"] * r_k, ones_bd, split=False) * p["v"]
                rows = ctx["rows"][ck]
                y_ref[bi, rows, :] = ((yn + bonus) * _silu(ga_ref[bi, rows, :].astype(F32))).astype(BF16)
                yield

    carried = {"st": [s_ref[bi * N_COLGROUPS + g] for bi, g in chains]}
    n_groups = tt // (C * SCAN_CHUNKS)
    ctx = load(0)
    yield from independent(ctx)
    for gi in range(1, n_groups):
        nxt = load(gi)
        yield from _interleave(dependent(ctx, carried), independent(nxt))
        ctx = nxt
    yield from dependent(ctx, carried)
    for (bi, g), s_new in zip(chains, carried["st"]):
        s_ref[bi * N_COLGROUPS + g] = s_new


ATT_TILE = 2048
ATT_UNROLL = (5, 6, 8)


def _attn_stages(q_refs, k_refs, v_refs, kp_refs, vp_refs, gb_ref, bias_ref, y_ref, o_refs, l_refs,
                 is_first):
    prev_limit = jnp.where(is_first, BLK, 0)
    ki = lax.broadcasted_iota(jnp.int32, (2 * BLK, 2 * BLK), 1)
    head0 = lax.broadcasted_iota(jnp.int32, (BLK, LANES), 1) < HEAD_DIM
    ones_cols = jnp.ones((2 * BLK, LANES), BF16)
    zero = jnp.zeros((BLK, LANES), BF16)

    def process(blocks):
        q2s, kws, vws, bias2s, stores = [], [], [], [], []
        for g, sub, res in blocks:
            d = DILATIONS[g]
            base = sub * (BLK * d) + res
            q = q_refs[g][0, res, sub * BLK:(sub + 1) * BLK, :]
            q2s.append(jnp.concatenate([jnp.where(head0, q, zero), jnp.where(head0, zero, q)], axis=0))
            if sub == 0:
                kw = jnp.concatenate([kp_refs[g][0, res], k_refs[g][0, res, 0:BLK, :]], axis=0)
                vw = jnp.concatenate([vp_refs[g][0, res], v_refs[g][0, res, 0:BLK, :]], axis=0)
            else:
                kw = k_refs[g][0, res, (sub - 1) * BLK:(sub + 1) * BLK, :]
                vw = v_refs[g][0, res, (sub - 1) * BLK:(sub + 1) * BLK, :]
            kws.append(kw)
            vws.append(jnp.concatenate([vw, ones_cols], axis=1))
            bias2s.append(bias_ref[g, 0].reshape(2 * BLK, 2 * BLK))
            stores.append((g, pl.ds(base, BLK) if d == 1 else pl.ds(base, BLK, stride=d)))
        logits = [jnp.where(bias2 > 0.5 * NEG_INF, _dot_nt(q2, kw) + bias2, NEG_INF)
                  for q2, kw, bias2 in zip(q2s, kws, bias2s)]
        logits = [jnp.where(ki < prev_limit, NEG_INF, x) if blk[1] == 0 else x
                  for x, blk in zip(logits, blocks)]
        yield
        ms = [jnp.max(x, axis=-1, keepdims=True) for x in logits]
        ps = [jnp.exp2(x - m).astype(BF16) for x, m in zip(logits, ms)]
        pvs = [_dot(p, vw) for p, vw in zip(ps, vws)]
        yield
        for (g, rows), pv, m in zip(stores, pvs, ms):
            num = jnp.where(head0, pv[0:BLK, 0:LANES], pv[BLK:, 0:LANES])
            den = jnp.where(head0, pv[0:BLK, LANES:], pv[BLK:, LANES:])
            o_refs[g][rows, :] = num / den
            l_refs[g][rows, :] = jnp.where(head0, m[0:BLK], m[BLK:]) + jnp.log2(den)

    for g, d in enumerate(DILATIONS):
        blocks = [(g, sub, res) for sub in range(ATT_TILE // (BLK * d)) for res in range(d)]
        for n in range(0, len(blocks), ATT_UNROLL[g]):
            yield from process(blocks[n:n + ATT_UNROLL[g]])

    l0, l1, l2 = l_refs[0][...], l_refs[1][...], l_refs[2][...]
    m = jnp.maximum(jnp.maximum(l0, l1), l2)
    w0, w1, w2 = jnp.exp2(l0 - m), jnp.exp2(l1 - m), jnp.exp2(l2 - m)
    y = (w0 * o_refs[0][...] + w1 * o_refs[1][...] + w2 * o_refs[2][...]) / (w0 + w1 + w2)
    y_ref[0] = (y * _silu(gb_ref[0].astype(F32))).astype(BF16)
    yield


N_SCAN_REFS = 9
N_ATTN_REFS = 17
MIXERS_VMEM_LIMIT = 62 * 1024 * 1024


def _mixers_kernel(*refs, nb, tt, tiles_per_seq, n_tiles):
    scan_in = refs[:N_SCAN_REFS]
    attn_in = refs[N_SCAN_REFS:N_SCAN_REFS + N_ATTN_REFS]
    ya_ref, yb_ref, s_ref = refs[N_SCAN_REFS + N_ATTN_REFS:N_SCAN_REFS + N_ATTN_REFS + 3]
    scratch = refs[N_SCAN_REFS + N_ATTN_REFS + 3:]
    is_first = ((pl.program_id(0) % n_tiles) % tiles_per_seq) == 0
    scan = _scan_stages(*scan_in, ya_ref, s_ref, nb=nb, tt=tt)
    attn = _attn_stages(attn_in[0:3], attn_in[3:6], attn_in[6:9], attn_in[9:12], attn_in[12:15],
                        attn_in[15], attn_in[16], yb_ref, scratch[0:3], scratch[3:6], is_first)
    for _ in _interleave(scan, attn):
        pass


def _mixers(r, cum, k, v, a, b, main, groups, vec, ones_bd, bias5, layer):
    B, S, W = r.shape
    n_pairs = HEADS_PER_GROUP // 2
    tiles_per_seq = S // ATT_TILE
    n_tiles = B * tiles_per_seq
    n_steps = n_pairs * n_tiles
    tt = S // n_steps
    assert tt * n_steps == S and tt % (CHUNK * SCAN_CHUNKS) == 0

    scan_spec = pl.BlockSpec((B, tt, W), lambda t: (0, t, 0))

    def full(arr):
        return pl.BlockSpec(arr.shape, lambda t: (0,) * arr.ndim)

    scan_specs = [scan_spec] * 6 + [pl.BlockSpec((B, tt, W), lambda t: (0, t, COL_GA // W)),
                                    _layer_block(vec, layer), full(ones_bd)]

    arrays = [main.reshape(B, 1, S, MAIN_WIDTH)] + list(groups)
    col_base = [COL_A0 // LANES, 0, 0]

    def where(t):
        tile = t % n_tiles
        return t // n_tiles, tile // tiles_per_seq, tile % tiles_per_seq

    def cur(g, part):
        d = DILATIONS[g]
        c0 = col_base[g] + part * (A_OUT_WIDTH // LANES)

        def index(t):
            hp, bi, ti = where(t)
            return bi, 0, ti, c0 + hp
        return pl.BlockSpec((1, d, ATT_TILE // d, LANES), index)

    def prev(g, part):
        d = DILATIONS[g]
        c0 = col_base[g] + part * (A_OUT_WIDTH // LANES)
        rb = ATT_TILE // (BLK * d)

        def index(t):
            hp, bi, ti = where(t)
            return bi, 0, jnp.maximum(ti * rb - 1, 0), c0 + hp
        return pl.BlockSpec((1, d, BLK, LANES), index)

    def tile(col0):
        def index(t):
            hp, bi, ti = where(t)
            return bi, ti, col0 // LANES + hp
        return pl.BlockSpec((1, ATT_TILE, LANES), index)

    attn_specs = ([cur(g, 0) for g in range(N_GROUPS)] + [cur(g, 1) for g in range(N_GROUPS)]
                  + [cur(g, 2) for g in range(N_GROUPS)]
                  + [prev(g, 1) for g in range(N_GROUPS)] + [prev(g, 2) for g in range(N_GROUPS)]
                  + [tile(COL_GB),
                     pl.BlockSpec((N_GROUPS, 1, 2, BLK, 2 * BLK), lambda t: (0, t // n_tiles, 0, 0, 0))])
    assert len(scan_specs) == N_SCAN_REFS and len(attn_specs) == N_ATTN_REFS
    return pl.pallas_call(
        functools.partial(_mixers_kernel, nb=B, tt=tt, tiles_per_seq=tiles_per_seq, n_tiles=n_tiles),
        grid=(n_steps,),
        in_specs=scan_specs + attn_specs,
        out_specs=[scan_spec, tile(0)],
        out_shape=[jax.ShapeDtypeStruct((B, S, W), BF16), jax.ShapeDtypeStruct((B, S, A_OUT_WIDTH), BF16)],
        scratch_shapes=([pltpu.VMEM((B * N_COLGROUPS, HEAD_DIM, MXU_DIM), F32)]
                        + [pltpu.VMEM((ATT_TILE, LANES), F32)] * 6),
        compiler_params=_cparams(("arbitrary",), MIXERS_VMEM_LIMIT),
        name="mixers",
    )(r, cum, k, v, a, b, main, vec, ones_bd, *(arrays * 5), main, bias5)


def _merge_kernel(ya_ref, yb_ref, ma_ref, mb_ref, x_ref, mod_ref, wa_ref, wb_ref, wo_ref, fg_ref,
                  o_ref, *, final_norm):
    pa = _dot(ya_ref[0], wa_ref[...])
    pb = _dot(yb_ref[0], wb_ref[...])
    merged = _sigmoid(ma_ref[0].astype(F32)) * pa + _sigmoid(mb_ref[0].astype(F32)) * pb
    out = _dot(merged.astype(BF16), wo_ref[...])
    gate = mod_ref[0, :, 2 * D_MODEL:3 * D_MODEL]
    xn = x_ref[0] + gate * out
    if final_norm:
        ms = jnp.mean(xn * xn, axis=-1, keepdims=True)
        xn = xn * lax.rsqrt(ms + RMS_EPS) * fg_ref[...]
    o_ref[0] = xn


def _merge(ya, yb, proj, x, mod, wa, wb, wo, final_g, final_norm, layer, tm=1024):
    B, S, D = x.shape

    def rows(width, c):
        return pl.BlockSpec((1, tm, width), lambda b, i: (b, i, c))

    def full(arr):
        return pl.BlockSpec(arr.shape, lambda b, i: (0,) * arr.ndim)

    return pl.pallas_call(
        functools.partial(_merge_kernel, final_norm=final_norm),
        grid=(B, S // tm),
        in_specs=[rows(R_WIDTH, 0), rows(A_OUT_WIDTH, 0),
                  rows(D, COL_MA // D), rows(D, COL_MB // D), rows(D, 0),
                  pl.BlockSpec((1, 1, 3 * D), lambda b, i: (layer * MOD_ROWS + b, 0, 0)),
                  _layer_block(wa, layer), _layer_block(wb, layer), _layer_block(wo, layer), full(final_g)],
        out_specs=rows(D, 0),
        out_shape=jax.ShapeDtypeStruct((B, S, D), F32),
        compiler_params=_cparams(("parallel", "parallel")),
        name="merge",
    )(ya, yb, proj, proj, x, mod, wa, wb, wo, final_g)


def _segment_ones():
    idx = np.arange(MXU_DIM)
    return jnp.asarray(idx[:, None] // HEAD_DIM == idx[None, :] // HEAD_DIM, BF16)


def kernel(x, c, norm_g, ada_w, ada_b, w_in, rwkv_mu_rkv, rwkv_mu_wa, rwkv_w0, rwkv_w1, rwkv_w2, rwkv_a0, rwkv_a1, rwkv_a2, rwkv_k_k, rwkv_k_a, rwkv_r_k, rwkv_ln_g, rwkv_ln_b, rwkv_mu_v, rwkv_v0, rwkv_v1, rwkv_v2, w_branch_a, w_branch_b, w_out, rel_bias, final_g):
    B, S, D = x.shape
    assert D == D_MODEL and S % ATT_TILE == 0 and w_in.shape[2] == PROJ_WIDTH
    ones_bd = _segment_ones()
    mod = _adaln_mod(c, ada_w, ada_b).reshape(DEPTH * MOD_ROWS, 1, 3 * D)
    bias = _rel_bias(rel_bias).reshape(N_GROUPS, HEADS_PER_GROUP // 2, 2, BLK, 2 * BLK)

    def cols(start, width):
        return w_in[:, :, start:start + width]

    def group_cols(g):
        return [cols(W_AQ + A_OUT_WIDTH * g, A_OUT_WIDTH) * (LOG2E / math.sqrt(HEAD_DIM)),
                cols(W_AK + A_OUT_WIDTH * g, A_OUT_WIDTH), cols(W_AV + A_OUT_WIDTH * g, A_OUT_WIDTH)]

    w_main = jnp.concatenate(
        [cols(W_R, 4 * R_WIDTH), cols(W_MA, 2 * D_MODEL), cols(W_GB, A_OUT_WIDTH)] + group_cols(0),
        axis=2).astype(BF16)
    w_groups = [jnp.concatenate(group_cols(g), axis=2).astype(BF16) for g in range(1, N_GROUPS)]
    zeros_rows = jnp.zeros((DEPTH, D), F32)
    pvec = jnp.stack([rwkv_mu_rkv[:, 0], rwkv_mu_rkv[:, 1], rwkv_mu_rkv[:, 2], rwkv_w0, rwkv_a0, rwkv_k_k,
                      rwkv_k_a, _first_layer_blank(rwkv_v0)] + [zeros_rows] * (PV_ROWS - 8), axis=1)
    lora = _pack_lora([(rwkv_mu_wa[:, 0], rwkv_w1, rwkv_w2), (rwkv_mu_wa[:, 1], rwkv_a1, rwkv_a2),
                       (_first_layer_blank(rwkv_mu_v), _first_layer_blank(rwkv_v1),
                        _first_layer_blank(rwkv_v2))])
    vec = jnp.stack([rwkv_r_k.reshape(DEPTH, -1), rwkv_ln_g, rwkv_ln_b] + [zeros_rows] * 5, axis=1)
    norm_g3 = norm_g.reshape(DEPTH, 1, D)
    wa, wb, wo = w_branch_a.astype(BF16), w_branch_b.astype(BF16), w_out.astype(BF16)

    v_first = None
    for i in range(DEPTH):
        proj, h = _norm_proj(x, mod, norm_g3, w_main, i)
        groups = [_group_proj(h, w_groups[g - 1], DILATIONS[g], i) for g in range(1, N_GROUPS)]
        r, cum, k, v, a, b = _rwkv_prep(h, proj, v_first, pvec, lora if i > 0 else lora[:3], ones_bd, i)
        if i == 0:
            v_first = v
        y_a, y_b = _mixers(r, cum, k, v, a, b, proj, groups, vec, ones_bd, bias, i)
        x = _merge(y_a, y_b, proj, x, mod, wa, wb, wo, final_g.reshape(1, D),
                   final_norm=(i == DEPTH - 1), layer=i)
    return x
```

```python
import functools
import math

import numpy as np
import jax
import jax.numpy as jnp
from jax import lax
from jax.experimental import pallas as pl
from jax.experimental.pallas import tpu as pltpu

F32 = jnp.float32
BF16 = jnp.bfloat16

D_MODEL = 1024
DEPTH = 2
HEAD_DIM = 64
R_WIDTH = 1024
N_GROUPS = 3
HEADS_PER_GROUP = 8
DILATIONS = (1, 4, 16)
BLK = 128
A_QK_WIDTH = 1536
A_OUT_WIDTH = 512
NUM_BUCKETS = 32
MAX_DISTANCE = 2048
PROJ_WIDTH = 4 * R_WIDTH + 3 * A_QK_WIDTH + A_OUT_WIDTH + 2 * D_MODEL
RMS_EPS = 1e-6
GN_EPS = 64e-5
NEG_INF = -1e30
LOG2E = math.log2(math.e)

LANES = 128
MXU_DIM = 256
HEADS_PER_TILE = MXU_DIM // HEAD_DIM
N_COLGROUPS = R_WIDTH // MXU_DIM
CHUNK = 64
SCAN_CHUNKS = 2

W_R, W_K, W_V, W_GA = 0, 1024, 2048, 3072
W_AQ, W_AK, W_AV = 4096, 5632, 7168
W_GB, W_MA, W_MB = 8704, 9216, 10240
COL_R, COL_K, COL_V, COL_GA, COL_MA, COL_MB, COL_GB, COL_A0 = 0, 1024, 2048, 3072, 4096, 5120, 6144, 6656
MAIN_WIDTH = 8192
GROUP_WIDTH = 3 * A_OUT_WIDTH

VMEM_LIMIT = 56 * 1024 * 1024
MOD_ROWS = 8


def _cparams(sem, vmem_limit=VMEM_LIMIT):
    return pltpu.CompilerParams(dimension_semantics=sem, vmem_limit_bytes=vmem_limit)


def _sigmoid(z):
    return 1.0 / (1.0 + jnp.exp(-z))


def _silu(z):
    return z * _sigmoid(z)


def _dot(a, b):
    return jnp.dot(a, b, preferred_element_type=F32)


def _dot_nt(a, b):
    return lax.dot_general(a, b, (((1,), (1,)), ((), ())), preferred_element_type=F32)


def _split2(x):
    hi = x.astype(BF16)
    lo = (x - hi.astype(F32)).astype(BF16)
    return hi, lo


def _segsum64(x, ones_bd, split):
    n = x.shape[0]
    xs = jnp.concatenate([x[:, MXU_DIM * g:MXU_DIM * (g + 1)] for g in range(N_COLGROUPS)], axis=0)
    if split:
        hi, lo = _split2(xs)
        s = _dot(hi, ones_bd) + _dot(lo, ones_bd)
    else:
        s = _dot(xs.astype(BF16), ones_bd)
    return jnp.concatenate([s[n * g:n * (g + 1)] for g in range(N_COLGROUPS)], axis=1)


def _mod_kernel(c_ref, w_ref, b_ref, o_ref):
    s = _silu(c_ref[...])
    o_ref[0] = jnp.dot(s, w_ref[0], preferred_element_type=F32,
                       precision=lax.Precision.HIGHEST) + b_ref[0]


def _adaln_mod(c, ada_w, ada_b):
    L = ada_w.shape[0]
    B = c.shape[0]
    c_rows = jnp.pad(c, ((0, MOD_ROWS - B), (0, 0)))
    nj = 3
    return pl.pallas_call(
        _mod_kernel,
        grid=(L, nj),
        in_specs=[pl.BlockSpec((MOD_ROWS, D_MODEL), lambda l, j: (0, 0)),
                  pl.BlockSpec((1, D_MODEL, D_MODEL), lambda l, j: (l, 0, j)),
                  pl.BlockSpec((1, 1, D_MODEL), lambda l, j: (l, 0, j))],
        out_specs=pl.BlockSpec((1, MOD_ROWS, D_MODEL), lambda l, j: (l, 0, j)),
        out_shape=jax.ShapeDtypeStruct((L, MOD_ROWS, 3 * D_MODEL), F32),
        compiler_params=_cparams(("parallel", "parallel")),
        name="adaln_mod",
    )(c_rows, ada_w, ada_b.reshape(L, 1, 3 * D_MODEL))


def _t5_bucket(dist):
    max_exact = NUM_BUCKETS // 2
    safe = np.maximum(dist, 1).astype(np.float32)
    large = max_exact + (np.log(safe / max_exact) / math.log(MAX_DISTANCE / max_exact)
                         * (NUM_BUCKETS - max_exact)).astype(np.int32)
    large = np.minimum(large, NUM_BUCKETS - 1)
    return np.where(dist < max_exact, dist, large).astype(np.int32)


def _bias_kernel(tab_ref, bucket_ref, o_ref):
    h = pl.program_id(0)
    bk = bucket_ref[0]
    acc = jnp.zeros(bk.shape, F32)
    for b in range(NUM_BUCKETS):
        acc = jnp.where(bk == b, tab_ref[h * NUM_BUCKETS + b], acc)
    o_ref[0] = jnp.where(bk >= 0, acc * LOG2E, NEG_INF)


def _rel_bias(rel_bias):
    n_heads = rel_bias.shape[1]
    qi = np.arange(BLK)[:, None]
    ki = np.arange(2 * BLK)[None, :]
    delta = qi + BLK - ki
    band = (delta >= 0) & (delta <= BLK)
    buckets = np.stack([np.where(band, _t5_bucket(np.maximum(delta, 0) * d), -1)
                        for d in DILATIONS]).astype(np.int32)
    table = rel_bias.T.reshape(-1)
    return pl.pallas_call(
        _bias_kernel,
        grid=(n_heads,),
        in_specs=[pl.BlockSpec(memory_space=pltpu.SMEM),
                  pl.BlockSpec((1, BLK, 2 * BLK), lambda h: (h // HEADS_PER_GROUP, 0, 0))],
        out_specs=pl.BlockSpec((1, BLK, 2 * BLK), lambda h: (h, 0, 0)),
        out_shape=jax.ShapeDtypeStruct((n_heads, BLK, 2 * BLK), F32),
        compiler_params=_cparams(("parallel",)),
        name="rel_bias",
    )(table, jnp.asarray(buckets))


def _proj_kernel(x_ref, mod_ref, g_ref, w_ref, proj_ref, h_ref):
    @pl.when(pl.program_id(2) == 0)
    def _():
        x = x_ref[0]
        ms = jnp.mean(x * x, axis=-1, keepdims=True)
        y = x * lax.rsqrt(ms + RMS_EPS) * g_ref[...]
        shift = mod_ref[0, :, 0:D_MODEL]
        scale = mod_ref[0, :, D_MODEL:2 * D_MODEL]
        h_ref[0] = (y * (1.0 + scale) + shift).astype(BF16)

    proj_ref[0] = _dot(h_ref[0], w_ref[...]).astype(BF16)


def _layer_block(arr, layer):
    tail = (0,) * (arr.ndim - 1)
    return pl.BlockSpec((None,) + arr.shape[1:], lambda *_: (layer,) + tail)


def _norm_proj(x, mod, norm_g, w_main, layer, tm=1024, tn=4096):
    B, S, D = x.shape
    N = w_main.shape[2]
    return pl.pallas_call(
        _proj_kernel,
        grid=(B, S // tm, N // tn),
        in_specs=[pl.BlockSpec((1, tm, D), lambda b, i, j: (b, i, 0)),
                  pl.BlockSpec((1, 1, 3 * D), lambda b, i, j: (layer * MOD_ROWS + b, 0, 0)),
                  _layer_block(norm_g, layer),
                  pl.BlockSpec((None, D, tn), lambda b, i, j: (layer, 0, j))],
        out_specs=[pl.BlockSpec((1, tm, tn), lambda b, i, j: (b, i, j)),
                   pl.BlockSpec((1, tm, D), lambda b, i, j: (b, i, 0))],
        out_shape=[jax.ShapeDtypeStruct((B, S, N), BF16),
                   jax.ShapeDtypeStruct((B, S, D), BF16)],
        compiler_params=_cparams(("parallel", "parallel", "arbitrary")),
        name="norm_proj",
    )(x, mod, norm_g, w_main)


GATHER_ROWS = MXU_DIM


def _group_proj_stages(h_ref, perm_ref, w_ref, o_ref, d):
    tm = h_ref.shape[1]
    per_res = GATHER_ROWS // d
    perm = perm_ref[...]
    for ck in range(tm // GATHER_ROWS):
        r0 = ck * GATHER_ROWS
        hp = _dot(perm, h_ref[0, r0:r0 + GATHER_ROWS, :]).astype(BF16)
        yield
        res = _dot(hp, w_ref[...]).astype(BF16)
        for r in range(d):
            o_ref[0, r, ck * per_res:(ck + 1) * per_res, :] = res[r * per_res:(r + 1) * per_res]
        yield


def _gather_perm(d):
    dst = np.arange(GATHER_ROWS)
    src = (dst % (GATHER_ROWS // d)) * d + dst // (GATHER_ROWS // d)
    return jnp.asarray(src[:, None] == np.arange(GATHER_ROWS)[None, :], BF16)


PV_MU_R, PV_MU_K, PV_MU_V, PV_W0, PV_A0, PV_KK, PV_KA, PV_V0 = range(8)
LORA_LANES = 256


def _pack_lora(paths):
    n_layers, d_model, _ = paths[0][1].shape
    used = sum(down.shape[2] for _, down, _ in paths)
    pad = jnp.zeros((n_layers, d_model, LORA_LANES - used), F32)
    keep = jnp.concatenate([(1.0 - mu)[:, :, None] * down for mu, down, _ in paths] + [pad], axis=2)
    shifted = jnp.concatenate([mu[:, :, None] * down for mu, down, _ in paths] + [pad], axis=2)
    ups, lane = [], 0
    for _, down, up in paths:
        rank = down.shape[2]
        ups.append(jnp.pad(up, ((0, 0), (lane, LORA_LANES - lane - rank), (0, 0))).astype(BF16))
        lane += rank
    return [jnp.concatenate([keep, shifted], axis=2).astype(BF16)] + ups


def _first_layer_blank(arr):
    return jnp.concatenate([jnp.zeros((1,) + arr.shape[1:], arr.dtype), arr], axis=0)
PV_ROWS = 16
PREV_ROWS = 16


def _shift_rows(t, prev_last):
    rolled = pltpu.roll(t, 1, axis=0)
    row = lax.broadcasted_iota(jnp.int32, t.shape, 0)
    return jnp.where(row == 0, prev_last, rolled)


def _rprep_stages(refs, has_vres):
    if has_vres:
        (h_ref, hp_ref, pr_ref, prp_ref, pk_ref, pkp_ref, pvv_ref, pvp_ref, vf_ref, pvec_ref,
         wd_ref, uw_ref, ua_ref, uv_ref, ones_ref, tril_ref,
         r_out, cum_out, k_out, v_out, a_out, b_out) = refs
    else:
        (h_ref, hp_ref, pr_ref, prp_ref, pk_ref, pkp_ref, pvv_ref, pvp_ref, pvec_ref,
         wd_ref, uw_ref, ua_ref, ones_ref, tril_ref,
         r_out, cum_out, k_out, v_out, a_out, b_out) = refs

    not_first = (pl.program_id(1) > 0).astype(F32)

    def prm(i):
        return pvec_ref[i:i + 1, :]

    def lerp_shift(cur_ref, prev_ref, mu):
        t = cur_ref[0].astype(F32)
        last = prev_ref[0, PREV_ROWS - 1:PREV_ROWS, :].astype(F32)
        return t + (_shift_rows(t, last * not_first) - t) * mu

    r = lerp_shift(pr_ref, prp_ref, prm(PV_MU_R))
    k = lerp_shift(pk_ref, pkp_ref, prm(PV_MU_K))
    v = lerp_shift(pvv_ref, pvp_ref, prm(PV_MU_V))
    yield

    wd = wd_ref[...]
    z2 = _dot(h_ref[0], wd)
    z_prev = _dot(hp_ref[0], wd[:, LORA_LANES:])[PREV_ROWS - 1:PREV_ROWS, :]
    z = z2[:, 0:LORA_LANES] + _shift_rows(z2[:, LORA_LANES:], z_prev * not_first)
    zb = z.astype(BF16)
    yield

    zw = prm(PV_W0) + _dot(jnp.tanh(z).astype(BF16), uw_ref[...])
    w = jnp.minimum(zw, 0.0) - jnp.log(1.0 + jnp.exp(-jnp.abs(zw))) - 0.5
    lw = -jnp.exp(w)
    yield
    hi, lo = _split2(lw)
    tril = tril_ref[...]
    for i in range(lw.shape[0] // MXU_DIM):
        blk = slice(MXU_DIM * i, MXU_DIM * (i + 1))
        cum_out[0, blk, :] = _dot(tril, hi[blk]) + _dot(tril, lo[blk])
    yield
    a = _sigmoid(prm(PV_A0) + _dot(zb, ua_ref[...]))
    if has_vres:
        mix = _sigmoid(prm(PV_V0) + _dot(zb, uv_ref[...]))
        v = v + (vf_ref[0].astype(F32) - v) * mix
    yield

    kk = k * prm(PV_KK)
    ss = _segsum64(kk * kk, ones_ref[...], split=False)
    kk = kk * lax.rsqrt(jnp.maximum(ss, 1e-24))
    yield
    r_out[0] = r.astype(BF16)
    k_out[0] = (k * (1.0 + (a - 1.0) * prm(PV_KA))).astype(BF16)
    v_out[0] = v.astype(BF16)
    a_out[0] = (-kk).astype(BF16)
    b_out[0] = (kk * a).astype(BF16)
    yield


N_GROUP_PROJ = N_GROUPS - 1


def _prep_groups_kernel(*refs, has_vres):
    n_in = len(refs) - 6 - N_GROUP_PROJ - 2 * N_GROUP_PROJ
    prep_in, rest = refs[:n_in], refs[n_in:]
    gp_in, outs = rest[:2 * N_GROUP_PROJ], rest[2 * N_GROUP_PROJ:]
    prep = _rprep_stages(tuple(prep_in) + tuple(outs[:6]), has_vres)
    h_ref = prep_in[0]
    groups = [_group_proj_stages(h_ref, gp_in[2 * n], gp_in[2 * n + 1], outs[6 + n], DILATIONS[n + 1])
              for n in range(N_GROUP_PROJ)]

    def all_groups():
        for gen in groups:
            yield from gen

    for _ in _interleave(prep, all_groups()):
        pass


def _rwkv_prep(h, proj, v_first, pvec, lora, ones_bd, w_groups, layer, tr=512):
    B, S, D = h.shape
    has_vres = v_first is not None
    rpb = tr // PREV_ROWS
    t = np.arange(MXU_DIM)
    tril_bd =jnp.asarray((t[None, :] <= t[:, None]) & (t[None, :] // CHUNK == t[:, None] // CHUNK), BF16)

    def cur(c):
        return pl.BlockSpec((1, tr, R_WIDTH), lambda b, i: (b, i, c))

    def prev(c):
        return pl.BlockSpec((1, PREV_ROWS, R_WIDTH), lambda b, i: (b, jnp.maximum(i * rpb - 1, 0), c))

    def full(arr):
        return pl.BlockSpec(arr.shape, lambda b, i: (0,) * arr.ndim)

    in_specs = [cur(0), prev(0)]
    args = [h, h]
    for c in (COL_R, COL_K, COL_V):
        in_specs += [cur(c // R_WIDTH), prev(c // R_WIDTH)]
        args += [proj, proj]
    if has_vres:
        in_specs.append(cur(0))
        args.append(v_first)
    for per_layer in [pvec] + list(lora):
        in_specs.append(_layer_block(per_layer, layer))
        args.append(per_layer)
    for const in (ones_bd, tril_bd):
        in_specs.append(full(const))
        args.append(const)
    out = [jax.ShapeDtypeStruct((B, S, R_WIDTH), F32 if n == 1 else BF16) for n in range(6)]
    out_specs = [cur(0)] * 6
    for n, w_group in enumerate(w_groups):
        d = DILATIONS[n + 1]
        width = w_group.shape[2]
        in_specs += [full(_gather_perm(d)), _layer_block(w_group, layer)]
        args += [_gather_perm(d), w_group]
        out.append(jax.ShapeDtypeStruct((B, d, S // d, width), BF16))
        out_specs.append(pl.BlockSpec((1, d, tr // d, width), lambda b, i: (b, 0, i, 0)))
    res = pl.pallas_call(
        functools.partial(_prep_groups_kernel, has_vres=has_vres),
        grid=(B, S // tr),
        in_specs=in_specs,
        out_specs=out_specs,
        out_shape=out,
        compiler_params=_cparams(("parallel", "parallel")),
        name="prep_groups",
    )(*args)
    return res[:6], res[6:]


def _interleave(*gens):
    live = list(gens)
    while live:
        for gen in list(live):
            if next(gen, StopIteration) is StopIteration:
                live.remove(gen)
        yield


def _scan_stages(r_ref, cum_ref, k_ref, v_ref, a_ref, b_ref, ga_ref, vec_ref, ones_ref,
                 y_ref, s_ref, *, nb, tt):
    C = CHUNK

    @pl.when(pl.program_id(0) == 0)
    def _():
        s_ref[...] = jnp.zeros(s_ref.shape, F32)

    row = lax.broadcasted_iota(jnp.int32, (C, MXU_DIM), 0)
    lane = lax.broadcasted_iota(jnp.int32, (C, MXU_DIM), 1)
    col = lane & (HEAD_DIM - 1)
    lhead = lane >> 6
    strict = col < row
    incl = col <= row
    eye = (col == row).astype(F32)
    head_masks = [lhead == hh for hh in range(HEADS_PER_TILE)]

    def bdrows(x):
        return jnp.concatenate([jnp.where(m, x, 0.0) for m in head_masks], axis=0).astype(BF16)

    def diag_blocks(full):
        acc = jnp.where(head_masks[0], full[0:C], 0.0)
        for hh in range(1, HEADS_PER_TILE):
            acc = acc + jnp.where(head_masks[hh], full[C * hh:C * (hh + 1)], 0.0)
        return acc

    row_full = lax.broadcasted_iota(jnp.int32, (C, R_WIDTH), 0)
    ones_bd = ones_ref[...]
    r_k = vec_ref[0:1, :]
    ln_g = vec_ref[1:2, :]
    ln_b = vec_ref[2:3, :]

    chains = [(bi, g) for bi in range(nb) for g in range(N_COLGROUPS)]
    insts = [(ck, bi, g) for ck in range(SCAN_CHUNKS) for bi, g in chains]


    def load(gi):
        rows = [slice((gi * SCAN_CHUNKS + ck) * C, (gi * SCAN_CHUNKS + ck + 1) * C) for ck in range(SCAN_CHUNKS)]
        ops = {}
        for ck in range(SCAN_CHUNKS):
            for bi in range(nb):
                cum = cum_ref[bi, rows[ck], :]
                r = r_ref[bi, rows[ck], :].astype(F32)
                k = k_ref[bi, rows[ck], :].astype(F32)
                v = v_ref[bi, rows[ck], :].astype(F32)
                a = a_ref[bi, rows[ck], :].astype(F32)
                b = b_ref[bi, rows[ck], :].astype(F32)
                total = cum[C - 1:C, :]
                p_in = jnp.exp(cum)
                p_inv = jnp.exp(-cum)
                p_rest = jnp.exp(total - cum)
                p_before = jnp.where(row_full == 0, 1.0, pltpu.roll(p_in, 1, axis=0))
                ops[ck, bi] = dict(r=r, k=k, v=v, a_t=a * p_before, r_t=r * p_in, b_t=b * p_inv,
                                   k_t=k * p_inv, bp=b * p_rest, kp=k * p_rest, p_all=jnp.exp(total))
        return dict(rows=rows, ops=ops)

    def part(ctx, name, ck, bi, g):
        return ctx["ops"][ck, bi][name][:, MXU_DIM * g:MXU_DIM * (g + 1)]

    def independent(ctx):
        res = [_dot_nt(jnp.concatenate([part(ctx, "a_t", *i), part(ctx, "r_t", *i)], axis=0).astype(BF16),
                       jnp.concatenate([bdrows(part(ctx, "b_t", *i)), bdrows(part(ctx, "k_t", *i))], axis=0))
               for i in insts]
        yield
        a_ab = [jnp.where(strict, x[0:C, 0:MXU_DIM], 0.0) for x in res]
        a_ak = [jnp.where(strict, x[0:C, MXU_DIM:], 0.0) for x in res]
        ctx["a_rb"] = [jnp.where(incl, x[C:, 0:MXU_DIM], 0.0).astype(BF16) for x in res]
        a_rk = [jnp.where(incl, x[C:, MXU_DIM:], 0.0) for x in res]

        pw = [_dot(x.astype(BF16), bdrows(x)) for x in a_ab]
        tinv = [eye + x for x in a_ab]
        yield
        for _ in range(4):
            both = [_dot(jnp.concatenate([p, t], axis=0).astype(BF16), bdrows(p)) for p, t in zip(pw, tinv)]
            tinv = [t + x[C:] for t, x in zip(tinv, both)]
            pw = [x[0:C] for x in both]
            yield
        tinv = [t + _dot(t.astype(BF16), bdrows(p)) for p, t in zip(pw, tinv)]
        yield
        tax = [_dot(t.astype(BF16), jnp.concatenate([bdrows(part(ctx, "a_t", *i)), bdrows(x)], axis=1))
               for t, x, i in zip(tinv, a_ak, insts)]
        yield
        ctx["from_v"] = [_dot(jnp.concatenate([x[:, MXU_DIM:], ark], axis=0).astype(BF16),
                              bdrows(part(ctx, "v", *i))) for x, ark, i in zip(tax, a_rk, insts)]
        ctx["tax"] = tax
        yield

    def dependent(ctx, carried):
        tax, from_v, a_rb = ctx["tax"], ctx["from_v"], ctx["a_rb"]
        st = carried["st"]
        y = {}
        for ck in range(SCAN_CHUNKS):
            sel = range(ck * len(chains), (ck + 1) * len(chains))
            from_state = [_dot_nt(jnp.concatenate([tax[n][:, 0:MXU_DIM], part(ctx, "r_t", *insts[n])],
                                                  axis=0).astype(BF16), bdrows(s))
                          for n, s in zip(sel, st)]
            yield
            u = [x[0:C] + from_v[n][0:C] for x, n in zip(from_state, sel)]
            for x, n, uu in zip(from_state, sel, u):
                y[insts[n]] = x[C:] + from_v[n][C:] + _dot(a_rb[n], bdrows(uu))
            upd = [_dot(jnp.concatenate([uu, part(ctx, "v", *insts[n])], axis=0).T.astype(BF16),
                        jnp.concatenate([part(ctx, "bp", *insts[n]), part(ctx, "kp", *insts[n])],
                                        axis=0).astype(BF16))
                   for uu, n in zip(u, sel)]
            yield
            st = [s_old * part(ctx, "p_all", *insts[n]) + diag_blocks(x) for s_old, x, n in zip(st, upd, sel)]
        carried["st"] = st

        for ck in range(SCAN_CHUNKS):
            for bi in range(nb):
                p = ctx["ops"][ck, bi]
                yc = jnp.concatenate([y[ck, bi, g] for g in range(N_COLGROUPS)], axis=1)
                mean = _segsum64(yc, ones_bd, split=True) * (1.0 / HEAD_DIM)
                yield
                yd = yc - mean
                var = _segsum64(yd * yd, ones_bd, split=False) * (1.0 / HEAD_DIM)
                yn = yd * lax.rsqrt(var + GN_EPS) * ln_g + ln_b
                bonus = _segsum64(p["r"] * p["k"] * r_k, ones_bd, split=False) * p["v"]
                rows = ctx["rows"][ck]
                y_ref[bi, rows, :] = ((yn + bonus) * _silu(ga_ref[bi, rows, :].astype(F32))).astype(BF16)
                yield

    carried = {"st": [s_ref[bi * N_COLGROUPS + g] for bi, g in chains]}
    n_groups = tt // (C * SCAN_CHUNKS)
    ctx = load(0)
    yield from independent(ctx)
    for gi in range(1, n_groups):
        nxt = load(gi)
        yield from _interleave(dependent(ctx, carried), independent(nxt))
        ctx = nxt
    yield from dependent(ctx, carried)
    for (bi, g), s_new in zip(chains, carried["st"]):
        s_ref[bi * N_COLGROUPS + g] = s_new


ATT_TILE = 2048
ATT_UNROLL = (5, 6, 8)


def _attn_stages(q_refs, k_refs, v_refs, kp_refs, vp_refs, gb_ref, bias_ref, y_ref, o_refs, l_refs,
                 is_first):
    prev_limit = jnp.where(is_first, BLK, 0)
    ki = lax.broadcasted_iota(jnp.int32, (2 * BLK, 2 * BLK), 1)
    head0 = lax.broadcasted_iota(jnp.int32, (BLK, LANES), 1) < HEAD_DIM
    ones_cols = jnp.ones((2 * BLK, LANES), BF16)
    zero = jnp.zeros((BLK, LANES), BF16)

    def process(blocks):
        q2s, kws, vws, bias2s, stores = [], [], [], [], []
        for g, sub, res in blocks:
            d = DILATIONS[g]
            base = sub * (BLK * d) + res
            q = q_refs[g][0, res, sub * BLK:(sub + 1) * BLK, :]
            q2s.append(jnp.concatenate([jnp.where(head0, q, zero), jnp.where(head0, zero, q)], axis=0))
            if sub == 0:
                kw = jnp.concatenate([kp_refs[g][0, res], k_refs[g][0, res, 0:BLK, :]], axis=0)
                vw = jnp.concatenate([vp_refs[g][0, res], v_refs[g][0, res, 0:BLK, :]], axis=0)
            else:
                kw = k_refs[g][0, res, (sub - 1) * BLK:(sub + 1) * BLK, :]
                vw = v_refs[g][0, res, (sub - 1) * BLK:(sub + 1) * BLK, :]
            kws.append(kw)
            vws.append(jnp.concatenate([vw, ones_cols], axis=1))
            bias2s.append(bias_ref[g, 0].reshape(2 * BLK, 2 * BLK))
            stores.append((g, pl.ds(base, BLK) if d == 1 else pl.ds(base, BLK, stride=d)))
        logits = [jnp.where(bias2 > 0.5 * NEG_INF, _dot_nt(q2, kw) + bias2, NEG_INF)
                  for q2, kw, bias2 in zip(q2s, kws, bias2s)]
        logits = [jnp.where(ki < prev_limit, NEG_INF, x) if blk[1] == 0 else x
                  for x, blk in zip(logits, blocks)]
        yield
        ms = [jnp.max(x, axis=-1, keepdims=True) for x in logits]
        ps = [jnp.exp2(x - m).astype(BF16) for x, m in zip(logits, ms)]
        pvs = [_dot(p, vw) for p, vw in zip(ps, vws)]
        yield
        for (g, rows), pv, m in zip(stores, pvs, ms):
            num = jnp.where(head0, pv[0:BLK, 0:LANES], pv[BLK:, 0:LANES])
            den = jnp.where(head0, pv[0:BLK, LANES:], pv[BLK:, LANES:])
            o_refs[g][rows, :] = num / den
            l_refs[g][rows, :] = jnp.where(head0, m[0:BLK], m[BLK:]) + jnp.log2(den)

    for g, d in enumerate(DILATIONS):
        blocks = [(g, sub, res) for sub in range(ATT_TILE // (BLK * d)) for res in range(d)]
        for n in range(0, len(blocks), ATT_UNROLL[g]):
            yield from process(blocks[n:n + ATT_UNROLL[g]])

    l0, l1, l2 = l_refs[0][...], l_refs[1][...], l_refs[2][...]
    m = jnp.maximum(jnp.maximum(l0, l1), l2)
    w0, w1, w2 = jnp.exp2(l0 - m), jnp.exp2(l1 - m), jnp.exp2(l2 - m)
    y = (w0 * o_refs[0][...] + w1 * o_refs[1][...] + w2 * o_refs[2][...]) / (w0 + w1 + w2)
    y_ref[0] = (y * _silu(gb_ref[0].astype(F32))).astype(BF16)
    yield


N_SCAN_REFS = 9
N_ATTN_REFS = 17
MIXERS_VMEM_LIMIT = 62 * 1024 * 1024


def _mixers_kernel(*refs, nb, tt, tiles_per_seq, n_tiles):
    scan_in = refs[:N_SCAN_REFS]
    attn_in = refs[N_SCAN_REFS:N_SCAN_REFS + N_ATTN_REFS]
    ya_ref, yb_ref, s_ref = refs[N_SCAN_REFS + N_ATTN_REFS:N_SCAN_REFS + N_ATTN_REFS + 3]
    scratch = refs[N_SCAN_REFS + N_ATTN_REFS + 3:]
    is_first = ((pl.program_id(0) % n_tiles) % tiles_per_seq) == 0
    scan = _scan_stages(*scan_in, ya_ref, s_ref, nb=nb, tt=tt)
    attn = _attn_stages(attn_in[0:3], attn_in[3:6], attn_in[6:9], attn_in[9:12], attn_in[12:15],
                        attn_in[15], attn_in[16], yb_ref, scratch[0:3], scratch[3:6], is_first)
    for _ in _interleave(scan, attn):
        pass


def _mixers(r, cum, k, v, a, b, main, groups, vec, ones_bd, bias5, layer):
    B, S, W = r.shape
    n_pairs = HEADS_PER_GROUP // 2
    tiles_per_seq = S // ATT_TILE
    n_tiles = B * tiles_per_seq
    n_steps = n_pairs * n_tiles
    tt = S // n_steps
    assert tt * n_steps == S and tt % (CHUNK * SCAN_CHUNKS) == 0

    scan_spec = pl.BlockSpec((B, tt, W), lambda t: (0, t, 0))

    def full(arr):
        return pl.BlockSpec(arr.shape, lambda t: (0,) * arr.ndim)

    scan_specs = [scan_spec] * 6 + [pl.BlockSpec((B, tt, W), lambda t: (0, t, COL_GA // W)),
                                    _layer_block(vec, layer), full(ones_bd)]

    arrays = [main.reshape(B, 1, S, MAIN_WIDTH)] + list(groups)
    col_base = [COL_A0 // LANES, 0, 0]

    def where(t):
        tile = t % n_tiles
        return t // n_tiles, tile // tiles_per_seq, tile % tiles_per_seq

    def cur(g, part):
        d = DILATIONS[g]
        c0 = col_base[g] + part * (A_OUT_WIDTH // LANES)

        def index(t):
            hp, bi, ti = where(t)
            return bi, 0, ti, c0 + hp
        return pl.BlockSpec((1, d, ATT_TILE // d, LANES), index)

    def prev(g, part):
        d = DILATIONS[g]
        c0 = col_base[g] + part * (A_OUT_WIDTH // LANES)
        rb = ATT_TILE // (BLK * d)

        def index(t):
            hp, bi, ti = where(t)
            return bi, 0, jnp.maximum(ti * rb - 1, 0), c0 + hp
        return pl.BlockSpec((1, d, BLK, LANES), index)

    def tile(col0):
        def index(t):
            hp, bi, ti = where(t)
            return bi, ti, col0 // LANES + hp
        return pl.BlockSpec((1, ATT_TILE, LANES), index)

    attn_specs = ([cur(g, 0) for g in range(N_GROUPS)] + [cur(g, 1) for g in range(N_GROUPS)]
                  + [cur(g, 2) for g in range(N_GROUPS)]
                  + [prev(g, 1) for g in range(N_GROUPS)] + [prev(g, 2) for g in range(N_GROUPS)]
                  + [tile(COL_GB),
                     pl.BlockSpec((N_GROUPS, 1, 2, BLK, 2 * BLK), lambda t: (0, t // n_tiles, 0, 0, 0))])
    assert len(scan_specs) == N_SCAN_REFS and len(attn_specs) == N_ATTN_REFS
    return pl.pallas_call(
        functools.partial(_mixers_kernel, nb=B, tt=tt, tiles_per_seq=tiles_per_seq, n_tiles=n_tiles),
        grid=(n_steps,),
        in_specs=scan_specs + attn_specs,
        out_specs=[scan_spec, tile(0)],
        out_shape=[jax.ShapeDtypeStruct((B, S, W), BF16), jax.ShapeDtypeStruct((B, S, A_OUT_WIDTH), BF16)],
        scratch_shapes=([pltpu.VMEM((B * N_COLGROUPS, HEAD_DIM, MXU_DIM), F32)]
                        + [pltpu.VMEM((ATT_TILE, LANES), F32)] * 6),
        compiler_params=_cparams(("arbitrary",), MIXERS_VMEM_LIMIT),
        name="mixers",
    )(r, cum, k, v, a, b, main, vec, ones_bd, *(arrays * 5), main, bias5)


def _merge_kernel(ya_ref, yb_ref, ma_ref, mb_ref, x_ref, mod_ref, wa_ref, wb_ref, wo_ref, fg_ref,
                  o_ref, *, final_norm):
    pa = _dot(ya_ref[0], wa_ref[...])
    pb = _dot(yb_ref[0], wb_ref[...])
    merged = _sigmoid(ma_ref[0].astype(F32)) * pa + _sigmoid(mb_ref[0].astype(F32)) * pb
    out = _dot(merged.astype(BF16), wo_ref[...])
    gate = mod_ref[0, :, 2 * D_MODEL:3 * D_MODEL]
    xn = x_ref[0] + gate * out
    if final_norm:
        ms = jnp.mean(xn * xn, axis=-1, keepdims=True)
        xn = xn * lax.rsqrt(ms + RMS_EPS) * fg_ref[...]
    o_ref[0] = xn


def _merge(ya, yb, proj, x, mod, wa, wb, wo, final_g, final_norm, layer, tm=1024):
    B, S, D = x.shape

    def rows(width, c):
        return pl.BlockSpec((1, tm, width), lambda b, i: (b, i, c))

    def full(arr):
        return pl.BlockSpec(arr.shape, lambda b, i: (0,) * arr.ndim)

    return pl.pallas_call(
        functools.partial(_merge_kernel, final_norm=final_norm),
        grid=(B, S // tm),
        in_specs=[rows(R_WIDTH, 0), rows(A_OUT_WIDTH, 0),
                  rows(D, COL_MA // D), rows(D, COL_MB // D), rows(D, 0),
                  pl.BlockSpec((1, 1, 3 * D), lambda b, i: (layer * MOD_ROWS + b, 0, 0)),
                  _layer_block(wa, layer), _layer_block(wb, layer), _layer_block(wo, layer), full(final_g)],
        out_specs=rows(D, 0),
        out_shape=jax.ShapeDtypeStruct((B, S, D), F32),
        compiler_params=_cparams(("parallel", "parallel")),
        name="merge",
    )(ya, yb, proj, proj, x, mod, wa, wb, wo, final_g)


def _segment_ones():
    idx = np.arange(MXU_DIM)
    return jnp.asarray(idx[:, None] // HEAD_DIM == idx[None, :] // HEAD_DIM, BF16)


def kernel(x, c, norm_g, ada_w, ada_b, w_in, rwkv_mu_rkv, rwkv_mu_wa, rwkv_w0, rwkv_w1, rwkv_w2, rwkv_a0, rwkv_a1, rwkv_a2, rwkv_k_k, rwkv_k_a, rwkv_r_k, rwkv_ln_g, rwkv_ln_b, rwkv_mu_v, rwkv_v0, rwkv_v1, rwkv_v2, w_branch_a, w_branch_b, w_out, rel_bias, final_g):
    B, S, D = x.shape
    assert D == D_MODEL and S % ATT_TILE == 0 and w_in.shape[2] == PROJ_WIDTH
    ones_bd = _segment_ones()
    mod = _adaln_mod(c, ada_w, ada_b).reshape(DEPTH * MOD_ROWS, 1, 3 * D)
    bias = _rel_bias(rel_bias).reshape(N_GROUPS, HEADS_PER_GROUP // 2, 2, BLK, 2 * BLK)

    def cols(start, width):
        return w_in[:, :, start:start + width]

    def group_cols(g):
        return [cols(W_AQ + A_OUT_WIDTH * g, A_OUT_WIDTH) * (LOG2E / math.sqrt(HEAD_DIM)),
                cols(W_AK + A_OUT_WIDTH * g, A_OUT_WIDTH), cols(W_AV + A_OUT_WIDTH * g, A_OUT_WIDTH)]

    w_main = jnp.concatenate(
        [cols(W_R, 4 * R_WIDTH), cols(W_MA, 2 * D_MODEL), cols(W_GB, A_OUT_WIDTH)] + group_cols(0),
        axis=2).astype(BF16)
    w_groups = [jnp.concatenate(group_cols(g), axis=2).astype(BF16) for g in range(1, N_GROUPS)]
    zeros_rows = jnp.zeros((DEPTH, D), F32)
    pvec = jnp.stack([rwkv_mu_rkv[:, 0], rwkv_mu_rkv[:, 1], rwkv_mu_rkv[:, 2], rwkv_w0, rwkv_a0, rwkv_k_k,
                      rwkv_k_a, _first_layer_blank(rwkv_v0)] + [zeros_rows] * (PV_ROWS - 8), axis=1)
    lora = _pack_lora([(rwkv_mu_wa[:, 0], rwkv_w1, rwkv_w2), (rwkv_mu_wa[:, 1], rwkv_a1, rwkv_a2),
                       (_first_layer_blank(rwkv_mu_v), _first_layer_blank(rwkv_v1),
                        _first_layer_blank(rwkv_v2))])
    vec = jnp.stack([rwkv_r_k.reshape(DEPTH, -1), rwkv_ln_g, rwkv_ln_b] + [zeros_rows] * 5, axis=1)
    norm_g3 = norm_g.reshape(DEPTH, 1, D)
    wa, wb, wo = w_branch_a.astype(BF16), w_branch_b.astype(BF16), w_out.astype(BF16)

    v_first = None
    for i in range(DEPTH):
        proj, h = _norm_proj(x, mod, norm_g3, w_main, i)
        (r, cum, k, v, a, b), groups = _rwkv_prep(h, proj, v_first, pvec, lora if i > 0 else lora[:3],
                                                  ones_bd, w_groups, i)
        if i == 0:
            v_first = v
        y_a, y_b = _mixers(r, cum, k, v, a, b, proj, groups, vec, ones_bd, bias, i)
        x = _merge(y_a, y_b, proj, x, mod, wa, wb, wo, final_g.reshape(1, D),
                   final_norm=(i == DEPTH - 1), layer=i)
    return x
```

```python
import functools
import math

import numpy as np
import jax
import jax.numpy as jnp
from jax import lax
from jax.experimental import pallas as pl
from jax.experimental.pallas import tpu as pltpu

F32 = jnp.float32
BF16 = jnp.bfloat16

D_MODEL = 1024
DEPTH = 2
HEAD_DIM = 64
R_WIDTH = 1024
N_GROUPS = 3
HEADS_PER_GROUP = 8
DILATIONS = (1, 4, 16)
BLK = 128
A_QK_WIDTH = 1536
A_OUT_WIDTH = 512
NUM_BUCKETS = 32
MAX_DISTANCE = 2048
PROJ_WIDTH = 4 * R_WIDTH + 3 * A_QK_WIDTH + A_OUT_WIDTH + 2 * D_MODEL
RMS_EPS = 1e-6
GN_EPS = 64e-5
NEG_INF = -1e30
LOG2E = math.log2(math.e)

LANES = 128
MXU_DIM = 256
HEADS_PER_TILE = MXU_DIM // HEAD_DIM
N_COLGROUPS = R_WIDTH // MXU_DIM
CHUNK = 64
SCAN_CHUNKS = 2

W_R, W_K, W_V, W_GA = 0, 1024, 2048, 3072
W_AQ, W_AK, W_AV = 4096, 5632, 7168
W_GB, W_MA, W_MB = 8704, 9216, 10240
COL_R, COL_K, COL_V, COL_GA, COL_MA, COL_MB, COL_GB, COL_A0 = 0, 1024, 2048, 3072, 4096, 5120, 6144, 6656
MAIN_WIDTH = 8192
GROUP_WIDTH = 3 * A_OUT_WIDTH

VMEM_LIMIT = 56 * 1024 * 1024
MOD_ROWS = 8


def _cparams(sem, vmem_limit=VMEM_LIMIT):
    return pltpu.CompilerParams(dimension_semantics=sem, vmem_limit_bytes=vmem_limit)


def _sigmoid(z):
    return 1.0 / (1.0 + jnp.exp(-z))


def _silu(z):
    return z * _sigmoid(z)


def _dot(a, b):
    return jnp.dot(a, b, preferred_element_type=F32)


def _dot_nt(a, b):
    return lax.dot_general(a, b, (((1,), (1,)), ((), ())), preferred_element_type=F32)


def _split2(x):
    hi = x.astype(BF16)
    lo = (x - hi.astype(F32)).astype(BF16)
    return hi, lo


def _segsum64(x, ones_bd, split):
    n = x.shape[0]
    xs = jnp.concatenate([x[:, MXU_DIM * g:MXU_DIM * (g + 1)] for g in range(N_COLGROUPS)], axis=0)
    if split:
        hi, lo = _split2(xs)
        s = _dot(hi, ones_bd) + _dot(lo, ones_bd)
    else:
        s = _dot(xs.astype(BF16), ones_bd)
    return jnp.concatenate([s[n * g:n * (g + 1)] for g in range(N_COLGROUPS)], axis=1)


def _mod_kernel(c_ref, w_ref, b_ref, o_ref):
    s = _silu(c_ref[...])
    o_ref[0] = _dot(s.astype(BF16), w_ref[0].astype(BF16)) + b_ref[0]


def _adaln_mod(c, ada_w, ada_b):
    L = ada_w.shape[0]
    B = c.shape[0]
    c_rows = jnp.pad(c, ((0, MOD_ROWS - B), (0, 0)))
    nj = 3
    return pl.pallas_call(
        _mod_kernel,
        grid=(L, nj),
        in_specs=[pl.BlockSpec((MOD_ROWS, D_MODEL), lambda l, j: (0, 0)),
                  pl.BlockSpec((1, D_MODEL, D_MODEL), lambda l, j: (l, 0, j)),
                  pl.BlockSpec((1, 1, D_MODEL), lambda l, j: (l, 0, j))],
        out_specs=pl.BlockSpec((1, MOD_ROWS, D_MODEL), lambda l, j: (l, 0, j)),
        out_shape=jax.ShapeDtypeStruct((L, MOD_ROWS, 3 * D_MODEL), F32),
        compiler_params=_cparams(("parallel", "parallel")),
        name="adaln_mod",
    )(c_rows, ada_w, ada_b.reshape(L, 1, 3 * D_MODEL))


def _t5_bucket(dist):
    max_exact = NUM_BUCKETS // 2
    safe = np.maximum(dist, 1).astype(np.float32)
    large = max_exact + (np.log(safe / max_exact) / math.log(MAX_DISTANCE / max_exact)
                         * (NUM_BUCKETS - max_exact)).astype(np.int32)
    large = np.minimum(large, NUM_BUCKETS - 1)
    return np.where(dist < max_exact, dist, large).astype(np.int32)


def _bias_kernel(tab_ref, bucket_ref, o_ref):
    g = pl.program_id(0)
    bk = bucket_ref[0]
    for hh in range(HEADS_PER_GROUP):
        h = g * HEADS_PER_GROUP + hh
        acc = jnp.zeros(bk.shape, F32)
        for b in range(NUM_BUCKETS):
            acc = jnp.where(bk == b, tab_ref[h * NUM_BUCKETS + b], acc)
        o_ref[hh] = jnp.where(bk >= 0, acc * LOG2E, NEG_INF)


def _rel_bias(rel_bias):
    n_heads = rel_bias.shape[1]
    qi = np.arange(BLK)[:, None]
    ki = np.arange(2 * BLK)[None, :]
    delta = qi + BLK - ki
    band = (delta >= 0) & (delta <= BLK)
    buckets = np.stack([np.where(band, _t5_bucket(np.maximum(delta, 0) * d), -1)
                        for d in DILATIONS]).astype(np.int32)
    table = rel_bias.T.reshape(-1)
    return pl.pallas_call(
        _bias_kernel,
        grid=(n_heads // HEADS_PER_GROUP,),
        in_specs=[pl.BlockSpec(memory_space=pltpu.SMEM),
                  pl.BlockSpec((1, BLK, 2 * BLK), lambda g: (g, 0, 0))],
        out_specs=pl.BlockSpec((HEADS_PER_GROUP, BLK, 2 * BLK), lambda g: (g, 0, 0)),
        out_shape=jax.ShapeDtypeStruct((n_heads, BLK, 2 * BLK), F32),
        compiler_params=_cparams(("parallel",)),
        name="rel_bias",
    )(table, jnp.asarray(buckets))


def _proj_kernel(x_ref, mod_ref, g_ref, w_ref, proj_ref, h_ref):
    @pl.when(pl.program_id(2) == 0)
    def _():
        x = x_ref[0]
        ms = jnp.mean(x * x, axis=-1, keepdims=True)
        y = x * lax.rsqrt(ms + RMS_EPS) * g_ref[...]
        shift = mod_ref[0, :, 0:D_MODEL]
        scale = mod_ref[0, :, D_MODEL:2 * D_MODEL]
        h_ref[0] = (y * (1.0 + scale) + shift).astype(BF16)

    proj_ref[0] = _dot(h_ref[0], w_ref[...]).astype(BF16)


def _layer_block(arr, layer):
    tail = (0,) * (arr.ndim - 1)
    return pl.BlockSpec((None,) + arr.shape[1:], lambda *_: (layer,) + tail)


def _norm_proj(x, mod, norm_g, w_main, layer, tm=1024, tn=4096):
    B, S, D = x.shape
    N = w_main.shape[2]
    return pl.pallas_call(
        _proj_kernel,
        grid=(B, S // tm, N // tn),
        in_specs=[pl.BlockSpec((1, tm, D), lambda b, i, j: (b, i, 0)),
                  pl.BlockSpec((1, 1, 3 * D), lambda b, i, j: (layer * MOD_ROWS + b, 0, 0)),
                  _layer_block(norm_g, layer),
                  pl.BlockSpec((None, D, tn), lambda b, i, j: (layer, 0, j))],
        out_specs=[pl.BlockSpec((1, tm, tn), lambda b, i, j: (b, i, j)),
                   pl.BlockSpec((1, tm, D), lambda b, i, j: (b, i, 0))],
        out_shape=[jax.ShapeDtypeStruct((B, S, N), BF16),
                   jax.ShapeDtypeStruct((B, S, D), BF16)],
        compiler_params=_cparams(("parallel", "parallel", "arbitrary")),
        name="norm_proj",
    )(x, mod, norm_g, w_main)


GATHER_ROWS = MXU_DIM


def _group_proj_stages(h_ref, perm_ref, w_ref, o_ref, d):
    tm = h_ref.shape[1]
    per_res = GATHER_ROWS // d
    perm = perm_ref[...]
    for ck in range(tm // GATHER_ROWS):
        r0 = ck * GATHER_ROWS
        hp = _dot(perm, h_ref[0, r0:r0 + GATHER_ROWS, :]).astype(BF16)
        yield
        res = _dot(hp, w_ref[...]).astype(BF16)
        for r in range(d):
            o_ref[0, r, ck * per_res:(ck + 1) * per_res, :] = res[r * per_res:(r + 1) * per_res]
        yield


def _gather_perm(d):
    dst = np.arange(GATHER_ROWS)
    src = (dst % (GATHER_ROWS // d)) * d + dst // (GATHER_ROWS // d)
    return jnp.asarray(src[:, None] == np.arange(GATHER_ROWS)[None, :], BF16)


PV_MU_R, PV_MU_K, PV_MU_V, PV_W0, PV_A0, PV_KK, PV_KA, PV_V0 = range(8)
LORA_LANES = 256


def _pack_lora(paths):
    n_layers, d_model, _ = paths[0][1].shape
    used = sum(down.shape[2] for _, down, _ in paths)
    pad = jnp.zeros((n_layers, d_model, LORA_LANES - used), F32)
    keep = jnp.concatenate([(1.0 - mu)[:, :, None] * down for mu, down, _ in paths] + [pad], axis=2)
    shifted = jnp.concatenate([mu[:, :, None] * down for mu, down, _ in paths] + [pad], axis=2)
    ups, lane = [], 0
    for _, down, up in paths:
        rank = down.shape[2]
        ups.append(jnp.pad(up, ((0, 0), (lane, LORA_LANES - lane - rank), (0, 0))).astype(BF16))
        lane += rank
    return [jnp.concatenate([keep, shifted], axis=2).astype(BF16)] + ups


def _first_layer_blank(arr):
    return jnp.concatenate([jnp.zeros((1,) + arr.shape[1:], arr.dtype), arr], axis=0)
PV_ROWS = 16
PREV_ROWS = 16


def _shift_rows(t, prev_last):
    rolled = pltpu.roll(t, 1, axis=0)
    row = lax.broadcasted_iota(jnp.int32, t.shape, 0)
    return jnp.where(row == 0, prev_last, rolled)


def _rprep_stages(refs, has_vres):
    if has_vres:
        (h_ref, hp_ref, pr_ref, prp_ref, pk_ref, pkp_ref, pvv_ref, pvp_ref, vf_ref, pvec_ref,
         wd_ref, uw_ref, ua_ref, uv_ref, ones_ref, tril_ref,
         r_out, cum_out, k_out, v_out, a_out, b_out) = refs
    else:
        (h_ref, hp_ref, pr_ref, prp_ref, pk_ref, pkp_ref, pvv_ref, pvp_ref, pvec_ref,
         wd_ref, uw_ref, ua_ref, ones_ref, tril_ref,
         r_out, cum_out, k_out, v_out, a_out, b_out) = refs

    not_first = (pl.program_id(1) > 0).astype(F32)

    def prm(i):
        return pvec_ref[i:i + 1, :]

    def lerp_shift(cur_ref, prev_ref, mu):
        t = cur_ref[0].astype(F32)
        last = prev_ref[0, PREV_ROWS - 1:PREV_ROWS, :].astype(F32)
        return t + (_shift_rows(t, last * not_first) - t) * mu

    r = lerp_shift(pr_ref, prp_ref, prm(PV_MU_R))
    k = lerp_shift(pk_ref, pkp_ref, prm(PV_MU_K))
    v = lerp_shift(pvv_ref, pvp_ref, prm(PV_MU_V))
    yield

    wd = wd_ref[...]
    z2 = _dot(h_ref[0], wd)
    z_prev = _dot(hp_ref[0], wd[:, LORA_LANES:])[PREV_ROWS - 1:PREV_ROWS, :]
    z = z2[:, 0:LORA_LANES] + _shift_rows(z2[:, LORA_LANES:], z_prev * not_first)
    zb = z.astype(BF16)
    yield

    zw = prm(PV_W0) + _dot(jnp.tanh(z).astype(BF16), uw_ref[...])
    w = jnp.minimum(zw, 0.0) - jnp.log(1.0 + jnp.exp(-jnp.abs(zw))) - 0.5
    lw = -jnp.exp(w)
    yield
    hi, lo = _split2(lw)
    tril = tril_ref[...]
    for i in range(lw.shape[0] // MXU_DIM):
        blk = slice(MXU_DIM * i, MXU_DIM * (i + 1))
        cum_out[0, blk, :] = _dot(tril, hi[blk]) + _dot(tril, lo[blk])
    yield
    a = _sigmoid(prm(PV_A0) + _dot(zb, ua_ref[...]))
    if has_vres:
        mix = _sigmoid(prm(PV_V0) + _dot(zb, uv_ref[...]))
        v = v + (vf_ref[0].astype(F32) - v) * mix
    yield

    kk = k * prm(PV_KK)
    ss = _segsum64(kk * kk, ones_ref[...], split=False)
    kk = kk * lax.rsqrt(jnp.maximum(ss, 1e-24))
    yield
    r_out[0] = r.astype(BF16)
    k_out[0] = (k * (1.0 + (a - 1.0) * prm(PV_KA))).astype(BF16)
    v_out[0] = v.astype(BF16)
    a_out[0] = (-kk).astype(BF16)
    b_out[0] = (kk * a).astype(BF16)
    yield


N_GROUP_PROJ = N_GROUPS - 1


def _prep_groups_kernel(*refs, has_vres):
    n_in = len(refs) - 6 - N_GROUP_PROJ - 2 * N_GROUP_PROJ
    prep_in, rest = refs[:n_in], refs[n_in:]
    gp_in, outs = rest[:2 * N_GROUP_PROJ], rest[2 * N_GROUP_PROJ:]
    prep = _rprep_stages(tuple(prep_in) + tuple(outs[:6]), has_vres)
    h_ref = prep_in[0]
    groups = [_group_proj_stages(h_ref, gp_in[2 * n], gp_in[2 * n + 1], outs[6 + n], DILATIONS[n + 1])
              for n in range(N_GROUP_PROJ)]

    def all_groups():
        for gen in groups:
            yield from gen

    for _ in _interleave(prep, all_groups()):
        pass


def _rwkv_prep(h, proj, v_first, pvec, lora, ones_bd, w_groups, layer, tr=512):
    B, S, D = h.shape
    has_vres = v_first is not None
    rpb = tr // PREV_ROWS
    t = np.arange(MXU_DIM)
    tril_bd =jnp.asarray((t[None, :] <= t[:, None]) & (t[None, :] // CHUNK == t[:, None] // CHUNK), BF16)

    def cur(c):
        return pl.BlockSpec((1, tr, R_WIDTH), lambda b, i: (b, i, c))

    def prev(c):
        return pl.BlockSpec((1, PREV_ROWS, R_WIDTH), lambda b, i: (b, jnp.maximum(i * rpb - 1, 0), c))

    def full(arr):
        return pl.BlockSpec(arr.shape, lambda b, i: (0,) * arr.ndim)

    in_specs = [cur(0), prev(0)]
    args = [h, h]
    for c in (COL_R, COL_K, COL_V):
        in_specs += [cur(c // R_WIDTH), prev(c // R_WIDTH)]
        args += [proj, proj]
    if has_vres:
        in_specs.append(cur(0))
        args.append(v_first)
    for per_layer in [pvec] + list(lora):
        in_specs.append(_layer_block(per_layer, layer))
        args.append(per_layer)
    for const in (ones_bd, tril_bd):
        in_specs.append(full(const))
        args.append(const)
    out = [jax.ShapeDtypeStruct((B, S, R_WIDTH), F32 if n == 1 else BF16) for n in range(6)]
    out_specs = [cur(0)] * 6
    for n, w_group in enumerate(w_groups):
        d = DILATIONS[n + 1]
        width = w_group.shape[2]
        in_specs += [full(_gather_perm(d)), _layer_block(w_group, layer)]
        args += [_gather_perm(d), w_group]
        out.append(jax.ShapeDtypeStruct((B, d, S // d, width), BF16))
        out_specs.append(pl.BlockSpec((1, d, tr // d, width), lambda b, i: (b, 0, i, 0)))
    res = pl.pallas_call(
        functools.partial(_prep_groups_kernel, has_vres=has_vres),
        grid=(B, S // tr),
        in_specs=in_specs,
        out_specs=out_specs,
        out_shape=out,
        compiler_params=_cparams(("parallel", "parallel")),
        name="prep_groups",
    )(*args)
    return res[:6], res[6:]


def _interleave(*gens):
    live = list(gens)
    while live:
        for gen in list(live):
            if next(gen, StopIteration) is StopIteration:
                live.remove(gen)
        yield


def _scan_stages(r_ref, cum_ref, k_ref, v_ref, a_ref, b_ref, ga_ref, vec_ref, ones_ref,
                 y_ref, s_ref, *, nb, tt):
    C = CHUNK

    @pl.when(pl.program_id(0) == 0)
    def _():
        s_ref[...] = jnp.zeros(s_ref.shape, F32)

    row = lax.broadcasted_iota(jnp.int32, (C, MXU_DIM), 0)
    lane = lax.broadcasted_iota(jnp.int32, (C, MXU_DIM), 1)
    col = lane & (HEAD_DIM - 1)
    lhead = lane >> 6
    strict = col < row
    incl = col <= row
    eye = (col == row).astype(F32)
    head_masks = [lhead == hh for hh in range(HEADS_PER_TILE)]

    def bdrows(x):
        return jnp.concatenate([jnp.where(m, x, 0.0) for m in head_masks], axis=0).astype(BF16)

    def diag_blocks(full):
        acc = jnp.where(head_masks[0], full[0:C], 0.0)
        for hh in range(1, HEADS_PER_TILE):
            acc = acc + jnp.where(head_masks[hh], full[C * hh:C * (hh + 1)], 0.0)
        return acc

    row_full = lax.broadcasted_iota(jnp.int32, (C, R_WIDTH), 0)
    ones_bd = ones_ref[...]
    r_k = vec_ref[0:1, :]
    ln_g = vec_ref[1:2, :]
    ln_b = vec_ref[2:3, :]

    chains = [(bi, g) for bi in range(nb) for g in range(N_COLGROUPS)]
    insts = [(ck, bi, g) for ck in range(SCAN_CHUNKS) for bi, g in chains]


    def load(gi):
        rows = [slice((gi * SCAN_CHUNKS + ck) * C, (gi * SCAN_CHUNKS + ck + 1) * C) for ck in range(SCAN_CHUNKS)]
        ops = {}
        for ck in range(SCAN_CHUNKS):
            for bi in range(nb):
                cum = cum_ref[bi, rows[ck], :]
                r = r_ref[bi, rows[ck], :].astype(F32)
                k = k_ref[bi, rows[ck], :].astype(F32)
                v = v_ref[bi, rows[ck], :].astype(F32)
                a = a_ref[bi, rows[ck], :].astype(F32)
                b = b_ref[bi, rows[ck], :].astype(F32)
                total = cum[C - 1:C, :]
                p_in = jnp.exp(cum)
                p_inv = jnp.exp(-cum)
                p_rest = jnp.exp(total - cum)
                p_before = jnp.where(row_full == 0, 1.0, pltpu.roll(p_in, 1, axis=0))
                ops[ck, bi] = dict(r=r, k=k, v=v, a_t=a * p_before, r_t=r * p_in, b_t=b * p_inv,
                                   k_t=k * p_inv, bp=b * p_rest, kp=k * p_rest, p_all=jnp.exp(total))
        return dict(rows=rows, ops=ops)

    def part(ctx, name, ck, bi, g):
        return ctx["ops"][ck, bi][name][:, MXU_DIM * g:MXU_DIM * (g + 1)]

    def independent(ctx):
        res = [_dot_nt(jnp.concatenate([part(ctx, "a_t", *i), part(ctx, "r_t", *i)], axis=0).astype(BF16),
                       jnp.concatenate([bdrows(part(ctx, "b_t", *i)), bdrows(part(ctx, "k_t", *i))], axis=0))
               for i in insts]
        yield
        a_ab = [jnp.where(strict, x[0:C, 0:MXU_DIM], 0.0) for x in res]
        a_ak = [jnp.where(strict, x[0:C, MXU_DIM:], 0.0) for x in res]
        ctx["a_rb"] = [jnp.where(incl, x[C:, 0:MXU_DIM], 0.0).astype(BF16) for x in res]
        a_rk = [jnp.where(incl, x[C:, MXU_DIM:], 0.0) for x in res]

        pw = [_dot(x.astype(BF16), bdrows(x)) for x in a_ab]
        tinv = [eye + x for x in a_ab]
        yield
        for _ in range(4):
            both = [_dot(jnp.concatenate([p, t], axis=0).astype(BF16), bdrows(p)) for p, t in zip(pw, tinv)]
            tinv = [t + x[C:] for t, x in zip(tinv, both)]
            pw = [x[0:C] for x in both]
            yield
        tinv = [t + _dot(t.astype(BF16), bdrows(p)) for p, t in zip(pw, tinv)]
        yield
        tax = [_dot(t.astype(BF16), jnp.concatenate([bdrows(part(ctx, "a_t", *i)), bdrows(x)], axis=1))
               for t, x, i in zip(tinv, a_ak, insts)]
        yield
        ctx["from_v"] = [_dot(jnp.concatenate([x[:, MXU_DIM:], ark], axis=0).astype(BF16),
                              bdrows(part(ctx, "v", *i))) for x, ark, i in zip(tax, a_rk, insts)]
        ctx["tax"] = tax
        yield

    def dependent(ctx, carried):
        tax, from_v, a_rb = ctx["tax"], ctx["from_v"], ctx["a_rb"]
        st = carried["st"]
        y = {}
        for ck in range(SCAN_CHUNKS):
            sel = range(ck * len(chains), (ck + 1) * len(chains))
            from_state = [_dot_nt(jnp.concatenate([tax[n][:, 0:MXU_DIM], part(ctx, "r_t", *insts[n])],
                                                  axis=0).astype(BF16), bdrows(s))
                          for n, s in zip(sel, st)]
            yield
            u = [x[0:C] + from_v[n][0:C] for x, n in zip(from_state, sel)]
            for x, n, uu in zip(from_state, sel, u):
                y[insts[n]] = x[C:] + from_v[n][C:] + _dot(a_rb[n], bdrows(uu))
            upd = [_dot(jnp.concatenate([uu, part(ctx, "v", *insts[n])], axis=0).T.astype(BF16),
                        jnp.concatenate([part(ctx, "bp", *insts[n]), part(ctx, "kp", *insts[n])],
                                        axis=0).astype(BF16))
                   for uu, n in zip(u, sel)]
            yield
            st = [s_old * part(ctx, "p_all", *insts[n]) + diag_blocks(x) for s_old, x, n in zip(st, upd, sel)]
        carried["st"] = st

        for ck in range(SCAN_CHUNKS):
            for bi in range(nb):
                p = ctx["ops"][ck, bi]
                yc = jnp.concatenate([y[ck, bi, g] for g in range(N_COLGROUPS)], axis=1)
                mean = _segsum64(yc, ones_bd, split=True) * (1.0 / HEAD_DIM)
                yield
                yd = yc - mean
                var = _segsum64(yd * yd, ones_bd, split=False) * (1.0 / HEAD_DIM)
                yn = yd * lax.rsqrt(var + GN_EPS) * ln_g + ln_b
                bonus = _segsum64(p["r"] * p["k"] * r_k, ones_bd, split=False) * p["v"]
                rows = ctx["rows"][ck]
                y_ref[bi, rows, :] = ((yn + bonus) * _silu(ga_ref[bi, rows, :].astype(F32))).astype(BF16)
                yield

    carried = {"st": [s_ref[bi * N_COLGROUPS + g] for bi, g in chains]}
    n_groups = tt // (C * SCAN_CHUNKS)
    ctx = load(0)
    yield from independent(ctx)
    for gi in range(1, n_groups):
        nxt = load(gi)
        yield from _interleave(dependent(ctx, carried), independent(nxt))
        ctx = nxt
    yield from dependent(ctx, carried)
    for (bi, g), s_new in zip(chains, carried["st"]):
        s_ref[bi * N_COLGROUPS + g] = s_new


ATT_TILE = 2048
ATT_UNROLL = (5, 6, 8)


def _attn_stages(q_refs, k_refs, v_refs, kp_refs, vp_refs, gb_ref, bias_ref, y_ref, o_refs, l_refs,
                 is_first):
    prev_limit = jnp.where(is_first, BLK, 0)
    ki = lax.broadcasted_iota(jnp.int32, (2 * BLK, 2 * BLK), 1)
    head0 = lax.broadcasted_iota(jnp.int32, (BLK, LANES), 1) < HEAD_DIM
    ones_cols = jnp.ones((2 * BLK, LANES), BF16)
    zero = jnp.zeros((BLK, LANES), BF16)

    def process(blocks):
        q2s, kws, vws, bias2s, stores = [], [], [], [], []
        for g, sub, res in blocks:
            d = DILATIONS[g]
            base = sub * (BLK * d) + res
            q = q_refs[g][0, res, sub * BLK:(sub + 1) * BLK, :]
            q2s.append(jnp.concatenate([jnp.where(head0, q, zero), jnp.where(head0, zero, q)], axis=0))
            if sub == 0:
                kw = jnp.concatenate([kp_refs[g][0, res], k_refs[g][0, res, 0:BLK, :]], axis=0)
                vw = jnp.concatenate([vp_refs[g][0, res], v_refs[g][0, res, 0:BLK, :]], axis=0)
            else:
                kw = k_refs[g][0, res, (sub - 1) * BLK:(sub + 1) * BLK, :]
                vw = v_refs[g][0, res, (sub - 1) * BLK:(sub + 1) * BLK, :]
            kws.append(kw)
            vws.append(jnp.concatenate([vw, ones_cols], axis=1))
            bias2s.append(bias_ref[g, 0].reshape(2 * BLK, 2 * BLK))
            stores.append((g, pl.ds(base, BLK) if d == 1 else pl.ds(base, BLK, stride=d)))
        logits = [jnp.where(bias2 > 0.5 * NEG_INF, _dot_nt(q2, kw) + bias2, NEG_INF)
                  for q2, kw, bias2 in zip(q2s, kws, bias2s)]
        logits = [jnp.where(ki < prev_limit, NEG_INF, x) if blk[1] == 0 else x
                  for x, blk in zip(logits, blocks)]
        yield
        ms = [jnp.max(x, axis=-1, keepdims=True) for x in logits]
        ps = [jnp.exp2(x - m).astype(BF16) for x, m in zip(logits, ms)]
        pvs = [_dot(p, vw) for p, vw in zip(ps, vws)]
        yield
        for (g, rows), pv, m in zip(stores, pvs, ms):
            num = jnp.where(head0, pv[0:BLK, 0:LANES], pv[BLK:, 0:LANES])
            den = jnp.where(head0, pv[0:BLK, LANES:], pv[BLK:, LANES:])
            o_refs[g][rows, :] = num / den
            l_refs[g][rows, :] = jnp.where(head0, m[0:BLK], m[BLK:]) + jnp.log2(den)

    for g, d in enumerate(DILATIONS):
        blocks = [(g, sub, res) for sub in range(ATT_TILE // (BLK * d)) for res in range(d)]
        for n in range(0, len(blocks), ATT_UNROLL[g]):
            yield from process(blocks[n:n + ATT_UNROLL[g]])

    l0, l1, l2 = l_refs[0][...], l_refs[1][...], l_refs[2][...]
    m = jnp.maximum(jnp.maximum(l0, l1), l2)
    w0, w1, w2 = jnp.exp2(l0 - m), jnp.exp2(l1 - m), jnp.exp2(l2 - m)
    y = (w0 * o_refs[0][...] + w1 * o_refs[1][...] + w2 * o_refs[2][...]) / (w0 + w1 + w2)
    y_ref[0] = (y * _silu(gb_ref[0].astype(F32))).astype(BF16)
    yield


N_SCAN_REFS = 9
N_ATTN_REFS = 17
MIXERS_VMEM_LIMIT = 62 * 1024 * 1024


def _mixers_kernel(*refs, nb, tt, tiles_per_seq, n_tiles):
    scan_in = refs[:N_SCAN_REFS]
    attn_in = refs[N_SCAN_REFS:N_SCAN_REFS + N_ATTN_REFS]
    ya_ref, yb_ref, s_ref = refs[N_SCAN_REFS + N_ATTN_REFS:N_SCAN_REFS + N_ATTN_REFS + 3]
    scratch = refs[N_SCAN_REFS + N_ATTN_REFS + 3:]
    is_first = ((pl.program_id(0) % n_tiles) % tiles_per_seq) == 0
    scan = _scan_stages(*scan_in, ya_ref, s_ref, nb=nb, tt=tt)
    attn = _attn_stages(attn_in[0:3], attn_in[3:6], attn_in[6:9], attn_in[9:12], attn_in[12:15],
                        attn_in[15], attn_in[16], yb_ref, scratch[0:3], scratch[3:6], is_first)
    for _ in _interleave(scan, attn):
        pass


def _mixers(r, cum, k, v, a, b, main, groups, vec, ones_bd, bias5, layer):
    B, S, W = r.shape
    n_pairs = HEADS_PER_GROUP // 2
    tiles_per_seq = S // ATT_TILE
    n_tiles = B * tiles_per_seq
    n_steps = n_pairs * n_tiles
    tt = S // n_steps
    assert tt * n_steps == S and tt % (CHUNK * SCAN_CHUNKS) == 0

    scan_spec = pl.BlockSpec((B, tt, W), lambda t: (0, t, 0))

    def full(arr):
        return pl.BlockSpec(arr.shape, lambda t: (0,) * arr.ndim)

    scan_specs = [scan_spec] * 6 + [pl.BlockSpec((B, tt, W), lambda t: (0, t, COL_GA // W)),
                                    _layer_block(vec, layer), full(ones_bd)]

    arrays = [main.reshape(B, 1, S, MAIN_WIDTH)] + list(groups)
    col_base = [COL_A0 // LANES, 0, 0]

    def where(t):
        tile = t % n_tiles
        return t // n_tiles, tile // tiles_per_seq, tile % tiles_per_seq

    def cur(g, part):
        d = DILATIONS[g]
        c0 = col_base[g] + part * (A_OUT_WIDTH // LANES)

        def index(t):
            hp, bi, ti = where(t)
            return bi, 0, ti, c0 + hp
        return pl.BlockSpec((1, d, ATT_TILE // d, LANES), index)

    def prev(g, part):
        d = DILATIONS[g]
        c0 = col_base[g] + part * (A_OUT_WIDTH // LANES)
        rb = ATT_TILE // (BLK * d)

        def index(t):
            hp, bi, ti = where(t)
            return bi, 0, jnp.maximum(ti * rb - 1, 0), c0 + hp
        return pl.BlockSpec((1, d, BLK, LANES), index)

    def tile(col0):
        def index(t):
            hp, bi, ti = where(t)
            return bi, ti, col0 // LANES + hp
        return pl.BlockSpec((1, ATT_TILE, LANES), index)

    attn_specs = ([cur(g, 0) for g in range(N_GROUPS)] + [cur(g, 1) for g in range(N_GROUPS)]
                  + [cur(g, 2) for g in range(N_GROUPS)]
                  + [prev(g, 1) for g in range(N_GROUPS)] + [prev(g, 2) for g in range(N_GROUPS)]
                  + [tile(COL_GB),
                     pl.BlockSpec((N_GROUPS, 1, 2, BLK, 2 * BLK), lambda t: (0, t // n_tiles, 0, 0, 0))])
    assert len(scan_specs) == N_SCAN_REFS and len(attn_specs) == N_ATTN_REFS
    return pl.pallas_call(
        functools.partial(_mixers_kernel, nb=B, tt=tt, tiles_per_seq=tiles_per_seq, n_tiles=n_tiles),
        grid=(n_steps,),
        in_specs=scan_specs + attn_specs,
        out_specs=[scan_spec, tile(0)],
        out_shape=[jax.ShapeDtypeStruct((B, S, W), BF16), jax.ShapeDtypeStruct((B, S, A_OUT_WIDTH), BF16)],
        scratch_shapes=([pltpu.VMEM((B * N_COLGROUPS, HEAD_DIM, MXU_DIM), F32)]
                        + [pltpu.VMEM((ATT_TILE, LANES), F32)] * 6),
        compiler_params=_cparams(("arbitrary",), MIXERS_VMEM_LIMIT),
        name="mixers",
    )(r, cum, k, v, a, b, main, vec, ones_bd, *(arrays * 5), main, bias5)


def _merge_kernel(ya_ref, yb_ref, ma_ref, mb_ref, x_ref, mod_ref, wa_ref, wb_ref, wo_ref, fg_ref,
                  o_ref, *, final_norm):
    pa = _dot(ya_ref[0], wa_ref[...])
    pb = _dot(yb_ref[0], wb_ref[...])
    merged = _sigmoid(ma_ref[0].astype(F32)) * pa + _sigmoid(mb_ref[0].astype(F32)) * pb
    out = _dot(merged.astype(BF16), wo_ref[...])
    gate = mod_ref[0, :, 2 * D_MODEL:3 * D_MODEL]
    xn = x_ref[0] + gate * out
    if final_norm:
        ms = jnp.mean(xn * xn, axis=-1, keepdims=True)
        xn = xn * lax.rsqrt(ms + RMS_EPS) * fg_ref[...]
    o_ref[0] = xn


def _merge(ya, yb, proj, x, mod, wa, wb, wo, final_g, final_norm, layer, tm=1024):
    B, S, D = x.shape

    def rows(width, c):
        return pl.BlockSpec((1, tm, width), lambda b, i: (b, i, c))

    def full(arr):
        return pl.BlockSpec(arr.shape, lambda b, i: (0,) * arr.ndim)

    return pl.pallas_call(
        functools.partial(_merge_kernel, final_norm=final_norm),
        grid=(B, S // tm),
        in_specs=[rows(R_WIDTH, 0), rows(A_OUT_WIDTH, 0),
                  rows(D, COL_MA // D), rows(D, COL_MB // D), rows(D, 0),
                  pl.BlockSpec((1, 1, 3 * D), lambda b, i: (layer * MOD_ROWS + b, 0, 0)),
                  _layer_block(wa, layer), _layer_block(wb, layer), _layer_block(wo, layer), full(final_g)],
        out_specs=rows(D, 0),
        out_shape=jax.ShapeDtypeStruct((B, S, D), F32),
        compiler_params=_cparams(("parallel", "parallel")),
        name="merge",
    )(ya, yb, proj, proj, x, mod, wa, wb, wo, final_g)


def _segment_ones():
    idx = np.arange(MXU_DIM)
    return jnp.asarray(idx[:, None] // HEAD_DIM == idx[None, :] // HEAD_DIM, BF16)


def kernel(x, c, norm_g, ada_w, ada_b, w_in, rwkv_mu_rkv, rwkv_mu_wa, rwkv_w0, rwkv_w1, rwkv_w2, rwkv_a0, rwkv_a1, rwkv_a2, rwkv_k_k, rwkv_k_a, rwkv_r_k, rwkv_ln_g, rwkv_ln_b, rwkv_mu_v, rwkv_v0, rwkv_v1, rwkv_v2, w_branch_a, w_branch_b, w_out, rel_bias, final_g):
    B, S, D = x.shape
    assert D == D_MODEL and S % ATT_TILE == 0 and w_in.shape[2] == PROJ_WIDTH
    ones_bd = _segment_ones()
    mod = _adaln_mod(c, ada_w, ada_b).reshape(DEPTH * MOD_ROWS, 1, 3 * D)
    bias = _rel_bias(rel_bias).reshape(N_GROUPS, HEADS_PER_GROUP // 2, 2, BLK, 2 * BLK)

    def cols(start, width):
        return w_in[:, :, start:start + width]

    def group_cols(g):
        return [cols(W_AQ + A_OUT_WIDTH * g, A_OUT_WIDTH) * (LOG2E / math.sqrt(HEAD_DIM)),
                cols(W_AK + A_OUT_WIDTH * g, A_OUT_WIDTH), cols(W_AV + A_OUT_WIDTH * g, A_OUT_WIDTH)]

    w_main = jnp.concatenate(
        [cols(W_R, 4 * R_WIDTH), cols(W_MA, 2 * D_MODEL), cols(W_GB, A_OUT_WIDTH)] + group_cols(0),
        axis=2).astype(BF16)
    w_groups = [jnp.concatenate(group_cols(g), axis=2).astype(BF16) for g in range(1, N_GROUPS)]
    zeros_rows = jnp.zeros((DEPTH, D), F32)
    pvec = jnp.stack([rwkv_mu_rkv[:, 0], rwkv_mu_rkv[:, 1], rwkv_mu_rkv[:, 2], rwkv_w0, rwkv_a0, rwkv_k_k,
                      rwkv_k_a, _first_layer_blank(rwkv_v0)] + [zeros_rows] * (PV_ROWS - 8), axis=1)
    lora = _pack_lora([(rwkv_mu_wa[:, 0], rwkv_w1, rwkv_w2), (rwkv_mu_wa[:, 1], rwkv_a1, rwkv_a2),
                       (_first_layer_blank(rwkv_mu_v), _first_layer_blank(rwkv_v1),
                        _first_layer_blank(rwkv_v2))])
    vec = jnp.stack([rwkv_r_k.reshape(DEPTH, -1), rwkv_ln_g, rwkv_ln_b] + [zeros_rows] * 5, axis=1)
    norm_g3 = norm_g.reshape(DEPTH, 1, D)
    wa, wb, wo = w_branch_a.astype(BF16), w_branch_b.astype(BF16), w_out.astype(BF16)

    v_first = None
    for i in range(DEPTH):
        proj, h = _norm_proj(x, mod, norm_g3, w_main, i)
        (r, cum, k, v, a, b), groups = _rwkv_prep(h, proj, v_first, pvec, lora if i > 0 else lora[:3],
                                                  ones_bd, w_groups, i)
        if i == 0:
            v_first = v
        y_a, y_b = _mixers(r, cum, k, v, a, b, proj, groups, vec, ones_bd, bias, i)
        x = _merge(y_a, y_b, proj, x, mod, wa, wb, wo, final_g.reshape(1, D),
                   final_norm=(i == DEPTH - 1), layer=i)
    return x
```

```python
import functools
import math

import numpy as np
import jax
import jax.numpy as jnp
from jax import lax
from jax.experimental import pallas as pl
from jax.experimental.pallas import tpu as pltpu

F32 = jnp.float32
BF16 = jnp.bfloat16

D_MODEL = 1024
DEPTH = 2
HEAD_DIM = 64
R_WIDTH = 1024
N_GROUPS = 3
HEADS_PER_GROUP = 8
DILATIONS = (1, 4, 16)
BLK = 128
A_QK_WIDTH = 1536
A_OUT_WIDTH = 512
NUM_BUCKETS = 32
MAX_DISTANCE = 2048
PROJ_WIDTH = 4 * R_WIDTH + 3 * A_QK_WIDTH + A_OUT_WIDTH + 2 * D_MODEL
RMS_EPS = 1e-6
GN_EPS = 64e-5
NEG_INF = -1e30
LOG2E = math.log2(math.e)

LANES = 128
MXU_DIM = 256
HEADS_PER_TILE = MXU_DIM // HEAD_DIM
N_COLGROUPS = R_WIDTH // MXU_DIM
CHUNK = 64
SCAN_CHUNKS = 2

W_R, W_K, W_V, W_GA = 0, 1024, 2048, 3072
W_AQ, W_AK, W_AV = 4096, 5632, 7168
W_GB, W_MA, W_MB = 8704, 9216, 10240
COL_R, COL_K, COL_V, COL_GA, COL_MA, COL_MB, COL_GB, COL_A0 = 0, 1024, 2048, 3072, 4096, 5120, 6144, 6656
MAIN_WIDTH = 8192
GROUP_WIDTH = 3 * A_OUT_WIDTH

VMEM_LIMIT = 56 * 1024 * 1024
MOD_ROWS = 8


def _cparams(sem, vmem_limit=VMEM_LIMIT):
    return pltpu.CompilerParams(dimension_semantics=sem, vmem_limit_bytes=vmem_limit)


def _sigmoid(z):
    return 1.0 / (1.0 + jnp.exp(-z))


def _silu(z):
    return z * _sigmoid(z)


def _dot(a, b):
    return jnp.dot(a, b, preferred_element_type=F32)


def _dot_nt(a, b):
    return lax.dot_general(a, b, (((1,), (1,)), ((), ())), preferred_element_type=F32)


def _split2(x):
    hi = x.astype(BF16)
    lo = (x - hi.astype(F32)).astype(BF16)
    return hi, lo


def _segsum64(x, ones_bd, split):
    n = x.shape[0]
    xs = jnp.concatenate([x[:, MXU_DIM * g:MXU_DIM * (g + 1)] for g in range(N_COLGROUPS)], axis=0)
    if split:
        hi, lo = _split2(xs)
        s = _dot(hi, ones_bd) + _dot(lo, ones_bd)
    else:
        s = _dot(xs.astype(BF16), ones_bd)
    return jnp.concatenate([s[n * g:n * (g + 1)] for g in range(N_COLGROUPS)], axis=1)


def _mod_kernel(c_ref, w_ref, b_ref, o_ref):
    s = _silu(c_ref[...])
    o_ref[0] = _dot(s.astype(BF16), w_ref[0].astype(BF16)) + b_ref[0]


def _adaln_mod(c, ada_w, ada_b):
    L = ada_w.shape[0]
    B = c.shape[0]
    c_rows = jnp.pad(c, ((0, MOD_ROWS - B), (0, 0)))
    nj = 3
    return pl.pallas_call(
        _mod_kernel,
        grid=(L, nj),
        in_specs=[pl.BlockSpec((MOD_ROWS, D_MODEL), lambda l, j: (0, 0)),
                  pl.BlockSpec((1, D_MODEL, D_MODEL), lambda l, j: (l, 0, j)),
                  pl.BlockSpec((1, 1, D_MODEL), lambda l, j: (l, 0, j))],
        out_specs=pl.BlockSpec((1, MOD_ROWS, D_MODEL), lambda l, j: (l, 0, j)),
        out_shape=jax.ShapeDtypeStruct((L, MOD_ROWS, 3 * D_MODEL), F32),
        compiler_params=_cparams(("parallel", "parallel")),
        name="adaln_mod",
    )(c_rows, ada_w, ada_b.reshape(L, 1, 3 * D_MODEL))


def _t5_bucket(dist):
    max_exact = NUM_BUCKETS // 2
    safe = np.maximum(dist, 1).astype(np.float32)
    large = max_exact + (np.log(safe / max_exact) / math.log(MAX_DISTANCE / max_exact)
                         * (NUM_BUCKETS - max_exact)).astype(np.int32)
    large = np.minimum(large, NUM_BUCKETS - 1)
    return np.where(dist < max_exact, dist, large).astype(np.int32)


def _bias_kernel(tab_ref, bucket_ref, o_ref):
    g = pl.program_id(0)
    bk = bucket_ref[0]
    for hh in range(HEADS_PER_GROUP):
        h = g * HEADS_PER_GROUP + hh
        acc = jnp.zeros(bk.shape, F32)
        for b in range(NUM_BUCKETS):
            acc = jnp.where(bk == b, tab_ref[h * NUM_BUCKETS + b], acc)
        o_ref[hh] = jnp.where(bk >= 0, acc * LOG2E, NEG_INF)


def _rel_bias(rel_bias):
    n_heads = rel_bias.shape[1]
    qi = np.arange(BLK)[:, None]
    ki = np.arange(2 * BLK)[None, :]
    delta = qi + BLK - ki
    band = (delta >= 0) & (delta <= BLK)
    buckets = np.stack([np.where(band, _t5_bucket(np.maximum(delta, 0) * d), -1)
                        for d in DILATIONS]).astype(np.int32)
    table = rel_bias.T.reshape(-1)
    return pl.pallas_call(
        _bias_kernel,
        grid=(n_heads // HEADS_PER_GROUP,),
        in_specs=[pl.BlockSpec(memory_space=pltpu.SMEM),
                  pl.BlockSpec((1, BLK, 2 * BLK), lambda g: (g, 0, 0))],
        out_specs=pl.BlockSpec((HEADS_PER_GROUP, BLK, 2 * BLK), lambda g: (g, 0, 0)),
        out_shape=jax.ShapeDtypeStruct((n_heads, BLK, 2 * BLK), F32),
        compiler_params=_cparams(("parallel",)),
        name="rel_bias",
    )(table, jnp.asarray(buckets))


def _proj_kernel(x_ref, mod_ref, g_ref, w_ref, proj_ref, h_ref):
    @pl.when(pl.program_id(2) == 0)
    def _():
        x = x_ref[0]
        ms = jnp.mean(x * x, axis=-1, keepdims=True)
        y = x * lax.rsqrt(ms + RMS_EPS) * g_ref[...]
        shift = mod_ref[0, :, 0:D_MODEL]
        scale = mod_ref[0, :, D_MODEL:2 * D_MODEL]
        h_ref[0] = (y * (1.0 + scale) + shift).astype(BF16)

    proj_ref[0] = _dot(h_ref[0], w_ref[...]).astype(BF16)


def _layer_block(arr, layer):
    tail = (0,) * (arr.ndim - 1)
    return pl.BlockSpec((None,) + arr.shape[1:], lambda *_: (layer,) + tail)


def _norm_proj(x, mod, norm_g, w_main, layer, tm=1024, tn=4096):
    B, S, D = x.shape
    N = w_main.shape[2]
    return pl.pallas_call(
        _proj_kernel,
        grid=(B, S // tm, N // tn),
        in_specs=[pl.BlockSpec((1, tm, D), lambda b, i, j: (b, i, 0)),
                  pl.BlockSpec((1, 1, 3 * D), lambda b, i, j: (layer * MOD_ROWS + b, 0, 0)),
                  _layer_block(norm_g, layer),
                  pl.BlockSpec((None, D, tn), lambda b, i, j: (layer, 0, j))],
        out_specs=[pl.BlockSpec((1, tm, tn), lambda b, i, j: (b, i, j)),
                   pl.BlockSpec((1, tm, D), lambda b, i, j: (b, i, 0))],
        out_shape=[jax.ShapeDtypeStruct((B, S, N), BF16),
                   jax.ShapeDtypeStruct((B, S, D), BF16)],
        compiler_params=_cparams(("parallel", "parallel", "arbitrary")),
        name="norm_proj",
    )(x, mod, norm_g, w_main)


GATHER_ROWS = MXU_DIM


def _group_proj_stages(h_ref, perm_ref, w_ref, o_ref, d):
    tm = h_ref.shape[1]
    per_res = GATHER_ROWS // d
    perm = perm_ref[...]
    for ck in range(tm // GATHER_ROWS):
        r0 = ck * GATHER_ROWS
        hp = _dot(perm, h_ref[0, r0:r0 + GATHER_ROWS, :]).astype(BF16)
        yield
        res = _dot(hp, w_ref[...]).astype(BF16)
        for r in range(d):
            o_ref[0, r, ck * per_res:(ck + 1) * per_res, :] = res[r * per_res:(r + 1) * per_res]
        yield


def _gather_perm(d):
    dst = np.arange(GATHER_ROWS)
    src = (dst % (GATHER_ROWS // d)) * d + dst // (GATHER_ROWS // d)
    return jnp.asarray(src[:, None] == np.arange(GATHER_ROWS)[None, :], BF16)


PV_MU_R, PV_MU_K, PV_MU_V, PV_W0, PV_A0, PV_KK, PV_KA, PV_V0 = range(8)
LORA_LANES = 256


def _pack_lora(paths):
    n_layers, d_model, _ = paths[0][1].shape
    used = sum(down.shape[2] for _, down, _ in paths)
    pad = jnp.zeros((n_layers, d_model, LORA_LANES - used), F32)
    keep = jnp.concatenate([(1.0 - mu)[:, :, None] * down for mu, down, _ in paths] + [pad], axis=2)
    shifted = jnp.concatenate([mu[:, :, None] * down for mu, down, _ in paths] + [pad], axis=2)
    ups, lane = [], 0
    for _, down, up in paths:
        rank = down.shape[2]
        ups.append(jnp.pad(up, ((0, 0), (lane, LORA_LANES - lane - rank), (0, 0))).astype(BF16))
        lane += rank
    return [jnp.concatenate([keep, shifted], axis=2).astype(BF16)] + ups


def _first_layer_blank(arr):
    return jnp.concatenate([jnp.zeros((1,) + arr.shape[1:], arr.dtype), arr], axis=0)
PV_ROWS = 16
PREV_ROWS = 16


def _shift_rows(t, prev_last):
    rolled = pltpu.roll(t, 1, axis=0)
    row = lax.broadcasted_iota(jnp.int32, t.shape, 0)
    return jnp.where(row == 0, prev_last, rolled)


def _rprep_stages(refs, has_vres):
    if has_vres:
        (h_ref, hp_ref, pr_ref, prp_ref, pk_ref, pkp_ref, pvv_ref, pvp_ref, vf_ref, pvec_ref,
         wd_ref, uw_ref, ua_ref, uv_ref, ones_ref, tril_ref,
         r_out, cum_out, k_out, v_out, a_out, b_out) = refs
    else:
        (h_ref, hp_ref, pr_ref, prp_ref, pk_ref, pkp_ref, pvv_ref, pvp_ref, pvec_ref,
         wd_ref, uw_ref, ua_ref, ones_ref, tril_ref,
         r_out, cum_out, k_out, v_out, a_out, b_out) = refs

    not_first = (pl.program_id(1) > 0).astype(F32)

    def prm(i):
        return pvec_ref[i:i + 1, :]

    def lerp_shift(cur_ref, prev_ref, mu):
        t = cur_ref[0].astype(F32)
        last = prev_ref[0, PREV_ROWS - 1:PREV_ROWS, :].astype(F32)
        return t + (_shift_rows(t, last * not_first) - t) * mu

    r = lerp_shift(pr_ref, prp_ref, prm(PV_MU_R))
    k = lerp_shift(pk_ref, pkp_ref, prm(PV_MU_K))
    v = lerp_shift(pvv_ref, pvp_ref, prm(PV_MU_V))
    yield

    wd = wd_ref[...]
    z2 = _dot(h_ref[0], wd)
    z_prev = _dot(hp_ref[0], wd[:, LORA_LANES:])[PREV_ROWS - 1:PREV_ROWS, :]
    z = z2[:, 0:LORA_LANES] + _shift_rows(z2[:, LORA_LANES:], z_prev * not_first)
    zb = z.astype(BF16)
    yield

    zw = prm(PV_W0) + _dot(jnp.tanh(z).astype(BF16), uw_ref[...])
    w = jnp.minimum(zw, 0.0) - jnp.log(1.0 + jnp.exp(-jnp.abs(zw))) - 0.5
    lw = -jnp.exp(w)
    yield
    hi, lo = _split2(lw)
    tril = tril_ref[...]
    for i in range(lw.shape[0] // MXU_DIM):
        blk = slice(MXU_DIM * i, MXU_DIM * (i + 1))
        cum_out[0, blk, :] = _dot(tril, hi[blk]) + _dot(tril, lo[blk])
    yield
    a = _sigmoid(prm(PV_A0) + _dot(zb, ua_ref[...]))
    if has_vres:
        mix = _sigmoid(prm(PV_V0) + _dot(zb, uv_ref[...]))
        v = v + (vf_ref[0].astype(F32) - v) * mix
    yield

    kk = k * prm(PV_KK)
    ss = _segsum64(kk * kk, ones_ref[...], split=False)
    kk = kk * lax.rsqrt(jnp.maximum(ss, 1e-24))
    yield
    r_out[0] = r.astype(BF16)
    k_out[0] = (k * (1.0 + (a - 1.0) * prm(PV_KA))).astype(BF16)
    v_out[0] = v.astype(BF16)
    a_out[0] = (-kk).astype(BF16)
    b_out[0] = (kk * a).astype(BF16)
    yield


N_GROUP_PROJ = N_GROUPS - 1


def _prep_groups_kernel(*refs, has_vres):
    n_in = len(refs) - 6 - N_GROUP_PROJ - 2 * N_GROUP_PROJ
    prep_in, rest = refs[:n_in], refs[n_in:]
    gp_in, outs = rest[:2 * N_GROUP_PROJ], rest[2 * N_GROUP_PROJ:]
    prep = _rprep_stages(tuple(prep_in) + tuple(outs[:6]), has_vres)
    h_ref = prep_in[0]
    groups = [_group_proj_stages(h_ref, gp_in[2 * n], gp_in[2 * n + 1], outs[6 + n], DILATIONS[n + 1])
              for n in range(N_GROUP_PROJ)]

    def all_groups():
        for gen in groups:
            yield from gen

    for _ in _interleave(prep, all_groups()):
        pass


def _rwkv_prep(h, proj, v_first, pvec, lora, ones_bd, w_groups, layer, tr=512):
    B, S, D = h.shape
    has_vres = v_first is not None
    rpb = tr // PREV_ROWS
    t = np.arange(MXU_DIM)
    tril_bd =jnp.asarray((t[None, :] <= t[:, None]) & (t[None, :] // CHUNK == t[:, None] // CHUNK), BF16)

    def cur(c):
        return pl.BlockSpec((1, tr, R_WIDTH), lambda b, i: (b, i, c))

    def prev(c):
        return pl.BlockSpec((1, PREV_ROWS, R_WIDTH), lambda b, i: (b, jnp.maximum(i * rpb - 1, 0), c))

    def full(arr):
        return pl.BlockSpec(arr.shape, lambda b, i: (0,) * arr.ndim)

    in_specs = [cur(0), prev(0)]
    args = [h, h]
    for c in (COL_R, COL_K, COL_V):
        in_specs += [cur(c // R_WIDTH), prev(c // R_WIDTH)]
        args += [proj, proj]
    if has_vres:
        in_specs.append(cur(0))
        args.append(v_first)
    for per_layer in [pvec] + list(lora):
        in_specs.append(_layer_block(per_layer, layer))
        args.append(per_layer)
    for const in (ones_bd, tril_bd):
        in_specs.append(full(const))
        args.append(const)
    out = [jax.ShapeDtypeStruct((B, S, R_WIDTH), F32 if n == 1 else BF16) for n in range(6)]
    out_specs = [cur(0)] * 6
    for n, w_group in enumerate(w_groups):
        d = DILATIONS[n + 1]
        width = w_group.shape[2]
        in_specs += [full(_gather_perm(d)), _layer_block(w_group, layer)]
        args += [_gather_perm(d), w_group]
        out.append(jax.ShapeDtypeStruct((B, d, S // d, width), BF16))
        out_specs.append(pl.BlockSpec((1, d, tr // d, width), lambda b, i: (b, 0, i, 0)))
    res = pl.pallas_call(
        functools.partial(_prep_groups_kernel, has_vres=has_vres),
        grid=(B, S // tr),
        in_specs=in_specs,
        out_specs=out_specs,
        out_shape=out,
        compiler_params=_cparams(("parallel", "parallel")),
        name="prep_groups",
    )(*args)
    return res[:6], res[6:]


def _interleave(*gens):
    live = list(gens)
    while live:
        for gen in list(live):
            if next(gen, StopIteration) is StopIteration:
                live.remove(gen)
        yield


def _scan_stages(r_ref, cum_ref, k_ref, v_ref, a_ref, b_ref, ga_ref, vec_ref, ones_ref,
                 y_ref, s_ref, *, nb, tt):
    C = CHUNK

    @pl.when(pl.program_id(0) == 0)
    def _():
        s_ref[...] = jnp.zeros(s_ref.shape, F32)

    row = lax.broadcasted_iota(jnp.int32, (C, MXU_DIM), 0)
    lane = lax.broadcasted_iota(jnp.int32, (C, MXU_DIM), 1)
    col = lane & (HEAD_DIM - 1)
    lhead = lane >> 6
    strict = col < row
    incl = col <= row
    eye = (col == row).astype(F32)
    head_masks = [lhead == hh for hh in range(HEADS_PER_TILE)]

    def bdrows(x):
        return jnp.concatenate([jnp.where(m, x, 0.0) for m in head_masks], axis=0).astype(BF16)

    def diag_blocks(full):
        acc = jnp.where(head_masks[0], full[0:C], 0.0)
        for hh in range(1, HEADS_PER_TILE):
            acc = acc + jnp.where(head_masks[hh], full[C * hh:C * (hh + 1)], 0.0)
        return acc

    row_full = lax.broadcasted_iota(jnp.int32, (C, R_WIDTH), 0)
    ones_bd = ones_ref[...]
    r_k = vec_ref[0:1, :]
    ln_g = vec_ref[1:2, :]
    ln_b = vec_ref[2:3, :]

    chains = [(bi, g) for bi in range(nb) for g in range(N_COLGROUPS)]
    insts = [(ck, bi, g) for ck in range(SCAN_CHUNKS) for bi, g in chains]


    def load(gi):
        rows = [slice((gi * SCAN_CHUNKS + ck) * C, (gi * SCAN_CHUNKS + ck + 1) * C) for ck in range(SCAN_CHUNKS)]
        ops = {}
        for ck in range(SCAN_CHUNKS):
            for bi in range(nb):
                cum = cum_ref[bi, rows[ck], :]
                r = r_ref[bi, rows[ck], :].astype(F32)
                k = k_ref[bi, rows[ck], :].astype(F32)
                v = v_ref[bi, rows[ck], :].astype(F32)
                a = a_ref[bi, rows[ck], :].astype(F32)
                b = b_ref[bi, rows[ck], :].astype(F32)
                total = cum[C - 1:C, :]
                p_in = jnp.exp(cum)
                p_inv = jnp.exp(-cum)
                p_rest = jnp.exp(total - cum)
                p_before = jnp.where(row_full == 0, 1.0, pltpu.roll(p_in, 1, axis=0))
                ops[ck, bi] = dict(r=r, k=k, v=v, a_t=a * p_before, r_t=r * p_in, b_t=b * p_inv,
                                   k_t=k * p_inv, bp=b * p_rest, kp=k * p_rest, p_all=jnp.exp(total))
        return dict(rows=rows, ops=ops)

    def part(ctx, name, ck, bi, g):
        return ctx["ops"][ck, bi][name][:, MXU_DIM * g:MXU_DIM * (g + 1)]

    def independent(ctx):
        res = [_dot_nt(jnp.concatenate([part(ctx, "a_t", *i), part(ctx, "r_t", *i)], axis=0).astype(BF16),
                       jnp.concatenate([bdrows(part(ctx, "b_t", *i)), bdrows(part(ctx, "k_t", *i))], axis=0))
               for i in insts]
        yield
        a_ab = [jnp.where(strict, x[0:C, 0:MXU_DIM], 0.0) for x in res]
        a_ak = [jnp.where(strict, x[0:C, MXU_DIM:], 0.0) for x in res]
        ctx["a_rb"] = [jnp.where(incl, x[C:, 0:MXU_DIM], 0.0).astype(BF16) for x in res]
        a_rk = [jnp.where(incl, x[C:, MXU_DIM:], 0.0) for x in res]

        pw = [_dot(x.astype(BF16), bdrows(x)) for x in a_ab]
        tinv = [eye + x for x in a_ab]
        yield
        for _ in range(4):
            both = [_dot(jnp.concatenate([p, t], axis=0).astype(BF16), bdrows(p)) for p, t in zip(pw, tinv)]
            tinv = [t + x[C:] for t, x in zip(tinv, both)]
            pw = [x[0:C] for x in both]
            yield
        tinv = [t + _dot(t.astype(BF16), bdrows(p)) for p, t in zip(pw, tinv)]
        yield
        tax = [_dot(t.astype(BF16), jnp.concatenate([bdrows(part(ctx, "a_t", *i)), bdrows(x)], axis=1))
               for t, x, i in zip(tinv, a_ak, insts)]
        yield
        ctx["from_v"] = [_dot(jnp.concatenate([x[:, MXU_DIM:], ark], axis=0).astype(BF16),
                              bdrows(part(ctx, "v", *i))) for x, ark, i in zip(tax, a_rk, insts)]
        ctx["tax"] = tax
        yield

    def dependent(ctx, carried):
        tax, from_v, a_rb = ctx["tax"], ctx["from_v"], ctx["a_rb"]
        st = carried["st"]
        y = {}
        for ck in range(SCAN_CHUNKS):
            sel = range(ck * len(chains), (ck + 1) * len(chains))
            from_state = [_dot_nt(jnp.concatenate([tax[n][:, 0:MXU_DIM], part(ctx, "r_t", *insts[n])],
                                                  axis=0).astype(BF16), bdrows(s))
                          for n, s in zip(sel, st)]
            yield
            u = [x[0:C] + from_v[n][0:C] for x, n in zip(from_state, sel)]
            for x, n, uu in zip(from_state, sel, u):
                y[insts[n]] = x[C:] + from_v[n][C:] + _dot(a_rb[n], bdrows(uu))
            upd = [_dot(jnp.concatenate([uu, part(ctx, "v", *insts[n])], axis=0).T.astype(BF16),
                        jnp.concatenate([part(ctx, "bp", *insts[n]), part(ctx, "kp", *insts[n])],
                                        axis=0).astype(BF16))
                   for uu, n in zip(u, sel)]
            yield
            st = [s_old * part(ctx, "p_all", *insts[n]) + diag_blocks(x) for s_old, x, n in zip(st, upd, sel)]
        carried["st"] = st

        for ck in range(SCAN_CHUNKS):
            for bi in range(nb):
                p = ctx["ops"][ck, bi]
                yc = jnp.concatenate([y[ck, bi, g] for g in range(N_COLGROUPS)], axis=1)
                mean = _segsum64(yc, ones_bd, split=True) * (1.0 / HEAD_DIM)
                yield
                yd = yc - mean
                var = _segsum64(yd * yd, ones_bd, split=False) * (1.0 / HEAD_DIM)
                yn = yd * lax.rsqrt(var + GN_EPS) * ln_g + ln_b
                bonus = _segsum64(p["r"] * p["k"] * r_k, ones_bd, split=False) * p["v"]
                rows = ctx["rows"][ck]
                y_ref[bi, rows, :] = ((yn + bonus) * _silu(ga_ref[bi, rows, :].astype(F32))).astype(BF16)
                yield

    carried = {"st": [s_ref[bi * N_COLGROUPS + g] for bi, g in chains]}
    n_groups = tt // (C * SCAN_CHUNKS)
    ctx = load(0)
    yield from independent(ctx)
    for gi in range(1, n_groups):
        nxt = load(gi)
        yield from _interleave(dependent(ctx, carried), independent(nxt))
        ctx = nxt
    yield from dependent(ctx, carried)
    for (bi, g), s_new in zip(chains, carried["st"]):
        s_ref[bi * N_COLGROUPS + g] = s_new


ATT_TILE = 2048
ATT_UNROLL = (2, 2, 2)


def _attn_stages(q_refs, k_refs, v_refs, kp_refs, vp_refs, gb_ref, bias_ref, y_ref, o_refs, l_refs,
                 is_first):
    prev_limit = jnp.where(is_first, BLK, 0)
    ki = lax.broadcasted_iota(jnp.int32, (2 * BLK, 2 * BLK), 1)
    head0 = lax.broadcasted_iota(jnp.int32, (BLK, LANES), 1) < HEAD_DIM
    ones_cols = jnp.ones((2 * BLK, LANES), BF16)
    zero = jnp.zeros((BLK, LANES), BF16)

    def process(blocks):
        q2s, kws, vws, bias2s, stores = [], [], [], [], []
        for g, sub, res in blocks:
            d = DILATIONS[g]
            base = sub * (BLK * d) + res
            q = q_refs[g][0, res, sub * BLK:(sub + 1) * BLK, :]
            q2s.append(jnp.concatenate([jnp.where(head0, q, zero), jnp.where(head0, zero, q)], axis=0))
            if sub == 0:
                kw = jnp.concatenate([kp_refs[g][0, res], k_refs[g][0, res, 0:BLK, :]], axis=0)
                vw = jnp.concatenate([vp_refs[g][0, res], v_refs[g][0, res, 0:BLK, :]], axis=0)
            else:
                kw = k_refs[g][0, res, (sub - 1) * BLK:(sub + 1) * BLK, :]
                vw = v_refs[g][0, res, (sub - 1) * BLK:(sub + 1) * BLK, :]
            kws.append(kw)
            vws.append(jnp.concatenate([vw, ones_cols], axis=1))
            bias2s.append(bias_ref[g, 0].reshape(2 * BLK, 2 * BLK))
            stores.append((g, pl.ds(base, BLK) if d == 1 else pl.ds(base, BLK, stride=d)))
        logits = [jnp.where(bias2 > 0.5 * NEG_INF, _dot_nt(q2, kw) + bias2, NEG_INF)
                  for q2, kw, bias2 in zip(q2s, kws, bias2s)]
        logits = [jnp.where(ki < prev_limit, NEG_INF, x) if blk[1] == 0 else x
                  for x, blk in zip(logits, blocks)]
        yield
        ms = [jnp.max(x, axis=-1, keepdims=True) for x in logits]
        ps = [jnp.exp2(x - m).astype(BF16) for x, m in zip(logits, ms)]
        pvs = [_dot(p, vw) for p, vw in zip(ps, vws)]
        yield
        for (g, rows), pv, m in zip(stores, pvs, ms):
            num = jnp.where(head0, pv[0:BLK, 0:LANES], pv[BLK:, 0:LANES])
            den = jnp.where(head0, pv[0:BLK, LANES:], pv[BLK:, LANES:])
            o_refs[g][rows, :] = num / den
            l_refs[g][rows, :] = jnp.where(head0, m[0:BLK], m[BLK:]) + jnp.log2(den)

    for g, d in enumerate(DILATIONS):
        blocks = [(g, sub, res) for sub in range(ATT_TILE // (BLK * d)) for res in range(d)]
        for n in range(0, len(blocks), ATT_UNROLL[g]):
            yield from process(blocks[n:n + ATT_UNROLL[g]])

    l0, l1, l2 = l_refs[0][...], l_refs[1][...], l_refs[2][...]
    m = jnp.maximum(jnp.maximum(l0, l1), l2)
    w0, w1, w2 = jnp.exp2(l0 - m), jnp.exp2(l1 - m), jnp.exp2(l2 - m)
    y = (w0 * o_refs[0][...] + w1 * o_refs[1][...] + w2 * o_refs[2][...]) / (w0 + w1 + w2)
    y_ref[0] = (y * _silu(gb_ref[0].astype(F32))).astype(BF16)
    yield


N_SCAN_REFS = 9
N_ATTN_REFS = 17
MIXERS_VMEM_LIMIT = 62 * 1024 * 1024


def _mixers_kernel(*refs, nb, tt, tiles_per_seq, n_tiles):
    scan_in = refs[:N_SCAN_REFS]
    attn_in = refs[N_SCAN_REFS:N_SCAN_REFS + N_ATTN_REFS]
    ya_ref, yb_ref, s_ref = refs[N_SCAN_REFS + N_ATTN_REFS:N_SCAN_REFS + N_ATTN_REFS + 3]
    scratch = refs[N_SCAN_REFS + N_ATTN_REFS + 3:]
    is_first = ((pl.program_id(0) % n_tiles) % tiles_per_seq) == 0
    scan = _scan_stages(*scan_in, ya_ref, s_ref, nb=nb, tt=tt)
    attn = _attn_stages(attn_in[0:3], attn_in[3:6], attn_in[6:9], attn_in[9:12], attn_in[12:15],
                        attn_in[15], attn_in[16], yb_ref, scratch[0:3], scratch[3:6], is_first)
    for _ in _interleave(scan, attn):
        pass


def _mixers(r, cum, k, v, a, b, main, groups, vec, ones_bd, bias5, layer):
    B, S, W = r.shape
    n_pairs = HEADS_PER_GROUP // 2
    tiles_per_seq = S // ATT_TILE
    n_tiles = B * tiles_per_seq
    n_steps = n_pairs * n_tiles
    tt = S // n_steps
    assert tt * n_steps == S and tt % (CHUNK * SCAN_CHUNKS) == 0

    scan_spec = pl.BlockSpec((B, tt, W), lambda t: (0, t, 0))

    def full(arr):
        return pl.BlockSpec(arr.shape, lambda t: (0,) * arr.ndim)

    scan_specs = [scan_spec] * 6 + [pl.BlockSpec((B, tt, W), lambda t: (0, t, COL_GA // W)),
                                    _layer_block(vec, layer), full(ones_bd)]

    arrays = [main.reshape(B, 1, S, MAIN_WIDTH)] + list(groups)
    col_base = [COL_A0 // LANES, 0, 0]

    def where(t):
        tile = t % n_tiles
        return t // n_tiles, tile // tiles_per_seq, tile % tiles_per_seq

    def cur(g, part):
        d = DILATIONS[g]
        c0 = col_base[g] + part * (A_OUT_WIDTH // LANES)

        def index(t):
            hp, bi, ti = where(t)
            return bi, 0, ti, c0 + hp
        return pl.BlockSpec((1, d, ATT_TILE // d, LANES), index)

    def prev(g, part):
        d = DILATIONS[g]
        c0 = col_base[g] + part * (A_OUT_WIDTH // LANES)
        rb = ATT_TILE // (BLK * d)

        def index(t):
            hp, bi, ti = where(t)
            return bi, 0, jnp.maximum(ti * rb - 1, 0), c0 + hp
        return pl.BlockSpec((1, d, BLK, LANES), index)

    def tile(col0):
        def index(t):
            hp, bi, ti = where(t)
            return bi, ti, col0 // LANES + hp
        return pl.BlockSpec((1, ATT_TILE, LANES), index)

    attn_specs = ([cur(g, 0) for g in range(N_GROUPS)] + [cur(g, 1) for g in range(N_GROUPS)]
                  + [cur(g, 2) for g in range(N_GROUPS)]
                  + [prev(g, 1) for g in range(N_GROUPS)] + [prev(g, 2) for g in range(N_GROUPS)]
                  + [tile(COL_GB),
                     pl.BlockSpec((N_GROUPS, 1, 2, BLK, 2 * BLK), lambda t: (0, t // n_tiles, 0, 0, 0))])
    assert len(scan_specs) == N_SCAN_REFS and len(attn_specs) == N_ATTN_REFS
    return pl.pallas_call(
        functools.partial(_mixers_kernel, nb=B, tt=tt, tiles_per_seq=tiles_per_seq, n_tiles=n_tiles),
        grid=(n_steps,),
        in_specs=scan_specs + attn_specs,
        out_specs=[scan_spec, tile(0)],
        out_shape=[jax.ShapeDtypeStruct((B, S, W), BF16), jax.ShapeDtypeStruct((B, S, A_OUT_WIDTH), BF16)],
        scratch_shapes=([pltpu.VMEM((B * N_COLGROUPS, HEAD_DIM, MXU_DIM), F32)]
                        + [pltpu.VMEM((ATT_TILE, LANES), F32)] * 6),
        compiler_params=_cparams(("arbitrary",), MIXERS_VMEM_LIMIT),
        name="mixers",
    )(r, cum, k, v, a, b, main, vec, ones_bd, *(arrays * 5), main, bias5)


def _merge_kernel(ya_ref, yb_ref, ma_ref, mb_ref, x_ref, mod_ref, wa_ref, wb_ref, wo_ref, fg_ref,
                  o_ref, *, final_norm):
    pa = _dot(ya_ref[0], wa_ref[...])
    pb = _dot(yb_ref[0], wb_ref[...])
    merged = _sigmoid(ma_ref[0].astype(F32)) * pa + _sigmoid(mb_ref[0].astype(F32)) * pb
    out = _dot(merged.astype(BF16), wo_ref[...])
    gate = mod_ref[0, :, 2 * D_MODEL:3 * D_MODEL]
    xn = x_ref[0] + gate * out
    if final_norm:
        ms = jnp.mean(xn * xn, axis=-1, keepdims=True)
        xn = xn * lax.rsqrt(ms + RMS_EPS) * fg_ref[...]
    o_ref[0] = xn


def _merge(ya, yb, proj, x, mod, wa, wb, wo, final_g, final_norm, layer, tm=1024):
    B, S, D = x.shape

    def rows(width, c):
        return pl.BlockSpec((1, tm, width), lambda b, i: (b, i, c))

    def full(arr):
        return pl.BlockSpec(arr.shape, lambda b, i: (0,) * arr.ndim)

    return pl.pallas_call(
        functools.partial(_merge_kernel, final_norm=final_norm),
        grid=(B, S // tm),
        in_specs=[rows(R_WIDTH, 0), rows(A_OUT_WIDTH, 0),
                  rows(D, COL_MA // D), rows(D, COL_MB // D), rows(D, 0),
                  pl.BlockSpec((1, 1, 3 * D), lambda b, i: (layer * MOD_ROWS + b, 0, 0)),
                  _layer_block(wa, layer), _layer_block(wb, layer), _layer_block(wo, layer), full(final_g)],
        out_specs=rows(D, 0),
        out_shape=jax.ShapeDtypeStruct((B, S, D), F32),
        compiler_params=_cparams(("parallel", "parallel")),
        name="merge",
    )(ya, yb, proj, proj, x, mod, wa, wb, wo, final_g)


def _segment_ones():
    idx = np.arange(MXU_DIM)
    return jnp.asarray(idx[:, None] // HEAD_DIM == idx[None, :] // HEAD_DIM, BF16)


def kernel(x, c, norm_g, ada_w, ada_b, w_in, rwkv_mu_rkv, rwkv_mu_wa, rwkv_w0, rwkv_w1, rwkv_w2, rwkv_a0, rwkv_a1, rwkv_a2, rwkv_k_k, rwkv_k_a, rwkv_r_k, rwkv_ln_g, rwkv_ln_b, rwkv_mu_v, rwkv_v0, rwkv_v1, rwkv_v2, w_branch_a, w_branch_b, w_out, rel_bias, final_g):
    B, S, D = x.shape
    assert D == D_MODEL and S % ATT_TILE == 0 and w_in.shape[2] == PROJ_WIDTH
    ones_bd = _segment_ones()
    mod = _adaln_mod(c, ada_w, ada_b).reshape(DEPTH * MOD_ROWS, 1, 3 * D)
    bias = _rel_bias(rel_bias).reshape(N_GROUPS, HEADS_PER_GROUP // 2, 2, BLK, 2 * BLK)

    def cols(start, width):
        return w_in[:, :, start:start + width]

    def group_cols(g):
        return [cols(W_AQ + A_OUT_WIDTH * g, A_OUT_WIDTH) * (LOG2E / math.sqrt(HEAD_DIM)),
                cols(W_AK + A_OUT_WIDTH * g, A_OUT_WIDTH), cols(W_AV + A_OUT_WIDTH * g, A_OUT_WIDTH)]

    w_main = jnp.concatenate(
        [cols(W_R, 4 * R_WIDTH), cols(W_MA, 2 * D_MODEL), cols(W_GB, A_OUT_WIDTH)] + group_cols(0),
        axis=2).astype(BF16)
    w_groups = [jnp.concatenate(group_cols(g), axis=2).astype(BF16) for g in range(1, N_GROUPS)]
    zeros_rows = jnp.zeros((DEPTH, D), F32)
    pvec = jnp.stack([rwkv_mu_rkv[:, 0], rwkv_mu_rkv[:, 1], rwkv_mu_rkv[:, 2], rwkv_w0, rwkv_a0, rwkv_k_k,
                      rwkv_k_a, _first_layer_blank(rwkv_v0)] + [zeros_rows] * (PV_ROWS - 8), axis=1)
    lora = _pack_lora([(rwkv_mu_wa[:, 0], rwkv_w1, rwkv_w2), (rwkv_mu_wa[:, 1], rwkv_a1, rwkv_a2),
                       (_first_layer_blank(rwkv_mu_v), _first_layer_blank(rwkv_v1),
                        _first_layer_blank(rwkv_v2))])
    vec = jnp.stack([rwkv_r_k.reshape(DEPTH, -1), rwkv_ln_g, rwkv_ln_b] + [zeros_rows] * 5, axis=1)
    norm_g3 = norm_g.reshape(DEPTH, 1, D)
    wa, wb, wo = w_branch_a.astype(BF16), w_branch_b.astype(BF16), w_out.astype(BF16)

    v_first = None
    for i in range(DEPTH):
        proj, h = _norm_proj(x, mod, norm_g3, w_main, i)
        (r, cum, k, v, a, b), groups = _rwkv_prep(h, proj, v_first, pvec, lora if i > 0 else lora[:3],
                                                  ones_bd, w_groups, i)
        if i == 0:
            v_first = v
        y_a, y_b = _mixers(r, cum, k, v, a, b, proj, groups, vec, ones_bd, bias, i)
        x = _merge(y_a, y_b, proj, x, mod, wa, wb, wo, final_g.reshape(1, D),
                   final_norm=(i == DEPTH - 1), layer=i)
    return x
```

```python
import functools
import math

import numpy as np
import jax
import jax.numpy as jnp
from jax import lax
from jax.experimental import pallas as pl
from jax.experimental.pallas import tpu as pltpu

F32 = jnp.float32
BF16 = jnp.bfloat16

D_MODEL = 1024
DEPTH = 2
HEAD_DIM = 64
R_WIDTH = 1024
N_GROUPS = 3
HEADS_PER_GROUP = 8
DILATIONS = (1, 4, 16)
BLK = 128
A_QK_WIDTH = 1536
A_OUT_WIDTH = 512
NUM_BUCKETS = 32
MAX_DISTANCE = 2048
PROJ_WIDTH = 4 * R_WIDTH + 3 * A_QK_WIDTH + A_OUT_WIDTH + 2 * D_MODEL
RMS_EPS = 1e-6
GN_EPS = 64e-5
NEG_INF = -1e30
LOG2E = math.log2(math.e)

LANES = 128
MXU_DIM = 256
HEADS_PER_TILE = MXU_DIM // HEAD_DIM
N_COLGROUPS = R_WIDTH // MXU_DIM
CHUNK = 64
SCAN_CHUNKS = 2

W_R, W_K, W_V, W_GA = 0, 1024, 2048, 3072
W_AQ, W_AK, W_AV = 4096, 5632, 7168
W_GB, W_MA, W_MB = 8704, 9216, 10240
COL_R, COL_K, COL_V, COL_GA, COL_MA, COL_MB, COL_GB, COL_A0 = 0, 1024, 2048, 3072, 4096, 5120, 6144, 6656
MAIN_WIDTH = 8192
GROUP_WIDTH = 3 * A_OUT_WIDTH

VMEM_LIMIT = 56 * 1024 * 1024
MOD_ROWS = 8


def _cparams(sem, vmem_limit=VMEM_LIMIT):
    return pltpu.CompilerParams(dimension_semantics=sem, vmem_limit_bytes=vmem_limit)


def _sigmoid(z):
    return 1.0 / (1.0 + jnp.exp(-z))


def _silu(z):
    return z * _sigmoid(z)


def _dot(a, b):
    return jnp.dot(a, b, preferred_element_type=F32)


def _dot_nt(a, b):
    return lax.dot_general(a, b, (((1,), (1,)), ((), ())), preferred_element_type=F32)


def _split2(x):
    hi = x.astype(BF16)
    lo = (x - hi.astype(F32)).astype(BF16)
    return hi, lo


def _segsum64(x, ones_bd, split):
    n = x.shape[0]
    xs = jnp.concatenate([x[:, MXU_DIM * g:MXU_DIM * (g + 1)] for g in range(N_COLGROUPS)], axis=0)
    if split:
        hi, lo = _split2(xs)
        s = _dot(hi, ones_bd) + _dot(lo, ones_bd)
    else:
        s = _dot(xs.astype(BF16), ones_bd)
    return jnp.concatenate([s[n * g:n * (g + 1)] for g in range(N_COLGROUPS)], axis=1)


def _mod_kernel(c_ref, w_ref, b_ref, o_ref):
    s = _silu(c_ref[...])
    o_ref[0] = _dot(s.astype(BF16), w_ref[0].astype(BF16)) + b_ref[0]


def _adaln_mod(c, ada_w, ada_b):
    L = ada_w.shape[0]
    B = c.shape[0]
    c_rows = jnp.pad(c, ((0, MOD_ROWS - B), (0, 0)))
    nj = 3
    return pl.pallas_call(
        _mod_kernel,
        grid=(L, nj),
        in_specs=[pl.BlockSpec((MOD_ROWS, D_MODEL), lambda l, j: (0, 0)),
                  pl.BlockSpec((1, D_MODEL, D_MODEL), lambda l, j: (l, 0, j)),
                  pl.BlockSpec((1, 1, D_MODEL), lambda l, j: (l, 0, j))],
        out_specs=pl.BlockSpec((1, MOD_ROWS, D_MODEL), lambda l, j: (l, 0, j)),
        out_shape=jax.ShapeDtypeStruct((L, MOD_ROWS, 3 * D_MODEL), F32),
        compiler_params=_cparams(("parallel", "parallel")),
        name="adaln_mod",
    )(c_rows, ada_w, ada_b.reshape(L, 1, 3 * D_MODEL))


def _t5_bucket(dist):
    max_exact = NUM_BUCKETS // 2
    safe = np.maximum(dist, 1).astype(np.float32)
    large = max_exact + (np.log(safe / max_exact) / math.log(MAX_DISTANCE / max_exact)
                         * (NUM_BUCKETS - max_exact)).astype(np.int32)
    large = np.minimum(large, NUM_BUCKETS - 1)
    return np.where(dist < max_exact, dist, large).astype(np.int32)


def _bias_kernel(tab_ref, bucket_ref, o_ref):
    g = pl.program_id(0)
    bk = bucket_ref[0]
    for hh in range(HEADS_PER_GROUP):
        h = g * HEADS_PER_GROUP + hh
        acc = jnp.zeros(bk.shape, F32)
        for b in range(NUM_BUCKETS):
            acc = jnp.where(bk == b, tab_ref[h * NUM_BUCKETS + b], acc)
        o_ref[hh] = jnp.where(bk >= 0, acc * LOG2E, NEG_INF)


def _rel_bias(rel_bias):
    n_heads = rel_bias.shape[1]
    qi = np.arange(BLK)[:, None]
    ki = np.arange(2 * BLK)[None, :]
    delta = qi + BLK - ki
    band = (delta >= 0) & (delta <= BLK)
    buckets = np.stack([np.where(band, _t5_bucket(np.maximum(delta, 0) * d), -1)
                        for d in DILATIONS]).astype(np.int32)
    table = rel_bias.T.reshape(-1)
    return pl.pallas_call(
        _bias_kernel,
        grid=(n_heads // HEADS_PER_GROUP,),
        in_specs=[pl.BlockSpec(memory_space=pltpu.SMEM),
                  pl.BlockSpec((1, BLK, 2 * BLK), lambda g: (g, 0, 0))],
        out_specs=pl.BlockSpec((HEADS_PER_GROUP, BLK, 2 * BLK), lambda g: (g, 0, 0)),
        out_shape=jax.ShapeDtypeStruct((n_heads, BLK, 2 * BLK), F32),
        compiler_params=_cparams(("parallel",)),
        name="rel_bias",
    )(table, jnp.asarray(buckets))


def _proj_kernel(x_ref, mod_ref, g_ref, w_ref, proj_ref, h_ref):
    @pl.when(pl.program_id(2) == 0)
    def _():
        x = x_ref[0]
        ms = jnp.mean(x * x, axis=-1, keepdims=True)
        y = x * lax.rsqrt(ms + RMS_EPS) * g_ref[...]
        shift = mod_ref[0, :, 0:D_MODEL]
        scale = mod_ref[0, :, D_MODEL:2 * D_MODEL]
        h_ref[0] = (y * (1.0 + scale) + shift).astype(BF16)

    proj_ref[0] = _dot(h_ref[0], w_ref[...]).astype(BF16)


def _layer_block(arr, layer):
    tail = (0,) * (arr.ndim - 1)
    return pl.BlockSpec((None,) + arr.shape[1:], lambda *_: (layer,) + tail)


def _norm_proj(x, mod, norm_g, w_main, layer, tm=1024, tn=4096):
    B, S, D = x.shape
    N = w_main.shape[2]
    return pl.pallas_call(
        _proj_kernel,
        grid=(B, S // tm, N // tn),
        in_specs=[pl.BlockSpec((1, tm, D), lambda b, i, j: (b, i, 0)),
                  pl.BlockSpec((1, 1, 3 * D), lambda b, i, j: (layer * MOD_ROWS + b, 0, 0)),
                  _layer_block(norm_g, layer),
                  pl.BlockSpec((None, D, tn), lambda b, i, j: (layer, 0, j))],
        out_specs=[pl.BlockSpec((1, tm, tn), lambda b, i, j: (b, i, j)),
                   pl.BlockSpec((1, tm, D), lambda b, i, j: (b, i, 0))],
        out_shape=[jax.ShapeDtypeStruct((B, S, N), BF16),
                   jax.ShapeDtypeStruct((B, S, D), BF16)],
        compiler_params=_cparams(("parallel", "parallel", "arbitrary")),
        name="norm_proj",
    )(x, mod, norm_g, w_main)


GATHER_ROWS = MXU_DIM


def _group_proj_stages(h_ref, perm_ref, w_ref, o_ref, d):
    tm = h_ref.shape[1]
    per_res = GATHER_ROWS // d
    perm = perm_ref[...]
    for ck in range(tm // GATHER_ROWS):
        r0 = ck * GATHER_ROWS
        hp = _dot(perm, h_ref[0, r0:r0 + GATHER_ROWS, :]).astype(BF16)
        yield
        res = _dot(hp, w_ref[...]).astype(BF16)
        for r in range(d):
            o_ref[0, r, ck * per_res:(ck + 1) * per_res, :] = res[r * per_res:(r + 1) * per_res]
        yield


def _gather_perm(d):
    dst = np.arange(GATHER_ROWS)
    src = (dst % (GATHER_ROWS // d)) * d + dst // (GATHER_ROWS // d)
    return jnp.asarray(src[:, None] == np.arange(GATHER_ROWS)[None, :], BF16)


PV_MU_R, PV_MU_K, PV_MU_V, PV_W0, PV_A0, PV_KK, PV_KA, PV_V0 = range(8)
LORA_LANES = 256


def _pack_lora(paths):
    n_layers, d_model, _ = paths[0][1].shape
    used = sum(down.shape[2] for _, down, _ in paths)
    pad = jnp.zeros((n_layers, d_model, LORA_LANES - used), F32)
    keep = jnp.concatenate([(1.0 - mu)[:, :, None] * down for mu, down, _ in paths] + [pad], axis=2)
    shifted = jnp.concatenate([mu[:, :, None] * down for mu, down, _ in paths] + [pad], axis=2)
    ups, lane = [], 0
    for _, down, up in paths:
        rank = down.shape[2]
        ups.append(jnp.pad(up, ((0, 0), (lane, LORA_LANES - lane - rank), (0, 0))).astype(BF16))
        lane += rank
    return [jnp.concatenate([keep, shifted], axis=2).astype(BF16)] + ups


def _first_layer_blank(arr):
    return jnp.concatenate([jnp.zeros((1,) + arr.shape[1:], arr.dtype), arr], axis=0)
PV_ROWS = 16
PREV_ROWS = 16


def _shift_rows(t, prev_last):
    rolled = pltpu.roll(t, 1, axis=0)
    row = lax.broadcasted_iota(jnp.int32, t.shape, 0)
    return jnp.where(row == 0, prev_last, rolled)


def _rprep_stages(refs, has_vres):
    if has_vres:
        (h_ref, hp_ref, pr_ref, prp_ref, pk_ref, pkp_ref, pvv_ref, pvp_ref, vf_ref, pvec_ref,
         wd_ref, uw_ref, ua_ref, uv_ref, ones_ref, tril_ref,
         r_out, cum_out, k_out, v_out, a_out, b_out) = refs
    else:
        (h_ref, hp_ref, pr_ref, prp_ref, pk_ref, pkp_ref, pvv_ref, pvp_ref, pvec_ref,
         wd_ref, uw_ref, ua_ref, ones_ref, tril_ref,
         r_out, cum_out, k_out, v_out, a_out, b_out) = refs

    not_first = (pl.program_id(1) > 0).astype(F32)

    def prm(i):
        return pvec_ref[i:i + 1, :]

    def lerp_shift(cur_ref, prev_ref, mu):
        t = cur_ref[0].astype(F32)
        last = prev_ref[0, PREV_ROWS - 1:PREV_ROWS, :].astype(F32)
        return t + (_shift_rows(t, last * not_first) - t) * mu

    r = lerp_shift(pr_ref, prp_ref, prm(PV_MU_R))
    k = lerp_shift(pk_ref, pkp_ref, prm(PV_MU_K))
    v = lerp_shift(pvv_ref, pvp_ref, prm(PV_MU_V))
    yield

    wd = wd_ref[...]
    z2 = _dot(h_ref[0], wd)
    z_prev = _dot(hp_ref[0], wd[:, LORA_LANES:])[PREV_ROWS - 1:PREV_ROWS, :]
    z = z2[:, 0:LORA_LANES] + _shift_rows(z2[:, LORA_LANES:], z_prev * not_first)
    zb = z.astype(BF16)
    yield

    zw = prm(PV_W0) + _dot(jnp.tanh(z).astype(BF16), uw_ref[...])
    w = jnp.minimum(zw, 0.0) - jnp.log(1.0 + jnp.exp(-jnp.abs(zw))) - 0.5
    lw = -jnp.exp(w)
    yield
    hi, lo = _split2(lw)
    tril = tril_ref[...]
    for i in range(lw.shape[0] // MXU_DIM):
        blk = slice(MXU_DIM * i, MXU_DIM * (i + 1))
        cum_out[0, blk, :] = _dot(tril, hi[blk]) + _dot(tril, lo[blk])
    yield
    a = _sigmoid(prm(PV_A0) + _dot(zb, ua_ref[...]))
    if has_vres:
        mix = _sigmoid(prm(PV_V0) + _dot(zb, uv_ref[...]))
        v = v + (vf_ref[0].astype(F32) - v) * mix
    yield

    kk = k * prm(PV_KK)
    ss = _segsum64(kk * kk, ones_ref[...], split=False)
    kk = kk * lax.rsqrt(jnp.maximum(ss, 1e-24))
    yield
    r_out[0] = r.astype(BF16)
    k_out[0] = (k * (1.0 + (a - 1.0) * prm(PV_KA))).astype(BF16)
    v_out[0] = v.astype(BF16)
    a_out[0] = (-kk).astype(BF16)
    b_out[0] = (kk * a).astype(BF16)
    yield


N_GROUP_PROJ = N_GROUPS - 1


def _prep_groups_kernel(*refs, has_vres):
    n_in = len(refs) - 6 - N_GROUP_PROJ - 2 * N_GROUP_PROJ
    prep_in, rest = refs[:n_in], refs[n_in:]
    gp_in, outs = rest[:2 * N_GROUP_PROJ], rest[2 * N_GROUP_PROJ:]
    prep = _rprep_stages(tuple(prep_in) + tuple(outs[:6]), has_vres)
    h_ref = prep_in[0]
    groups = [_group_proj_stages(h_ref, gp_in[2 * n], gp_in[2 * n + 1], outs[6 + n], DILATIONS[n + 1])
              for n in range(N_GROUP_PROJ)]

    def all_groups():
        for gen in groups:
            yield from gen

    for _ in _interleave(prep, all_groups()):
        pass


def _rwkv_prep(h, proj, v_first, pvec, lora, ones_bd, w_groups, layer, tr=512):
    B, S, D = h.shape
    has_vres = v_first is not None
    rpb = tr // PREV_ROWS
    t = np.arange(MXU_DIM)
    tril_bd =jnp.asarray((t[None, :] <= t[:, None]) & (t[None, :] // CHUNK == t[:, None] // CHUNK), BF16)

    def cur(c):
        return pl.BlockSpec((1, tr, R_WIDTH), lambda b, i: (b, i, c))

    def prev(c):
        return pl.BlockSpec((1, PREV_ROWS, R_WIDTH), lambda b, i: (b, jnp.maximum(i * rpb - 1, 0), c))

    def full(arr):
        return pl.BlockSpec(arr.shape, lambda b, i: (0,) * arr.ndim)

    in_specs = [cur(0), prev(0)]
    args = [h, h]
    for c in (COL_R, COL_K, COL_V):
        in_specs += [cur(c // R_WIDTH), prev(c // R_WIDTH)]
        args += [proj, proj]
    if has_vres:
        in_specs.append(cur(0))
        args.append(v_first)
    for per_layer in [pvec] + list(lora):
        in_specs.append(_layer_block(per_layer, layer))
        args.append(per_layer)
    for const in (ones_bd, tril_bd):
        in_specs.append(full(const))
        args.append(const)
    out = [jax.ShapeDtypeStruct((B, S, R_WIDTH), F32 if n == 1 else BF16) for n in range(6)]
    out_specs = [cur(0)] * 6
    for n, w_group in enumerate(w_groups):
        d = DILATIONS[n + 1]
        width = w_group.shape[2]
        in_specs += [full(_gather_perm(d)), _layer_block(w_group, layer)]
        args += [_gather_perm(d), w_group]
        out.append(jax.ShapeDtypeStruct((B, d, S // d, width), BF16))
        out_specs.append(pl.BlockSpec((1, d, tr // d, width), lambda b, i: (b, 0, i, 0)))
    res = pl.pallas_call(
        functools.partial(_prep_groups_kernel, has_vres=has_vres),
        grid=(B, S // tr),
        in_specs=in_specs,
        out_specs=out_specs,
        out_shape=out,
        compiler_params=_cparams(("parallel", "parallel")),
        name="prep_groups",
    )(*args)
    return res[:6], res[6:]


def _interleave(*gens):
    live = list(gens)
    while live:
        for gen in list(live):
            if next(gen, StopIteration) is StopIteration:
                live.remove(gen)
        yield


def _scan_stages(r_ref, cum_ref, k_ref, v_ref, a_ref, b_ref, ga_ref, vec_ref, ones_ref,
                 y_ref, s_ref, *, nb, tt):
    C = CHUNK

    @pl.when(pl.program_id(0) == 0)
    def _():
        s_ref[...] = jnp.zeros(s_ref.shape, F32)

    row = lax.broadcasted_iota(jnp.int32, (C, MXU_DIM), 0)
    lane = lax.broadcasted_iota(jnp.int32, (C, MXU_DIM), 1)
    col = lane & (HEAD_DIM - 1)
    lhead = lane >> 6
    strict = col < row
    incl = col <= row
    eye = (col == row).astype(F32)
    head_masks = [lhead == hh for hh in range(HEADS_PER_TILE)]

    def bdrows(x):
        return jnp.concatenate([jnp.where(m, x, 0.0) for m in head_masks], axis=0).astype(BF16)

    def diag_blocks(full):
        acc = jnp.where(head_masks[0], full[0:C], 0.0)
        for hh in range(1, HEADS_PER_TILE):
            acc = acc + jnp.where(head_masks[hh], full[C * hh:C * (hh + 1)], 0.0)
        return acc

    row_full = lax.broadcasted_iota(jnp.int32, (C, R_WIDTH), 0)
    ones_bd = ones_ref[...]
    r_k = vec_ref[0:1, :]
    ln_g = vec_ref[1:2, :]
    ln_b = vec_ref[2:3, :]

    chains = [(bi, g) for bi in range(nb) for g in range(N_COLGROUPS)]
    insts = [(ck, bi, g) for ck in range(SCAN_CHUNKS) for bi, g in chains]


    def load(gi):
        rows = [slice((gi * SCAN_CHUNKS + ck) * C, (gi * SCAN_CHUNKS + ck + 1) * C) for ck in range(SCAN_CHUNKS)]
        ops = {}
        for ck in range(SCAN_CHUNKS):
            for bi in range(nb):
                cum = cum_ref[bi, rows[ck], :]
                r = r_ref[bi, rows[ck], :].astype(F32)
                k = k_ref[bi, rows[ck], :].astype(F32)
                v = v_ref[bi, rows[ck], :].astype(F32)
                a = a_ref[bi, rows[ck], :].astype(F32)
                b = b_ref[bi, rows[ck], :].astype(F32)
                total = cum[C - 1:C, :]
                p_in = jnp.exp(cum)
                p_inv = jnp.exp(-cum)
                p_rest = jnp.exp(total - cum)
                p_before = jnp.where(row_full == 0, 1.0, pltpu.roll(p_in, 1, axis=0))
                ops[ck, bi] = dict(r=r, k=k, v=v, a_t=a * p_before, r_t=r * p_in, b_t=b * p_inv,
                                   k_t=k * p_inv, bp=b * p_rest, kp=k * p_rest, p_all=jnp.exp(total))
        return dict(rows=rows, ops=ops)

    def part(ctx, name, ck, bi, g):
        return ctx["ops"][ck, bi][name][:, MXU_DIM * g:MXU_DIM * (g + 1)]

    def independent(ctx):
        res = [_dot_nt(jnp.concatenate([part(ctx, "a_t", *i), part(ctx, "r_t", *i)], axis=0).astype(BF16),
                       jnp.concatenate([bdrows(part(ctx, "b_t", *i)), bdrows(part(ctx, "k_t", *i))], axis=0))
               for i in insts]
        yield
        a_ab = [jnp.where(strict, x[0:C, 0:MXU_DIM], 0.0) for x in res]
        a_ak = [jnp.where(strict, x[0:C, MXU_DIM:], 0.0) for x in res]
        ctx["a_rb"] = [jnp.where(incl, x[C:, 0:MXU_DIM], 0.0).astype(BF16) for x in res]
        a_rk = [jnp.where(incl, x[C:, MXU_DIM:], 0.0) for x in res]

        pw = [_dot(x.astype(BF16), bdrows(x)) for x in a_ab]
        tinv = [eye + x for x in a_ab]
        yield
        for _ in range(4):
            both = [_dot(jnp.concatenate([p, t], axis=0).astype(BF16), bdrows(p)) for p, t in zip(pw, tinv)]
            tinv = [t + x[C:] for t, x in zip(tinv, both)]
            pw = [x[0:C] for x in both]
            yield
        tinv = [t + _dot(t.astype(BF16), bdrows(p)) for p, t in zip(pw, tinv)]
        yield
        tax = [_dot(t.astype(BF16), jnp.concatenate([bdrows(part(ctx, "a_t", *i)), bdrows(x)], axis=1))
               for t, x, i in zip(tinv, a_ak, insts)]
        yield
        ctx["from_v"] = [_dot(jnp.concatenate([x[:, MXU_DIM:], ark], axis=0).astype(BF16),
                              bdrows(part(ctx, "v", *i))) for x, ark, i in zip(tax, a_rk, insts)]
        ctx["tax"] = tax
        yield

    def dependent(ctx, carried):
        tax, from_v, a_rb = ctx["tax"], ctx["from_v"], ctx["a_rb"]
        st = carried["st"]
        y = {}
        for ck in range(SCAN_CHUNKS):
            sel = range(ck * len(chains), (ck + 1) * len(chains))
            from_state = [_dot_nt(jnp.concatenate([tax[n][:, 0:MXU_DIM], part(ctx, "r_t", *insts[n])],
                                                  axis=0).astype(BF16), bdrows(s))
                          for n, s in zip(sel, st)]
            yield
            u = [x[0:C] + from_v[n][0:C] for x, n in zip(from_state, sel)]
            for x, n, uu in zip(from_state, sel, u):
                y[insts[n]] = x[C:] + from_v[n][C:] + _dot(a_rb[n], bdrows(uu))
            upd = [_dot(jnp.concatenate([uu, part(ctx, "v", *insts[n])], axis=0).T.astype(BF16),
                        jnp.concatenate([part(ctx, "bp", *insts[n]), part(ctx, "kp", *insts[n])],
                                        axis=0).astype(BF16))
                   for uu, n in zip(u, sel)]
            yield
            st = [s_old * part(ctx, "p_all", *insts[n]) + diag_blocks(x) for s_old, x, n in zip(st, upd, sel)]
        carried["st"] = st

        for ck in range(SCAN_CHUNKS):
            for bi in range(nb):
                p = ctx["ops"][ck, bi]
                yc = jnp.concatenate([y[ck, bi, g] for g in range(N_COLGROUPS)], axis=1)
                mean = _segsum64(yc, ones_bd, split=True) * (1.0 / HEAD_DIM)
                yield
                yd = yc - mean
                var = _segsum64(yd * yd, ones_bd, split=False) * (1.0 / HEAD_DIM)
                yn = yd * lax.rsqrt(var + GN_EPS) * ln_g + ln_b
                bonus = _segsum64(p["r"] * p["k"] * r_k, ones_bd, split=False) * p["v"]
                rows = ctx["rows"][ck]
                y_ref[bi, rows, :] = ((yn + bonus) * _silu(ga_ref[bi, rows, :].astype(F32))).astype(BF16)
                yield

    carried = {"st": [s_ref[bi * N_COLGROUPS + g] for bi, g in chains]}
    n_groups = tt // (C * SCAN_CHUNKS)
    ctx = load(0)
    yield from independent(ctx)
    for gi in range(1, n_groups):
        nxt = load(gi)
        yield from _interleave(dependent(ctx, carried), independent(nxt))
        ctx = nxt
    yield from dependent(ctx, carried)
    for (bi, g), s_new in zip(chains, carried["st"]):
        s_ref[bi * N_COLGROUPS + g] = s_new


ATT_TILE = 2048
ATT_UNROLL = (8, 8, 8)


def _attn_stages(q_refs, k_refs, v_refs, kp_refs, vp_refs, gb_ref, bias_ref, y_ref, o_refs, l_refs,
                 is_first):
    prev_limit = jnp.where(is_first, BLK, 0)
    ki = lax.broadcasted_iota(jnp.int32, (2 * BLK, 2 * BLK), 1)
    head0 = lax.broadcasted_iota(jnp.int32, (BLK, LANES), 1) < HEAD_DIM
    ones_cols = jnp.ones((2 * BLK, LANES), BF16)
    zero = jnp.zeros((BLK, LANES), BF16)

    def process(blocks):
        q2s, kws, vws, bias2s, stores = [], [], [], [], []
        for g, sub, res in blocks:
            d = DILATIONS[g]
            base = sub * (BLK * d) + res
            q = q_refs[g][0, res, sub * BLK:(sub + 1) * BLK, :]
            q2s.append(jnp.concatenate([jnp.where(head0, q, zero), jnp.where(head0, zero, q)], axis=0))
            if sub == 0:
                kw = jnp.concatenate([kp_refs[g][0, res], k_refs[g][0, res, 0:BLK, :]], axis=0)
                vw = jnp.concatenate([vp_refs[g][0, res], v_refs[g][0, res, 0:BLK, :]], axis=0)
            else:
                kw = k_refs[g][0, res, (sub - 1) * BLK:(sub + 1) * BLK, :]
                vw = v_refs[g][0, res, (sub - 1) * BLK:(sub + 1) * BLK, :]
            kws.append(kw)
            vws.append(jnp.concatenate([vw, ones_cols], axis=1))
            bias2s.append(bias_ref[g, 0].reshape(2 * BLK, 2 * BLK))
            stores.append((g, pl.ds(base, BLK) if d == 1 else pl.ds(base, BLK, stride=d)))
        logits = [jnp.where(bias2 > 0.5 * NEG_INF, _dot_nt(q2, kw) + bias2, NEG_INF)
                  for q2, kw, bias2 in zip(q2s, kws, bias2s)]
        logits = [jnp.where(ki < prev_limit, NEG_INF, x) if blk[1] == 0 else x
                  for x, blk in zip(logits, blocks)]
        yield
        ms = [jnp.max(x, axis=-1, keepdims=True) for x in logits]
        ps = [jnp.exp2(x - m).astype(BF16) for x, m in zip(logits, ms)]
        pvs = [_dot(p, vw) for p, vw in zip(ps, vws)]
        yield
        for (g, rows), pv, m in zip(stores, pvs, ms):
            num = jnp.where(head0, pv[0:BLK, 0:LANES], pv[BLK:, 0:LANES])
            den = jnp.where(head0, pv[0:BLK, LANES:], pv[BLK:, LANES:])
            o_refs[g][rows, :] = num / den
            l_refs[g][rows, :] = jnp.where(head0, m[0:BLK], m[BLK:]) + jnp.log2(den)

    for g, d in enumerate(DILATIONS):
        blocks = [(g, sub, res) for sub in range(ATT_TILE // (BLK * d)) for res in range(d)]
        for n in range(0, len(blocks), ATT_UNROLL[g]):
            yield from process(blocks[n:n + ATT_UNROLL[g]])

    l0, l1, l2 = l_refs[0][...], l_refs[1][...], l_refs[2][...]
    m = jnp.maximum(jnp.maximum(l0, l1), l2)
    w0, w1, w2 = jnp.exp2(l0 - m), jnp.exp2(l1 - m), jnp.exp2(l2 - m)
    y = (w0 * o_refs[0][...] + w1 * o_refs[1][...] + w2 * o_refs[2][...]) / (w0 + w1 + w2)
    y_ref[0] = (y * _silu(gb_ref[0].astype(F32))).astype(BF16)
    yield


N_SCAN_REFS = 9
N_ATTN_REFS = 17
MIXERS_VMEM_LIMIT = 62 * 1024 * 1024


def _mixers_kernel(*refs, nb, tt, tiles_per_seq, n_tiles):
    scan_in = refs[:N_SCAN_REFS]
    attn_in = refs[N_SCAN_REFS:N_SCAN_REFS + N_ATTN_REFS]
    ya_ref, yb_ref, s_ref = refs[N_SCAN_REFS + N_ATTN_REFS:N_SCAN_REFS + N_ATTN_REFS + 3]
    scratch = refs[N_SCAN_REFS + N_ATTN_REFS + 3:]
    is_first = ((pl.program_id(0) % n_tiles) % tiles_per_seq) == 0
    scan = _scan_stages(*scan_in, ya_ref, s_ref, nb=nb, tt=tt)
    attn = _attn_stages(attn_in[0:3], attn_in[3:6], attn_in[6:9], attn_in[9:12], attn_in[12:15],
                        attn_in[15], attn_in[16], yb_ref, scratch[0:3], scratch[3:6], is_first)
    for _ in _interleave(scan, attn):
        pass


def _mixers(r, cum, k, v, a, b, main, groups, vec, ones_bd, bias5, layer):
    B, S, W = r.shape
    n_pairs = HEADS_PER_GROUP // 2
    tiles_per_seq = S // ATT_TILE
    n_tiles = B * tiles_per_seq
    n_steps = n_pairs * n_tiles
    tt = S // n_steps
    assert tt * n_steps == S and tt % (CHUNK * SCAN_CHUNKS) == 0

    scan_spec = pl.BlockSpec((B, tt, W), lambda t: (0, t, 0))

    def full(arr):
        return pl.BlockSpec(arr.shape, lambda t: (0,) * arr.ndim)

    scan_specs = [scan_spec] * 6 + [pl.BlockSpec((B, tt, W), lambda t: (0, t, COL_GA // W)),
                                    _layer_block(vec, layer), full(ones_bd)]

    arrays = [main.reshape(B, 1, S, MAIN_WIDTH)] + list(groups)
    col_base = [COL_A0 // LANES, 0, 0]

    def where(t):
        tile = t % n_tiles
        return t // n_tiles, tile // tiles_per_seq, tile % tiles_per_seq

    def cur(g, part):
        d = DILATIONS[g]
        c0 = col_base[g] + part * (A_OUT_WIDTH // LANES)

        def index(t):
            hp, bi, ti = where(t)
            return bi, 0, ti, c0 + hp
        return pl.BlockSpec((1, d, ATT_TILE // d, LANES), index)

    def prev(g, part):
        d = DILATIONS[g]
        c0 = col_base[g] + part * (A_OUT_WIDTH // LANES)
        rb = ATT_TILE // (BLK * d)

        def index(t):
            hp, bi, ti = where(t)
            return bi, 0, jnp.maximum(ti * rb - 1, 0), c0 + hp
        return pl.BlockSpec((1, d, BLK, LANES), index)

    def tile(col0):
        def index(t):
            hp, bi, ti = where(t)
            return bi, ti, col0 // LANES + hp
        return pl.BlockSpec((1, ATT_TILE, LANES), index)

    attn_specs = ([cur(g, 0) for g in range(N_GROUPS)] + [cur(g, 1) for g in range(N_GROUPS)]
                  + [cur(g, 2) for g in range(N_GROUPS)]
                  + [prev(g, 1) for g in range(N_GROUPS)] + [prev(g, 2) for g in range(N_GROUPS)]
                  + [tile(COL_GB),
                     pl.BlockSpec((N_GROUPS, 1, 2, BLK, 2 * BLK), lambda t: (0, t // n_tiles, 0, 0, 0))])
    assert len(scan_specs) == N_SCAN_REFS and len(attn_specs) == N_ATTN_REFS
    return pl.pallas_call(
        functools.partial(_mixers_kernel, nb=B, tt=tt, tiles_per_seq=tiles_per_seq, n_tiles=n_tiles),
        grid=(n_steps,),
        in_specs=scan_specs + attn_specs,
        out_specs=[scan_spec, tile(0)],
        out_shape=[jax.ShapeDtypeStruct((B, S, W), BF16), jax.ShapeDtypeStruct((B, S, A_OUT_WIDTH), BF16)],
        scratch_shapes=([pltpu.VMEM((B * N_COLGROUPS, HEAD_DIM, MXU_DIM), F32)]
                        + [pltpu.VMEM((ATT_TILE, LANES), F32)] * 6),
        compiler_params=_cparams(("arbitrary",), MIXERS_VMEM_LIMIT),
        name="mixers",
    )(r, cum, k, v, a, b, main, vec, ones_bd, *(arrays * 5), main, bias5)


def _merge_kernel(ya_ref, yb_ref, ma_ref, mb_ref, x_ref, mod_ref, wa_ref, wb_ref, wo_ref, fg_ref,
                  o_ref, *, final_norm):
    pa = _dot(ya_ref[0], wa_ref[...])
    pb = _dot(yb_ref[0], wb_ref[...])
    merged = _sigmoid(ma_ref[0].astype(F32)) * pa + _sigmoid(mb_ref[0].astype(F32)) * pb
    out = _dot(merged.astype(BF16), wo_ref[...])
    gate = mod_ref[0, :, 2 * D_MODEL:3 * D_MODEL]
    xn = x_ref[0] + gate * out
    if final_norm:
        ms = jnp.mean(xn * xn, axis=-1, keepdims=True)
        xn = xn * lax.rsqrt(ms + RMS_EPS) * fg_ref[...]
    o_ref[0] = xn


def _merge(ya, yb, proj, x, mod, wa, wb, wo, final_g, final_norm, layer, tm=1024):
    B, S, D = x.shape

    def rows(width, c):
        return pl.BlockSpec((1, tm, width), lambda b, i: (b, i, c))

    def full(arr):
        return pl.BlockSpec(arr.shape, lambda b, i: (0,) * arr.ndim)

    return pl.pallas_call(
        functools.partial(_merge_kernel, final_norm=final_norm),
        grid=(B, S // tm),
        in_specs=[rows(R_WIDTH, 0), rows(A_OUT_WIDTH, 0),
                  rows(D, COL_MA // D), rows(D, COL_MB // D), rows(D, 0),
                  pl.BlockSpec((1, 1, 3 * D), lambda b, i: (layer * MOD_ROWS + b, 0, 0)),
                  _layer_block(wa, layer), _layer_block(wb, layer), _layer_block(wo, layer), full(final_g)],
        out_specs=rows(D, 0),
        out_shape=jax.ShapeDtypeStruct((B, S, D), F32),
        compiler_params=_cparams(("parallel", "parallel")),
        name="merge",
    )(ya, yb, proj, proj, x, mod, wa, wb, wo, final_g)


def _segment_ones():
    idx = np.arange(MXU_DIM)
    return jnp.asarray(idx[:, None] // HEAD_DIM == idx[None, :] // HEAD_DIM, BF16)


def kernel(x, c, norm_g, ada_w, ada_b, w_in, rwkv_mu_rkv, rwkv_mu_wa, rwkv_w0, rwkv_w1, rwkv_w2, rwkv_a0, rwkv_a1, rwkv_a2, rwkv_k_k, rwkv_k_a, rwkv_r_k, rwkv_ln_g, rwkv_ln_b, rwkv_mu_v, rwkv_v0, rwkv_v1, rwkv_v2, w_branch_a, w_branch_b, w_out, rel_bias, final_g):
    B, S, D = x.shape
    assert D == D_MODEL and S % ATT_TILE == 0 and w_in.shape[2] == PROJ_WIDTH
    ones_bd = _segment_ones()
    mod = _adaln_mod(c, ada_w, ada_b).reshape(DEPTH * MOD_ROWS, 1, 3 * D)
    bias = _rel_bias(rel_bias).reshape(N_GROUPS, HEADS_PER_GROUP // 2, 2, BLK, 2 * BLK)

    def cols(start, width):
        return w_in[:, :, start:start + width]

    def group_cols(g):
        return [cols(W_AQ + A_OUT_WIDTH * g, A_OUT_WIDTH) * (LOG2E / math.sqrt(HEAD_DIM)),
                cols(W_AK + A_OUT_WIDTH * g, A_OUT_WIDTH), cols(W_AV + A_OUT_WIDTH * g, A_OUT_WIDTH)]

    w_main = jnp.concatenate(
        [cols(W_R, 4 * R_WIDTH), cols(W_MA, 2 * D_MODEL), cols(W_GB, A_OUT_WIDTH)] + group_cols(0),
        axis=2).astype(BF16)
    w_groups = [jnp.concatenate(group_cols(g), axis=2).astype(BF16) for g in range(1, N_GROUPS)]
    zeros_rows = jnp.zeros((DEPTH, D), F32)
    pvec = jnp.stack([rwkv_mu_rkv[:, 0], rwkv_mu_rkv[:, 1], rwkv_mu_rkv[:, 2], rwkv_w0, rwkv_a0, rwkv_k_k,
                      rwkv_k_a, _first_layer_blank(rwkv_v0)] + [zeros_rows] * (PV_ROWS - 8), axis=1)
    lora = _pack_lora([(rwkv_mu_wa[:, 0], rwkv_w1, rwkv_w2), (rwkv_mu_wa[:, 1], rwkv_a1, rwkv_a2),
                       (_first_layer_blank(rwkv_mu_v), _first_layer_blank(rwkv_v1),
                        _first_layer_blank(rwkv_v2))])
    vec = jnp.stack([rwkv_r_k.reshape(DEPTH, -1), rwkv_ln_g, rwkv_ln_b] + [zeros_rows] * 5, axis=1)
    norm_g3 = norm_g.reshape(DEPTH, 1, D)
    wa, wb, wo = w_branch_a.astype(BF16), w_branch_b.astype(BF16), w_out.astype(BF16)

    v_first = None
    for i in range(DEPTH):
        proj, h = _norm_proj(x, mod, norm_g3, w_main, i)
        (r, cum, k, v, a, b), groups = _rwkv_prep(h, proj, v_first, pvec, lora if i > 0 else lora[:3],
                                                  ones_bd, w_groups, i)
        if i == 0:
            v_first = v
        y_a, y_b = _mixers(r, cum, k, v, a, b, proj, groups, vec, ones_bd, bias, i)
        x = _merge(y_a, y_b, proj, x, mod, wa, wb, wo, final_g.reshape(1, D),
                   final_norm=(i == DEPTH - 1), layer=i)
    return x
```

```python
import functools
import math

import numpy as np
import jax
import jax.numpy as jnp
from jax import lax
from jax.experimental import pallas as pl
from jax.experimental.pallas import tpu as pltpu

F32 = jnp.float32
BF16 = jnp.bfloat16

D_MODEL = 1024
DEPTH = 2
HEAD_DIM = 64
R_WIDTH = 1024
N_GROUPS = 3
HEADS_PER_GROUP = 8
DILATIONS = (1, 4, 16)
BLK = 128
A_QK_WIDTH = 1536
A_OUT_WIDTH = 512
NUM_BUCKETS = 32
MAX_DISTANCE = 2048
PROJ_WIDTH = 4 * R_WIDTH + 3 * A_QK_WIDTH + A_OUT_WIDTH + 2 * D_MODEL
RMS_EPS = 1e-6
GN_EPS = 64e-5
NEG_INF = -1e30
LOG2E = math.log2(math.e)

LANES = 128
MXU_DIM = 256
HEADS_PER_TILE = MXU_DIM // HEAD_DIM
N_COLGROUPS = R_WIDTH // MXU_DIM
CHUNK = 64
SCAN_CHUNKS = 2

W_R, W_K, W_V, W_GA = 0, 1024, 2048, 3072
W_AQ, W_AK, W_AV = 4096, 5632, 7168
W_GB, W_MA, W_MB = 8704, 9216, 10240
COL_R, COL_K, COL_V, COL_GA, COL_MA, COL_MB, COL_GB, COL_A0 = 0, 1024, 2048, 3072, 4096, 5120, 6144, 6656
MAIN_WIDTH = 8192
GROUP_WIDTH = 3 * A_OUT_WIDTH

VMEM_LIMIT = 56 * 1024 * 1024
MOD_ROWS = 8


def _cparams(sem, vmem_limit=VMEM_LIMIT):
    return pltpu.CompilerParams(dimension_semantics=sem, vmem_limit_bytes=vmem_limit)


def _sigmoid(z):
    return 1.0 / (1.0 + jnp.exp(-z))


def _silu(z):
    return z * _sigmoid(z)


def _dot(a, b):
    return jnp.dot(a, b, preferred_element_type=F32)


def _dot_nt(a, b):
    return lax.dot_general(a, b, (((1,), (1,)), ((), ())), preferred_element_type=F32)


def _split2(x):
    hi = x.astype(BF16)
    lo = (x - hi.astype(F32)).astype(BF16)
    return hi, lo


def _segsum64(x, ones_bd, split):
    n = x.shape[0]
    xs = jnp.concatenate([x[:, MXU_DIM * g:MXU_DIM * (g + 1)] for g in range(N_COLGROUPS)], axis=0)
    if split:
        hi, lo = _split2(xs)
        s = _dot(hi, ones_bd) + _dot(lo, ones_bd)
    else:
        s = _dot(xs.astype(BF16), ones_bd)
    return jnp.concatenate([s[n * g:n * (g + 1)] for g in range(N_COLGROUPS)], axis=1)


def _mod_kernel(c_ref, w_ref, b_ref, o_ref):
    s = _silu(c_ref[...])
    o_ref[0] = _dot(s.astype(BF16), w_ref[0].astype(BF16)) + b_ref[0]


def _adaln_mod(c, ada_w, ada_b):
    L = ada_w.shape[0]
    B = c.shape[0]
    c_rows = jnp.pad(c, ((0, MOD_ROWS - B), (0, 0)))
    nj = 3
    return pl.pallas_call(
        _mod_kernel,
        grid=(L, nj),
        in_specs=[pl.BlockSpec((MOD_ROWS, D_MODEL), lambda l, j: (0, 0)),
                  pl.BlockSpec((1, D_MODEL, D_MODEL), lambda l, j: (l, 0, j)),
                  pl.BlockSpec((1, 1, D_MODEL), lambda l, j: (l, 0, j))],
        out_specs=pl.BlockSpec((1, MOD_ROWS, D_MODEL), lambda l, j: (l, 0, j)),
        out_shape=jax.ShapeDtypeStruct((L, MOD_ROWS, 3 * D_MODEL), F32),
        compiler_params=_cparams(("parallel", "parallel")),
        name="adaln_mod",
    )(c_rows, ada_w, ada_b.reshape(L, 1, 3 * D_MODEL))


def _t5_bucket(dist):
    max_exact = NUM_BUCKETS // 2
    safe = np.maximum(dist, 1).astype(np.float32)
    large = max_exact + (np.log(safe / max_exact) / math.log(MAX_DISTANCE / max_exact)
                         * (NUM_BUCKETS - max_exact)).astype(np.int32)
    large = np.minimum(large, NUM_BUCKETS - 1)
    return np.where(dist < max_exact, dist, large).astype(np.int32)


def _bias_kernel(tab_ref, bucket_ref, o_ref):
    g = pl.program_id(0)
    bk = bucket_ref[0]
    for hh in range(HEADS_PER_GROUP):
        h = g * HEADS_PER_GROUP + hh
        acc = jnp.zeros(bk.shape, F32)
        for b in range(NUM_BUCKETS):
            acc = jnp.where(bk == b, tab_ref[h * NUM_BUCKETS + b], acc)
        o_ref[hh] = jnp.where(bk >= 0, acc * LOG2E, NEG_INF)


def _rel_bias(rel_bias):
    n_heads = rel_bias.shape[1]
    qi = np.arange(BLK)[:, None]
    ki = np.arange(2 * BLK)[None, :]
    delta = qi + BLK - ki
    band = (delta >= 0) & (delta <= BLK)
    buckets = np.stack([np.where(band, _t5_bucket(np.maximum(delta, 0) * d), -1)
                        for d in DILATIONS]).astype(np.int32)
    table = rel_bias.T.reshape(-1)
    return pl.pallas_call(
        _bias_kernel,
        grid=(n_heads // HEADS_PER_GROUP,),
        in_specs=[pl.BlockSpec(memory_space=pltpu.SMEM),
                  pl.BlockSpec((1, BLK, 2 * BLK), lambda g: (g, 0, 0))],
        out_specs=pl.BlockSpec((HEADS_PER_GROUP, BLK, 2 * BLK), lambda g: (g, 0, 0)),
        out_shape=jax.ShapeDtypeStruct((n_heads, BLK, 2 * BLK), F32),
        compiler_params=_cparams(("parallel",)),
        name="rel_bias",
    )(table, jnp.asarray(buckets))


def _proj_kernel(x_ref, mod_ref, g_ref, w_ref, proj_ref, h_ref):
    @pl.when(pl.program_id(2) == 0)
    def _():
        x = x_ref[0]
        ms = jnp.mean(x * x, axis=-1, keepdims=True)
        y = x * lax.rsqrt(ms + RMS_EPS) * g_ref[...]
        shift = mod_ref[0, :, 0:D_MODEL]
        scale = mod_ref[0, :, D_MODEL:2 * D_MODEL]
        h_ref[0] = (y * (1.0 + scale) + shift).astype(BF16)

    proj_ref[0] = _dot(h_ref[0], w_ref[...]).astype(BF16)


def _layer_block(arr, layer):
    tail = (0,) * (arr.ndim - 1)
    return pl.BlockSpec((None,) + arr.shape[1:], lambda *_: (layer,) + tail)


def _norm_proj(x, mod, norm_g, w_main, layer, tm=1024, tn=4096):
    B, S, D = x.shape
    N = w_main.shape[2]
    return pl.pallas_call(
        _proj_kernel,
        grid=(B, S // tm, N // tn),
        in_specs=[pl.BlockSpec((1, tm, D), lambda b, i, j: (b, i, 0)),
                  pl.BlockSpec((1, 1, 3 * D), lambda b, i, j: (layer * MOD_ROWS + b, 0, 0)),
                  _layer_block(norm_g, layer),
                  pl.BlockSpec((None, D, tn), lambda b, i, j: (layer, 0, j))],
        out_specs=[pl.BlockSpec((1, tm, tn), lambda b, i, j: (b, i, j)),
                   pl.BlockSpec((1, tm, D), lambda b, i, j: (b, i, 0))],
        out_shape=[jax.ShapeDtypeStruct((B, S, N), BF16),
                   jax.ShapeDtypeStruct((B, S, D), BF16)],
        compiler_params=_cparams(("parallel", "parallel", "arbitrary")),
        name="norm_proj",
    )(x, mod, norm_g, w_main)


GATHER_ROWS = MXU_DIM


def _group_proj_stages(h_ref, perm_ref, w_ref, o_ref, d):
    tm = h_ref.shape[1]
    per_res = GATHER_ROWS // d
    perm = perm_ref[...]
    for ck in range(tm // GATHER_ROWS):
        r0 = ck * GATHER_ROWS
        hp = _dot(perm, h_ref[0, r0:r0 + GATHER_ROWS, :]).astype(BF16)
        yield
        res = _dot(hp, w_ref[...]).astype(BF16)
        for r in range(d):
            o_ref[0, r, ck * per_res:(ck + 1) * per_res, :] = res[r * per_res:(r + 1) * per_res]
        yield


def _gather_perm(d):
    dst = np.arange(GATHER_ROWS)
    src = (dst % (GATHER_ROWS // d)) * d + dst // (GATHER_ROWS // d)
    return jnp.asarray(src[:, None] == np.arange(GATHER_ROWS)[None, :], BF16)


PV_MU_R, PV_MU_K, PV_MU_V, PV_W0, PV_A0, PV_KK, PV_KA, PV_V0 = range(8)
LORA_LANES = 256


def _pack_lora(paths):
    n_layers, d_model, _ = paths[0][1].shape
    used = sum(down.shape[2] for _, down, _ in paths)
    pad = jnp.zeros((n_layers, d_model, LORA_LANES - used), F32)
    keep = jnp.concatenate([(1.0 - mu)[:, :, None] * down for mu, down, _ in paths] + [pad], axis=2)
    shifted = jnp.concatenate([mu[:, :, None] * down for mu, down, _ in paths] + [pad], axis=2)
    ups, lane = [], 0
    for _, down, up in paths:
        rank = down.shape[2]
        ups.append(jnp.pad(up, ((0, 0), (lane, LORA_LANES - lane - rank), (0, 0))).astype(BF16))
        lane += rank
    return [jnp.concatenate([keep, shifted], axis=2).astype(BF16)] + ups


def _first_layer_blank(arr):
    return jnp.concatenate([jnp.zeros((1,) + arr.shape[1:], arr.dtype), arr], axis=0)
PV_ROWS = 16
PREV_ROWS = 16


def _shift_rows(t, prev_last):
    rolled = pltpu.roll(t, 1, axis=0)
    row = lax.broadcasted_iota(jnp.int32, t.shape, 0)
    return jnp.where(row == 0, prev_last, rolled)


def _rprep_stages(refs, has_vres):
    if has_vres:
        (h_ref, hp_ref, pr_ref, prp_ref, pk_ref, pkp_ref, pvv_ref, pvp_ref, vf_ref, pvec_ref,
         wd_ref, uw_ref, ua_ref, uv_ref, ones_ref, tril_ref,
         r_out, cum_out, k_out, v_out, a_out, b_out) = refs
    else:
        (h_ref, hp_ref, pr_ref, prp_ref, pk_ref, pkp_ref, pvv_ref, pvp_ref, pvec_ref,
         wd_ref, uw_ref, ua_ref, ones_ref, tril_ref,
         r_out, cum_out, k_out, v_out, a_out, b_out) = refs

    not_first = (pl.program_id(1) > 0).astype(F32)

    def prm(i):
        return pvec_ref[i:i + 1, :]

    def lerp_shift(cur_ref, prev_ref, mu):
        t = cur_ref[0].astype(F32)
        last = prev_ref[0, PREV_ROWS - 1:PREV_ROWS, :].astype(F32)
        return t + (_shift_rows(t, last * not_first) - t) * mu

    r = lerp_shift(pr_ref, prp_ref, prm(PV_MU_R))
    k = lerp_shift(pk_ref, pkp_ref, prm(PV_MU_K))
    v = lerp_shift(pvv_ref, pvp_ref, prm(PV_MU_V))
    yield

    wd = wd_ref[...]
    z2 = _dot(h_ref[0], wd)
    z_prev = _dot(hp_ref[0], wd[:, LORA_LANES:])[PREV_ROWS - 1:PREV_ROWS, :]
    z = z2[:, 0:LORA_LANES] + _shift_rows(z2[:, LORA_LANES:], z_prev * not_first)
    zb = z.astype(BF16)
    yield

    zw = prm(PV_W0) + _dot(jnp.tanh(z).astype(BF16), uw_ref[...])
    w = jnp.minimum(zw, 0.0) - jnp.log(1.0 + jnp.exp(-jnp.abs(zw))) - 0.5
    lw = -jnp.exp(w)
    yield
    hi, lo = _split2(lw)
    tril = tril_ref[...]
    for i in range(lw.shape[0] // MXU_DIM):
        blk = slice(MXU_DIM * i, MXU_DIM * (i + 1))
        cum_out[0, blk, :] = _dot(tril, hi[blk]) + _dot(tril, lo[blk])
    yield
    a = _sigmoid(prm(PV_A0) + _dot(zb, ua_ref[...]))
    if has_vres:
        mix = _sigmoid(prm(PV_V0) + _dot(zb, uv_ref[...]))
        v = v + (vf_ref[0].astype(F32) - v) * mix
    yield

    kk = k * prm(PV_KK)
    ss = _segsum64(kk * kk, ones_ref[...], split=False)
    kk = kk * lax.rsqrt(jnp.maximum(ss, 1e-24))
    yield
    r_out[0] = r.astype(BF16)
    k_out[0] = (k * (1.0 + (a - 1.0) * prm(PV_KA))).astype(BF16)
    v_out[0] = v.astype(BF16)
    a_out[0] = (-kk).astype(BF16)
    b_out[0] = (kk * a).astype(BF16)
    yield


N_GROUP_PROJ = N_GROUPS - 1


def _prep_groups_kernel(*refs, has_vres):
    n_in = len(refs) - 6 - N_GROUP_PROJ - 2 * N_GROUP_PROJ
    prep_in, rest = refs[:n_in], refs[n_in:]
    gp_in, outs = rest[:2 * N_GROUP_PROJ], rest[2 * N_GROUP_PROJ:]
    prep = _rprep_stages(tuple(prep_in) + tuple(outs[:6]), has_vres)
    h_ref = prep_in[0]
    groups = [_group_proj_stages(h_ref, gp_in[2 * n], gp_in[2 * n + 1], outs[6 + n], DILATIONS[n + 1])
              for n in range(N_GROUP_PROJ)]

    def all_groups():
        for gen in groups:
            yield from gen

    for _ in _interleave(prep, all_groups()):
        pass


def _rwkv_prep(h, proj, v_first, pvec, lora, ones_bd, w_groups, layer, tr=512):
    B, S, D = h.shape
    has_vres = v_first is not None
    rpb = tr // PREV_ROWS
    t = np.arange(MXU_DIM)
    tril_bd =jnp.asarray((t[None, :] <= t[:, None]) & (t[None, :] // CHUNK == t[:, None] // CHUNK), BF16)

    def cur(c):
        return pl.BlockSpec((1, tr, R_WIDTH), lambda b, i: (b, i, c))

    def prev(c):
        return pl.BlockSpec((1, PREV_ROWS, R_WIDTH), lambda b, i: (b, jnp.maximum(i * rpb - 1, 0), c))

    def full(arr):
        return pl.BlockSpec(arr.shape, lambda b, i: (0,) * arr.ndim)

    in_specs = [cur(0), prev(0)]
    args = [h, h]
    for c in (COL_R, COL_K, COL_V):
        in_specs += [cur(c // R_WIDTH), prev(c // R_WIDTH)]
        args += [proj, proj]
    if has_vres:
        in_specs.append(cur(0))
        args.append(v_first)
    for per_layer in [pvec] + list(lora):
        in_specs.append(_layer_block(per_layer, layer))
        args.append(per_layer)
    for const in (ones_bd, tril_bd):
        in_specs.append(full(const))
        args.append(const)
    out = [jax.ShapeDtypeStruct((B, S, R_WIDTH), F32 if n == 1 else BF16) for n in range(6)]
    out_specs = [cur(0)] * 6
    for n, w_group in enumerate(w_groups):
        d = DILATIONS[n + 1]
        width = w_group.shape[2]
        in_specs += [full(_gather_perm(d)), _layer_block(w_group, layer)]
        args += [_gather_perm(d), w_group]
        out.append(jax.ShapeDtypeStruct((B, d, S // d, width), BF16))
        out_specs.append(pl.BlockSpec((1, d, tr // d, width), lambda b, i: (b, 0, i, 0)))
    res = pl.pallas_call(
        functools.partial(_prep_groups_kernel, has_vres=has_vres),
        grid=(B, S // tr),
        in_specs=in_specs,
        out_specs=out_specs,
        out_shape=out,
        compiler_params=_cparams(("parallel", "parallel")),
        name="prep_groups",
    )(*args)
    return res[:6], res[6:]


VPU_STAGE = "vpu"


def _interleave(*gens):
    live = list(gens)
    while live:
        tags = []
        for gen in list(live):
            tag = next(gen, StopIteration)
            if tag is StopIteration:
                live.remove(gen)
            else:
                tags.append(tag)
        yield VPU_STAGE if tags and all(t == VPU_STAGE for t in tags) else None


def _scan_stages(r_ref, cum_ref, k_ref, v_ref, a_ref, b_ref, ga_ref, vec_ref, ones_ref,
                 y_ref, s_ref, *, nb, tt):
    C = CHUNK

    @pl.when(pl.program_id(0) == 0)
    def _():
        s_ref[...] = jnp.zeros(s_ref.shape, F32)

    row = lax.broadcasted_iota(jnp.int32, (C, MXU_DIM), 0)
    lane = lax.broadcasted_iota(jnp.int32, (C, MXU_DIM), 1)
    col = lane & (HEAD_DIM - 1)
    lhead = lane >> 6
    strict = col < row
    incl = col <= row
    eye = (col == row).astype(F32)
    head_masks = [lhead == hh for hh in range(HEADS_PER_TILE)]

    def bdrows(x):
        return jnp.concatenate([jnp.where(m, x, 0.0) for m in head_masks], axis=0).astype(BF16)

    def diag_blocks(full):
        acc = jnp.where(head_masks[0], full[0:C], 0.0)
        for hh in range(1, HEADS_PER_TILE):
            acc = acc + jnp.where(head_masks[hh], full[C * hh:C * (hh + 1)], 0.0)
        return acc

    row_full = lax.broadcasted_iota(jnp.int32, (C, R_WIDTH), 0)
    ones_bd = ones_ref[...]
    r_k = vec_ref[0:1, :]
    ln_g = vec_ref[1:2, :]
    ln_b = vec_ref[2:3, :]

    chains = [(bi, g) for bi in range(nb) for g in range(N_COLGROUPS)]
    insts = [(ck, bi, g) for ck in range(SCAN_CHUNKS) for bi, g in chains]


    def load(gi):
        rows = [slice((gi * SCAN_CHUNKS + ck) * C, (gi * SCAN_CHUNKS + ck + 1) * C) for ck in range(SCAN_CHUNKS)]
        ops = {}
        for ck in range(SCAN_CHUNKS):
            for bi in range(nb):
                cum = cum_ref[bi, rows[ck], :]
                r = r_ref[bi, rows[ck], :].astype(F32)
                k = k_ref[bi, rows[ck], :].astype(F32)
                v = v_ref[bi, rows[ck], :].astype(F32)
                a = a_ref[bi, rows[ck], :].astype(F32)
                b = b_ref[bi, rows[ck], :].astype(F32)
                total = cum[C - 1:C, :]
                p_in = jnp.exp(cum)
                p_inv = jnp.exp(-cum)
                p_rest = jnp.exp(total - cum)
                p_before = jnp.where(row_full == 0, 1.0, pltpu.roll(p_in, 1, axis=0))
                ops[ck, bi] = dict(r=r, k=k, v=v, a_t=a * p_before, r_t=r * p_in, b_t=b * p_inv,
                                   k_t=k * p_inv, bp=b * p_rest, kp=k * p_rest, p_all=jnp.exp(total))
        return dict(rows=rows, ops=ops)

    def part(ctx, name, ck, bi, g):
        return ctx["ops"][ck, bi][name][:, MXU_DIM * g:MXU_DIM * (g + 1)]

    def independent(ctx):
        res = [_dot_nt(jnp.concatenate([part(ctx, "a_t", *i), part(ctx, "r_t", *i)], axis=0).astype(BF16),
                       jnp.concatenate([bdrows(part(ctx, "b_t", *i)), bdrows(part(ctx, "k_t", *i))], axis=0))
               for i in insts]
        yield
        a_ab = [jnp.where(strict, x[0:C, 0:MXU_DIM], 0.0) for x in res]
        a_ak = [jnp.where(strict, x[0:C, MXU_DIM:], 0.0) for x in res]
        ctx["a_rb"] = [jnp.where(incl, x[C:, 0:MXU_DIM], 0.0).astype(BF16) for x in res]
        a_rk = [jnp.where(incl, x[C:, MXU_DIM:], 0.0) for x in res]

        pw = [_dot(x.astype(BF16), bdrows(x)) for x in a_ab]
        tinv = [eye + x for x in a_ab]
        yield
        for _ in range(4):
            both = [_dot(jnp.concatenate([p, t], axis=0).astype(BF16), bdrows(p)) for p, t in zip(pw, tinv)]
            tinv = [t + x[C:] for t, x in zip(tinv, both)]
            pw = [x[0:C] for x in both]
            yield
        tinv = [t + _dot(t.astype(BF16), bdrows(p)) for p, t in zip(pw, tinv)]
        yield
        tax = [_dot(t.astype(BF16), jnp.concatenate([bdrows(part(ctx, "a_t", *i)), bdrows(x)], axis=1))
               for t, x, i in zip(tinv, a_ak, insts)]
        yield
        ctx["from_v"] = [_dot(jnp.concatenate([x[:, MXU_DIM:], ark], axis=0).astype(BF16),
                              bdrows(part(ctx, "v", *i))) for x, ark, i in zip(tax, a_rk, insts)]
        ctx["tax"] = tax
        yield

    def dependent(ctx, carried):
        tax, from_v, a_rb = ctx["tax"], ctx["from_v"], ctx["a_rb"]
        st = carried["st"]
        y = {}
        for ck in range(SCAN_CHUNKS):
            sel = range(ck * len(chains), (ck + 1) * len(chains))
            from_state = [_dot_nt(jnp.concatenate([tax[n][:, 0:MXU_DIM], part(ctx, "r_t", *insts[n])],
                                                  axis=0).astype(BF16), bdrows(s))
                          for n, s in zip(sel, st)]
            yield
            u = [x[0:C] + from_v[n][0:C] for x, n in zip(from_state, sel)]
            for x, n, uu in zip(from_state, sel, u):
                y[insts[n]] = x[C:] + from_v[n][C:] + _dot(a_rb[n], bdrows(uu))
            upd = [_dot(jnp.concatenate([uu, part(ctx, "v", *insts[n])], axis=0).T.astype(BF16),
                        jnp.concatenate([part(ctx, "bp", *insts[n]), part(ctx, "kp", *insts[n])],
                                        axis=0).astype(BF16))
                   for uu, n in zip(u, sel)]
            yield
            st = [s_old * part(ctx, "p_all", *insts[n]) + diag_blocks(x) for s_old, x, n in zip(st, upd, sel)]
        carried["st"] = st

        for ck in range(SCAN_CHUNKS):
            for bi in range(nb):
                p = ctx["ops"][ck, bi]
                yc = jnp.concatenate([y[ck, bi, g] for g in range(N_COLGROUPS)], axis=1)
                mean = _segsum64(yc, ones_bd, split=True) * (1.0 / HEAD_DIM)
                yield VPU_STAGE
                yd = yc - mean
                var = _segsum64(yd * yd, ones_bd, split=False) * (1.0 / HEAD_DIM)
                yn = yd * lax.rsqrt(var + GN_EPS) * ln_g + ln_b
                bonus = _segsum64(p["r"] * p["k"] * r_k, ones_bd, split=False) * p["v"]
                rows = ctx["rows"][ck]
                y_ref[bi, rows, :] = ((yn + bonus) * _silu(ga_ref[bi, rows, :].astype(F32))).astype(BF16)
                yield VPU_STAGE

    carried = {"st": [s_ref[bi * N_COLGROUPS + g] for bi, g in chains]}
    n_groups = tt // (C * SCAN_CHUNKS)
    ctx = load(0)
    yield from independent(ctx)
    for gi in range(1, n_groups):
        nxt = load(gi)
        yield from _interleave(dependent(ctx, carried), independent(nxt))
        ctx = nxt
    yield from dependent(ctx, carried)
    for (bi, g), s_new in zip(chains, carried["st"]):
        s_ref[bi * N_COLGROUPS + g] = s_new


ATT_TILE = 2048
ATT_UNROLL = (5, 6, 8)


def _attn_stages(q_refs, k_refs, v_refs, kp_refs, vp_refs, gb_ref, bias_ref, y_ref, o_refs, l_refs,
                 is_first):
    prev_limit = jnp.where(is_first, BLK, 0)
    ki = lax.broadcasted_iota(jnp.int32, (2 * BLK, 2 * BLK), 1)
    head0 = lax.broadcasted_iota(jnp.int32, (BLK, LANES), 1) < HEAD_DIM
    ones_cols = jnp.ones((2 * BLK, LANES), BF16)
    zero = jnp.zeros((BLK, LANES), BF16)

    def process(blocks):
        q2s, kws, vws, bias2s, stores = [], [], [], [], []
        for g, sub, res in blocks:
            d = DILATIONS[g]
            base = sub * (BLK * d) + res
            q = q_refs[g][0, res, sub * BLK:(sub + 1) * BLK, :]
            q2s.append(jnp.concatenate([jnp.where(head0, q, zero), jnp.where(head0, zero, q)], axis=0))
            if sub == 0:
                kw = jnp.concatenate([kp_refs[g][0, res], k_refs[g][0, res, 0:BLK, :]], axis=0)
                vw = jnp.concatenate([vp_refs[g][0, res], v_refs[g][0, res, 0:BLK, :]], axis=0)
            else:
                kw = k_refs[g][0, res, (sub - 1) * BLK:(sub + 1) * BLK, :]
                vw = v_refs[g][0, res, (sub - 1) * BLK:(sub + 1) * BLK, :]
            kws.append(kw)
            vws.append(jnp.concatenate([vw, ones_cols], axis=1))
            bias2s.append(bias_ref[g, 0].reshape(2 * BLK, 2 * BLK))
            stores.append((g, pl.ds(base, BLK) if d == 1 else pl.ds(base, BLK, stride=d)))
        logits = [jnp.where(bias2 > 0.5 * NEG_INF, _dot_nt(q2, kw) + bias2, NEG_INF)
                  for q2, kw, bias2 in zip(q2s, kws, bias2s)]
        logits = [jnp.where(ki < prev_limit, NEG_INF, x) if blk[1] == 0 else x
                  for x, blk in zip(logits, blocks)]
        yield
        ms = [jnp.max(x, axis=-1, keepdims=True) for x in logits]
        ps = [jnp.exp2(x - m).astype(BF16) for x, m in zip(logits, ms)]
        pvs = [_dot(p, vw) for p, vw in zip(ps, vws)]
        yield
        for (g, rows), pv, m in zip(stores, pvs, ms):
            num = jnp.where(head0, pv[0:BLK, 0:LANES], pv[BLK:, 0:LANES])
            den = jnp.where(head0, pv[0:BLK, LANES:], pv[BLK:, LANES:])
            o_refs[g][rows, :] = num / den
            l_refs[g][rows, :] = jnp.where(head0, m[0:BLK], m[BLK:]) + jnp.log2(den)

    for g, d in enumerate(DILATIONS):
        blocks = [(g, sub, res) for sub in range(ATT_TILE // (BLK * d)) for res in range(d)]
        for n in range(0, len(blocks), ATT_UNROLL[g]):
            yield from process(blocks[n:n + ATT_UNROLL[g]])

    l0, l1, l2 = l_refs[0][...], l_refs[1][...], l_refs[2][...]
    m = jnp.maximum(jnp.maximum(l0, l1), l2)
    w0, w1, w2 = jnp.exp2(l0 - m), jnp.exp2(l1 - m), jnp.exp2(l2 - m)
    y = (w0 * o_refs[0][...] + w1 * o_refs[1][...] + w2 * o_refs[2][...]) / (w0 + w1 + w2)
    y_ref[0] = (y * _silu(gb_ref[0].astype(F32))).astype(BF16)
    yield


N_SCAN_REFS = 9
N_ATTN_REFS = 17
MIXERS_VMEM_LIMIT = 62 * 1024 * 1024


def _mixers_kernel(*refs, nb, tt, tiles_per_seq, n_tiles):
    scan_in = refs[:N_SCAN_REFS]
    attn_in = refs[N_SCAN_REFS:N_SCAN_REFS + N_ATTN_REFS]
    ya_ref, yb_ref, s_ref = refs[N_SCAN_REFS + N_ATTN_REFS:N_SCAN_REFS + N_ATTN_REFS + 3]
    scratch = refs[N_SCAN_REFS + N_ATTN_REFS + 3:]
    is_first = ((pl.program_id(0) % n_tiles) % tiles_per_seq) == 0
    scan = _scan_stages(*scan_in, ya_ref, s_ref, nb=nb, tt=tt)
    attn = _attn_stages(attn_in[0:3], attn_in[3:6], attn_in[6:9], attn_in[9:12], attn_in[12:15],
                        attn_in[15], attn_in[16], yb_ref, scratch[0:3], scratch[3:6], is_first)
    for tag in scan:
        if tag != VPU_STAGE:
            next(attn, None)
    for _ in attn:
        pass


def _mixers(r, cum, k, v, a, b, main, groups, vec, ones_bd, bias5, layer):
    B, S, W = r.shape
    n_pairs = HEADS_PER_GROUP // 2
    tiles_per_seq = S // ATT_TILE
    n_tiles = B * tiles_per_seq
    n_steps = n_pairs * n_tiles
    tt = S // n_steps
    assert tt * n_steps == S and tt % (CHUNK * SCAN_CHUNKS) == 0

    scan_spec = pl.BlockSpec((B, tt, W), lambda t: (0, t, 0))

    def full(arr):
        return pl.BlockSpec(arr.shape, lambda t: (0,) * arr.ndim)

    scan_specs = [scan_spec] * 6 + [pl.BlockSpec((B, tt, W), lambda t: (0, t, COL_GA // W)),
                                    _layer_block(vec, layer), full(ones_bd)]

    arrays = [main.reshape(B, 1, S, MAIN_WIDTH)] + list(groups)
    col_base = [COL_A0 // LANES, 0, 0]

    def where(t):
        tile = t % n_tiles
        return t // n_tiles, tile // tiles_per_seq, tile % tiles_per_seq

    def cur(g, part):
        d = DILATIONS[g]
        c0 = col_base[g] + part * (A_OUT_WIDTH // LANES)

        def index(t):
            hp, bi, ti = where(t)
            return bi, 0, ti, c0 + hp
        return pl.BlockSpec((1, d, ATT_TILE // d, LANES), index)

    def prev(g, part):
        d = DILATIONS[g]
        c0 = col_base[g] + part * (A_OUT_WIDTH // LANES)
        rb = ATT_TILE // (BLK * d)

        def index(t):
            hp, bi, ti = where(t)
            return bi, 0, jnp.maximum(ti * rb - 1, 0), c0 + hp
        return pl.BlockSpec((1, d, BLK, LANES), index)

    def tile(col0):
        def index(t):
            hp, bi, ti = where(t)
            return bi, ti, col0 // LANES + hp
        return pl.BlockSpec((1, ATT_TILE, LANES), index)

    attn_specs = ([cur(g, 0) for g in range(N_GROUPS)] + [cur(g, 1) for g in range(N_GROUPS)]
                  + [cur(g, 2) for g in range(N_GROUPS)]
                  + [prev(g, 1) for g in range(N_GROUPS)] + [prev(g, 2) for g in range(N_GROUPS)]
                  + [tile(COL_GB),
                     pl.BlockSpec((N_GROUPS, 1, 2, BLK, 2 * BLK), lambda t: (0, t // n_tiles, 0, 0, 0))])
    assert len(scan_specs) == N_SCAN_REFS and len(attn_specs) == N_ATTN_REFS
    return pl.pallas_call(
        functools.partial(_mixers_kernel, nb=B, tt=tt, tiles_per_seq=tiles_per_seq, n_tiles=n_tiles),
        grid=(n_steps,),
        in_specs=scan_specs + attn_specs,
        out_specs=[scan_spec, tile(0)],
        out_shape=[jax.ShapeDtypeStruct((B, S, W), BF16), jax.ShapeDtypeStruct((B, S, A_OUT_WIDTH), BF16)],
        scratch_shapes=([pltpu.VMEM((B * N_COLGROUPS, HEAD_DIM, MXU_DIM), F32)]
                        + [pltpu.VMEM((ATT_TILE, LANES), F32)] * 6),
        compiler_params=_cparams(("arbitrary",), MIXERS_VMEM_LIMIT),
        name="mixers",
    )(r, cum, k, v, a, b, main, vec, ones_bd, *(arrays * 5), main, bias5)


def _merge_kernel(ya_ref, yb_ref, ma_ref, mb_ref, x_ref, mod_ref, wa_ref, wb_ref, wo_ref, fg_ref,
                  o_ref, *, final_norm):
    pa = _dot(ya_ref[0], wa_ref[...])
    pb = _dot(yb_ref[0], wb_ref[...])
    merged = _sigmoid(ma_ref[0].astype(F32)) * pa + _sigmoid(mb_ref[0].astype(F32)) * pb
    out = _dot(merged.astype(BF16), wo_ref[...])
    gate = mod_ref[0, :, 2 * D_MODEL:3 * D_MODEL]
    xn = x_ref[0] + gate * out
    if final_norm:
        ms = jnp.mean(xn * xn, axis=-1, keepdims=True)
        xn = xn * lax.rsqrt(ms + RMS_EPS) * fg_ref[...]
    o_ref[0] = xn


def _merge(ya, yb, proj, x, mod, wa, wb, wo, final_g, final_norm, layer, tm=1024):
    B, S, D = x.shape

    def rows(width, c):
        return pl.BlockSpec((1, tm, width), lambda b, i: (b, i, c))

    def full(arr):
        return pl.BlockSpec(arr.shape, lambda b, i: (0,) * arr.ndim)

    return pl.pallas_call(
        functools.partial(_merge_kernel, final_norm=final_norm),
        grid=(B, S // tm),
        in_specs=[rows(R_WIDTH, 0), rows(A_OUT_WIDTH, 0),
                  rows(D, COL_MA // D), rows(D, COL_MB // D), rows(D, 0),
                  pl.BlockSpec((1, 1, 3 * D), lambda b, i: (layer * MOD_ROWS + b, 0, 0)),
                  _layer_block(wa, layer), _layer_block(wb, layer), _layer_block(wo, layer), full(final_g)],
        out_specs=rows(D, 0),
        out_shape=jax.ShapeDtypeStruct((B, S, D), F32),
        compiler_params=_cparams(("parallel", "parallel")),
        name="merge",
    )(ya, yb, proj, proj, x, mod, wa, wb, wo, final_g)


def _segment_ones():
    idx = np.arange(MXU_DIM)
    return jnp.asarray(idx[:, None] // HEAD_DIM == idx[None, :] // HEAD_DIM, BF16)


def kernel(x, c, norm_g, ada_w, ada_b, w_in, rwkv_mu_rkv, rwkv_mu_wa, rwkv_w0, rwkv_w1, rwkv_w2, rwkv_a0, rwkv_a1, rwkv_a2, rwkv_k_k, rwkv_k_a, rwkv_r_k, rwkv_ln_g, rwkv_ln_b, rwkv_mu_v, rwkv_v0, rwkv_v1, rwkv_v2, w_branch_a, w_branch_b, w_out, rel_bias, final_g):
    B, S, D = x.shape
    assert D == D_MODEL and S % ATT_TILE == 0 and w_in.shape[2] == PROJ_WIDTH
    ones_bd = _segment_ones()
    mod = _adaln_mod(c, ada_w, ada_b).reshape(DEPTH * MOD_ROWS, 1, 3 * D)
    bias = _rel_bias(rel_bias).reshape(N_GROUPS, HEADS_PER_GROUP // 2, 2, BLK, 2 * BLK)

    def cols(start, width):
        return w_in[:, :, start:start + width]

    def group_cols(g):
        return [cols(W_AQ + A_OUT_WIDTH * g, A_OUT_WIDTH) * (LOG2E / math.sqrt(HEAD_DIM)),
                cols(W_AK + A_OUT_WIDTH * g, A_OUT_WIDTH), cols(W_AV + A_OUT_WIDTH * g, A_OUT_WIDTH)]

    w_main = jnp.concatenate(
        [cols(W_R, 4 * R_WIDTH), cols(W_MA, 2 * D_MODEL), cols(W_GB, A_OUT_WIDTH)] + group_cols(0),
        axis=2).astype(BF16)
    w_groups = [jnp.concatenate(group_cols(g), axis=2).astype(BF16) for g in range(1, N_GROUPS)]
    zeros_rows = jnp.zeros((DEPTH, D), F32)
    pvec = jnp.stack([rwkv_mu_rkv[:, 0], rwkv_mu_rkv[:, 1], rwkv_mu_rkv[:, 2], rwkv_w0, rwkv_a0, rwkv_k_k,
                      rwkv_k_a, _first_layer_blank(rwkv_v0)] + [zeros_rows] * (PV_ROWS - 8), axis=1)
    lora = _pack_lora([(rwkv_mu_wa[:, 0], rwkv_w1, rwkv_w2), (rwkv_mu_wa[:, 1], rwkv_a1, rwkv_a2),
                       (_first_layer_blank(rwkv_mu_v), _first_layer_blank(rwkv_v1),
                        _first_layer_blank(rwkv_v2))])
    vec = jnp.stack([rwkv_r_k.reshape(DEPTH, -1), rwkv_ln_g, rwkv_ln_b] + [zeros_rows] * 5, axis=1)
    norm_g3 = norm_g.reshape(DEPTH, 1, D)
    wa, wb, wo = w_branch_a.astype(BF16), w_branch_b.astype(BF16), w_out.astype(BF16)

    v_first = None
    for i in range(DEPTH):
        proj, h = _norm_proj(x, mod, norm_g3, w_main, i)
        (r, cum, k, v, a, b), groups = _rwkv_prep(h, proj, v_first, pvec, lora if i > 0 else lora[:3],
                                                  ones_bd, w_groups, i)
        if i == 0:
            v_first = v
        y_a, y_b = _mixers(r, cum, k, v, a, b, proj, groups, vec, ones_bd, bias, i)
        x = _merge(y_a, y_b, proj, x, mod, wa, wb, wo, final_g.reshape(1, D),
                   final_norm=(i == DEPTH - 1), layer=i)
    return x
```

```python
import functools
import math

import numpy as np
import jax
import jax.numpy as jnp
from jax import lax
from jax.experimental import pallas as pl
from jax.experimental.pallas import tpu as pltpu

F32 = jnp.float32
BF16 = jnp.bfloat16

D_MODEL = 1024
DEPTH = 2
HEAD_DIM = 64
R_WIDTH = 1024
N_GROUPS = 3
HEADS_PER_GROUP = 8
DILATIONS = (1, 4, 16)
BLK = 128
A_QK_WIDTH = 1536
A_OUT_WIDTH = 512
NUM_BUCKETS = 32
MAX_DISTANCE = 2048
PROJ_WIDTH = 4 * R_WIDTH + 3 * A_QK_WIDTH + A_OUT_WIDTH + 2 * D_MODEL
RMS_EPS = 1e-6
GN_EPS = 64e-5
NEG_INF = -1e30
LOG2E = math.log2(math.e)

LANES = 128
MXU_DIM = 256
HEADS_PER_TILE = MXU_DIM // HEAD_DIM
N_COLGROUPS = R_WIDTH // MXU_DIM
CHUNK = 64
SCAN_CHUNKS = 2

W_R, W_K, W_V, W_GA = 0, 1024, 2048, 3072
W_AQ, W_AK, W_AV = 4096, 5632, 7168
W_GB, W_MA, W_MB = 8704, 9216, 10240
COL_R, COL_K, COL_V, COL_GA, COL_MA, COL_MB, COL_GB, COL_A0 = 0, 1024, 2048, 3072, 4096, 5120, 6144, 6656
MAIN_WIDTH = 8192
GROUP_WIDTH = 3 * A_OUT_WIDTH

VMEM_LIMIT = 56 * 1024 * 1024
MOD_ROWS = 8


def _cparams(sem, vmem_limit=VMEM_LIMIT):
    return pltpu.CompilerParams(dimension_semantics=sem, vmem_limit_bytes=vmem_limit)


def _sigmoid(z):
    return 1.0 / (1.0 + jnp.exp(-z))


def _silu(z):
    return z * _sigmoid(z)


def _dot(a, b):
    return jnp.dot(a, b, preferred_element_type=F32)


def _dot_nt(a, b):
    return lax.dot_general(a, b, (((1,), (1,)), ((), ())), preferred_element_type=F32)


def _split2(x):
    hi = x.astype(BF16)
    lo = (x - hi.astype(F32)).astype(BF16)
    return hi, lo


def _segsum64(x, ones_bd, split):
    n = x.shape[0]
    xs = jnp.concatenate([x[:, MXU_DIM * g:MXU_DIM * (g + 1)] for g in range(N_COLGROUPS)], axis=0)
    if split:
        hi, lo = _split2(xs)
        s = _dot(hi, ones_bd) + _dot(lo, ones_bd)
    else:
        s = _dot(xs.astype(BF16), ones_bd)
    return jnp.concatenate([s[n * g:n * (g + 1)] for g in range(N_COLGROUPS)], axis=1)


def _mod_kernel(c_ref, w_ref, b_ref, o_ref):
    s = _silu(c_ref[...])
    o_ref[0] = _dot(s.astype(BF16), w_ref[0].astype(BF16)) + b_ref[0]


def _adaln_mod(c, ada_w, ada_b):
    L = ada_w.shape[0]
    B = c.shape[0]
    c_rows = jnp.pad(c, ((0, MOD_ROWS - B), (0, 0)))
    nj = 3
    return pl.pallas_call(
        _mod_kernel,
        grid=(L, nj),
        in_specs=[pl.BlockSpec((MOD_ROWS, D_MODEL), lambda l, j: (0, 0)),
                  pl.BlockSpec((1, D_MODEL, D_MODEL), lambda l, j: (l, 0, j)),
                  pl.BlockSpec((1, 1, D_MODEL), lambda l, j: (l, 0, j))],
        out_specs=pl.BlockSpec((1, MOD_ROWS, D_MODEL), lambda l, j: (l, 0, j)),
        out_shape=jax.ShapeDtypeStruct((L, MOD_ROWS, 3 * D_MODEL), F32),
        compiler_params=_cparams(("parallel", "parallel")),
        name="adaln_mod",
    )(c_rows, ada_w, ada_b.reshape(L, 1, 3 * D_MODEL))


def _t5_bucket(dist):
    max_exact = NUM_BUCKETS // 2
    safe = np.maximum(dist, 1).astype(np.float32)
    large = max_exact + (np.log(safe / max_exact) / math.log(MAX_DISTANCE / max_exact)
                         * (NUM_BUCKETS - max_exact)).astype(np.int32)
    large = np.minimum(large, NUM_BUCKETS - 1)
    return np.where(dist < max_exact, dist, large).astype(np.int32)


def _bias_kernel(tab_ref, bucket_ref, o_ref):
    g = pl.program_id(0)
    bk = bucket_ref[0]
    for hh in range(HEADS_PER_GROUP):
        h = g * HEADS_PER_GROUP + hh
        acc = jnp.zeros(bk.shape, F32)
        for b in range(NUM_BUCKETS):
            acc = jnp.where(bk == b, tab_ref[h * NUM_BUCKETS + b], acc)
        o_ref[hh] = jnp.where(bk >= 0, acc * LOG2E, NEG_INF)


def _rel_bias(rel_bias):
    n_heads = rel_bias.shape[1]
    qi = np.arange(BLK)[:, None]
    ki = np.arange(2 * BLK)[None, :]
    delta = qi + BLK - ki
    band = (delta >= 0) & (delta <= BLK)
    buckets = np.stack([np.where(band, _t5_bucket(np.maximum(delta, 0) * d), -1)
                        for d in DILATIONS]).astype(np.int32)
    table = rel_bias.T.reshape(-1)
    return pl.pallas_call(
        _bias_kernel,
        grid=(n_heads // HEADS_PER_GROUP,),
        in_specs=[pl.BlockSpec(memory_space=pltpu.SMEM),
                  pl.BlockSpec((1, BLK, 2 * BLK), lambda g: (g, 0, 0))],
        out_specs=pl.BlockSpec((HEADS_PER_GROUP, BLK, 2 * BLK), lambda g: (g, 0, 0)),
        out_shape=jax.ShapeDtypeStruct((n_heads, BLK, 2 * BLK), F32),
        compiler_params=_cparams(("parallel",)),
        name="rel_bias",
    )(table, jnp.asarray(buckets))


def _proj_kernel(x_ref, mod_ref, g_ref, w_ref, proj_ref, h_ref):
    @pl.when(pl.program_id(2) == 0)
    def _():
        x = x_ref[0]
        ms = jnp.mean(x * x, axis=-1, keepdims=True)
        y = x * lax.rsqrt(ms + RMS_EPS) * g_ref[...]
        shift = mod_ref[0, :, 0:D_MODEL]
        scale = mod_ref[0, :, D_MODEL:2 * D_MODEL]
        h_ref[0] = (y * (1.0 + scale) + shift).astype(BF16)

    proj_ref[0] = _dot(h_ref[0], w_ref[...]).astype(BF16)


def _layer_block(arr, layer):
    tail = (0,) * (arr.ndim - 1)
    return pl.BlockSpec((None,) + arr.shape[1:], lambda *_: (layer,) + tail)


def _norm_proj(x, mod, norm_g, w_main, layer, tm=1024, tn=4096):
    B, S, D = x.shape
    N = w_main.shape[2]
    return pl.pallas_call(
        _proj_kernel,
        grid=(B, S // tm, N // tn),
        in_specs=[pl.BlockSpec((1, tm, D), lambda b, i, j: (b, i, 0)),
                  pl.BlockSpec((1, 1, 3 * D), lambda b, i, j: (layer * MOD_ROWS + b, 0, 0)),
                  _layer_block(norm_g, layer),
                  pl.BlockSpec((None, D, tn), lambda b, i, j: (layer, 0, j))],
        out_specs=[pl.BlockSpec((1, tm, tn), lambda b, i, j: (b, i, j)),
                   pl.BlockSpec((1, tm, D), lambda b, i, j: (b, i, 0))],
        out_shape=[jax.ShapeDtypeStruct((B, S, N), BF16),
                   jax.ShapeDtypeStruct((B, S, D), BF16)],
        compiler_params=_cparams(("parallel", "parallel", "arbitrary")),
        name="norm_proj",
    )(x, mod, norm_g, w_main)


GATHER_ROWS = MXU_DIM


def _group_proj_stages(h_ref, perm_ref, w_ref, o_ref, d):
    tm = h_ref.shape[1]
    per_res = GATHER_ROWS // d
    perm = perm_ref[...]
    for ck in range(tm // GATHER_ROWS):
        r0 = ck * GATHER_ROWS
        hp = _dot(perm, h_ref[0, r0:r0 + GATHER_ROWS, :]).astype(BF16)
        yield
        res = _dot(hp, w_ref[...]).astype(BF16)
        for r in range(d):
            o_ref[0, r, ck * per_res:(ck + 1) * per_res, :] = res[r * per_res:(r + 1) * per_res]
        yield


def _gather_perm(d):
    dst = np.arange(GATHER_ROWS)
    src = (dst % (GATHER_ROWS // d)) * d + dst // (GATHER_ROWS // d)
    return jnp.asarray(src[:, None] == np.arange(GATHER_ROWS)[None, :], BF16)


PV_MU_R, PV_MU_K, PV_MU_V, PV_W0, PV_A0, PV_KK, PV_KA, PV_V0 = range(8)
LORA_LANES = 256


def _pack_lora(paths):
    n_layers, d_model, _ = paths[0][1].shape
    used = sum(down.shape[2] for _, down, _ in paths)
    pad = jnp.zeros((n_layers, d_model, LORA_LANES - used), F32)
    keep = jnp.concatenate([(1.0 - mu)[:, :, None] * down for mu, down, _ in paths] + [pad], axis=2)
    shifted = jnp.concatenate([mu[:, :, None] * down for mu, down, _ in paths] + [pad], axis=2)
    ups, lane = [], 0
    for _, down, up in paths:
        rank = down.shape[2]
        ups.append(jnp.pad(up, ((0, 0), (lane, LORA_LANES - lane - rank), (0, 0))).astype(BF16))
        lane += rank
    return [jnp.concatenate([keep, shifted], axis=2).astype(BF16)] + ups


def _first_layer_blank(arr):
    return jnp.concatenate([jnp.zeros((1,) + arr.shape[1:], arr.dtype), arr], axis=0)
PV_ROWS = 16
PREV_ROWS = 16


def _shift_rows(t, prev_last):
    rolled = pltpu.roll(t, 1, axis=0)
    row = lax.broadcasted_iota(jnp.int32, t.shape, 0)
    return jnp.where(row == 0, prev_last, rolled)


def _rprep_stages(refs, has_vres):
    if has_vres:
        (h_ref, hp_ref, pr_ref, prp_ref, pk_ref, pkp_ref, pvv_ref, pvp_ref, vf_ref, pvec_ref,
         wd_ref, uw_ref, ua_ref, uv_ref, ones_ref, tril_ref,
         r_out, cum_out, k_out, v_out, a_out, b_out) = refs
    else:
        (h_ref, hp_ref, pr_ref, prp_ref, pk_ref, pkp_ref, pvv_ref, pvp_ref, pvec_ref,
         wd_ref, uw_ref, ua_ref, ones_ref, tril_ref,
         r_out, cum_out, k_out, v_out, a_out, b_out) = refs

    not_first = (pl.program_id(1) > 0).astype(F32)

    def prm(i):
        return pvec_ref[i:i + 1, :]

    def lerp_shift(cur_ref, prev_ref, mu):
        t = cur_ref[0].astype(F32)
        last = prev_ref[0, PREV_ROWS - 1:PREV_ROWS, :].astype(F32)
        return t + (_shift_rows(t, last * not_first) - t) * mu

    r = lerp_shift(pr_ref, prp_ref, prm(PV_MU_R))
    k = lerp_shift(pk_ref, pkp_ref, prm(PV_MU_K))
    v = lerp_shift(pvv_ref, pvp_ref, prm(PV_MU_V))
    yield

    wd = wd_ref[...]
    z2 = _dot(h_ref[0], wd)
    z_prev = _dot(hp_ref[0], wd[:, LORA_LANES:])[PREV_ROWS - 1:PREV_ROWS, :]
    z = z2[:, 0:LORA_LANES] + _shift_rows(z2[:, LORA_LANES:], z_prev * not_first)
    zb = z.astype(BF16)
    yield

    zw = prm(PV_W0) + _dot(jnp.tanh(z).astype(BF16), uw_ref[...])
    w = jnp.minimum(zw, 0.0) - jnp.log(1.0 + jnp.exp(-jnp.abs(zw))) - 0.5
    lw = -jnp.exp(w)
    yield
    hi, lo = _split2(lw)
    tril = tril_ref[...]
    for i in range(lw.shape[0] // MXU_DIM):
        blk = slice(MXU_DIM * i, MXU_DIM * (i + 1))
        cum_out[0, blk, :] = _dot(tril, hi[blk]) + _dot(tril, lo[blk])
    yield
    a = _sigmoid(prm(PV_A0) + _dot(zb, ua_ref[...]))
    if has_vres:
        mix = _sigmoid(prm(PV_V0) + _dot(zb, uv_ref[...]))
        v = v + (vf_ref[0].astype(F32) - v) * mix
    yield

    kk = k * prm(PV_KK)
    ss = _segsum64(kk * kk, ones_ref[...], split=False)
    kk = kk * lax.rsqrt(jnp.maximum(ss, 1e-24))
    yield
    r_out[0] = r.astype(BF16)
    k_out[0] = (k * (1.0 + (a - 1.0) * prm(PV_KA))).astype(BF16)
    v_out[0] = v.astype(BF16)
    a_out[0] = (-kk).astype(BF16)
    b_out[0] = (kk * a).astype(BF16)
    yield


N_GROUP_PROJ = N_GROUPS - 1


def _prep_groups_kernel(*refs, has_vres):
    n_in = len(refs) - 6 - N_GROUP_PROJ - 2 * N_GROUP_PROJ
    prep_in, rest = refs[:n_in], refs[n_in:]
    gp_in, outs = rest[:2 * N_GROUP_PROJ], rest[2 * N_GROUP_PROJ:]
    prep = _rprep_stages(tuple(prep_in) + tuple(outs[:6]), has_vres)
    h_ref = prep_in[0]
    groups = [_group_proj_stages(h_ref, gp_in[2 * n], gp_in[2 * n + 1], outs[6 + n], DILATIONS[n + 1])
              for n in range(N_GROUP_PROJ)]

    def all_groups():
        for gen in groups:
            yield from gen

    for _ in _interleave(prep, all_groups()):
        pass


def _rwkv_prep(h, proj, v_first, pvec, lora, ones_bd, w_groups, layer, tr=512):
    B, S, D = h.shape
    has_vres = v_first is not None
    rpb = tr // PREV_ROWS
    t = np.arange(MXU_DIM)
    tril_bd =jnp.asarray((t[None, :] <= t[:, None]) & (t[None, :] // CHUNK == t[:, None] // CHUNK), BF16)

    def cur(c):
        return pl.BlockSpec((1, tr, R_WIDTH), lambda b, i: (b, i, c))

    def prev(c):
        return pl.BlockSpec((1, PREV_ROWS, R_WIDTH), lambda b, i: (b, jnp.maximum(i * rpb - 1, 0), c))

    def full(arr):
        return pl.BlockSpec(arr.shape, lambda b, i: (0,) * arr.ndim)

    in_specs = [cur(0), prev(0)]
    args = [h, h]
    for c in (COL_R, COL_K, COL_V):
        in_specs += [cur(c // R_WIDTH), prev(c // R_WIDTH)]
        args += [proj, proj]
    if has_vres:
        in_specs.append(cur(0))
        args.append(v_first)
    for per_layer in [pvec] + list(lora):
        in_specs.append(_layer_block(per_layer, layer))
        args.append(per_layer)
    for const in (ones_bd, tril_bd):
        in_specs.append(full(const))
        args.append(const)
    out = [jax.ShapeDtypeStruct((B, S, R_WIDTH), F32 if n == 1 else BF16) for n in range(6)]
    out_specs = [cur(0)] * 6
    for n, w_group in enumerate(w_groups):
        d = DILATIONS[n + 1]
        width = w_group.shape[2]
        in_specs += [full(_gather_perm(d)), _layer_block(w_group, layer)]
        args += [_gather_perm(d), w_group]
        out.append(jax.ShapeDtypeStruct((B, d, S // d, width), BF16))
        out_specs.append(pl.BlockSpec((1, d, tr // d, width), lambda b, i: (b, 0, i, 0)))
    res = pl.pallas_call(
        functools.partial(_prep_groups_kernel, has_vres=has_vres),
        grid=(B, S // tr),
        in_specs=in_specs,
        out_specs=out_specs,
        out_shape=out,
        compiler_params=_cparams(("parallel", "parallel")),
        name="prep_groups",
    )(*args)
    return res[:6], res[6:]


VPU_STAGE = "vpu"


def _interleave(*gens):
    live = list(gens)
    while live:
        tags = []
        for gen in list(live):
            tag = next(gen, StopIteration)
            if tag is StopIteration:
                live.remove(gen)
            else:
                tags.append(tag)
        yield VPU_STAGE if tags and all(t == VPU_STAGE for t in tags) else None


def _scan_stages(r_ref, cum_ref, k_ref, v_ref, a_ref, b_ref, ga_ref, vec_ref, ones_ref,
                 y_ref, s_ref, *, nb, tt):
    C = CHUNK

    @pl.when(pl.program_id(0) == 0)
    def _():
        s_ref[...] = jnp.zeros(s_ref.shape, F32)

    row = lax.broadcasted_iota(jnp.int32, (C, MXU_DIM), 0)
    lane = lax.broadcasted_iota(jnp.int32, (C, MXU_DIM), 1)
    col = lane & (HEAD_DIM - 1)
    lhead = lane >> 6
    strict = col < row
    incl = col <= row
    eye = (col == row).astype(F32)
    head_masks = [lhead == hh for hh in range(HEADS_PER_TILE)]

    def bdrows(x):
        return jnp.concatenate([jnp.where(m, x, 0.0) for m in head_masks], axis=0).astype(BF16)

    def diag_blocks(full):
        acc = jnp.where(head_masks[0], full[0:C], 0.0)
        for hh in range(1, HEADS_PER_TILE):
            acc = acc + jnp.where(head_masks[hh], full[C * hh:C * (hh + 1)], 0.0)
        return acc

    row_full = lax.broadcasted_iota(jnp.int32, (C, R_WIDTH), 0)
    ones_bd = ones_ref[...]
    r_k = vec_ref[0:1, :]
    ln_g = vec_ref[1:2, :]
    ln_b = vec_ref[2:3, :]

    chains = [(bi, g) for bi in range(nb) for g in range(N_COLGROUPS)]
    insts = [(ck, bi, g) for ck in range(SCAN_CHUNKS) for bi, g in chains]


    def load(gi):
        rows = [slice((gi * SCAN_CHUNKS + ck) * C, (gi * SCAN_CHUNKS + ck + 1) * C) for ck in range(SCAN_CHUNKS)]
        ops = {}
        for ck in range(SCAN_CHUNKS):
            for bi in range(nb):
                cum = cum_ref[bi, rows[ck], :]
                r = r_ref[bi, rows[ck], :].astype(F32)
                k = k_ref[bi, rows[ck], :].astype(F32)
                v = v_ref[bi, rows[ck], :].astype(F32)
                a = a_ref[bi, rows[ck], :].astype(F32)
                b = b_ref[bi, rows[ck], :].astype(F32)
                total = cum[C - 1:C, :]
                p_in = jnp.exp(cum)
                p_inv = jnp.exp(-cum)
                p_rest = jnp.exp(total - cum)
                p_before = jnp.where(row_full == 0, 1.0, pltpu.roll(p_in, 1, axis=0))
                ops[ck, bi] = dict(r=r, k=k, v=v, a_t=a * p_before, r_t=r * p_in, b_t=b * p_inv,
                                   k_t=k * p_inv, bp=b * p_rest, kp=k * p_rest, p_all=jnp.exp(total))
        return dict(rows=rows, ops=ops)

    def part(ctx, name, ck, bi, g):
        return ctx["ops"][ck, bi][name][:, MXU_DIM * g:MXU_DIM * (g + 1)]

    def independent(ctx):
        res = [_dot_nt(jnp.concatenate([part(ctx, "a_t", *i), part(ctx, "r_t", *i)], axis=0).astype(BF16),
                       jnp.concatenate([bdrows(part(ctx, "b_t", *i)), bdrows(part(ctx, "k_t", *i))], axis=0))
               for i in insts]
        yield
        a_ab = [jnp.where(strict, x[0:C, 0:MXU_DIM], 0.0) for x in res]
        a_ak = [jnp.where(strict, x[0:C, MXU_DIM:], 0.0) for x in res]
        ctx["a_rb"] = [jnp.where(incl, x[C:, 0:MXU_DIM], 0.0).astype(BF16) for x in res]
        a_rk = [jnp.where(incl, x[C:, MXU_DIM:], 0.0) for x in res]

        pw = [_dot(x.astype(BF16), bdrows(x)) for x in a_ab]
        tinv = [eye + x for x in a_ab]
        yield
        for _ in range(4):
            both = [_dot(jnp.concatenate([p, t], axis=0).astype(BF16), bdrows(p)) for p, t in zip(pw, tinv)]
            tinv = [t + x[C:] for t, x in zip(tinv, both)]
            pw = [x[0:C] for x in both]
            yield
        tinv = [t + _dot(t.astype(BF16), bdrows(p)) for p, t in zip(pw, tinv)]
        yield
        tax = [_dot(t.astype(BF16), jnp.concatenate([bdrows(part(ctx, "a_t", *i)), bdrows(x)], axis=1))
               for t, x, i in zip(tinv, a_ak, insts)]
        yield
        ctx["from_v"] = [_dot(jnp.concatenate([x[:, MXU_DIM:], ark], axis=0).astype(BF16),
                              bdrows(part(ctx, "v", *i))) for x, ark, i in zip(tax, a_rk, insts)]
        ctx["tax"] = tax
        yield

    def dependent(ctx, carried):
        tax, from_v, a_rb = ctx["tax"], ctx["from_v"], ctx["a_rb"]
        st = carried["st"]
        y = {}
        for ck in range(SCAN_CHUNKS):
            sel = range(ck * len(chains), (ck + 1) * len(chains))
            from_state = [_dot_nt(jnp.concatenate([tax[n][:, 0:MXU_DIM], part(ctx, "r_t", *insts[n])],
                                                  axis=0).astype(BF16), bdrows(s))
                          for n, s in zip(sel, st)]
            yield
            u = [x[0:C] + from_v[n][0:C] for x, n in zip(from_state, sel)]
            for x, n, uu in zip(from_state, sel, u):
                y[insts[n]] = x[C:] + from_v[n][C:] + _dot(a_rb[n], bdrows(uu))
            upd = [_dot(jnp.concatenate([uu, part(ctx, "v", *insts[n])], axis=0).T.astype(BF16),
                        jnp.concatenate([part(ctx, "bp", *insts[n]), part(ctx, "kp", *insts[n])],
                                        axis=0).astype(BF16))
                   for uu, n in zip(u, sel)]
            yield
            st = [s_old * part(ctx, "p_all", *insts[n]) + diag_blocks(x) for s_old, x, n in zip(st, upd, sel)]
        carried["st"] = st

        for ck in range(SCAN_CHUNKS):
            for bi in range(nb):
                p = ctx["ops"][ck, bi]
                yc = jnp.concatenate([y[ck, bi, g] for g in range(N_COLGROUPS)], axis=1)
                mean = _segsum64(yc, ones_bd, split=True) * (1.0 / HEAD_DIM)
                yield VPU_STAGE
                yd = yc - mean
                var = _segsum64(yd * yd, ones_bd, split=False) * (1.0 / HEAD_DIM)
                yn = yd * lax.rsqrt(var + GN_EPS) * ln_g + ln_b
                bonus = _segsum64(p["r"] * p["k"] * r_k, ones_bd, split=False) * p["v"]
                rows = ctx["rows"][ck]
                y_ref[bi, rows, :] = ((yn + bonus) * _silu(ga_ref[bi, rows, :].astype(F32))).astype(BF16)
                yield VPU_STAGE

    carried = {"st": [s_ref[bi * N_COLGROUPS + g] for bi, g in chains]}
    n_groups = tt // (C * SCAN_CHUNKS)
    ctx = load(0)
    yield from independent(ctx)
    for gi in range(1, n_groups):
        nxt = load(gi)
        yield from _interleave(dependent(ctx, carried), independent(nxt))
        ctx = nxt
    yield from dependent(ctx, carried)
    for (bi, g), s_new in zip(chains, carried["st"]):
        s_ref[bi * N_COLGROUPS + g] = s_new


ATT_TILE = 2048
ATT_UNROLL = (5, 6, 8)


def _attn_stages(q_refs, k_refs, v_refs, kp_refs, vp_refs, gb_ref, bias_ref, y_ref, o_refs, l_refs,
                 is_first):
    prev_limit = jnp.where(is_first, BLK, 0)
    ki = lax.broadcasted_iota(jnp.int32, (2 * BLK, 2 * BLK), 1)
    head0 = lax.broadcasted_iota(jnp.int32, (BLK, LANES), 1) < HEAD_DIM
    ones_cols = jnp.ones((2 * BLK, LANES), BF16)
    zero = jnp.zeros((BLK, LANES), BF16)

    def process(blocks):
        q2s, kws, vws, bias2s, stores = [], [], [], [], []
        for g, sub, res in blocks:
            d = DILATIONS[g]
            base = sub * (BLK * d) + res
            q = q_refs[g][0, res, sub * BLK:(sub + 1) * BLK, :]
            q2s.append(jnp.concatenate([jnp.where(head0, q, zero), jnp.where(head0, zero, q)], axis=0))
            if sub == 0:
                kw = jnp.concatenate([kp_refs[g][0, res], k_refs[g][0, res, 0:BLK, :]], axis=0)
                vw = jnp.concatenate([vp_refs[g][0, res], v_refs[g][0, res, 0:BLK, :]], axis=0)
            else:
                kw = k_refs[g][0, res, (sub - 1) * BLK:(sub + 1) * BLK, :]
                vw = v_refs[g][0, res, (sub - 1) * BLK:(sub + 1) * BLK, :]
            kws.append(kw)
            vws.append(jnp.concatenate([vw, ones_cols], axis=1))
            bias2s.append(bias_ref[g, 0].reshape(2 * BLK, 2 * BLK))
            stores.append((g, pl.ds(base, BLK) if d == 1 else pl.ds(base, BLK, stride=d)))
        logits = [jnp.where(bias2 > 0.5 * NEG_INF, _dot_nt(q2, kw) + bias2, NEG_INF)
                  for q2, kw, bias2 in zip(q2s, kws, bias2s)]
        logits = [jnp.where(ki < prev_limit, NEG_INF, x) if blk[1] == 0 else x
                  for x, blk in zip(logits, blocks)]
        yield
        ms = [jnp.max(x, axis=-1, keepdims=True) for x in logits]
        ps = [jnp.exp2(x - m).astype(BF16) for x, m in zip(logits, ms)]
        pvs = [_dot(p, vw) for p, vw in zip(ps, vws)]
        yield
        for (g, rows), pv, m in zip(stores, pvs, ms):
            num = jnp.where(head0, pv[0:BLK, 0:LANES], pv[BLK:, 0:LANES])
            den = jnp.where(head0, pv[0:BLK, LANES:], pv[BLK:, LANES:])
            o_refs[g][rows, :] = num / den
            l_refs[g][rows, :] = jnp.where(head0, m[0:BLK], m[BLK:]) + jnp.log2(den)

    for g, d in enumerate(DILATIONS):
        blocks = [(g, sub, res) for sub in range(ATT_TILE // (BLK * d)) for res in range(d)]
        for n in range(0, len(blocks), ATT_UNROLL[g]):
            yield from process(blocks[n:n + ATT_UNROLL[g]])

    l0, l1, l2 = l_refs[0][...], l_refs[1][...], l_refs[2][...]
    m = jnp.maximum(jnp.maximum(l0, l1), l2)
    w0, w1, w2 = jnp.exp2(l0 - m), jnp.exp2(l1 - m), jnp.exp2(l2 - m)
    y = (w0 * o_refs[0][...] + w1 * o_refs[1][...] + w2 * o_refs[2][...]) / (w0 + w1 + w2)
    y_ref[0] = (y * _silu(gb_ref[0].astype(F32))).astype(BF16)
    yield


N_SCAN_REFS = 9
N_ATTN_REFS = 17
MIXERS_VMEM_LIMIT = 62 * 1024 * 1024
ATTN_START_ROUND = 6


def _mixers_kernel(*refs, nb, tt, tiles_per_seq, n_tiles):
    scan_in = refs[:N_SCAN_REFS]
    attn_in = refs[N_SCAN_REFS:N_SCAN_REFS + N_ATTN_REFS]
    ya_ref, yb_ref, s_ref = refs[N_SCAN_REFS + N_ATTN_REFS:N_SCAN_REFS + N_ATTN_REFS + 3]
    scratch = refs[N_SCAN_REFS + N_ATTN_REFS + 3:]
    is_first = ((pl.program_id(0) % n_tiles) % tiles_per_seq) == 0
    scan = _scan_stages(*scan_in, ya_ref, s_ref, nb=nb, tt=tt)
    attn = _attn_stages(attn_in[0:3], attn_in[3:6], attn_in[6:9], attn_in[9:12], attn_in[12:15],
                        attn_in[15], attn_in[16], yb_ref, scratch[0:3], scratch[3:6], is_first)
    for n, tag in enumerate(scan):
        if tag != VPU_STAGE and n >= ATTN_START_ROUND:
            next(attn, None)
    for _ in attn:
        pass


def _mixers(r, cum, k, v, a, b, main, groups, vec, ones_bd, bias5, layer):
    B, S, W = r.shape
    n_pairs = HEADS_PER_GROUP // 2
    tiles_per_seq = S // ATT_TILE
    n_tiles = B * tiles_per_seq
    n_steps = n_pairs * n_tiles
    tt = S // n_steps
    assert tt * n_steps == S and tt % (CHUNK * SCAN_CHUNKS) == 0

    scan_spec = pl.BlockSpec((B, tt, W), lambda t: (0, t, 0))

    def full(arr):
        return pl.BlockSpec(arr.shape, lambda t: (0,) * arr.ndim)

    scan_specs = [scan_spec] * 6 + [pl.BlockSpec((B, tt, W), lambda t: (0, t, COL_GA // W)),
                                    _layer_block(vec, layer), full(ones_bd)]

    arrays = [main.reshape(B, 1, S, MAIN_WIDTH)] + list(groups)
    col_base = [COL_A0 // LANES, 0, 0]

    def where(t):
        tile = t % n_tiles
        return t // n_tiles, tile // tiles_per_seq, tile % tiles_per_seq

    def cur(g, part):
        d = DILATIONS[g]
        c0 = col_base[g] + part * (A_OUT_WIDTH // LANES)

        def index(t):
            hp, bi, ti = where(t)
            return bi, 0, ti, c0 + hp
        return pl.BlockSpec((1, d, ATT_TILE // d, LANES), index)

    def prev(g, part):
        d = DILATIONS[g]
        c0 = col_base[g] + part * (A_OUT_WIDTH // LANES)
        rb = ATT_TILE // (BLK * d)

        def index(t):
            hp, bi, ti = where(t)
            return bi, 0, jnp.maximum(ti * rb - 1, 0), c0 + hp
        return pl.BlockSpec((1, d, BLK, LANES), index)

    def tile(col0):
        def index(t):
            hp, bi, ti = where(t)
            return bi, ti, col0 // LANES + hp
        return pl.BlockSpec((1, ATT_TILE, LANES), index)

    attn_specs = ([cur(g, 0) for g in range(N_GROUPS)] + [cur(g, 1) for g in range(N_GROUPS)]
                  + [cur(g, 2) for g in range(N_GROUPS)]
                  + [prev(g, 1) for g in range(N_GROUPS)] + [prev(g, 2) for g in range(N_GROUPS)]
                  + [tile(COL_GB),
                     pl.BlockSpec((N_GROUPS, 1, 2, BLK, 2 * BLK), lambda t: (0, t // n_tiles, 0, 0, 0))])
    assert len(scan_specs) == N_SCAN_REFS and len(attn_specs) == N_ATTN_REFS
    return pl.pallas_call(
        functools.partial(_mixers_kernel, nb=B, tt=tt, tiles_per_seq=tiles_per_seq, n_tiles=n_tiles),
        grid=(n_steps,),
        in_specs=scan_specs + attn_specs,
        out_specs=[scan_spec, tile(0)],
        out_shape=[jax.ShapeDtypeStruct((B, S, W), BF16), jax.ShapeDtypeStruct((B, S, A_OUT_WIDTH), BF16)],
        scratch_shapes=([pltpu.VMEM((B * N_COLGROUPS, HEAD_DIM, MXU_DIM), F32)]
                        + [pltpu.VMEM((ATT_TILE, LANES), F32)] * 6),
        compiler_params=_cparams(("arbitrary",), MIXERS_VMEM_LIMIT),
        name="mixers",
    )(r, cum, k, v, a, b, main, vec, ones_bd, *(arrays * 5), main, bias5)


def _merge_kernel(ya_ref, yb_ref, ma_ref, mb_ref, x_ref, mod_ref, wa_ref, wb_ref, wo_ref, fg_ref,
                  o_ref, *, final_norm):
    pa = _dot(ya_ref[0], wa_ref[...])
    pb = _dot(yb_ref[0], wb_ref[...])
    merged = _sigmoid(ma_ref[0].astype(F32)) * pa + _sigmoid(mb_ref[0].astype(F32)) * pb
    out = _dot(merged.astype(BF16), wo_ref[...])
    gate = mod_ref[0, :, 2 * D_MODEL:3 * D_MODEL]
    xn = x_ref[0] + gate * out
    if final_norm:
        ms = jnp.mean(xn * xn, axis=-1, keepdims=True)
        xn = xn * lax.rsqrt(ms + RMS_EPS) * fg_ref[...]
    o_ref[0] = xn


def _merge(ya, yb, proj, x, mod, wa, wb, wo, final_g, final_norm, layer, tm=1024):
    B, S, D = x.shape

    def rows(width, c):
        return pl.BlockSpec((1, tm, width), lambda b, i: (b, i, c))

    def full(arr):
        return pl.BlockSpec(arr.shape, lambda b, i: (0,) * arr.ndim)

    return pl.pallas_call(
        functools.partial(_merge_kernel, final_norm=final_norm),
        grid=(B, S // tm),
        in_specs=[rows(R_WIDTH, 0), rows(A_OUT_WIDTH, 0),
                  rows(D, COL_MA // D), rows(D, COL_MB // D), rows(D, 0),
                  pl.BlockSpec((1, 1, 3 * D), lambda b, i: (layer * MOD_ROWS + b, 0, 0)),
                  _layer_block(wa, layer), _layer_block(wb, layer), _layer_block(wo, layer), full(final_g)],
        out_specs=rows(D, 0),
        out_shape=jax.ShapeDtypeStruct((B, S, D), F32),
        compiler_params=_cparams(("parallel", "parallel")),
        name="merge",
    )(ya, yb, proj, proj, x, mod, wa, wb, wo, final_g)


def _segment_ones():
    idx = np.arange(MXU_DIM)
    return jnp.asarray(idx[:, None] // HEAD_DIM == idx[None, :] // HEAD_DIM, BF16)


def kernel(x, c, norm_g, ada_w, ada_b, w_in, rwkv_mu_rkv, rwkv_mu_wa, rwkv_w0, rwkv_w1, rwkv_w2, rwkv_a0, rwkv_a1, rwkv_a2, rwkv_k_k, rwkv_k_a, rwkv_r_k, rwkv_ln_g, rwkv_ln_b, rwkv_mu_v, rwkv_v0, rwkv_v1, rwkv_v2, w_branch_a, w_branch_b, w_out, rel_bias, final_g):
    B, S, D = x.shape
    assert D == D_MODEL and S % ATT_TILE == 0 and w_in.shape[2] == PROJ_WIDTH
    ones_bd = _segment_ones()
    mod = _adaln_mod(c, ada_w, ada_b).reshape(DEPTH * MOD_ROWS, 1, 3 * D)
    bias = _rel_bias(rel_bias).reshape(N_GROUPS, HEADS_PER_GROUP // 2, 2, BLK, 2 * BLK)

    def cols(start, width):
        return w_in[:, :, start:start + width]

    def group_cols(g):
        return [cols(W_AQ + A_OUT_WIDTH * g, A_OUT_WIDTH) * (LOG2E / math.sqrt(HEAD_DIM)),
                cols(W_AK + A_OUT_WIDTH * g, A_OUT_WIDTH), cols(W_AV + A_OUT_WIDTH * g, A_OUT_WIDTH)]

    w_main = jnp.concatenate(
        [cols(W_R, 4 * R_WIDTH), cols(W_MA, 2 * D_MODEL), cols(W_GB, A_OUT_WIDTH)] + group_cols(0),
        axis=2).astype(BF16)
    w_groups = [jnp.concatenate(group_cols(g), axis=2).astype(BF16) for g in range(1, N_GROUPS)]
    zeros_rows = jnp.zeros((DEPTH, D), F32)
    pvec = jnp.stack([rwkv_mu_rkv[:, 0], rwkv_mu_rkv[:, 1], rwkv_mu_rkv[:, 2], rwkv_w0, rwkv_a0, rwkv_k_k,
                      rwkv_k_a, _first_layer_blank(rwkv_v0)] + [zeros_rows] * (PV_ROWS - 8), axis=1)
    lora = _pack_lora([(rwkv_mu_wa[:, 0], rwkv_w1, rwkv_w2), (rwkv_mu_wa[:, 1], rwkv_a1, rwkv_a2),
                       (_first_layer_blank(rwkv_mu_v), _first_layer_blank(rwkv_v1),
                        _first_layer_blank(rwkv_v2))])
    vec = jnp.stack([rwkv_r_k.reshape(DEPTH, -1), rwkv_ln_g, rwkv_ln_b] + [zeros_rows] * 5, axis=1)
    norm_g3 = norm_g.reshape(DEPTH, 1, D)
    wa, wb, wo = w_branch_a.astype(BF16), w_branch_b.astype(BF16), w_out.astype(BF16)

    v_first = None
    for i in range(DEPTH):
        proj, h = _norm_proj(x, mod, norm_g3, w_main, i)
        (r, cum, k, v, a, b), groups = _rwkv_prep(h, proj, v_first, pvec, lora if i > 0 else lora[:3],
                                                  ones_bd, w_groups, i)
        if i == 0:
            v_first = v
        y_a, y_b = _mixers(r, cum, k, v, a, b, proj, groups, vec, ones_bd, bias, i)
        x = _merge(y_a, y_b, proj, x, mod, wa, wb, wo, final_g.reshape(1, D),
                   final_norm=(i == DEPTH - 1), layer=i)
    return x
```

```python
import functools
import math

import numpy as np
import jax
import jax.numpy as jnp
from jax import lax
from jax.experimental import pallas as pl
from jax.experimental.pallas import tpu as pltpu

F32 = jnp.float32
BF16 = jnp.bfloat16

D_MODEL = 1024
DEPTH = 2
HEAD_DIM = 64
R_WIDTH = 1024
N_GROUPS = 3
HEADS_PER_GROUP = 8
DILATIONS = (1, 4, 16)
BLK = 128
A_QK_WIDTH = 1536
A_OUT_WIDTH = 512
NUM_BUCKETS = 32
MAX_DISTANCE = 2048
PROJ_WIDTH = 4 * R_WIDTH + 3 * A_QK_WIDTH + A_OUT_WIDTH + 2 * D_MODEL
RMS_EPS = 1e-6
GN_EPS = 64e-5
NEG_INF = -1e30
LOG2E = math.log2(math.e)

LANES = 128
MXU_DIM = 256
HEADS_PER_TILE = MXU_DIM // HEAD_DIM
N_COLGROUPS = R_WIDTH // MXU_DIM
CHUNK = 64
SCAN_CHUNKS = 2

W_R, W_K, W_V, W_GA = 0, 1024, 2048, 3072
W_AQ, W_AK, W_AV = 4096, 5632, 7168
W_GB, W_MA, W_MB = 8704, 9216, 10240
COL_R, COL_K, COL_V, COL_GA, COL_MA, COL_MB, COL_GB, COL_A0 = 0, 1024, 2048, 3072, 4096, 5120, 6144, 6656
MAIN_WIDTH = 8192

VMEM_LIMIT = 56 * 1024 * 1024
MOD_ROWS = 8


def _cparams(sem, vmem_limit=VMEM_LIMIT):
    return pltpu.CompilerParams(dimension_semantics=sem, vmem_limit_bytes=vmem_limit)


def _sigmoid(z):
    return 1.0 / (1.0 + jnp.exp(-z))


def _silu(z):
    return z * _sigmoid(z)


def _dot(a, b):
    return jnp.dot(a, b, preferred_element_type=F32)


def _dot_nt(a, b):
    return lax.dot_general(a, b, (((1,), (1,)), ((), ())), preferred_element_type=F32)


def _split2(x):
    hi = x.astype(BF16)
    lo = (x - hi.astype(F32)).astype(BF16)
    return hi, lo


def _segsum64(x, ones_bd):
    n = x.shape[0]
    xs = jnp.concatenate([x[:, MXU_DIM * g:MXU_DIM * (g + 1)] for g in range(N_COLGROUPS)], axis=0)
    s = _dot(xs.astype(BF16), ones_bd)
    return jnp.concatenate([s[n * g:n * (g + 1)] for g in range(N_COLGROUPS)], axis=1)


def _mod_kernel(c_ref, w_ref, b_ref, o_ref):
    s = _silu(c_ref[...])
    o_ref[0] = _dot(s.astype(BF16), w_ref[0].astype(BF16)) + b_ref[0]


def _adaln_mod(c, ada_w, ada_b):
    L = ada_w.shape[0]
    B = c.shape[0]
    c_rows = jnp.pad(c, ((0, MOD_ROWS - B), (0, 0)))
    nj = 3
    return pl.pallas_call(
        _mod_kernel,
        grid=(L, nj),
        in_specs=[pl.BlockSpec((MOD_ROWS, D_MODEL), lambda l, j: (0, 0)),
                  pl.BlockSpec((1, D_MODEL, D_MODEL), lambda l, j: (l, 0, j)),
                  pl.BlockSpec((1, 1, D_MODEL), lambda l, j: (l, 0, j))],
        out_specs=pl.BlockSpec((1, MOD_ROWS, D_MODEL), lambda l, j: (l, 0, j)),
        out_shape=jax.ShapeDtypeStruct((L, MOD_ROWS, 3 * D_MODEL), F32),
        compiler_params=_cparams(("parallel", "parallel")),
        name="adaln_mod",
    )(c_rows, ada_w, ada_b.reshape(L, 1, 3 * D_MODEL))


def _t5_bucket(dist):
    max_exact = NUM_BUCKETS // 2
    safe = np.maximum(dist, 1).astype(np.float32)
    large = max_exact + (np.log(safe / max_exact) / math.log(MAX_DISTANCE / max_exact)
                         * (NUM_BUCKETS - max_exact)).astype(np.int32)
    large = np.minimum(large, NUM_BUCKETS - 1)
    return np.where(dist < max_exact, dist, large).astype(np.int32)


def _bias_kernel(tab_ref, bucket_ref, o_ref):
    g = pl.program_id(0)
    bk = bucket_ref[0]
    for hh in range(HEADS_PER_GROUP):
        h = g * HEADS_PER_GROUP + hh
        acc = jnp.zeros(bk.shape, F32)
        for b in range(NUM_BUCKETS):
            acc = jnp.where(bk == b, tab_ref[h * NUM_BUCKETS + b], acc)
        o_ref[hh] = jnp.where(bk >= 0, acc * LOG2E, NEG_INF)


def _rel_bias(rel_bias):
    n_heads = rel_bias.shape[1]
    qi = np.arange(BLK)[:, None]
    ki = np.arange(2 * BLK)[None, :]
    delta = qi + BLK - ki
    band = (delta >= 0) & (delta <= BLK)
    buckets = np.stack([np.where(band, _t5_bucket(np.maximum(delta, 0) * d), -1)
                        for d in DILATIONS]).astype(np.int32)
    table = rel_bias.T.reshape(-1)
    return pl.pallas_call(
        _bias_kernel,
        grid=(n_heads // HEADS_PER_GROUP,),
        in_specs=[pl.BlockSpec(memory_space=pltpu.SMEM),
                  pl.BlockSpec((1, BLK, 2 * BLK), lambda g: (g, 0, 0))],
        out_specs=pl.BlockSpec((HEADS_PER_GROUP, BLK, 2 * BLK), lambda g: (g, 0, 0)),
        out_shape=jax.ShapeDtypeStruct((n_heads, BLK, 2 * BLK), F32),
        compiler_params=_cparams(("parallel",)),
        name="rel_bias",
    )(table, jnp.asarray(buckets))


def _proj_kernel(x_ref, mod_ref, g_ref, w_ref, proj_ref, h_ref):
    @pl.when(pl.program_id(2) == 0)
    def _():
        x = x_ref[0]
        ms = jnp.mean(x * x, axis=-1, keepdims=True)
        y = x * lax.rsqrt(ms + RMS_EPS) * g_ref[...]
        shift = mod_ref[0, :, 0:D_MODEL]
        scale = mod_ref[0, :, D_MODEL:2 * D_MODEL]
        h_ref[0] = (y * (1.0 + scale) + shift).astype(BF16)

    proj_ref[0] = _dot(h_ref[0], w_ref[...]).astype(BF16)


def _layer_block(arr, layer):
    tail = (0,) * (arr.ndim - 1)
    return pl.BlockSpec((None,) + arr.shape[1:], lambda *_: (layer,) + tail)


def _norm_proj(x, mod, norm_g, w_main, layer, tm=1024, tn=4096):
    B, S, D = x.shape
    N = w_main.shape[2]
    return pl.pallas_call(
        _proj_kernel,
        grid=(B, S // tm, N // tn),
        in_specs=[pl.BlockSpec((1, tm, D), lambda b, i, j: (b, i, 0)),
                  pl.BlockSpec((1, 1, 3 * D), lambda b, i, j: (layer * MOD_ROWS + b, 0, 0)),
                  _layer_block(norm_g, layer),
                  pl.BlockSpec((None, D, tn), lambda b, i, j: (layer, 0, j))],
        out_specs=[pl.BlockSpec((1, tm, tn), lambda b, i, j: (b, i, j)),
                   pl.BlockSpec((1, tm, D), lambda b, i, j: (b, i, 0))],
        out_shape=[jax.ShapeDtypeStruct((B, S, N), BF16),
                   jax.ShapeDtypeStruct((B, S, D), BF16)],
        compiler_params=_cparams(("parallel", "parallel", "arbitrary")),
        name="norm_proj",
    )(x, mod, norm_g, w_main)


GATHER_ROWS = MXU_DIM


def _group_proj_stages(h_ref, perm_ref, w_ref, o_ref, d):
    tm = h_ref.shape[1]
    per_res = GATHER_ROWS // d
    perm = perm_ref[...]
    for ck in range(tm // GATHER_ROWS):
        r0 = ck * GATHER_ROWS
        hp = _dot(perm, h_ref[0, r0:r0 + GATHER_ROWS, :]).astype(BF16)
        yield
        res = _dot(hp, w_ref[...]).astype(BF16)
        for r in range(d):
            o_ref[0, r, ck * per_res:(ck + 1) * per_res, :] = res[r * per_res:(r + 1) * per_res]
        yield


def _gather_perm(d):
    dst = np.arange(GATHER_ROWS)
    src = (dst % (GATHER_ROWS // d)) * d + dst // (GATHER_ROWS // d)
    return jnp.asarray(src[:, None] == np.arange(GATHER_ROWS)[None, :], BF16)


PV_MU_R, PV_MU_K, PV_MU_V, PV_W0, PV_A0, PV_KK, PV_KA, PV_V0 = range(8)
LORA_LANES = 256


def _pack_lora(paths):
    n_layers, d_model, _ = paths[0][1].shape
    used = sum(down.shape[2] for _, down, _ in paths)
    pad = jnp.zeros((n_layers, d_model, LORA_LANES - used), F32)
    keep = jnp.concatenate([(1.0 - mu)[:, :, None] * down for mu, down, _ in paths] + [pad], axis=2)
    shifted = jnp.concatenate([mu[:, :, None] * down for mu, down, _ in paths] + [pad], axis=2)
    ups, lane = [], 0
    for _, down, up in paths:
        rank = down.shape[2]
        ups.append(jnp.pad(up, ((0, 0), (lane, LORA_LANES - lane - rank), (0, 0))).astype(BF16))
        lane += rank
    return [jnp.concatenate([keep, shifted], axis=2).astype(BF16)] + ups


def _first_layer_blank(arr):
    return jnp.concatenate([jnp.zeros((1,) + arr.shape[1:], arr.dtype), arr], axis=0)
PV_ROWS = 16
PREV_ROWS = 16


def _shift_rows(t, prev_last):
    rolled = pltpu.roll(t, 1, axis=0)
    row = lax.broadcasted_iota(jnp.int32, t.shape, 0)
    return jnp.where(row == 0, prev_last, rolled)


def _rprep_stages(refs, has_vres):
    if has_vres:
        (h_ref, hp_ref, pr_ref, prp_ref, pk_ref, pkp_ref, pvv_ref, pvp_ref, vf_ref, pvec_ref,
         wd_ref, uw_ref, ua_ref, uv_ref, ones_ref, tril_ref,
         r_out, cum_out, k_out, v_out, a_out, b_out) = refs
    else:
        (h_ref, hp_ref, pr_ref, prp_ref, pk_ref, pkp_ref, pvv_ref, pvp_ref, pvec_ref,
         wd_ref, uw_ref, ua_ref, ones_ref, tril_ref,
         r_out, cum_out, k_out, v_out, a_out, b_out) = refs

    not_first = (pl.program_id(1) > 0).astype(F32)

    def prm(i):
        return pvec_ref[i:i + 1, :]

    def lerp_shift(cur_ref, prev_ref, mu):
        t = cur_ref[0].astype(F32)
        last = prev_ref[0, PREV_ROWS - 1:PREV_ROWS, :].astype(F32)
        return t + (_shift_rows(t, last * not_first) - t) * mu

    r = lerp_shift(pr_ref, prp_ref, prm(PV_MU_R))
    k = lerp_shift(pk_ref, pkp_ref, prm(PV_MU_K))
    v = lerp_shift(pvv_ref, pvp_ref, prm(PV_MU_V))
    yield

    wd = wd_ref[...]
    z2 = _dot(h_ref[0], wd)
    z_prev = _dot(hp_ref[0], wd[:, LORA_LANES:])[PREV_ROWS - 1:PREV_ROWS, :]
    z = z2[:, 0:LORA_LANES] + _shift_rows(z2[:, LORA_LANES:], z_prev * not_first)
    zb = z.astype(BF16)
    yield

    zw = prm(PV_W0) + _dot(jnp.tanh(z).astype(BF16), uw_ref[...])
    w = jnp.minimum(zw, 0.0) - jnp.log(1.0 + jnp.exp(-jnp.abs(zw))) - 0.5
    lw = -jnp.exp(w)
    yield
    hi, lo = _split2(lw)
    tril = tril_ref[...]
    for i in range(lw.shape[0] // MXU_DIM):
        blk = slice(MXU_DIM * i, MXU_DIM * (i + 1))
        cum_out[0, blk, :] = _dot(tril, hi[blk]) + _dot(tril, lo[blk])
    yield
    a = _sigmoid(prm(PV_A0) + _dot(zb, ua_ref[...]))
    if has_vres:
        mix = _sigmoid(prm(PV_V0) + _dot(zb, uv_ref[...]))
        v = v + (vf_ref[0].astype(F32) - v) * mix
    yield

    kk = k * prm(PV_KK)
    ss = _segsum64(kk * kk, ones_ref[...])
    kk = kk * lax.rsqrt(jnp.maximum(ss, 1e-24))
    yield
    r_out[0] = r.astype(BF16)
    k_out[0] = (k * (1.0 + (a - 1.0) * prm(PV_KA))).astype(BF16)
    v_out[0] = v.astype(BF16)
    a_out[0] = (-kk).astype(BF16)
    b_out[0] = (kk * a).astype(BF16)
    yield


N_GROUP_PROJ = N_GROUPS - 1


def _prep_groups_kernel(*refs, has_vres):
    n_in = len(refs) - 6 - N_GROUP_PROJ - 2 * N_GROUP_PROJ
    prep_in, rest = refs[:n_in], refs[n_in:]
    gp_in, outs = rest[:2 * N_GROUP_PROJ], rest[2 * N_GROUP_PROJ:]
    prep = _rprep_stages(tuple(prep_in) + tuple(outs[:6]), has_vres)
    h_ref = prep_in[0]
    groups = [_group_proj_stages(h_ref, gp_in[2 * n], gp_in[2 * n + 1], outs[6 + n], DILATIONS[n + 1])
              for n in range(N_GROUP_PROJ)]

    def all_groups():
        for gen in groups:
            yield from gen

    for _ in _interleave(prep, all_groups()):
        pass


def _rwkv_prep(h, proj, v_first, pvec, lora, ones_bd, w_groups, layer, tr=512):
    B, S, D = h.shape
    has_vres = v_first is not None
    rpb = tr // PREV_ROWS
    t = np.arange(MXU_DIM)
    tril_bd =jnp.asarray((t[None, :] <= t[:, None]) & (t[None, :] // CHUNK == t[:, None] // CHUNK), BF16)

    def cur(c):
        return pl.BlockSpec((1, tr, R_WIDTH), lambda b, i: (b, i, c))

    def prev(c):
        return pl.BlockSpec((1, PREV_ROWS, R_WIDTH), lambda b, i: (b, jnp.maximum(i * rpb - 1, 0), c))

    def full(arr):
        return pl.BlockSpec(arr.shape, lambda b, i: (0,) * arr.ndim)

    in_specs = [cur(0), prev(0)]
    args = [h, h]
    for c in (COL_R, COL_K, COL_V):
        in_specs += [cur(c // R_WIDTH), prev(c // R_WIDTH)]
        args += [proj, proj]
    if has_vres:
        in_specs.append(cur(0))
        args.append(v_first)
    for per_layer in [pvec] + list(lora):
        in_specs.append(_layer_block(per_layer, layer))
        args.append(per_layer)
    for const in (ones_bd, tril_bd):
        in_specs.append(full(const))
        args.append(const)
    out = [jax.ShapeDtypeStruct((B, S, R_WIDTH), F32 if n == 1 else BF16) for n in range(6)]
    out_specs = [cur(0)] * 6
    for n, w_group in enumerate(w_groups):
        d = DILATIONS[n + 1]
        width = w_group.shape[2]
        in_specs += [full(_gather_perm(d)), _layer_block(w_group, layer)]
        args += [_gather_perm(d), w_group]
        out.append(jax.ShapeDtypeStruct((B, d, S // d, width), BF16))
        out_specs.append(pl.BlockSpec((1, d, tr // d, width), lambda b, i: (b, 0, i, 0)))
    res = pl.pallas_call(
        functools.partial(_prep_groups_kernel, has_vres=has_vres),
        grid=(B, S // tr),
        in_specs=in_specs,
        out_specs=out_specs,
        out_shape=out,
        compiler_params=_cparams(("parallel", "parallel")),
        name="prep_groups",
    )(*args)
    return res[:6], res[6:]


def _interleave(*gens):
    live = list(gens)
    while live:
        for gen in list(live):
            if next(gen, StopIteration) is StopIteration:
                live.remove(gen)
        yield


def _scan_stages(r_ref, cum_ref, k_ref, v_ref, a_ref, b_ref, ga_ref, vec_ref, ones_ref,
                 y_ref, s_ref, *, nb, tt):
    C = CHUNK

    @pl.when(pl.program_id(0) == 0)
    def _():
        s_ref[...] = jnp.zeros(s_ref.shape, F32)

    row = lax.broadcasted_iota(jnp.int32, (C, MXU_DIM), 0)
    lane = lax.broadcasted_iota(jnp.int32, (C, MXU_DIM), 1)
    col = lane & (HEAD_DIM - 1)
    lhead = lane >> 6
    strict = col < row
    incl = col <= row
    eye = (col == row).astype(F32)
    head_masks = [lhead == hh for hh in range(HEADS_PER_TILE)]

    def bdrows(x):
        return jnp.concatenate([jnp.where(m, x, 0.0) for m in head_masks], axis=0).astype(BF16)

    def diag_blocks(full):
        acc = jnp.where(head_masks[0], full[0:C], 0.0)
        for hh in range(1, HEADS_PER_TILE):
            acc = acc + jnp.where(head_masks[hh], full[C * hh:C * (hh + 1)], 0.0)
        return acc

    row_full = lax.broadcasted_iota(jnp.int32, (C, R_WIDTH), 0)
    ones_bd = ones_ref[...]
    r_k = vec_ref[0:1, :]
    ln_g = vec_ref[1:2, :]
    ln_b = vec_ref[2:3, :]

    chains = [(bi, g) for bi in range(nb) for g in range(N_COLGROUPS)]
    insts = [(ck, bi, g) for ck in range(SCAN_CHUNKS) for bi, g in chains]


    def load(gi):
        rows = [slice((gi * SCAN_CHUNKS + ck) * C, (gi * SCAN_CHUNKS + ck + 1) * C) for ck in range(SCAN_CHUNKS)]
        ops = {}
        for ck in range(SCAN_CHUNKS):
            for bi in range(nb):
                cum = cum_ref[bi, rows[ck], :]
                r = r_ref[bi, rows[ck], :].astype(F32)
                k = k_ref[bi, rows[ck], :].astype(F32)
                v = v_ref[bi, rows[ck], :].astype(F32)
                a = a_ref[bi, rows[ck], :].astype(F32)
                b = b_ref[bi, rows[ck], :].astype(F32)
                total = cum[C - 1:C, :]
                p_in = jnp.exp(cum)
                p_inv = jnp.exp(-cum)
                p_rest = jnp.exp(total - cum)
                p_before = jnp.where(row_full == 0, 1.0, pltpu.roll(p_in, 1, axis=0))
                ops[ck, bi] = dict(r=r, k=k, v=v, a_t=a * p_before, r_t=r * p_in, b_t=b * p_inv,
                                   k_t=k * p_inv, bp=b * p_rest, kp=k * p_rest, p_all=jnp.exp(total))
        return dict(rows=rows, ops=ops)

    def part(ctx, name, ck, bi, g):
        return ctx["ops"][ck, bi][name][:, MXU_DIM * g:MXU_DIM * (g + 1)]

    def independent(ctx):
        res = [_dot_nt(jnp.concatenate([part(ctx, "a_t", *i), part(ctx, "r_t", *i)], axis=0).astype(BF16),
                       jnp.concatenate([bdrows(part(ctx, "b_t", *i)), bdrows(part(ctx, "k_t", *i))], axis=0))
               for i in insts]
        yield
        a_ab = [jnp.where(strict, x[0:C, 0:MXU_DIM], 0.0) for x in res]
        a_ak = [jnp.where(strict, x[0:C, MXU_DIM:], 0.0) for x in res]
        ctx["a_rb"] = [jnp.where(incl, x[C:, 0:MXU_DIM], 0.0).astype(BF16) for x in res]
        a_rk = [jnp.where(incl, x[C:, MXU_DIM:], 0.0) for x in res]

        pw = [_dot(x.astype(BF16), bdrows(x)) for x in a_ab]
        tinv = [eye + x for x in a_ab]
        yield
        for _ in range(4):
            both = [_dot(jnp.concatenate([p, t], axis=0).astype(BF16), bdrows(p)) for p, t in zip(pw, tinv)]
            tinv = [t + x[C:] for t, x in zip(tinv, both)]
            pw = [x[0:C] for x in both]
            yield
        tinv = [t + _dot(t.astype(BF16), bdrows(p)) for p, t in zip(pw, tinv)]
        yield
        tax = [_dot(t.astype(BF16), jnp.concatenate([bdrows(part(ctx, "a_t", *i)), bdrows(x)], axis=1))
               for t, x, i in zip(tinv, a_ak, insts)]
        yield
        ctx["from_v"] = [_dot(jnp.concatenate([x[:, MXU_DIM:], ark], axis=0).astype(BF16),
                              bdrows(part(ctx, "v", *i))) for x, ark, i in zip(tax, a_rk, insts)]
        ctx["tax"] = tax
        yield

    def dependent(ctx, carried):
        tax, from_v, a_rb = ctx["tax"], ctx["from_v"], ctx["a_rb"]
        st = carried["st"]
        y = {}
        for ck in range(SCAN_CHUNKS):
            sel = range(ck * len(chains), (ck + 1) * len(chains))
            from_state = [_dot_nt(jnp.concatenate([tax[n][:, 0:MXU_DIM], part(ctx, "r_t", *insts[n])],
                                                  axis=0).astype(BF16), bdrows(s))
                          for n, s in zip(sel, st)]
            yield
            u = [x[0:C] + from_v[n][0:C] for x, n in zip(from_state, sel)]
            for x, n, uu in zip(from_state, sel, u):
                y[insts[n]] = x[C:] + from_v[n][C:] + _dot(a_rb[n], bdrows(uu))
            upd = [_dot(jnp.concatenate([uu, part(ctx, "v", *insts[n])], axis=0).T.astype(BF16),
                        jnp.concatenate([part(ctx, "bp", *insts[n]), part(ctx, "kp", *insts[n])],
                                        axis=0).astype(BF16))
                   for uu, n in zip(u, sel)]
            yield
            st = [s_old * part(ctx, "p_all", *insts[n]) + diag_blocks(x) for s_old, x, n in zip(st, upd, sel)]
        carried["st"] = st

        for ck in range(SCAN_CHUNKS):
            for bi in range(nb):
                p = ctx["ops"][ck, bi]
                yc = jnp.concatenate([y[ck, bi, g] for g in range(N_COLGROUPS)], axis=1)
                mean = _segsum64(yc, ones_bd) * (1.0 / HEAD_DIM)
                yield
                yd = yc - mean
                var = _segsum64(yd * yd, ones_bd) * (1.0 / HEAD_DIM)
                yn = yd * lax.rsqrt(var + GN_EPS) * ln_g + ln_b
                bonus = _segsum64(p["r"] * p["k"] * r_k, ones_bd) * p["v"]
                rows = ctx["rows"][ck]
                y_ref[bi, rows, :] = ((yn + bonus) * _silu(ga_ref[bi, rows, :].astype(F32))).astype(BF16)
                yield

    carried = {"st": [s_ref[bi * N_COLGROUPS + g] for bi, g in chains]}
    n_groups = tt // (C * SCAN_CHUNKS)
    ctx = load(0)
    yield from independent(ctx)
    for gi in range(1, n_groups):
        nxt = load(gi)
        yield from _interleave(dependent(ctx, carried), independent(nxt))
        ctx = nxt
    yield from dependent(ctx, carried)
    for (bi, g), s_new in zip(chains, carried["st"]):
        s_ref[bi * N_COLGROUPS + g] = s_new


ATT_TILE = 2048
ATT_UNROLL = (5, 6, 8)


def _attn_stages(q_refs, k_refs, v_refs, kp_refs, vp_refs, gb_ref, bias_ref, y_ref, o_refs, l_refs,
                 is_first):
    prev_limit = jnp.where(is_first, BLK, 0)
    ki = lax.broadcasted_iota(jnp.int32, (2 * BLK, 2 * BLK), 1)
    head0 = lax.broadcasted_iota(jnp.int32, (BLK, LANES), 1) < HEAD_DIM
    ones_cols = jnp.ones((2 * BLK, LANES), BF16)
    zero = jnp.zeros((BLK, LANES), BF16)

    def process(blocks):
        q2s, kws, vws, bias2s, stores = [], [], [], [], []
        for g, sub, res in blocks:
            d = DILATIONS[g]
            base = sub * (BLK * d) + res
            q = q_refs[g][0, res, sub * BLK:(sub + 1) * BLK, :]
            q2s.append(jnp.concatenate([jnp.where(head0, q, zero), jnp.where(head0, zero, q)], axis=0))
            if sub == 0:
                kw = jnp.concatenate([kp_refs[g][0, res], k_refs[g][0, res, 0:BLK, :]], axis=0)
                vw = jnp.concatenate([vp_refs[g][0, res], v_refs[g][0, res, 0:BLK, :]], axis=0)
            else:
                kw = k_refs[g][0, res, (sub - 1) * BLK:(sub + 1) * BLK, :]
                vw = v_refs[g][0, res, (sub - 1) * BLK:(sub + 1) * BLK, :]
            kws.append(kw)
            vws.append(jnp.concatenate([vw, ones_cols], axis=1))
            bias2s.append(bias_ref[g, 0].reshape(2 * BLK, 2 * BLK))
            stores.append((g, pl.ds(base, BLK) if d == 1 else pl.ds(base, BLK, stride=d)))
        logits = [jnp.where(bias2 > 0.5 * NEG_INF, _dot_nt(q2, kw) + bias2, NEG_INF)
                  for q2, kw, bias2 in zip(q2s, kws, bias2s)]
        logits = [jnp.where(ki < prev_limit, NEG_INF, x) if blk[1] == 0 else x
                  for x, blk in zip(logits, blocks)]
        yield
        ms = [jnp.max(x, axis=-1, keepdims=True) for x in logits]
        ps = [jnp.exp2(x - m).astype(BF16) for x, m in zip(logits, ms)]
        pvs = [_dot(p, vw) for p, vw in zip(ps, vws)]
        yield
        for (g, rows), pv, m in zip(stores, pvs, ms):
            num = jnp.where(head0, pv[0:BLK, 0:LANES], pv[BLK:, 0:LANES])
            den = jnp.where(head0, pv[0:BLK, LANES:], pv[BLK:, LANES:])
            o_refs[g][rows, :] = num / den
            l_refs[g][rows, :] = jnp.where(head0, m[0:BLK], m[BLK:]) + jnp.log2(den)

    for g, d in enumerate(DILATIONS):
        blocks = [(g, sub, res) for sub in range(ATT_TILE // (BLK * d)) for res in range(d)]
        for n in range(0, len(blocks), ATT_UNROLL[g]):
            yield from process(blocks[n:n + ATT_UNROLL[g]])

    l0, l1, l2 = l_refs[0][...], l_refs[1][...], l_refs[2][...]
    m = jnp.maximum(jnp.maximum(l0, l1), l2)
    w0, w1, w2 = jnp.exp2(l0 - m), jnp.exp2(l1 - m), jnp.exp2(l2 - m)
    y = (w0 * o_refs[0][...] + w1 * o_refs[1][...] + w2 * o_refs[2][...]) / (w0 + w1 + w2)
    y_ref[0] = (y * _silu(gb_ref[0].astype(F32))).astype(BF16)
    yield


N_SCAN_REFS = 9
N_ATTN_REFS = 17
MIXERS_VMEM_LIMIT = 62 * 1024 * 1024


def _mixers_kernel(*refs, nb, tt, tiles_per_seq, n_tiles):
    scan_in = refs[:N_SCAN_REFS]
    attn_in = refs[N_SCAN_REFS:N_SCAN_REFS + N_ATTN_REFS]
    ya_ref, yb_ref, s_ref = refs[N_SCAN_REFS + N_ATTN_REFS:N_SCAN_REFS + N_ATTN_REFS + 3]
    scratch = refs[N_SCAN_REFS + N_ATTN_REFS + 3:]
    is_first = ((pl.program_id(0) % n_tiles) % tiles_per_seq) == 0
    scan = _scan_stages(*scan_in, ya_ref, s_ref, nb=nb, tt=tt)
    attn = _attn_stages(attn_in[0:3], attn_in[3:6], attn_in[6:9], attn_in[9:12], attn_in[12:15],
                        attn_in[15], attn_in[16], yb_ref, scratch[0:3], scratch[3:6], is_first)
    for _ in _interleave(scan, attn):
        pass


def _mixers(r, cum, k, v, a, b, main, groups, vec, ones_bd, bias5, layer):
    B, S, W = r.shape
    n_pairs = HEADS_PER_GROUP // 2
    tiles_per_seq = S // ATT_TILE
    n_tiles = B * tiles_per_seq
    n_steps = n_pairs * n_tiles
    tt = S // n_steps
    assert tt * n_steps == S and tt % (CHUNK * SCAN_CHUNKS) == 0

    scan_spec = pl.BlockSpec((B, tt, W), lambda t: (0, t, 0))

    def full(arr):
        return pl.BlockSpec(arr.shape, lambda t: (0,) * arr.ndim)

    scan_specs = [scan_spec] * 6 + [pl.BlockSpec((B, tt, W), lambda t: (0, t, COL_GA // W)),
                                    _layer_block(vec, layer), full(ones_bd)]

    arrays = [main.reshape(B, 1, S, MAIN_WIDTH)] + list(groups)
    col_base = [COL_A0 // LANES, 0, 0]

    def where(t):
        tile = t % n_tiles
        return t // n_tiles, tile // tiles_per_seq, tile % tiles_per_seq

    def cur(g, part):
        d = DILATIONS[g]
        c0 = col_base[g] + part * (A_OUT_WIDTH // LANES)

        def index(t):
            hp, bi, ti = where(t)
            return bi, 0, ti, c0 + hp
        return pl.BlockSpec((1, d, ATT_TILE // d, LANES), index)

    def prev(g, part):
        d = DILATIONS[g]
        c0 = col_base[g] + part * (A_OUT_WIDTH // LANES)
        rb = ATT_TILE // (BLK * d)

        def index(t):
            hp, bi, ti = where(t)
            return bi, 0, jnp.maximum(ti * rb - 1, 0), c0 + hp
        return pl.BlockSpec((1, d, BLK, LANES), index)

    def tile(col0):
        def index(t):
            hp, bi, ti = where(t)
            return bi, ti, col0 // LANES + hp
        return pl.BlockSpec((1, ATT_TILE, LANES), index)

    attn_specs = ([cur(g, 0) for g in range(N_GROUPS)] + [cur(g, 1) for g in range(N_GROUPS)]
                  + [cur(g, 2) for g in range(N_GROUPS)]
                  + [prev(g, 1) for g in range(N_GROUPS)] + [prev(g, 2) for g in range(N_GROUPS)]
                  + [tile(COL_GB),
                     pl.BlockSpec((N_GROUPS, 1, 2, BLK, 2 * BLK), lambda t: (0, t // n_tiles, 0, 0, 0))])
    assert len(scan_specs) == N_SCAN_REFS and len(attn_specs) == N_ATTN_REFS
    return pl.pallas_call(
        functools.partial(_mixers_kernel, nb=B, tt=tt, tiles_per_seq=tiles_per_seq, n_tiles=n_tiles),
        grid=(n_steps,),
        in_specs=scan_specs + attn_specs,
        out_specs=[scan_spec, tile(0)],
        out_shape=[jax.ShapeDtypeStruct((B, S, W), BF16), jax.ShapeDtypeStruct((B, S, A_OUT_WIDTH), BF16)],
        scratch_shapes=([pltpu.VMEM((B * N_COLGROUPS, HEAD_DIM, MXU_DIM), F32)]
                        + [pltpu.VMEM((ATT_TILE, LANES), F32)] * 6),
        compiler_params=_cparams(("arbitrary",), MIXERS_VMEM_LIMIT),
        name="mixers",
    )(r, cum, k, v, a, b, main, vec, ones_bd, *(arrays * 5), main, bias5)


def _merge_kernel(ya_ref, yb_ref, ma_ref, mb_ref, x_ref, mod_ref, wa_ref, wb_ref, wo_ref, fg_ref,
                  o_ref, *, final_norm):
    pa = _dot(ya_ref[0], wa_ref[...])
    pb = _dot(yb_ref[0], wb_ref[...])
    merged = _sigmoid(ma_ref[0].astype(F32)) * pa + _sigmoid(mb_ref[0].astype(F32)) * pb
    out = _dot(merged.astype(BF16), wo_ref[...])
    gate = mod_ref[0, :, 2 * D_MODEL:3 * D_MODEL]
    xn = x_ref[0] + gate * out
    if final_norm:
        ms = jnp.mean(xn * xn, axis=-1, keepdims=True)
        xn = xn * lax.rsqrt(ms + RMS_EPS) * fg_ref[...]
    o_ref[0] = xn


def _merge(ya, yb, proj, x, mod, wa, wb, wo, final_g, final_norm, layer, tm=1024):
    B, S, D = x.shape

    def rows(width, c):
        return pl.BlockSpec((1, tm, width), lambda b, i: (b, i, c))

    def full(arr):
        return pl.BlockSpec(arr.shape, lambda b, i: (0,) * arr.ndim)

    return pl.pallas_call(
        functools.partial(_merge_kernel, final_norm=final_norm),
        grid=(B, S // tm),
        in_specs=[rows(R_WIDTH, 0), rows(A_OUT_WIDTH, 0),
                  rows(D, COL_MA // D), rows(D, COL_MB // D), rows(D, 0),
                  pl.BlockSpec((1, 1, 3 * D), lambda b, i: (layer * MOD_ROWS + b, 0, 0)),
                  _layer_block(wa, layer), _layer_block(wb, layer), _layer_block(wo, layer), full(final_g)],
        out_specs=rows(D, 0),
        out_shape=jax.ShapeDtypeStruct((B, S, D), F32),
        compiler_params=_cparams(("parallel", "parallel")),
        name="merge",
    )(ya, yb, proj, proj, x, mod, wa, wb, wo, final_g)


def _segment_ones():
    idx = np.arange(MXU_DIM)
    return jnp.asarray(idx[:, None] // HEAD_DIM == idx[None, :] // HEAD_DIM, BF16)


def kernel(x, c, norm_g, ada_w, ada_b, w_in, rwkv_mu_rkv, rwkv_mu_wa, rwkv_w0, rwkv_w1, rwkv_w2, rwkv_a0, rwkv_a1, rwkv_a2, rwkv_k_k, rwkv_k_a, rwkv_r_k, rwkv_ln_g, rwkv_ln_b, rwkv_mu_v, rwkv_v0, rwkv_v1, rwkv_v2, w_branch_a, w_branch_b, w_out, rel_bias, final_g):
    B, S, D = x.shape
    assert D == D_MODEL and S % ATT_TILE == 0 and w_in.shape[2] == PROJ_WIDTH
    ones_bd = _segment_ones()
    mod = _adaln_mod(c, ada_w, ada_b).reshape(DEPTH * MOD_ROWS, 1, 3 * D)
    bias = _rel_bias(rel_bias).reshape(N_GROUPS, HEADS_PER_GROUP // 2, 2, BLK, 2 * BLK)

    def cols(start, width):
        return w_in[:, :, start:start + width]

    def group_cols(g):
        return [cols(W_AQ + A_OUT_WIDTH * g, A_OUT_WIDTH) * (LOG2E / math.sqrt(HEAD_DIM)),
                cols(W_AK + A_OUT_WIDTH * g, A_OUT_WIDTH), cols(W_AV + A_OUT_WIDTH * g, A_OUT_WIDTH)]

    w_main = jnp.concatenate(
        [cols(W_R, 4 * R_WIDTH), cols(W_MA, 2 * D_MODEL), cols(W_GB, A_OUT_WIDTH)] + group_cols(0),
        axis=2).astype(BF16)
    w_groups = [jnp.concatenate(group_cols(g), axis=2).astype(BF16) for g in range(1, N_GROUPS)]
    zeros_rows = jnp.zeros((DEPTH, D), F32)
    pvec = jnp.stack([rwkv_mu_rkv[:, 0], rwkv_mu_rkv[:, 1], rwkv_mu_rkv[:, 2], rwkv_w0, rwkv_a0, rwkv_k_k,
                      rwkv_k_a, _first_layer_blank(rwkv_v0)] + [zeros_rows] * (PV_ROWS - 8), axis=1)
    lora = _pack_lora([(rwkv_mu_wa[:, 0], rwkv_w1, rwkv_w2), (rwkv_mu_wa[:, 1], rwkv_a1, rwkv_a2),
                       (_first_layer_blank(rwkv_mu_v), _first_layer_blank(rwkv_v1),
                        _first_layer_blank(rwkv_v2))])
    vec = jnp.stack([rwkv_r_k.reshape(DEPTH, -1), rwkv_ln_g, rwkv_ln_b] + [zeros_rows] * 5, axis=1)
    norm_g3 = norm_g.reshape(DEPTH, 1, D)
    wa, wb, wo = w_branch_a.astype(BF16), w_branch_b.astype(BF16), w_out.astype(BF16)

    v_first = None
    for i in range(DEPTH):
        proj, h = _norm_proj(x, mod, norm_g3, w_main, i)
        (r, cum, k, v, a, b), groups = _rwkv_prep(h, proj, v_first, pvec, lora if i > 0 else lora[:3],
                                                  ones_bd, w_groups, i)
        if i == 0:
            v_first = v
        y_a, y_b = _mixers(r, cum, k, v, a, b, proj, groups, vec, ones_bd, bias, i)
        x = _merge(y_a, y_b, proj, x, mod, wa, wb, wo, final_g.reshape(1, D),
                   final_norm=(i == DEPTH - 1), layer=i)
    return x
```

```python
import functools
import math

import numpy as np
import jax
import jax.numpy as jnp
from jax import lax
from jax.experimental import pallas as pl
from jax.experimental.pallas import tpu as pltpu

F32 = jnp.float32
BF16 = jnp.bfloat16

D_MODEL = 1024
DEPTH = 2
HEAD_DIM = 64
R_WIDTH = 1024
N_GROUPS = 3
HEADS_PER_GROUP = 8
DILATIONS = (1, 4, 16)
BLK = 128
A_QK_WIDTH = 1536
A_OUT_WIDTH = 512
NUM_BUCKETS = 32
MAX_DISTANCE = 2048
PROJ_WIDTH = 4 * R_WIDTH + 3 * A_QK_WIDTH + A_OUT_WIDTH + 2 * D_MODEL
RMS_EPS = 1e-6
GN_EPS = 64e-5
NEG_INF = -1e30
LOG2E = math.log2(math.e)

LANES = 128
MXU_DIM = 256
HEADS_PER_TILE = MXU_DIM // HEAD_DIM
N_COLGROUPS = R_WIDTH // MXU_DIM
CHUNK = 64
SCAN_CHUNKS = 2

W_R, W_K, W_V, W_GA = 0, 1024, 2048, 3072
W_AQ, W_AK, W_AV = 4096, 5632, 7168
W_GB, W_MA, W_MB = 8704, 9216, 10240
COL_R, COL_K, COL_V, COL_GA, COL_MA, COL_MB, COL_GB, COL_A0 = 0, 1024, 2048, 3072, 4096, 5120, 6144, 6656
MAIN_WIDTH = 8192

VMEM_LIMIT = 56 * 1024 * 1024
MOD_ROWS = 8


def _cparams(sem, vmem_limit=VMEM_LIMIT):
    return pltpu.CompilerParams(dimension_semantics=sem, vmem_limit_bytes=vmem_limit)


def _sigmoid(z):
    return 1.0 / (1.0 + jnp.exp(-z))


def _silu(z):
    return z * _sigmoid(z)


def _dot(a, b):
    return jnp.dot(a, b, preferred_element_type=F32)


def _dot_nt(a, b):
    return lax.dot_general(a, b, (((1,), (1,)), ((), ())), preferred_element_type=F32)


def _split2(x):
    hi = x.astype(BF16)
    lo = (x - hi.astype(F32)).astype(BF16)
    return hi, lo


def _segsum64(x, ones_bd):
    n = x.shape[0]
    xs = jnp.concatenate([x[:, MXU_DIM * g:MXU_DIM * (g + 1)] for g in range(N_COLGROUPS)], axis=0)
    s = _dot(xs.astype(BF16), ones_bd)
    return jnp.concatenate([s[n * g:n * (g + 1)] for g in range(N_COLGROUPS)], axis=1)


def _mod_kernel(c_ref, w_ref, b_ref, o_ref):
    s = _silu(c_ref[...])
    o_ref[0] = _dot(s.astype(BF16), w_ref[0].astype(BF16)) + b_ref[0]


def _adaln_mod(c, ada_w, ada_b):
    L = ada_w.shape[0]
    B = c.shape[0]
    c_rows = jnp.pad(c, ((0, MOD_ROWS - B), (0, 0)))
    nj = 3
    return pl.pallas_call(
        _mod_kernel,
        grid=(L, nj),
        in_specs=[pl.BlockSpec((MOD_ROWS, D_MODEL), lambda l, j: (0, 0)),
                  pl.BlockSpec((1, D_MODEL, D_MODEL), lambda l, j: (l, 0, j)),
                  pl.BlockSpec((1, 1, D_MODEL), lambda l, j: (l, 0, j))],
        out_specs=pl.BlockSpec((1, MOD_ROWS, D_MODEL), lambda l, j: (l, 0, j)),
        out_shape=jax.ShapeDtypeStruct((L, MOD_ROWS, 3 * D_MODEL), F32),
        compiler_params=_cparams(("parallel", "parallel")),
        name="adaln_mod",
    )(c_rows, ada_w, ada_b.reshape(L, 1, 3 * D_MODEL))


def _t5_bucket(dist):
    max_exact = NUM_BUCKETS // 2
    safe = np.maximum(dist, 1).astype(np.float32)
    large = max_exact + (np.log(safe / max_exact) / math.log(MAX_DISTANCE / max_exact)
                         * (NUM_BUCKETS - max_exact)).astype(np.int32)
    large = np.minimum(large, NUM_BUCKETS - 1)
    return np.where(dist < max_exact, dist, large).astype(np.int32)


def _bias_kernel(tab_ref, bucket_ref, o_ref):
    g = pl.program_id(0)
    bk = bucket_ref[0]
    for hh in range(HEADS_PER_GROUP):
        h = g * HEADS_PER_GROUP + hh
        acc = jnp.zeros(bk.shape, F32)
        for b in range(NUM_BUCKETS):
            acc = jnp.where(bk == b, tab_ref[h * NUM_BUCKETS + b], acc)
        o_ref[hh] = jnp.where(bk >= 0, acc * LOG2E, NEG_INF)


def _rel_bias(rel_bias):
    n_heads = rel_bias.shape[1]
    qi = np.arange(BLK)[:, None]
    ki = np.arange(2 * BLK)[None, :]
    delta = qi + BLK - ki
    band = (delta >= 0) & (delta <= BLK)
    buckets = np.stack([np.where(band, _t5_bucket(np.maximum(delta, 0) * d), -1)
                        for d in DILATIONS]).astype(np.int32)
    table = rel_bias.T.reshape(-1)
    return pl.pallas_call(
        _bias_kernel,
        grid=(n_heads // HEADS_PER_GROUP,),
        in_specs=[pl.BlockSpec(memory_space=pltpu.SMEM),
                  pl.BlockSpec((1, BLK, 2 * BLK), lambda g: (g, 0, 0))],
        out_specs=pl.BlockSpec((HEADS_PER_GROUP, BLK, 2 * BLK), lambda g: (g, 0, 0)),
        out_shape=jax.ShapeDtypeStruct((n_heads, BLK, 2 * BLK), F32),
        compiler_params=_cparams(("parallel",)),
        name="rel_bias",
    )(table, jnp.asarray(buckets))


def _proj_kernel(x_ref, mod_ref, g_ref, w_ref, proj_ref, h_ref):
    @pl.when(pl.program_id(2) == 0)
    def _():
        x = x_ref[0]
        ms = jnp.mean(x * x, axis=-1, keepdims=True)
        y = x * lax.rsqrt(ms + RMS_EPS) * g_ref[...]
        shift = mod_ref[0, :, 0:D_MODEL]
        scale = mod_ref[0, :, D_MODEL:2 * D_MODEL]
        h_ref[0] = (y * (1.0 + scale) + shift).astype(BF16)

    proj_ref[0] = _dot(h_ref[0], w_ref[...]).astype(BF16)


def _layer_block(arr, layer):
    tail = (0,) * (arr.ndim - 1)
    return pl.BlockSpec((None,) + arr.shape[1:], lambda *_: (layer,) + tail)


def _norm_proj(x, mod, norm_g, w_main, layer, tm=1024, tn=4096):
    B, S, D = x.shape
    N = w_main.shape[2]
    return pl.pallas_call(
        _proj_kernel,
        grid=(B, S // tm, N // tn),
        in_specs=[pl.BlockSpec((1, tm, D), lambda b, i, j: (b, i, 0)),
                  pl.BlockSpec((1, 1, 3 * D), lambda b, i, j: (layer * MOD_ROWS + b, 0, 0)),
                  _layer_block(norm_g, layer),
                  pl.BlockSpec((None, D, tn), lambda b, i, j: (layer, 0, j))],
        out_specs=[pl.BlockSpec((1, tm, tn), lambda b, i, j: (b, i, j)),
                   pl.BlockSpec((1, tm, D), lambda b, i, j: (b, i, 0))],
        out_shape=[jax.ShapeDtypeStruct((B, S, N), BF16),
                   jax.ShapeDtypeStruct((B, S, D), BF16)],
        compiler_params=_cparams(("parallel", "parallel", "arbitrary")),
        name="norm_proj",
    )(x, mod, norm_g, w_main)


GATHER_ROWS = MXU_DIM


def _group_proj_stages(h_ref, perm_ref, w_ref, o_ref, d):
    tm = h_ref.shape[1]
    per_res = GATHER_ROWS // d
    perm = perm_ref[...]
    for ck in range(tm // GATHER_ROWS):
        r0 = ck * GATHER_ROWS
        hp = _dot(perm, h_ref[0, r0:r0 + GATHER_ROWS, :]).astype(BF16)
        yield
        res = _dot(hp, w_ref[...]).astype(BF16)
        for r in range(d):
            o_ref[0, r, ck * per_res:(ck + 1) * per_res, :] = res[r * per_res:(r + 1) * per_res]
        yield


def _gather_perm(d):
    dst = np.arange(GATHER_ROWS)
    src = (dst % (GATHER_ROWS // d)) * d + dst // (GATHER_ROWS // d)
    return jnp.asarray(src[:, None] == np.arange(GATHER_ROWS)[None, :], BF16)


PV_MU_R, PV_MU_K, PV_MU_V, PV_W0, PV_A0, PV_KK, PV_KA, PV_V0 = range(8)
LORA_LANES = 256


def _pack_lora(paths):
    n_layers, d_model, _ = paths[0][1].shape
    used = sum(down.shape[2] for _, down, _ in paths)
    pad = jnp.zeros((n_layers, d_model, LORA_LANES - used), F32)
    keep = jnp.concatenate([(1.0 - mu)[:, :, None] * down for mu, down, _ in paths] + [pad], axis=2)
    shifted = jnp.concatenate([mu[:, :, None] * down for mu, down, _ in paths] + [pad], axis=2)
    ups, lane = [], 0
    for _, down, up in paths:
        rank = down.shape[2]
        ups.append(jnp.pad(up, ((0, 0), (lane, LORA_LANES - lane - rank), (0, 0))).astype(BF16))
        lane += rank
    return [jnp.concatenate([keep, shifted], axis=2).astype(BF16)] + ups


def _first_layer_blank(arr):
    return jnp.concatenate([jnp.zeros((1,) + arr.shape[1:], arr.dtype), arr], axis=0)
PV_ROWS = 16
PREV_ROWS = 16


def _shift_rows(t, prev_last):
    rolled = pltpu.roll(t, 1, axis=0)
    row = lax.broadcasted_iota(jnp.int32, t.shape, 0)
    return jnp.where(row == 0, prev_last, rolled)


def _rprep_stages(refs, has_vres):
    if has_vres:
        (h_ref, hp_ref, pr_ref, prp_ref, pk_ref, pkp_ref, pvv_ref, pvp_ref, vf_ref, pvec_ref,
         wd_ref, uw_ref, ua_ref, uv_ref, ones_ref, tril_ref,
         r_out, cum_out, k_out, v_out, a_out, b_out) = refs
    else:
        (h_ref, hp_ref, pr_ref, prp_ref, pk_ref, pkp_ref, pvv_ref, pvp_ref, pvec_ref,
         wd_ref, uw_ref, ua_ref, ones_ref, tril_ref,
         r_out, cum_out, k_out, v_out, a_out, b_out) = refs

    not_first = (pl.program_id(1) > 0).astype(F32)

    def prm(i):
        return pvec_ref[i:i + 1, :]

    def lerp_shift(cur_ref, prev_ref, mu):
        t = cur_ref[0].astype(F32)
        last = prev_ref[0, PREV_ROWS - 1:PREV_ROWS, :].astype(F32)
        return t + (_shift_rows(t, last * not_first) - t) * mu

    r = lerp_shift(pr_ref, prp_ref, prm(PV_MU_R))
    k = lerp_shift(pk_ref, pkp_ref, prm(PV_MU_K))
    v = lerp_shift(pvv_ref, pvp_ref, prm(PV_MU_V))
    yield

    wd = wd_ref[...]
    z2 = _dot(h_ref[0], wd)
    z_prev = _dot(hp_ref[0], wd[:, LORA_LANES:])[PREV_ROWS - 1:PREV_ROWS, :]
    z = z2[:, 0:LORA_LANES] + _shift_rows(z2[:, LORA_LANES:], z_prev * not_first)
    zb = z.astype(BF16)
    yield

    zw = prm(PV_W0) + _dot(jnp.tanh(z).astype(BF16), uw_ref[...])
    w = jnp.minimum(zw, 0.0) - jnp.log(1.0 + jnp.exp(-jnp.abs(zw))) - 0.5
    lw = -jnp.exp(w)
    yield
    hi, lo = _split2(lw)
    tril = tril_ref[...]
    for i in range(lw.shape[0] // MXU_DIM):
        blk = slice(MXU_DIM * i, MXU_DIM * (i + 1))
        cum_out[0, blk, :] = _dot(tril, hi[blk]) + _dot(tril, lo[blk])
    yield
    a = _sigmoid(prm(PV_A0) + _dot(zb, ua_ref[...]))
    if has_vres:
        mix = _sigmoid(prm(PV_V0) + _dot(zb, uv_ref[...]))
        v = v + (vf_ref[0].astype(F32) - v) * mix
    yield

    kk = k * prm(PV_KK)
    ss = _segsum64(kk * kk, ones_ref[...])
    kk = kk * lax.rsqrt(jnp.maximum(ss, 1e-24))
    yield
    r_out[0] = r.astype(BF16)
    k_out[0] = (k * (1.0 + (a - 1.0) * prm(PV_KA))).astype(BF16)
    v_out[0] = v.astype(BF16)
    a_out[0] = (-kk).astype(BF16)
    b_out[0] = (kk * a).astype(BF16)
    yield


N_GROUP_PROJ = N_GROUPS - 1


def _prep_groups_kernel(*refs, has_vres):
    n_in = len(refs) - 6 - N_GROUP_PROJ - 2 * N_GROUP_PROJ
    prep_in, rest = refs[:n_in], refs[n_in:]
    gp_in, outs = rest[:2 * N_GROUP_PROJ], rest[2 * N_GROUP_PROJ:]
    prep = _rprep_stages(tuple(prep_in) + tuple(outs[:6]), has_vres)
    h_ref = prep_in[0]
    groups = [_group_proj_stages(h_ref, gp_in[2 * n], gp_in[2 * n + 1], outs[6 + n], DILATIONS[n + 1])
              for n in range(N_GROUP_PROJ)]

    def all_groups():
        for gen in groups:
            yield from gen

    for _ in _interleave(prep, all_groups()):
        pass


def _rwkv_prep(h, proj, v_first, pvec, lora, ones_bd, w_groups, layer, tr=512):
    B, S, D = h.shape
    has_vres = v_first is not None
    rpb = tr // PREV_ROWS
    t = np.arange(MXU_DIM)
    tril_bd = jnp.asarray((t[None, :] <= t[:, None]) & (t[None, :] // CHUNK == t[:, None] // CHUNK), BF16)

    def cur(c):
        return pl.BlockSpec((1, tr, R_WIDTH), lambda b, i: (b, i, c))

    def prev(c):
        return pl.BlockSpec((1, PREV_ROWS, R_WIDTH), lambda b, i: (b, jnp.maximum(i * rpb - 1, 0), c))

    def full(arr):
        return pl.BlockSpec(arr.shape, lambda b, i: (0,) * arr.ndim)

    in_specs = [cur(0), prev(0)]
    args = [h, h]
    for c in (COL_R, COL_K, COL_V):
        in_specs += [cur(c // R_WIDTH), prev(c // R_WIDTH)]
        args += [proj, proj]
    if has_vres:
        in_specs.append(cur(0))
        args.append(v_first)
    for per_layer in [pvec] + list(lora):
        in_specs.append(_layer_block(per_layer, layer))
        args.append(per_layer)
    for const in (ones_bd, tril_bd):
        in_specs.append(full(const))
        args.append(const)
    out = [jax.ShapeDtypeStruct((B, S, R_WIDTH), F32 if n == 1 else BF16) for n in range(6)]
    out_specs = [cur(0)] * 6
    for n, w_group in enumerate(w_groups):
        d = DILATIONS[n + 1]
        width = w_group.shape[2]
        in_specs += [full(_gather_perm(d)), _layer_block(w_group, layer)]
        args += [_gather_perm(d), w_group]
        out.append(jax.ShapeDtypeStruct((B, d, S // d, width), BF16))
        out_specs.append(pl.BlockSpec((1, d, tr // d, width), lambda b, i: (b, 0, i, 0)))
    res = pl.pallas_call(
        functools.partial(_prep_groups_kernel, has_vres=has_vres),
        grid=(B, S // tr),
        in_specs=in_specs,
        out_specs=out_specs,
        out_shape=out,
        compiler_params=_cparams(("parallel", "parallel")),
        name="prep_groups",
    )(*args)
    return res[:6], res[6:]


def _interleave(*gens):
    live = list(gens)
    while live:
        for gen in list(live):
            if next(gen, StopIteration) is StopIteration:
                live.remove(gen)
        yield


def _scan_stages(r_ref, cum_ref, k_ref, v_ref, a_ref, b_ref, ga_ref, vec_ref, ones_ref,
                 y_ref, s_ref, *, nb, tt):
    C = CHUNK

    @pl.when(pl.program_id(0) == 0)
    def _():
        s_ref[...] = jnp.zeros(s_ref.shape, F32)

    row = lax.broadcasted_iota(jnp.int32, (C, MXU_DIM), 0)
    lane = lax.broadcasted_iota(jnp.int32, (C, MXU_DIM), 1)
    col = lane & (HEAD_DIM - 1)
    lhead = lane >> 6
    strict = col < row
    incl = col <= row
    eye = (col == row).astype(F32)
    head_masks = [lhead == hh for hh in range(HEADS_PER_TILE)]

    def bdrows(x):
        return jnp.concatenate([jnp.where(m, x, 0.0) for m in head_masks], axis=0).astype(BF16)

    def diag_blocks(full):
        acc = jnp.where(head_masks[0], full[0:C], 0.0)
        for hh in range(1, HEADS_PER_TILE):
            acc = acc + jnp.where(head_masks[hh], full[C * hh:C * (hh + 1)], 0.0)
        return acc

    row_full = lax.broadcasted_iota(jnp.int32, (C, R_WIDTH), 0)
    ones_bd = ones_ref[...]
    r_k = vec_ref[0:1, :]
    ln_g = vec_ref[1:2, :]
    ln_b = vec_ref[2:3, :]

    chains = [(bi, g) for bi in range(nb) for g in range(N_COLGROUPS)]
    insts = [(ck, bi, g) for ck in range(SCAN_CHUNKS) for bi, g in chains]


    def load(gi):
        rows = [slice((gi * SCAN_CHUNKS + ck) * C, (gi * SCAN_CHUNKS + ck + 1) * C) for ck in range(SCAN_CHUNKS)]
        ops = {}
        for ck in range(SCAN_CHUNKS):
            for bi in range(nb):
                cum = cum_ref[bi, rows[ck], :]
                r = r_ref[bi, rows[ck], :].astype(F32)
                k = k_ref[bi, rows[ck], :].astype(F32)
                v = v_ref[bi, rows[ck], :].astype(F32)
                a = a_ref[bi, rows[ck], :].astype(F32)
                b = b_ref[bi, rows[ck], :].astype(F32)
                total = cum[C - 1:C, :]
                p_in = jnp.exp(cum)
                p_inv = jnp.exp(-cum)
                p_rest = jnp.exp(total - cum)
                p_before = jnp.where(row_full == 0, 1.0, pltpu.roll(p_in, 1, axis=0))
                ops[ck, bi] = dict(r=r, k=k, v=v, a_t=a * p_before, r_t=r * p_in, b_t=b * p_inv,
                                   k_t=k * p_inv, bp=b * p_rest, kp=k * p_rest, p_all=jnp.exp(total))
        return dict(rows=rows, ops=ops)

    def part(ctx, name, ck, bi, g):
        return ctx["ops"][ck, bi][name][:, MXU_DIM * g:MXU_DIM * (g + 1)]

    def independent(ctx):
        res = [_dot_nt(jnp.concatenate([part(ctx, "a_t", *i), part(ctx, "r_t", *i)], axis=0).astype(BF16),
                       jnp.concatenate([bdrows(part(ctx, "b_t", *i)), bdrows(part(ctx, "k_t", *i))], axis=0))
               for i in insts]
        yield
        a_ab = [jnp.where(strict, x[0:C, 0:MXU_DIM], 0.0) for x in res]
        a_ak = [jnp.where(strict, x[0:C, MXU_DIM:], 0.0) for x in res]
        ctx["a_rb"] = [jnp.where(incl, x[C:, 0:MXU_DIM], 0.0).astype(BF16) for x in res]
        a_rk = [jnp.where(incl, x[C:, MXU_DIM:], 0.0) for x in res]

        pw = [_dot(x.astype(BF16), bdrows(x)) for x in a_ab]
        tinv = [eye + x for x in a_ab]
        yield
        for _ in range(4):
            both = [_dot(jnp.concatenate([p, t], axis=0).astype(BF16), bdrows(p)) for p, t in zip(pw, tinv)]
            tinv = [t + x[C:] for t, x in zip(tinv, both)]
            pw = [x[0:C] for x in both]
            yield
        tinv = [t + _dot(t.astype(BF16), bdrows(p)) for p, t in zip(pw, tinv)]
        yield
        tax = [_dot(t.astype(BF16), jnp.concatenate([bdrows(part(ctx, "a_t", *i)), bdrows(x)], axis=1))
               for t, x, i in zip(tinv, a_ak, insts)]
        yield
        ctx["from_v"] = [_dot(jnp.concatenate([x[:, MXU_DIM:], ark], axis=0).astype(BF16),
                              bdrows(part(ctx, "v", *i))) for x, ark, i in zip(tax, a_rk, insts)]
        ctx["tax"] = tax
        yield

    def dependent(ctx, carried):
        tax, from_v, a_rb = ctx["tax"], ctx["from_v"], ctx["a_rb"]
        st = carried["st"]
        y = {}
        for ck in range(SCAN_CHUNKS):
            sel = range(ck * len(chains), (ck + 1) * len(chains))
            from_state = [_dot_nt(jnp.concatenate([tax[n][:, 0:MXU_DIM], part(ctx, "r_t", *insts[n])],
                                                  axis=0).astype(BF16), bdrows(s))
                          for n, s in zip(sel, st)]
            yield
            u = [x[0:C] + from_v[n][0:C] for x, n in zip(from_state, sel)]
            for x, n, uu in zip(from_state, sel, u):
                y[insts[n]] = x[C:] + from_v[n][C:] + _dot(a_rb[n], bdrows(uu))
            upd = [_dot(jnp.concatenate([uu, part(ctx, "v", *insts[n])], axis=0).T.astype(BF16),
                        jnp.concatenate([part(ctx, "bp", *insts[n]), part(ctx, "kp", *insts[n])],
                                        axis=0).astype(BF16))
                   for uu, n in zip(u, sel)]
            yield
            st = [s_old * part(ctx, "p_all", *insts[n]) + diag_blocks(x) for s_old, x, n in zip(st, upd, sel)]
        carried["st"] = st

        for ck in range(SCAN_CHUNKS):
            for bi in range(nb):
                p = ctx["ops"][ck, bi]
                yc = jnp.concatenate([y[ck, bi, g] for g in range(N_COLGROUPS)], axis=1)
                mean = _segsum64(yc, ones_bd) * (1.0 / HEAD_DIM)
                yield
                yd = yc - mean
                var = _segsum64(yd * yd, ones_bd) * (1.0 / HEAD_DIM)
                yn = yd * lax.rsqrt(var + GN_EPS) * ln_g + ln_b
                bonus = _segsum64(p["r"] * p["k"] * r_k, ones_bd) * p["v"]
                rows = ctx["rows"][ck]
                y_ref[bi, rows, :] = ((yn + bonus) * _silu(ga_ref[bi, rows, :].astype(F32))).astype(BF16)
                yield

    carried = {"st": [s_ref[bi * N_COLGROUPS + g] for bi, g in chains]}
    n_groups = tt // (C * SCAN_CHUNKS)
    ctx = load(0)
    yield from independent(ctx)
    for gi in range(1, n_groups):
        nxt = load(gi)
        yield from _interleave(dependent(ctx, carried), independent(nxt))
        ctx = nxt
    yield from dependent(ctx, carried)
    for (bi, g), s_new in zip(chains, carried["st"]):
        s_ref[bi * N_COLGROUPS + g] = s_new


ATT_TILE = 2048
ATT_UNROLL = (5, 6, 8)


def _attn_stages(q_refs, k_refs, v_refs, kp_refs, vp_refs, gb_ref, bias_ref, y_ref, o_refs, l_refs,
                 is_first):
    prev_limit = jnp.where(is_first, BLK, 0)
    ki = lax.broadcasted_iota(jnp.int32, (2 * BLK, 2 * BLK), 1)
    head0 = lax.broadcasted_iota(jnp.int32, (BLK, LANES), 1) < HEAD_DIM
    ones_cols = jnp.ones((2 * BLK, LANES), BF16)
    zero = jnp.zeros((BLK, LANES), BF16)

    def process(blocks):
        q2s, kws, vws, bias2s, stores = [], [], [], [], []
        for g, sub, res in blocks:
            d = DILATIONS[g]
            base = sub * (BLK * d) + res
            q = q_refs[g][0, res, sub * BLK:(sub + 1) * BLK, :]
            q2s.append(jnp.concatenate([jnp.where(head0, q, zero), jnp.where(head0, zero, q)], axis=0))
            if sub == 0:
                kw = jnp.concatenate([kp_refs[g][0, res], k_refs[g][0, res, 0:BLK, :]], axis=0)
                vw = jnp.concatenate([vp_refs[g][0, res], v_refs[g][0, res, 0:BLK, :]], axis=0)
            else:
                kw = k_refs[g][0, res, (sub - 1) * BLK:(sub + 1) * BLK, :]
                vw = v_refs[g][0, res, (sub - 1) * BLK:(sub + 1) * BLK, :]
            kws.append(kw)
            vws.append(jnp.concatenate([vw, ones_cols], axis=1))
            bias2s.append(bias_ref[g, 0].reshape(2 * BLK, 2 * BLK))
            stores.append((g, pl.ds(base, BLK) if d == 1 else pl.ds(base, BLK, stride=d)))
        logits = [jnp.where(bias2 > 0.5 * NEG_INF, _dot_nt(q2, kw) + bias2, NEG_INF)
                  for q2, kw, bias2 in zip(q2s, kws, bias2s)]
        logits = [jnp.where(ki < prev_limit, NEG_INF, x) if blk[1] == 0 else x
                  for x, blk in zip(logits, blocks)]
        yield
        ms = [jnp.max(x, axis=-1, keepdims=True) for x in logits]
        ps = [jnp.exp2(x - m).astype(BF16) for x, m in zip(logits, ms)]
        pvs = [_dot(p, vw) for p, vw in zip(ps, vws)]
        yield
        for (g, rows), pv, m in zip(stores, pvs, ms):
            num = jnp.where(head0, pv[0:BLK, 0:LANES], pv[BLK:, 0:LANES])
            den = jnp.where(head0, pv[0:BLK, LANES:], pv[BLK:, LANES:])
            o_refs[g][rows, :] = num / den
            l_refs[g][rows, :] = jnp.where(head0, m[0:BLK], m[BLK:]) + jnp.log2(den)

    for g, d in enumerate(DILATIONS):
        blocks = [(g, sub, res) for sub in range(ATT_TILE // (BLK * d)) for res in range(d)]
        for n in range(0, len(blocks), ATT_UNROLL[g]):
            yield from process(blocks[n:n + ATT_UNROLL[g]])

    l0, l1, l2 = l_refs[0][...], l_refs[1][...], l_refs[2][...]
    m = jnp.maximum(jnp.maximum(l0, l1), l2)
    w0, w1, w2 = jnp.exp2(l0 - m), jnp.exp2(l1 - m), jnp.exp2(l2 - m)
    y = (w0 * o_refs[0][...] + w1 * o_refs[1][...] + w2 * o_refs[2][...]) / (w0 + w1 + w2)
    y_ref[0] = (y * _silu(gb_ref[0].astype(F32))).astype(BF16)
    yield


N_SCAN_REFS = 9
N_ATTN_REFS = 17
MIXERS_VMEM_LIMIT = 62 * 1024 * 1024


def _mixers_kernel(*refs, nb, tt, tiles_per_seq, n_tiles):
    scan_in = refs[:N_SCAN_REFS]
    attn_in = refs[N_SCAN_REFS:N_SCAN_REFS + N_ATTN_REFS]
    ya_ref, yb_ref, s_ref = refs[N_SCAN_REFS + N_ATTN_REFS:N_SCAN_REFS + N_ATTN_REFS + 3]
    scratch = refs[N_SCAN_REFS + N_ATTN_REFS + 3:]
    is_first = ((pl.program_id(0) % n_tiles) % tiles_per_seq) == 0
    scan = _scan_stages(*scan_in, ya_ref, s_ref, nb=nb, tt=tt)
    attn = _attn_stages(attn_in[0:3], attn_in[3:6], attn_in[6:9], attn_in[9:12], attn_in[12:15],
                        attn_in[15], attn_in[16], yb_ref, scratch[0:3], scratch[3:6], is_first)
    for _ in _interleave(scan, attn):
        pass


def _mixers(r, cum, k, v, a, b, main, groups, vec, ones_bd, bias5, layer):
    B, S, W = r.shape
    n_pairs = HEADS_PER_GROUP // 2
    tiles_per_seq = S // ATT_TILE
    n_tiles = B * tiles_per_seq
    n_steps = n_pairs * n_tiles
    tt = S // n_steps
    assert tt * n_steps == S and tt % (CHUNK * SCAN_CHUNKS) == 0

    scan_spec = pl.BlockSpec((B, tt, W), lambda t: (0, t, 0))

    def full(arr):
        return pl.BlockSpec(arr.shape, lambda t: (0,) * arr.ndim)

    scan_specs = [scan_spec] * 6 + [pl.BlockSpec((B, tt, W), lambda t: (0, t, COL_GA // W)),
                                    _layer_block(vec, layer), full(ones_bd)]

    arrays = [main.reshape(B, 1, S, MAIN_WIDTH)] + list(groups)
    col_base = [COL_A0 // LANES, 0, 0]

    def where(t):
        tile = t % n_tiles
        return t // n_tiles, tile // tiles_per_seq, tile % tiles_per_seq

    def cur(g, part):
        d = DILATIONS[g]
        c0 = col_base[g] + part * (A_OUT_WIDTH // LANES)

        def index(t):
            hp, bi, ti = where(t)
            return bi, 0, ti, c0 + hp
        return pl.BlockSpec((1, d, ATT_TILE // d, LANES), index)

    def prev(g, part):
        d = DILATIONS[g]
        c0 = col_base[g] + part * (A_OUT_WIDTH // LANES)
        rb = ATT_TILE // (BLK * d)

        def index(t):
            hp, bi, ti = where(t)
            return bi, 0, jnp.maximum(ti * rb - 1, 0), c0 + hp
        return pl.BlockSpec((1, d, BLK, LANES), index)

    def tile(col0):
        def index(t):
            hp, bi, ti = where(t)
            return bi, ti, col0 // LANES + hp
        return pl.BlockSpec((1, ATT_TILE, LANES), index)

    attn_specs = ([cur(g, 0) for g in range(N_GROUPS)] + [cur(g, 1) for g in range(N_GROUPS)]
                  + [cur(g, 2) for g in range(N_GROUPS)]
                  + [prev(g, 1) for g in range(N_GROUPS)] + [prev(g, 2) for g in range(N_GROUPS)]
                  + [tile(COL_GB),
                     pl.BlockSpec((N_GROUPS, 1, 2, BLK, 2 * BLK), lambda t: (0, t // n_tiles, 0, 0, 0))])
    assert len(scan_specs) == N_SCAN_REFS and len(attn_specs) == N_ATTN_REFS
    return pl.pallas_call(
        functools.partial(_mixers_kernel, nb=B, tt=tt, tiles_per_seq=tiles_per_seq, n_tiles=n_tiles),
        grid=(n_steps,),
        in_specs=scan_specs + attn_specs,
        out_specs=[scan_spec, tile(0)],
        out_shape=[jax.ShapeDtypeStruct((B, S, W), BF16), jax.ShapeDtypeStruct((B, S, A_OUT_WIDTH), BF16)],
        scratch_shapes=([pltpu.VMEM((B * N_COLGROUPS, HEAD_DIM, MXU_DIM), F32)]
                        + [pltpu.VMEM((ATT_TILE, LANES), F32)] * 6),
        compiler_params=_cparams(("arbitrary",), MIXERS_VMEM_LIMIT),
        name="mixers",
    )(r, cum, k, v, a, b, main, vec, ones_bd, *(arrays * 5), main, bias5)


def _merge_kernel(ya_ref, yb_ref, ma_ref, mb_ref, x_ref, mod_ref, wa_ref, wb_ref, wo_ref, fg_ref,
                  o_ref, *, final_norm):
    pa = _dot(ya_ref[0], wa_ref[...])
    pb = _dot(yb_ref[0], wb_ref[...])
    merged = _sigmoid(ma_ref[0].astype(F32)) * pa + _sigmoid(mb_ref[0].astype(F32)) * pb
    out = _dot(merged.astype(BF16), wo_ref[...])
    gate = mod_ref[0, :, 2 * D_MODEL:3 * D_MODEL]
    xn = x_ref[0] + gate * out
    if final_norm:
        ms = jnp.mean(xn * xn, axis=-1, keepdims=True)
        xn = xn * lax.rsqrt(ms + RMS_EPS) * fg_ref[...]
    o_ref[0] = xn


def _merge(ya, yb, proj, x, mod, wa, wb, wo, final_g, final_norm, layer, tm=1024):
    B, S, D = x.shape

    def rows(width, c):
        return pl.BlockSpec((1, tm, width), lambda b, i: (b, i, c))

    def full(arr):
        return pl.BlockSpec(arr.shape, lambda b, i: (0,) * arr.ndim)

    return pl.pallas_call(
        functools.partial(_merge_kernel, final_norm=final_norm),
        grid=(B, S // tm),
        in_specs=[rows(R_WIDTH, 0), rows(A_OUT_WIDTH, 0),
                  rows(D, COL_MA // D), rows(D, COL_MB // D), rows(D, 0),
                  pl.BlockSpec((1, 1, 3 * D), lambda b, i: (layer * MOD_ROWS + b, 0, 0)),
                  _layer_block(wa, layer), _layer_block(wb, layer), _layer_block(wo, layer), full(final_g)],
        out_specs=rows(D, 0),
        out_shape=jax.ShapeDtypeStruct((B, S, D), F32),
        compiler_params=_cparams(("parallel", "parallel")),
        name="merge",
    )(ya, yb, proj, proj, x, mod, wa, wb, wo, final_g)


def _segment_ones():
    idx = np.arange(MXU_DIM)
    return jnp.asarray(idx[:, None] // HEAD_DIM == idx[None, :] // HEAD_DIM, BF16)


def kernel(x, c, norm_g, ada_w, ada_b, w_in, rwkv_mu_rkv, rwkv_mu_wa, rwkv_w0, rwkv_w1, rwkv_w2, rwkv_a0, rwkv_a1, rwkv_a2, rwkv_k_k, rwkv_k_a, rwkv_r_k, rwkv_ln_g, rwkv_ln_b, rwkv_mu_v, rwkv_v0, rwkv_v1, rwkv_v2, w_branch_a, w_branch_b, w_out, rel_bias, final_g):
    B, S, D = x.shape
    assert D == D_MODEL and S % ATT_TILE == 0 and w_in.shape[2] == PROJ_WIDTH
    ones_bd = _segment_ones()
    mod = _adaln_mod(c, ada_w, ada_b).reshape(DEPTH * MOD_ROWS, 1, 3 * D)
    bias = _rel_bias(rel_bias).reshape(N_GROUPS, HEADS_PER_GROUP // 2, 2, BLK, 2 * BLK)

    def cols(start, width):
        return w_in[:, :, start:start + width]

    def group_cols(g):
        return [cols(W_AQ + A_OUT_WIDTH * g, A_OUT_WIDTH) * (LOG2E / math.sqrt(HEAD_DIM)),
                cols(W_AK + A_OUT_WIDTH * g, A_OUT_WIDTH), cols(W_AV + A_OUT_WIDTH * g, A_OUT_WIDTH)]

    w_main = jnp.concatenate(
        [cols(W_R, 4 * R_WIDTH), cols(W_MA, 2 * D_MODEL), cols(W_GB, A_OUT_WIDTH)] + group_cols(0),
        axis=2).astype(BF16)
    w_groups = [jnp.concatenate(group_cols(g), axis=2).astype(BF16) for g in range(1, N_GROUPS)]
    zeros_rows = jnp.zeros((DEPTH, D), F32)
    pvec = jnp.stack([rwkv_mu_rkv[:, 0], rwkv_mu_rkv[:, 1], rwkv_mu_rkv[:, 2], rwkv_w0, rwkv_a0, rwkv_k_k,
                      rwkv_k_a, _first_layer_blank(rwkv_v0)] + [zeros_rows] * (PV_ROWS - 8), axis=1)
    lora = _pack_lora([(rwkv_mu_wa[:, 0], rwkv_w1, rwkv_w2), (rwkv_mu_wa[:, 1], rwkv_a1, rwkv_a2),
                       (_first_layer_blank(rwkv_mu_v), _first_layer_blank(rwkv_v1),
                        _first_layer_blank(rwkv_v2))])
    vec = jnp.stack([rwkv_r_k.reshape(DEPTH, -1), rwkv_ln_g, rwkv_ln_b] + [zeros_rows] * 5, axis=1)
    norm_g3 = norm_g.reshape(DEPTH, 1, D)
    wa, wb, wo = w_branch_a.astype(BF16), w_branch_b.astype(BF16), w_out.astype(BF16)

    v_first = None
    for i in range(DEPTH):
        proj, h = _norm_proj(x, mod, norm_g3, w_main, i)
        (r, cum, k, v, a, b), groups = _rwkv_prep(h, proj, v_first, pvec, lora if i > 0 else lora[:3],
                                                  ones_bd, w_groups, i)
        if i == 0:
            v_first = v
        y_a, y_b = _mixers(r, cum, k, v, a, b, proj, groups, vec, ones_bd, bias, i)
        x = _merge(y_a, y_b, proj, x, mod, wa, wb, wo, final_g.reshape(1, D),
                   final_norm=(i == DEPTH - 1), layer=i)
    return x
```

```python
import functools
import math

import numpy as np
import jax
import jax.numpy as jnp
from jax import lax
from jax.experimental import pallas as pl
from jax.experimental.pallas import tpu as pltpu

F32 = jnp.float32
BF16 = jnp.bfloat16

D_MODEL = 1024
DEPTH = 2
HEAD_DIM = 64
R_WIDTH = 1024
N_GROUPS = 3
HEADS_PER_GROUP = 8
DILATIONS = (1, 4, 16)
BLK = 128
A_QK_WIDTH = 1536
A_OUT_WIDTH = 512
NUM_BUCKETS = 32
MAX_DISTANCE = 2048
PROJ_WIDTH = 4 * R_WIDTH + 3 * A_QK_WIDTH + A_OUT_WIDTH + 2 * D_MODEL
RMS_EPS = 1e-6
GN_EPS = 64e-5
NEG_INF = -1e30
LOG2E = math.log2(math.e)

LANES = 128
MXU_DIM = 256
HEADS_PER_TILE = MXU_DIM // HEAD_DIM
N_COLGROUPS = R_WIDTH // MXU_DIM
CHUNK = 64
SCAN_CHUNKS = 1

W_R, W_K, W_V, W_GA = 0, 1024, 2048, 3072
W_AQ, W_AK, W_AV = 4096, 5632, 7168
W_GB, W_MA, W_MB = 8704, 9216, 10240
COL_R, COL_K, COL_V, COL_GA, COL_MA, COL_MB, COL_GB, COL_A0 = 0, 1024, 2048, 3072, 4096, 5120, 6144, 6656
MAIN_WIDTH = 8192

VMEM_LIMIT = 56 * 1024 * 1024
MOD_ROWS = 8


def _cparams(sem, vmem_limit=VMEM_LIMIT):
    return pltpu.CompilerParams(dimension_semantics=sem, vmem_limit_bytes=vmem_limit)


def _sigmoid(z):
    return 1.0 / (1.0 + jnp.exp(-z))


def _silu(z):
    return z * _sigmoid(z)


def _dot(a, b):
    return jnp.dot(a, b, preferred_element_type=F32)


def _dot_nt(a, b):
    return lax.dot_general(a, b, (((1,), (1,)), ((), ())), preferred_element_type=F32)


def _split2(x):
    hi = x.astype(BF16)
    lo = (x - hi.astype(F32)).astype(BF16)
    return hi, lo


def _segsum64(x, ones_bd):
    n = x.shape[0]
    xs = jnp.concatenate([x[:, MXU_DIM * g:MXU_DIM * (g + 1)] for g in range(N_COLGROUPS)], axis=0)
    s = _dot(xs.astype(BF16), ones_bd)
    return jnp.concatenate([s[n * g:n * (g + 1)] for g in range(N_COLGROUPS)], axis=1)


def _mod_kernel(c_ref, w_ref, b_ref, o_ref):
    s = _silu(c_ref[...])
    o_ref[0] = _dot(s.astype(BF16), w_ref[0].astype(BF16)) + b_ref[0]


def _adaln_mod(c, ada_w, ada_b):
    L = ada_w.shape[0]
    B = c.shape[0]
    c_rows = jnp.pad(c, ((0, MOD_ROWS - B), (0, 0)))
    nj = 3
    return pl.pallas_call(
        _mod_kernel,
        grid=(L, nj),
        in_specs=[pl.BlockSpec((MOD_ROWS, D_MODEL), lambda l, j: (0, 0)),
                  pl.BlockSpec((1, D_MODEL, D_MODEL), lambda l, j: (l, 0, j)),
                  pl.BlockSpec((1, 1, D_MODEL), lambda l, j: (l, 0, j))],
        out_specs=pl.BlockSpec((1, MOD_ROWS, D_MODEL), lambda l, j: (l, 0, j)),
        out_shape=jax.ShapeDtypeStruct((L, MOD_ROWS, 3 * D_MODEL), F32),
        compiler_params=_cparams(("parallel", "parallel")),
        name="adaln_mod",
    )(c_rows, ada_w, ada_b.reshape(L, 1, 3 * D_MODEL))


def _t5_bucket(dist):
    max_exact = NUM_BUCKETS // 2
    safe = np.maximum(dist, 1).astype(np.float32)
    large = max_exact + (np.log(safe / max_exact) / math.log(MAX_DISTANCE / max_exact)
                         * (NUM_BUCKETS - max_exact)).astype(np.int32)
    large = np.minimum(large, NUM_BUCKETS - 1)
    return np.where(dist < max_exact, dist, large).astype(np.int32)


def _bias_kernel(tab_ref, bucket_ref, o_ref):
    g = pl.program_id(0)
    bk = bucket_ref[0]
    for hh in range(HEADS_PER_GROUP):
        h = g * HEADS_PER_GROUP + hh
        acc = jnp.zeros(bk.shape, F32)
        for b in range(NUM_BUCKETS):
            acc = jnp.where(bk == b, tab_ref[h * NUM_BUCKETS + b], acc)
        o_ref[hh] = jnp.where(bk >= 0, acc * LOG2E, NEG_INF)


def _rel_bias(rel_bias):
    n_heads = rel_bias.shape[1]
    qi = np.arange(BLK)[:, None]
    ki = np.arange(2 * BLK)[None, :]
    delta = qi + BLK - ki
    band = (delta >= 0) & (delta <= BLK)
    buckets = np.stack([np.where(band, _t5_bucket(np.maximum(delta, 0) * d), -1)
                        for d in DILATIONS]).astype(np.int32)
    table = rel_bias.T.reshape(-1)
    return pl.pallas_call(
        _bias_kernel,
        grid=(n_heads // HEADS_PER_GROUP,),
        in_specs=[pl.BlockSpec(memory_space=pltpu.SMEM),
                  pl.BlockSpec((1, BLK, 2 * BLK), lambda g: (g, 0, 0))],
        out_specs=pl.BlockSpec((HEADS_PER_GROUP, BLK, 2 * BLK), lambda g: (g, 0, 0)),
        out_shape=jax.ShapeDtypeStruct((n_heads, BLK, 2 * BLK), F32),
        compiler_params=_cparams(("parallel",)),
        name="rel_bias",
    )(table, jnp.asarray(buckets))


def _proj_kernel(x_ref, mod_ref, g_ref, w_ref, proj_ref, h_ref):
    @pl.when(pl.program_id(2) == 0)
    def _():
        x = x_ref[0]
        ms = jnp.mean(x * x, axis=-1, keepdims=True)
        y = x * lax.rsqrt(ms + RMS_EPS) * g_ref[...]
        shift = mod_ref[0, :, 0:D_MODEL]
        scale = mod_ref[0, :, D_MODEL:2 * D_MODEL]
        h_ref[0] = (y * (1.0 + scale) + shift).astype(BF16)

    proj_ref[0] = _dot(h_ref[0], w_ref[...]).astype(BF16)


def _layer_block(arr, layer):
    tail = (0,) * (arr.ndim - 1)
    return pl.BlockSpec((None,) + arr.shape[1:], lambda *_: (layer,) + tail)


def _norm_proj(x, mod, norm_g, w_main, layer, tm=1024, tn=4096):
    B, S, D = x.shape
    N = w_main.shape[2]
    return pl.pallas_call(
        _proj_kernel,
        grid=(B, S // tm, N // tn),
        in_specs=[pl.BlockSpec((1, tm, D), lambda b, i, j: (b, i, 0)),
                  pl.BlockSpec((1, 1, 3 * D), lambda b, i, j: (layer * MOD_ROWS + b, 0, 0)),
                  _layer_block(norm_g, layer),
                  pl.BlockSpec((None, D, tn), lambda b, i, j: (layer, 0, j))],
        out_specs=[pl.BlockSpec((1, tm, tn), lambda b, i, j: (b, i, j)),
                   pl.BlockSpec((1, tm, D), lambda b, i, j: (b, i, 0))],
        out_shape=[jax.ShapeDtypeStruct((B, S, N), BF16),
                   jax.ShapeDtypeStruct((B, S, D), BF16)],
        compiler_params=_cparams(("parallel", "parallel", "arbitrary")),
        name="norm_proj",
    )(x, mod, norm_g, w_main)


GATHER_ROWS = MXU_DIM


def _group_proj_stages(h_ref, perm_ref, w_ref, o_ref, d):
    tm = h_ref.shape[1]
    per_res = GATHER_ROWS // d
    perm = perm_ref[...]
    for ck in range(tm // GATHER_ROWS):
        r0 = ck * GATHER_ROWS
        hp = _dot(perm, h_ref[0, r0:r0 + GATHER_ROWS, :]).astype(BF16)
        yield
        res = _dot(hp, w_ref[...]).astype(BF16)
        for r in range(d):
            o_ref[0, r, ck * per_res:(ck + 1) * per_res, :] = res[r * per_res:(r + 1) * per_res]
        yield


def _gather_perm(d):
    dst = np.arange(GATHER_ROWS)
    src = (dst % (GATHER_ROWS // d)) * d + dst // (GATHER_ROWS // d)
    return jnp.asarray(src[:, None] == np.arange(GATHER_ROWS)[None, :], BF16)


PV_MU_R, PV_MU_K, PV_MU_V, PV_W0, PV_A0, PV_KK, PV_KA, PV_V0 = range(8)
LORA_LANES = 256


def _pack_lora(paths):
    n_layers, d_model, _ = paths[0][1].shape
    used = sum(down.shape[2] for _, down, _ in paths)
    pad = jnp.zeros((n_layers, d_model, LORA_LANES - used), F32)
    keep = jnp.concatenate([(1.0 - mu)[:, :, None] * down for mu, down, _ in paths] + [pad], axis=2)
    shifted = jnp.concatenate([mu[:, :, None] * down for mu, down, _ in paths] + [pad], axis=2)
    ups, lane = [], 0
    for _, down, up in paths:
        rank = down.shape[2]
        ups.append(jnp.pad(up, ((0, 0), (lane, LORA_LANES - lane - rank), (0, 0))).astype(BF16))
        lane += rank
    return [jnp.concatenate([keep, shifted], axis=2).astype(BF16)] + ups


def _first_layer_blank(arr):
    return jnp.concatenate([jnp.zeros((1,) + arr.shape[1:], arr.dtype), arr], axis=0)
PV_ROWS = 16
PREV_ROWS = 16


def _shift_rows(t, prev_last):
    rolled = pltpu.roll(t, 1, axis=0)
    row = lax.broadcasted_iota(jnp.int32, t.shape, 0)
    return jnp.where(row == 0, prev_last, rolled)


def _rprep_stages(refs, has_vres):
    if has_vres:
        (h_ref, hp_ref, pr_ref, prp_ref, pk_ref, pkp_ref, pvv_ref, pvp_ref, vf_ref, pvec_ref,
         wd_ref, uw_ref, ua_ref, uv_ref, ones_ref, tril_ref,
         r_out, cum_out, k_out, v_out, a_out, b_out) = refs
    else:
        (h_ref, hp_ref, pr_ref, prp_ref, pk_ref, pkp_ref, pvv_ref, pvp_ref, pvec_ref,
         wd_ref, uw_ref, ua_ref, ones_ref, tril_ref,
         r_out, cum_out, k_out, v_out, a_out, b_out) = refs

    not_first = (pl.program_id(1) > 0).astype(F32)

    def prm(i):
        return pvec_ref[i:i + 1, :]

    def lerp_shift(cur_ref, prev_ref, mu):
        t = cur_ref[0].astype(F32)
        last = prev_ref[0, PREV_ROWS - 1:PREV_ROWS, :].astype(F32)
        return t + (_shift_rows(t, last * not_first) - t) * mu

    r = lerp_shift(pr_ref, prp_ref, prm(PV_MU_R))
    k = lerp_shift(pk_ref, pkp_ref, prm(PV_MU_K))
    v = lerp_shift(pvv_ref, pvp_ref, prm(PV_MU_V))
    yield

    wd = wd_ref[...]
    z2 = _dot(h_ref[0], wd)
    z_prev = _dot(hp_ref[0], wd[:, LORA_LANES:])[PREV_ROWS - 1:PREV_ROWS, :]
    z = z2[:, 0:LORA_LANES] + _shift_rows(z2[:, LORA_LANES:], z_prev * not_first)
    zb = z.astype(BF16)
    yield

    zw = prm(PV_W0) + _dot(jnp.tanh(z).astype(BF16), uw_ref[...])
    w = jnp.minimum(zw, 0.0) - jnp.log(1.0 + jnp.exp(-jnp.abs(zw))) - 0.5
    lw = -jnp.exp(w)
    yield
    hi, lo = _split2(lw)
    tril = tril_ref[...]
    for i in range(lw.shape[0] // MXU_DIM):
        blk = slice(MXU_DIM * i, MXU_DIM * (i + 1))
        cum_out[0, blk, :] = _dot(tril, hi[blk]) + _dot(tril, lo[blk])
    yield
    a = _sigmoid(prm(PV_A0) + _dot(zb, ua_ref[...]))
    if has_vres:
        mix = _sigmoid(prm(PV_V0) + _dot(zb, uv_ref[...]))
        v = v + (vf_ref[0].astype(F32) - v) * mix
    yield

    kk = k * prm(PV_KK)
    ss = _segsum64(kk * kk, ones_ref[...])
    kk = kk * lax.rsqrt(jnp.maximum(ss, 1e-24))
    yield
    r_out[0] = r.astype(BF16)
    k_out[0] = (k * (1.0 + (a - 1.0) * prm(PV_KA))).astype(BF16)
    v_out[0] = v.astype(BF16)
    a_out[0] = (-kk).astype(BF16)
    b_out[0] = (kk * a).astype(BF16)
    yield


N_GROUP_PROJ = N_GROUPS - 1


def _prep_groups_kernel(*refs, has_vres):
    n_in = len(refs) - 6 - N_GROUP_PROJ - 2 * N_GROUP_PROJ
    prep_in, rest = refs[:n_in], refs[n_in:]
    gp_in, outs = rest[:2 * N_GROUP_PROJ], rest[2 * N_GROUP_PROJ:]
    prep = _rprep_stages(tuple(prep_in) + tuple(outs[:6]), has_vres)
    h_ref = prep_in[0]
    groups = [_group_proj_stages(h_ref, gp_in[2 * n], gp_in[2 * n + 1], outs[6 + n], DILATIONS[n + 1])
              for n in range(N_GROUP_PROJ)]

    def all_groups():
        for gen in groups:
            yield from gen

    for _ in _interleave(prep, all_groups()):
        pass


def _rwkv_prep(h, proj, v_first, pvec, lora, ones_bd, w_groups, layer, tr=512):
    B, S, D = h.shape
    has_vres = v_first is not None
    rpb = tr // PREV_ROWS
    t = np.arange(MXU_DIM)
    tril_bd = jnp.asarray((t[None, :] <= t[:, None]) & (t[None, :] // CHUNK == t[:, None] // CHUNK), BF16)

    def cur(c):
        return pl.BlockSpec((1, tr, R_WIDTH), lambda b, i: (b, i, c))

    def prev(c):
        return pl.BlockSpec((1, PREV_ROWS, R_WIDTH), lambda b, i: (b, jnp.maximum(i * rpb - 1, 0), c))

    def full(arr):
        return pl.BlockSpec(arr.shape, lambda b, i: (0,) * arr.ndim)

    in_specs = [cur(0), prev(0)]
    args = [h, h]
    for c in (COL_R, COL_K, COL_V):
        in_specs += [cur(c // R_WIDTH), prev(c // R_WIDTH)]
        args += [proj, proj]
    if has_vres:
        in_specs.append(cur(0))
        args.append(v_first)
    for per_layer in [pvec] + list(lora):
        in_specs.append(_layer_block(per_layer, layer))
        args.append(per_layer)
    for const in (ones_bd, tril_bd):
        in_specs.append(full(const))
        args.append(const)
    out = [jax.ShapeDtypeStruct((B, S, R_WIDTH), F32 if n == 1 else BF16) for n in range(6)]
    out_specs = [cur(0)] * 6
    for n, w_group in enumerate(w_groups):
        d = DILATIONS[n + 1]
        width = w_group.shape[2]
        in_specs += [full(_gather_perm(d)), _layer_block(w_group, layer)]
        args += [_gather_perm(d), w_group]
        out.append(jax.ShapeDtypeStruct((B, d, S // d, width), BF16))
        out_specs.append(pl.BlockSpec((1, d, tr // d, width), lambda b, i: (b, 0, i, 0)))
    res = pl.pallas_call(
        functools.partial(_prep_groups_kernel, has_vres=has_vres),
        grid=(B, S // tr),
        in_specs=in_specs,
        out_specs=out_specs,
        out_shape=out,
        compiler_params=_cparams(("parallel", "parallel")),
        name="prep_groups",
    )(*args)
    return res[:6], res[6:]


def _interleave(*gens):
    live = list(gens)
    while live:
        for gen in list(live):
            if next(gen, StopIteration) is StopIteration:
                live.remove(gen)
        yield


def _scan_stages(r_ref, cum_ref, k_ref, v_ref, a_ref, b_ref, ga_ref, vec_ref, ones_ref,
                 y_ref, s_ref, *, nb, tt):
    C = CHUNK

    @pl.when(pl.program_id(0) == 0)
    def _():
        s_ref[...] = jnp.zeros(s_ref.shape, F32)

    row = lax.broadcasted_iota(jnp.int32, (C, MXU_DIM), 0)
    lane = lax.broadcasted_iota(jnp.int32, (C, MXU_DIM), 1)
    col = lane & (HEAD_DIM - 1)
    lhead = lane >> 6
    strict = col < row
    incl = col <= row
    eye = (col == row).astype(F32)
    head_masks = [lhead == hh for hh in range(HEADS_PER_TILE)]

    def bdrows(x):
        return jnp.concatenate([jnp.where(m, x, 0.0) for m in head_masks], axis=0).astype(BF16)

    def diag_blocks(full):
        acc = jnp.where(head_masks[0], full[0:C], 0.0)
        for hh in range(1, HEADS_PER_TILE):
            acc = acc + jnp.where(head_masks[hh], full[C * hh:C * (hh + 1)], 0.0)
        return acc

    row_full = lax.broadcasted_iota(jnp.int32, (C, R_WIDTH), 0)
    ones_bd = ones_ref[...]
    r_k = vec_ref[0:1, :]
    ln_g = vec_ref[1:2, :]
    ln_b = vec_ref[2:3, :]

    chains = [(bi, g) for bi in range(nb) for g in range(N_COLGROUPS)]
    insts = [(ck, bi, g) for ck in range(SCAN_CHUNKS) for bi, g in chains]


    def load(gi):
        rows = [slice((gi * SCAN_CHUNKS + ck) * C, (gi * SCAN_CHUNKS + ck + 1) * C) for ck in range(SCAN_CHUNKS)]
        ops = {}
        for ck in range(SCAN_CHUNKS):
            for bi in range(nb):
                cum = cum_ref[bi, rows[ck], :]
                r = r_ref[bi, rows[ck], :].astype(F32)
                k = k_ref[bi, rows[ck], :].astype(F32)
                v = v_ref[bi, rows[ck], :].astype(F32)
                a = a_ref[bi, rows[ck], :].astype(F32)
                b = b_ref[bi, rows[ck], :].astype(F32)
                total = cum[C - 1:C, :]
                p_in = jnp.exp(cum)
                p_inv = jnp.exp(-cum)
                p_rest = jnp.exp(total - cum)
                p_before = jnp.where(row_full == 0, 1.0, pltpu.roll(p_in, 1, axis=0))
                ops[ck, bi] = dict(r=r, k=k, v=v, a_t=a * p_before, r_t=r * p_in, b_t=b * p_inv,
                                   k_t=k * p_inv, bp=b * p_rest, kp=k * p_rest, p_all=jnp.exp(total))
        return dict(rows=rows, ops=ops)

    def part(ctx, name, ck, bi, g):
        return ctx["ops"][ck, bi][name][:, MXU_DIM * g:MXU_DIM * (g + 1)]

    def independent(ctx):
        res = [_dot_nt(jnp.concatenate([part(ctx, "a_t", *i), part(ctx, "r_t", *i)], axis=0).astype(BF16),
                       jnp.concatenate([bdrows(part(ctx, "b_t", *i)), bdrows(part(ctx, "k_t", *i))], axis=0))
               for i in insts]
        yield
        a_ab = [jnp.where(strict, x[0:C, 0:MXU_DIM], 0.0) for x in res]
        a_ak = [jnp.where(strict, x[0:C, MXU_DIM:], 0.0) for x in res]
        ctx["a_rb"] = [jnp.where(incl, x[C:, 0:MXU_DIM], 0.0).astype(BF16) for x in res]
        a_rk = [jnp.where(incl, x[C:, MXU_DIM:], 0.0) for x in res]

        pw = [_dot(x.astype(BF16), bdrows(x)) for x in a_ab]
        tinv = [eye + x for x in a_ab]
        yield
        for _ in range(4):
            both = [_dot(jnp.concatenate([p, t], axis=0).astype(BF16), bdrows(p)) for p, t in zip(pw, tinv)]
            tinv = [t + x[C:] for t, x in zip(tinv, both)]
            pw = [x[0:C] for x in both]
            yield
        tinv = [t + _dot(t.astype(BF16), bdrows(p)) for p, t in zip(pw, tinv)]
        yield
        tax = [_dot(t.astype(BF16), jnp.concatenate([bdrows(part(ctx, "a_t", *i)), bdrows(x)], axis=1))
               for t, x, i in zip(tinv, a_ak, insts)]
        yield
        ctx["from_v"] = [_dot(jnp.concatenate([x[:, MXU_DIM:], ark], axis=0).astype(BF16),
                              bdrows(part(ctx, "v", *i))) for x, ark, i in zip(tax, a_rk, insts)]
        ctx["tax"] = tax
        yield

    def dependent(ctx, carried):
        tax, from_v, a_rb = ctx["tax"], ctx["from_v"], ctx["a_rb"]
        st = carried["st"]
        y = {}
        for ck in range(SCAN_CHUNKS):
            sel = range(ck * len(chains), (ck + 1) * len(chains))
            from_state = [_dot_nt(jnp.concatenate([tax[n][:, 0:MXU_DIM], part(ctx, "r_t", *insts[n])],
                                                  axis=0).astype(BF16), bdrows(s))
                          for n, s in zip(sel, st)]
            yield
            u = [x[0:C] + from_v[n][0:C] for x, n in zip(from_state, sel)]
            for x, n, uu in zip(from_state, sel, u):
                y[insts[n]] = x[C:] + from_v[n][C:] + _dot(a_rb[n], bdrows(uu))
            upd = [_dot(jnp.concatenate([uu, part(ctx, "v", *insts[n])], axis=0).T.astype(BF16),
                        jnp.concatenate([part(ctx, "bp", *insts[n]), part(ctx, "kp", *insts[n])],
                                        axis=0).astype(BF16))
                   for uu, n in zip(u, sel)]
            yield
            st = [s_old * part(ctx, "p_all", *insts[n]) + diag_blocks(x) for s_old, x, n in zip(st, upd, sel)]
        carried["st"] = st

        for ck in range(SCAN_CHUNKS):
            for bi in range(nb):
                p = ctx["ops"][ck, bi]
                yc = jnp.concatenate([y[ck, bi, g] for g in range(N_COLGROUPS)], axis=1)
                mean = _segsum64(yc, ones_bd) * (1.0 / HEAD_DIM)
                yield
                yd = yc - mean
                var = _segsum64(yd * yd, ones_bd) * (1.0 / HEAD_DIM)
                yn = yd * lax.rsqrt(var + GN_EPS) * ln_g + ln_b
                bonus = _segsum64(p["r"] * p["k"] * r_k, ones_bd) * p["v"]
                rows = ctx["rows"][ck]
                y_ref[bi, rows, :] = ((yn + bonus) * _silu(ga_ref[bi, rows, :].astype(F32))).astype(BF16)
                yield

    carried = {"st": [s_ref[bi * N_COLGROUPS + g] for bi, g in chains]}
    n_groups = tt // (C * SCAN_CHUNKS)
    ctx = load(0)
    yield from independent(ctx)
    for gi in range(1, n_groups):
        nxt = load(gi)
        yield from _interleave(dependent(ctx, carried), independent(nxt))
        ctx = nxt
    yield from dependent(ctx, carried)
    for (bi, g), s_new in zip(chains, carried["st"]):
        s_ref[bi * N_COLGROUPS + g] = s_new


ATT_TILE = 2048
ATT_UNROLL = (5, 6, 8)


def _attn_stages(q_refs, k_refs, v_refs, kp_refs, vp_refs, gb_ref, bias_ref, y_ref, o_refs, l_refs,
                 is_first):
    prev_limit = jnp.where(is_first, BLK, 0)
    ki = lax.broadcasted_iota(jnp.int32, (2 * BLK, 2 * BLK), 1)
    head0 = lax.broadcasted_iota(jnp.int32, (BLK, LANES), 1) < HEAD_DIM
    ones_cols = jnp.ones((2 * BLK, LANES), BF16)
    zero = jnp.zeros((BLK, LANES), BF16)

    def process(blocks):
        q2s, kws, vws, bias2s, stores = [], [], [], [], []
        for g, sub, res in blocks:
            d = DILATIONS[g]
            base = sub * (BLK * d) + res
            q = q_refs[g][0, res, sub * BLK:(sub + 1) * BLK, :]
            q2s.append(jnp.concatenate([jnp.where(head0, q, zero), jnp.where(head0, zero, q)], axis=0))
            if sub == 0:
                kw = jnp.concatenate([kp_refs[g][0, res], k_refs[g][0, res, 0:BLK, :]], axis=0)
                vw = jnp.concatenate([vp_refs[g][0, res], v_refs[g][0, res, 0:BLK, :]], axis=0)
            else:
                kw = k_refs[g][0, res, (sub - 1) * BLK:(sub + 1) * BLK, :]
                vw = v_refs[g][0, res, (sub - 1) * BLK:(sub + 1) * BLK, :]
            kws.append(kw)
            vws.append(jnp.concatenate([vw, ones_cols], axis=1))
            bias2s.append(bias_ref[g, 0].reshape(2 * BLK, 2 * BLK))
            stores.append((g, pl.ds(base, BLK) if d == 1 else pl.ds(base, BLK, stride=d)))
        logits = [jnp.where(bias2 > 0.5 * NEG_INF, _dot_nt(q2, kw) + bias2, NEG_INF)
                  for q2, kw, bias2 in zip(q2s, kws, bias2s)]
        logits = [jnp.where(ki < prev_limit, NEG_INF, x) if blk[1] == 0 else x
                  for x, blk in zip(logits, blocks)]
        yield
        ms = [jnp.max(x, axis=-1, keepdims=True) for x in logits]
        ps = [jnp.exp2(x - m).astype(BF16) for x, m in zip(logits, ms)]
        pvs = [_dot(p, vw) for p, vw in zip(ps, vws)]
        yield
        for (g, rows), pv, m in zip(stores, pvs, ms):
            num = jnp.where(head0, pv[0:BLK, 0:LANES], pv[BLK:, 0:LANES])
            den = jnp.where(head0, pv[0:BLK, LANES:], pv[BLK:, LANES:])
            o_refs[g][rows, :] = num / den
            l_refs[g][rows, :] = jnp.where(head0, m[0:BLK], m[BLK:]) + jnp.log2(den)

    for g, d in enumerate(DILATIONS):
        blocks = [(g, sub, res) for sub in range(ATT_TILE // (BLK * d)) for res in range(d)]
        for n in range(0, len(blocks), ATT_UNROLL[g]):
            yield from process(blocks[n:n + ATT_UNROLL[g]])

    l0, l1, l2 = l_refs[0][...], l_refs[1][...], l_refs[2][...]
    m = jnp.maximum(jnp.maximum(l0, l1), l2)
    w0, w1, w2 = jnp.exp2(l0 - m), jnp.exp2(l1 - m), jnp.exp2(l2 - m)
    y = (w0 * o_refs[0][...] + w1 * o_refs[1][...] + w2 * o_refs[2][...]) / (w0 + w1 + w2)
    y_ref[0] = (y * _silu(gb_ref[0].astype(F32))).astype(BF16)
    yield


N_SCAN_REFS = 9
N_ATTN_REFS = 17
MIXERS_VMEM_LIMIT = 62 * 1024 * 1024


def _mixers_kernel(*refs, nb, tt, tiles_per_seq, n_tiles):
    scan_in = refs[:N_SCAN_REFS]
    attn_in = refs[N_SCAN_REFS:N_SCAN_REFS + N_ATTN_REFS]
    ya_ref, yb_ref, s_ref = refs[N_SCAN_REFS + N_ATTN_REFS:N_SCAN_REFS + N_ATTN_REFS + 3]
    scratch = refs[N_SCAN_REFS + N_ATTN_REFS + 3:]
    is_first = ((pl.program_id(0) % n_tiles) % tiles_per_seq) == 0
    scan = _scan_stages(*scan_in, ya_ref, s_ref, nb=nb, tt=tt)
    attn = _attn_stages(attn_in[0:3], attn_in[3:6], attn_in[6:9], attn_in[9:12], attn_in[12:15],
                        attn_in[15], attn_in[16], yb_ref, scratch[0:3], scratch[3:6], is_first)
    for _ in _interleave(scan, attn):
        pass


def _mixers(r, cum, k, v, a, b, main, groups, vec, ones_bd, bias5, layer):
    B, S, W = r.shape
    n_pairs = HEADS_PER_GROUP // 2
    tiles_per_seq = S // ATT_TILE
    n_tiles = B * tiles_per_seq
    n_steps = n_pairs * n_tiles
    tt = S // n_steps
    assert tt * n_steps == S and tt % (CHUNK * SCAN_CHUNKS) == 0

    scan_spec = pl.BlockSpec((B, tt, W), lambda t: (0, t, 0))

    def full(arr):
        return pl.BlockSpec(arr.shape, lambda t: (0,) * arr.ndim)

    scan_specs = [scan_spec] * 6 + [pl.BlockSpec((B, tt, W), lambda t: (0, t, COL_GA // W)),
                                    _layer_block(vec, layer), full(ones_bd)]

    arrays = [main.reshape(B, 1, S, MAIN_WIDTH)] + list(groups)
    col_base = [COL_A0 // LANES, 0, 0]

    def where(t):
        tile = t % n_tiles
        return t // n_tiles, tile // tiles_per_seq, tile % tiles_per_seq

    def cur(g, part):
        d = DILATIONS[g]
        c0 = col_base[g] + part * (A_OUT_WIDTH // LANES)

        def index(t):
            hp, bi, ti = where(t)
            return bi, 0, ti, c0 + hp
        return pl.BlockSpec((1, d, ATT_TILE // d, LANES), index)

    def prev(g, part):
        d = DILATIONS[g]
        c0 = col_base[g] + part * (A_OUT_WIDTH // LANES)
        rb = ATT_TILE // (BLK * d)

        def index(t):
            hp, bi, ti = where(t)
            return bi, 0, jnp.maximum(ti * rb - 1, 0), c0 + hp
        return pl.BlockSpec((1, d, BLK, LANES), index)

    def tile(col0):
        def index(t):
            hp, bi, ti = where(t)
            return bi, ti, col0 // LANES + hp
        return pl.BlockSpec((1, ATT_TILE, LANES), index)

    attn_specs = ([cur(g, 0) for g in range(N_GROUPS)] + [cur(g, 1) for g in range(N_GROUPS)]
                  + [cur(g, 2) for g in range(N_GROUPS)]
                  + [prev(g, 1) for g in range(N_GROUPS)] + [prev(g, 2) for g in range(N_GROUPS)]
                  + [tile(COL_GB),
                     pl.BlockSpec((N_GROUPS, 1, 2, BLK, 2 * BLK), lambda t: (0, t // n_tiles, 0, 0, 0))])
    assert len(scan_specs) == N_SCAN_REFS and len(attn_specs) == N_ATTN_REFS
    return pl.pallas_call(
        functools.partial(_mixers_kernel, nb=B, tt=tt, tiles_per_seq=tiles_per_seq, n_tiles=n_tiles),
        grid=(n_steps,),
        in_specs=scan_specs + attn_specs,
        out_specs=[scan_spec, tile(0)],
        out_shape=[jax.ShapeDtypeStruct((B, S, W), BF16), jax.ShapeDtypeStruct((B, S, A_OUT_WIDTH), BF16)],
        scratch_shapes=([pltpu.VMEM((B * N_COLGROUPS, HEAD_DIM, MXU_DIM), F32)]
                        + [pltpu.VMEM((ATT_TILE, LANES), F32)] * 6),
        compiler_params=_cparams(("arbitrary",), MIXERS_VMEM_LIMIT),
        name="mixers",
    )(r, cum, k, v, a, b, main, vec, ones_bd, *(arrays * 5), main, bias5)


def _merge_kernel(ya_ref, yb_ref, ma_ref, mb_ref, x_ref, mod_ref, wa_ref, wb_ref, wo_ref, fg_ref,
                  o_ref, *, final_norm):
    pa = _dot(ya_ref[0], wa_ref[...])
    pb = _dot(yb_ref[0], wb_ref[...])
    merged = _sigmoid(ma_ref[0].astype(F32)) * pa + _sigmoid(mb_ref[0].astype(F32)) * pb
    out = _dot(merged.astype(BF16), wo_ref[...])
    gate = mod_ref[0, :, 2 * D_MODEL:3 * D_MODEL]
    xn = x_ref[0] + gate * out
    if final_norm:
        ms = jnp.mean(xn * xn, axis=-1, keepdims=True)
        xn = xn * lax.rsqrt(ms + RMS_EPS) * fg_ref[...]
    o_ref[0] = xn


def _merge(ya, yb, proj, x, mod, wa, wb, wo, final_g, final_norm, layer, tm=1024):
    B, S, D = x.shape

    def rows(width, c):
        return pl.BlockSpec((1, tm, width), lambda b, i: (b, i, c))

    def full(arr):
        return pl.BlockSpec(arr.shape, lambda b, i: (0,) * arr.ndim)

    return pl.pallas_call(
        functools.partial(_merge_kernel, final_norm=final_norm),
        grid=(B, S // tm),
        in_specs=[rows(R_WIDTH, 0), rows(A_OUT_WIDTH, 0),
                  rows(D, COL_MA // D), rows(D, COL_MB // D), rows(D, 0),
                  pl.BlockSpec((1, 1, 3 * D), lambda b, i: (layer * MOD_ROWS + b, 0, 0)),
                  _layer_block(wa, layer), _layer_block(wb, layer), _layer_block(wo, layer), full(final_g)],
        out_specs=rows(D, 0),
        out_shape=jax.ShapeDtypeStruct((B, S, D), F32),
        compiler_params=_cparams(("parallel", "parallel")),
        name="merge",
    )(ya, yb, proj, proj, x, mod, wa, wb, wo, final_g)


def _segment_ones():
    idx = np.arange(MXU_DIM)
    return jnp.asarray(idx[:, None] // HEAD_DIM == idx[None, :] // HEAD_DIM, BF16)


def kernel(x, c, norm_g, ada_w, ada_b, w_in, rwkv_mu_rkv, rwkv_mu_wa, rwkv_w0, rwkv_w1, rwkv_w2, rwkv_a0, rwkv_a1, rwkv_a2, rwkv_k_k, rwkv_k_a, rwkv_r_k, rwkv_ln_g, rwkv_ln_b, rwkv_mu_v, rwkv_v0, rwkv_v1, rwkv_v2, w_branch_a, w_branch_b, w_out, rel_bias, final_g):
    B, S, D = x.shape
    assert D == D_MODEL and S % ATT_TILE == 0 and w_in.shape[2] == PROJ_WIDTH
    ones_bd = _segment_ones()
    mod = _adaln_mod(c, ada_w, ada_b).reshape(DEPTH * MOD_ROWS, 1, 3 * D)
    bias = _rel_bias(rel_bias).reshape(N_GROUPS, HEADS_PER_GROUP // 2, 2, BLK, 2 * BLK)

    def cols(start, width):
        return w_in[:, :, start:start + width]

    def group_cols(g):
        return [cols(W_AQ + A_OUT_WIDTH * g, A_OUT_WIDTH) * (LOG2E / math.sqrt(HEAD_DIM)),
                cols(W_AK + A_OUT_WIDTH * g, A_OUT_WIDTH), cols(W_AV + A_OUT_WIDTH * g, A_OUT_WIDTH)]

    w_main = jnp.concatenate(
        [cols(W_R, 4 * R_WIDTH), cols(W_MA, 2 * D_MODEL), cols(W_GB, A_OUT_WIDTH)] + group_cols(0),
        axis=2).astype(BF16)
    w_groups = [jnp.concatenate(group_cols(g), axis=2).astype(BF16) for g in range(1, N_GROUPS)]
    zeros_rows = jnp.zeros((DEPTH, D), F32)
    pvec = jnp.stack([rwkv_mu_rkv[:, 0], rwkv_mu_rkv[:, 1], rwkv_mu_rkv[:, 2], rwkv_w0, rwkv_a0, rwkv_k_k,
                      rwkv_k_a, _first_layer_blank(rwkv_v0)] + [zeros_rows] * (PV_ROWS - 8), axis=1)
    lora = _pack_lora([(rwkv_mu_wa[:, 0], rwkv_w1, rwkv_w2), (rwkv_mu_wa[:, 1], rwkv_a1, rwkv_a2),
                       (_first_layer_blank(rwkv_mu_v), _first_layer_blank(rwkv_v1),
                        _first_layer_blank(rwkv_v2))])
    vec = jnp.stack([rwkv_r_k.reshape(DEPTH, -1), rwkv_ln_g, rwkv_ln_b] + [zeros_rows] * 5, axis=1)
    norm_g3 = norm_g.reshape(DEPTH, 1, D)
    wa, wb, wo = w_branch_a.astype(BF16), w_branch_b.astype(BF16), w_out.astype(BF16)

    v_first = None
    for i in range(DEPTH):
        proj, h = _norm_proj(x, mod, norm_g3, w_main, i)
        (r, cum, k, v, a, b), groups = _rwkv_prep(h, proj, v_first, pvec, lora if i > 0 else lora[:3],
                                                  ones_bd, w_groups, i)
        if i == 0:
            v_first = v
        y_a, y_b = _mixers(r, cum, k, v, a, b, proj, groups, vec, ones_bd, bias, i)
        x = _merge(y_a, y_b, proj, x, mod, wa, wb, wo, final_g.reshape(1, D),
                   final_norm=(i == DEPTH - 1), layer=i)
    return x
```

```python
import functools
import math

import numpy as np
import jax
import jax.numpy as jnp
from jax import lax
from jax.experimental import pallas as pl
from jax.experimental.pallas import tpu as pltpu

F32 = jnp.float32
BF16 = jnp.bfloat16

D_MODEL = 1024
DEPTH = 2
HEAD_DIM = 64
R_WIDTH = 1024
N_GROUPS = 3
HEADS_PER_GROUP = 8
DILATIONS = (1, 4, 16)
BLK = 128
A_QK_WIDTH = 1536
A_OUT_WIDTH = 512
NUM_BUCKETS = 32
MAX_DISTANCE = 2048
PROJ_WIDTH = 4 * R_WIDTH + 3 * A_QK_WIDTH + A_OUT_WIDTH + 2 * D_MODEL
RMS_EPS = 1e-6
GN_EPS = 64e-5
NEG_INF = -1e30
LOG2E = math.log2(math.e)

LANES = 128
MXU_DIM = 256
HEADS_PER_TILE = MXU_DIM // HEAD_DIM
N_COLGROUPS = R_WIDTH // MXU_DIM
CHUNK = 64
SCAN_CHUNKS = 2

W_R, W_K, W_V, W_GA = 0, 1024, 2048, 3072
W_AQ, W_AK, W_AV = 4096, 5632, 7168
W_GB, W_MA, W_MB = 8704, 9216, 10240
COL_R, COL_K, COL_V, COL_GA, COL_MA, COL_MB, COL_GB, COL_A0 = 0, 1024, 2048, 3072, 4096, 5120, 6144, 6656
MAIN_WIDTH = 8192

VMEM_LIMIT = 56 * 1024 * 1024
MOD_ROWS = 8


def _cparams(sem, vmem_limit=VMEM_LIMIT):
    return pltpu.CompilerParams(dimension_semantics=sem, vmem_limit_bytes=vmem_limit)


def _sigmoid(z):
    return 1.0 / (1.0 + jnp.exp(-z))


def _silu(z):
    return z * _sigmoid(z)


def _dot(a, b):
    return jnp.dot(a, b, preferred_element_type=F32)


def _dot_nt(a, b):
    return lax.dot_general(a, b, (((1,), (1,)), ((), ())), preferred_element_type=F32)


def _split2(x):
    hi = x.astype(BF16)
    lo = (x - hi.astype(F32)).astype(BF16)
    return hi, lo


def _segsum64(x, ones_bd):
    n = x.shape[0]
    xs = jnp.concatenate([x[:, MXU_DIM * g:MXU_DIM * (g + 1)] for g in range(N_COLGROUPS)], axis=0)
    s = _dot(xs.astype(BF16), ones_bd)
    return jnp.concatenate([s[n * g:n * (g + 1)] for g in range(N_COLGROUPS)], axis=1)


def _mod_kernel(c_ref, w_ref, b_ref, o_ref):
    s = _silu(c_ref[...])
    o_ref[0] = _dot(s.astype(BF16), w_ref[0].astype(BF16)) + b_ref[0]


def _adaln_mod(c, ada_w, ada_b):
    L = ada_w.shape[0]
    B = c.shape[0]
    c_rows = jnp.pad(c, ((0, MOD_ROWS - B), (0, 0)))
    nj = 3
    return pl.pallas_call(
        _mod_kernel,
        grid=(L, nj),
        in_specs=[pl.BlockSpec((MOD_ROWS, D_MODEL), lambda l, j: (0, 0)),
                  pl.BlockSpec((1, D_MODEL, D_MODEL), lambda l, j: (l, 0, j)),
                  pl.BlockSpec((1, 1, D_MODEL), lambda l, j: (l, 0, j))],
        out_specs=pl.BlockSpec((1, MOD_ROWS, D_MODEL), lambda l, j: (l, 0, j)),
        out_shape=jax.ShapeDtypeStruct((L, MOD_ROWS, 3 * D_MODEL), F32),
        compiler_params=_cparams(("parallel", "parallel")),
        name="adaln_mod",
    )(c_rows, ada_w, ada_b.reshape(L, 1, 3 * D_MODEL))


def _t5_bucket(dist):
    max_exact = NUM_BUCKETS // 2
    safe = np.maximum(dist, 1).astype(np.float32)
    large = max_exact + (np.log(safe / max_exact) / math.log(MAX_DISTANCE / max_exact)
                         * (NUM_BUCKETS - max_exact)).astype(np.int32)
    large = np.minimum(large, NUM_BUCKETS - 1)
    return np.where(dist < max_exact, dist, large).astype(np.int32)


def _bias_kernel(tab_ref, bucket_ref, o_ref):
    g = pl.program_id(0)
    bk = bucket_ref[0]
    for hh in range(HEADS_PER_GROUP):
        h = g * HEADS_PER_GROUP + hh
        acc = jnp.zeros(bk.shape, F32)
        for b in range(NUM_BUCKETS):
            acc = jnp.where(bk == b, tab_ref[h * NUM_BUCKETS + b], acc)
        o_ref[hh] = jnp.where(bk >= 0, acc * LOG2E, NEG_INF)


def _rel_bias(rel_bias):
    n_heads = rel_bias.shape[1]
    qi = np.arange(BLK)[:, None]
    ki = np.arange(2 * BLK)[None, :]
    delta = qi + BLK - ki
    band = (delta >= 0) & (delta <= BLK)
    buckets = np.stack([np.where(band, _t5_bucket(np.maximum(delta, 0) * d), -1)
                        for d in DILATIONS]).astype(np.int32)
    table = rel_bias.T.reshape(-1)
    return pl.pallas_call(
        _bias_kernel,
        grid=(n_heads // HEADS_PER_GROUP,),
        in_specs=[pl.BlockSpec(memory_space=pltpu.SMEM),
                  pl.BlockSpec((1, BLK, 2 * BLK), lambda g: (g, 0, 0))],
        out_specs=pl.BlockSpec((HEADS_PER_GROUP, BLK, 2 * BLK), lambda g: (g, 0, 0)),
        out_shape=jax.ShapeDtypeStruct((n_heads, BLK, 2 * BLK), F32),
        compiler_params=_cparams(("parallel",)),
        name="rel_bias",
    )(table, jnp.asarray(buckets))


def _proj_kernel(x_ref, mod_ref, g_ref, w_ref, proj_ref, h_ref):
    @pl.when(pl.program_id(2) == 0)
    def _():
        x = x_ref[0]
        ms = jnp.mean(x * x, axis=-1, keepdims=True)
        y = x * lax.rsqrt(ms + RMS_EPS) * g_ref[...]
        shift = mod_ref[0, :, 0:D_MODEL]
        scale = mod_ref[0, :, D_MODEL:2 * D_MODEL]
        h_ref[0] = (y * (1.0 + scale) + shift).astype(BF16)

    proj_ref[0] = _dot(h_ref[0], w_ref[...]).astype(BF16)


def _layer_block(arr, layer):
    tail = (0,) * (arr.ndim - 1)
    return pl.BlockSpec((None,) + arr.shape[1:], lambda *_: (layer,) + tail)


def _norm_proj(x, mod, norm_g, w_main, layer, tm=1024, tn=4096):
    B, S, D = x.shape
    N = w_main.shape[2]
    return pl.pallas_call(
        _proj_kernel,
        grid=(B, S // tm, N // tn),
        in_specs=[pl.BlockSpec((1, tm, D), lambda b, i, j: (b, i, 0)),
                  pl.BlockSpec((1, 1, 3 * D), lambda b, i, j: (layer * MOD_ROWS + b, 0, 0)),
                  _layer_block(norm_g, layer),
                  pl.BlockSpec((None, D, tn), lambda b, i, j: (layer, 0, j))],
        out_specs=[pl.BlockSpec((1, tm, tn), lambda b, i, j: (b, i, j)),
                   pl.BlockSpec((1, tm, D), lambda b, i, j: (b, i, 0))],
        out_shape=[jax.ShapeDtypeStruct((B, S, N), BF16),
                   jax.ShapeDtypeStruct((B, S, D), BF16)],
        compiler_params=_cparams(("parallel", "parallel", "arbitrary")),
        name="norm_proj",
    )(x, mod, norm_g, w_main)


GATHER_ROWS = MXU_DIM


def _group_proj_stages(h_ref, perm_ref, w_ref, o_ref, d):
    tm = h_ref.shape[1]
    per_res = GATHER_ROWS // d
    perm = perm_ref[...]
    for ck in range(tm // GATHER_ROWS):
        r0 = ck * GATHER_ROWS
        hp = _dot(perm, h_ref[0, r0:r0 + GATHER_ROWS, :]).astype(BF16)
        yield
        res = _dot(hp, w_ref[...]).astype(BF16)
        for r in range(d):
            o_ref[0, r, ck * per_res:(ck + 1) * per_res, :] = res[r * per_res:(r + 1) * per_res]
        yield


def _gather_perm(d):
    dst = np.arange(GATHER_ROWS)
    src = (dst % (GATHER_ROWS // d)) * d + dst // (GATHER_ROWS // d)
    return jnp.asarray(src[:, None] == np.arange(GATHER_ROWS)[None, :], BF16)


PV_MU_R, PV_MU_K, PV_MU_V, PV_W0, PV_A0, PV_KK, PV_KA, PV_V0 = range(8)
LORA_LANES = 256


def _pack_lora(paths):
    n_layers, d_model, _ = paths[0][1].shape
    used = sum(down.shape[2] for _, down, _ in paths)
    pad = jnp.zeros((n_layers, d_model, LORA_LANES - used), F32)
    keep = jnp.concatenate([(1.0 - mu)[:, :, None] * down for mu, down, _ in paths] + [pad], axis=2)
    shifted = jnp.concatenate([mu[:, :, None] * down for mu, down, _ in paths] + [pad], axis=2)
    ups, lane = [], 0
    for _, down, up in paths:
        rank = down.shape[2]
        ups.append(jnp.pad(up, ((0, 0), (lane, LORA_LANES - lane - rank), (0, 0))).astype(BF16))
        lane += rank
    return [jnp.concatenate([keep, shifted], axis=2).astype(BF16)] + ups


def _first_layer_blank(arr):
    return jnp.concatenate([jnp.zeros((1,) + arr.shape[1:], arr.dtype), arr], axis=0)
PV_ROWS = 16
PREV_ROWS = 16


def _shift_rows(t, prev_last):
    rolled = pltpu.roll(t, 1, axis=0)
    row = lax.broadcasted_iota(jnp.int32, t.shape, 0)
    return jnp.where(row == 0, prev_last, rolled)


def _rprep_stages(refs, has_vres):
    if has_vres:
        (h_ref, hp_ref, pr_ref, prp_ref, pk_ref, pkp_ref, pvv_ref, pvp_ref, vf_ref, pvec_ref,
         wd_ref, uw_ref, ua_ref, uv_ref, ones_ref, tril_ref,
         r_out, cum_out, k_out, v_out, a_out, b_out) = refs
    else:
        (h_ref, hp_ref, pr_ref, prp_ref, pk_ref, pkp_ref, pvv_ref, pvp_ref, pvec_ref,
         wd_ref, uw_ref, ua_ref, ones_ref, tril_ref,
         r_out, cum_out, k_out, v_out, a_out, b_out) = refs

    not_first = (pl.program_id(1) > 0).astype(F32)

    def prm(i):
        return pvec_ref[i:i + 1, :]

    def lerp_shift(cur_ref, prev_ref, mu):
        t = cur_ref[0].astype(F32)
        last = prev_ref[0, PREV_ROWS - 1:PREV_ROWS, :].astype(F32)
        return t + (_shift_rows(t, last * not_first) - t) * mu

    r = lerp_shift(pr_ref, prp_ref, prm(PV_MU_R))
    k = lerp_shift(pk_ref, pkp_ref, prm(PV_MU_K))
    v = lerp_shift(pvv_ref, pvp_ref, prm(PV_MU_V))
    yield

    wd = wd_ref[...]
    z2 = _dot(h_ref[0], wd)
    z_prev = _dot(hp_ref[0], wd[:, LORA_LANES:])[PREV_ROWS - 1:PREV_ROWS, :]
    z = z2[:, 0:LORA_LANES] + _shift_rows(z2[:, LORA_LANES:], z_prev * not_first)
    zb = z.astype(BF16)
    yield

    zw = prm(PV_W0) + _dot(jnp.tanh(z).astype(BF16), uw_ref[...])
    w = jnp.minimum(zw, 0.0) - jnp.log(1.0 + jnp.exp(-jnp.abs(zw))) - 0.5
    lw = -jnp.exp(w)
    yield
    hi, lo = _split2(lw)
    tril = tril_ref[...]
    for i in range(lw.shape[0] // MXU_DIM):
        blk = slice(MXU_DIM * i, MXU_DIM * (i + 1))
        cum_out[0, blk, :] = _dot(tril, hi[blk]) + _dot(tril, lo[blk])
    yield
    a = _sigmoid(prm(PV_A0) + _dot(zb, ua_ref[...]))
    if has_vres:
        mix = _sigmoid(prm(PV_V0) + _dot(zb, uv_ref[...]))
        v = v + (vf_ref[0].astype(F32) - v) * mix
    yield

    kk = k * prm(PV_KK)
    ss = _segsum64(kk * kk, ones_ref[...])
    kk = kk * lax.rsqrt(jnp.maximum(ss, 1e-24))
    yield
    r_out[0] = r.astype(BF16)
    k_out[0] = (k * (1.0 + (a - 1.0) * prm(PV_KA))).astype(BF16)
    v_out[0] = v.astype(BF16)
    a_out[0] = (-kk).astype(BF16)
    b_out[0] = (kk * a).astype(BF16)
    yield


N_GROUP_PROJ = N_GROUPS - 1


def _prep_groups_kernel(*refs, has_vres):
    n_in = len(refs) - 6 - N_GROUP_PROJ - 2 * N_GROUP_PROJ
    prep_in, rest = refs[:n_in], refs[n_in:]
    gp_in, outs = rest[:2 * N_GROUP_PROJ], rest[2 * N_GROUP_PROJ:]
    prep = _rprep_stages(tuple(prep_in) + tuple(outs[:6]), has_vres)
    h_ref = prep_in[0]
    groups = [_group_proj_stages(h_ref, gp_in[2 * n], gp_in[2 * n + 1], outs[6 + n], DILATIONS[n + 1])
              for n in range(N_GROUP_PROJ)]

    def all_groups():
        for gen in groups:
            yield from gen

    for _ in _interleave(prep, all_groups()):
        pass


def _rwkv_prep(h, proj, v_first, pvec, lora, ones_bd, w_groups, layer, tr=512):
    B, S, D = h.shape
    has_vres = v_first is not None
    rpb = tr // PREV_ROWS
    t = np.arange(MXU_DIM)
    tril_bd = jnp.asarray((t[None, :] <= t[:, None]) & (t[None, :] // CHUNK == t[:, None] // CHUNK), BF16)

    def cur(c):
        return pl.BlockSpec((1, tr, R_WIDTH), lambda b, i: (b, i, c))

    def prev(c):
        return pl.BlockSpec((1, PREV_ROWS, R_WIDTH), lambda b, i: (b, jnp.maximum(i * rpb - 1, 0), c))

    def full(arr):
        return pl.BlockSpec(arr.shape, lambda b, i: (0,) * arr.ndim)

    in_specs = [cur(0), prev(0)]
    args = [h, h]
    for c in (COL_R, COL_K, COL_V):
        in_specs += [cur(c // R_WIDTH), prev(c // R_WIDTH)]
        args += [proj, proj]
    if has_vres:
        in_specs.append(cur(0))
        args.append(v_first)
    for per_layer in [pvec] + list(lora):
        in_specs.append(_layer_block(per_layer, layer))
        args.append(per_layer)
    for const in (ones_bd, tril_bd):
        in_specs.append(full(const))
        args.append(const)
    out = [jax.ShapeDtypeStruct((B, S, R_WIDTH), F32 if n == 1 else BF16) for n in range(6)]
    out_specs = [cur(0)] * 6
    for n, w_group in enumerate(w_groups):
        d = DILATIONS[n + 1]
        width = w_group.shape[2]
        in_specs += [full(_gather_perm(d)), _layer_block(w_group, layer)]
        args += [_gather_perm(d), w_group]
        out.append(jax.ShapeDtypeStruct((B, d, S // d, width), BF16))
        out_specs.append(pl.BlockSpec((1, d, tr // d, width), lambda b, i: (b, 0, i, 0)))
    res = pl.pallas_call(
        functools.partial(_prep_groups_kernel, has_vres=has_vres),
        grid=(B, S // tr),
        in_specs=in_specs,
        out_specs=out_specs,
        out_shape=out,
        compiler_params=_cparams(("parallel", "parallel")),
        name="prep_groups",
    )(*args)
    return res[:6], res[6:]


def _interleave(*gens):
    live = list(gens)
    while live:
        for gen in list(live):
            if next(gen, StopIteration) is StopIteration:
                live.remove(gen)
        yield


def _scan_stages(r_ref, cum_ref, k_ref, v_ref, a_ref, b_ref, ga_ref, vec_ref, ones_ref,
                 y_ref, s_ref, *, nb, tt):
    C = CHUNK

    @pl.when(pl.program_id(0) == 0)
    def _():
        s_ref[...] = jnp.zeros(s_ref.shape, F32)

    row = lax.broadcasted_iota(jnp.int32, (C, MXU_DIM), 0)
    lane = lax.broadcasted_iota(jnp.int32, (C, MXU_DIM), 1)
    col = lane & (HEAD_DIM - 1)
    lhead = lane >> 6
    strict = col < row
    incl = col <= row
    eye = (col == row).astype(F32)
    head_masks = [lhead == hh for hh in range(HEADS_PER_TILE)]

    def bdrows(x):
        return jnp.concatenate([jnp.where(m, x, 0.0) for m in head_masks], axis=0).astype(BF16)

    def diag_blocks(full):
        acc = jnp.where(head_masks[0], full[0:C], 0.0)
        for hh in range(1, HEADS_PER_TILE):
            acc = acc + jnp.where(head_masks[hh], full[C * hh:C * (hh + 1)], 0.0)
        return acc

    row_full = lax.broadcasted_iota(jnp.int32, (C, R_WIDTH), 0)
    ones_bd = ones_ref[...]
    r_k = vec_ref[0:1, :]
    ln_g = vec_ref[1:2, :]
    ln_b = vec_ref[2:3, :]

    chains = [(bi, g) for bi in range(nb) for g in range(N_COLGROUPS)]
    insts = [(ck, bi, g) for ck in range(SCAN_CHUNKS) for bi, g in chains]


    def load(gi):
        rows = [slice((gi * SCAN_CHUNKS + ck) * C, (gi * SCAN_CHUNKS + ck + 1) * C) for ck in range(SCAN_CHUNKS)]
        ops = {}
        for ck in range(SCAN_CHUNKS):
            for bi in range(nb):
                cum = cum_ref[bi, rows[ck], :]
                r = r_ref[bi, rows[ck], :].astype(F32)
                k = k_ref[bi, rows[ck], :].astype(F32)
                v = v_ref[bi, rows[ck], :].astype(F32)
                a = a_ref[bi, rows[ck], :].astype(F32)
                b = b_ref[bi, rows[ck], :].astype(F32)
                total = cum[C - 1:C, :]
                p_in = jnp.exp(cum)
                p_inv = jnp.exp(-cum)
                p_rest = jnp.exp(total - cum)
                p_before = jnp.where(row_full == 0, 1.0, pltpu.roll(p_in, 1, axis=0))
                ops[ck, bi] = dict(r=r, k=k, v=v, a_t=a * p_before, r_t=r * p_in, b_t=b * p_inv,
                                   k_t=k * p_inv, bp=b * p_rest, kp=k * p_rest, p_all=jnp.exp(total))
        return dict(rows=rows, ops=ops)

    def part(ctx, name, ck, bi, g):
        return ctx["ops"][ck, bi][name][:, MXU_DIM * g:MXU_DIM * (g + 1)]

    def independent(ctx):
        res = [_dot_nt(jnp.concatenate([part(ctx, "a_t", *i), part(ctx, "r_t", *i)], axis=0).astype(BF16),
                       jnp.concatenate([bdrows(part(ctx, "b_t", *i)), bdrows(part(ctx, "k_t", *i))], axis=0))
               for i in insts]
        yield
        a_ab = [jnp.where(strict, x[0:C, 0:MXU_DIM], 0.0) for x in res]
        a_ak = [jnp.where(strict, x[0:C, MXU_DIM:], 0.0) for x in res]
        ctx["a_rb"] = [jnp.where(incl, x[C:, 0:MXU_DIM], 0.0).astype(BF16) for x in res]
        a_rk = [jnp.where(incl, x[C:, MXU_DIM:], 0.0) for x in res]

        pw = [_dot(x.astype(BF16), bdrows(x)) for x in a_ab]
        tinv = [eye + x for x in a_ab]
        yield
        for _ in range(4):
            both = [_dot(jnp.concatenate([p, t], axis=0).astype(BF16), bdrows(p)) for p, t in zip(pw, tinv)]
            tinv = [t + x[C:] for t, x in zip(tinv, both)]
            pw = [x[0:C] for x in both]
            yield
        tinv = [t + _dot(t.astype(BF16), bdrows(p)) for p, t in zip(pw, tinv)]
        yield
        tax = [_dot(t.astype(BF16), jnp.concatenate([bdrows(part(ctx, "a_t", *i)), bdrows(x)], axis=1))
               for t, x, i in zip(tinv, a_ak, insts)]
        yield
        ctx["from_v"] = [_dot(jnp.concatenate([x[:, MXU_DIM:], ark], axis=0).astype(BF16),
                              bdrows(part(ctx, "v", *i))) for x, ark, i in zip(tax, a_rk, insts)]
        ctx["tax"] = tax
        yield

    def dependent(ctx, carried):
        tax, from_v, a_rb = ctx["tax"], ctx["from_v"], ctx["a_rb"]
        st = carried["st"]
        y = {}
        for ck in range(SCAN_CHUNKS):
            sel = range(ck * len(chains), (ck + 1) * len(chains))
            from_state = [_dot_nt(jnp.concatenate([tax[n][:, 0:MXU_DIM], part(ctx, "r_t", *insts[n])],
                                                  axis=0).astype(BF16), bdrows(s))
                          for n, s in zip(sel, st)]
            yield
            u = [x[0:C] + from_v[n][0:C] for x, n in zip(from_state, sel)]
            for x, n, uu in zip(from_state, sel, u):
                y[insts[n]] = x[C:] + from_v[n][C:] + _dot(a_rb[n], bdrows(uu))
            upd = [_dot(jnp.concatenate([uu, part(ctx, "v", *insts[n])], axis=0).T.astype(BF16),
                        jnp.concatenate([part(ctx, "bp", *insts[n]), part(ctx, "kp", *insts[n])],
                                        axis=0).astype(BF16))
                   for uu, n in zip(u, sel)]
            yield
            st = [s_old * part(ctx, "p_all", *insts[n]) + diag_blocks(x) for s_old, x, n in zip(st, upd, sel)]
        carried["st"] = st

        for ck in range(SCAN_CHUNKS):
            for bi in range(nb):
                p = ctx["ops"][ck, bi]
                yc = jnp.concatenate([y[ck, bi, g] for g in range(N_COLGROUPS)], axis=1)
                mean = _segsum64(yc, ones_bd) * (1.0 / HEAD_DIM)
                yield
                yd = yc - mean
                var = _segsum64(yd * yd, ones_bd) * (1.0 / HEAD_DIM)
                yn = yd * lax.rsqrt(var + GN_EPS) * ln_g + ln_b
                bonus = _segsum64(p["r"] * p["k"] * r_k, ones_bd) * p["v"]
                rows = ctx["rows"][ck]
                y_ref[bi, rows, :] = ((yn + bonus) * _silu(ga_ref[bi, rows, :].astype(F32))).astype(BF16)
                yield

    carried = {"st": [s_ref[bi * N_COLGROUPS + g] for bi, g in chains]}
    n_groups = tt // (C * SCAN_CHUNKS)
    ctx = load(0)
    yield from independent(ctx)
    for gi in range(1, n_groups):
        nxt = load(gi)
        yield from _interleave(dependent(ctx, carried), independent(nxt))
        ctx = nxt
    yield from dependent(ctx, carried)
    for (bi, g), s_new in zip(chains, carried["st"]):
        s_ref[bi * N_COLGROUPS + g] = s_new


ATT_TILE = 2048
ATT_UNROLL = (4, 4, 8)


def _attn_stages(q_refs, k_refs, v_refs, kp_refs, vp_refs, gb_ref, bias_ref, y_ref, o_refs, l_refs,
                 is_first):
    prev_limit = jnp.where(is_first, BLK, 0)
    ki = lax.broadcasted_iota(jnp.int32, (2 * BLK, 2 * BLK), 1)
    head0 = lax.broadcasted_iota(jnp.int32, (BLK, LANES), 1) < HEAD_DIM
    ones_cols = jnp.ones((2 * BLK, LANES), BF16)
    zero = jnp.zeros((BLK, LANES), BF16)

    def process(blocks):
        q2s, kws, vws, bias2s, stores = [], [], [], [], []
        for g, sub, res in blocks:
            d = DILATIONS[g]
            base = sub * (BLK * d) + res
            q = q_refs[g][0, res, sub * BLK:(sub + 1) * BLK, :]
            q2s.append(jnp.concatenate([jnp.where(head0, q, zero), jnp.where(head0, zero, q)], axis=0))
            if sub == 0:
                kw = jnp.concatenate([kp_refs[g][0, res], k_refs[g][0, res, 0:BLK, :]], axis=0)
                vw = jnp.concatenate([vp_refs[g][0, res], v_refs[g][0, res, 0:BLK, :]], axis=0)
            else:
                kw = k_refs[g][0, res, (sub - 1) * BLK:(sub + 1) * BLK, :]
                vw = v_refs[g][0, res, (sub - 1) * BLK:(sub + 1) * BLK, :]
            kws.append(kw)
            vws.append(jnp.concatenate([vw, ones_cols], axis=1))
            bias2s.append(bias_ref[g, 0].reshape(2 * BLK, 2 * BLK))
            stores.append((g, pl.ds(base, BLK) if d == 1 else pl.ds(base, BLK, stride=d)))
        logits = [jnp.where(bias2 > 0.5 * NEG_INF, _dot_nt(q2, kw) + bias2, NEG_INF)
                  for q2, kw, bias2 in zip(q2s, kws, bias2s)]
        logits = [jnp.where(ki < prev_limit, NEG_INF, x) if blk[1] == 0 else x
                  for x, blk in zip(logits, blocks)]
        yield
        ms = [jnp.max(x, axis=-1, keepdims=True) for x in logits]
        ps = [jnp.exp2(x - m).astype(BF16) for x, m in zip(logits, ms)]
        pvs = [_dot(p, vw) for p, vw in zip(ps, vws)]
        yield
        for (g, rows), pv, m in zip(stores, pvs, ms):
            num = jnp.where(head0, pv[0:BLK, 0:LANES], pv[BLK:, 0:LANES])
            den = jnp.where(head0, pv[0:BLK, LANES:], pv[BLK:, LANES:])
            o_refs[g][rows, :] = num / den
            l_refs[g][rows, :] = jnp.where(head0, m[0:BLK], m[BLK:]) + jnp.log2(den)

    for g, d in enumerate(DILATIONS):
        blocks = [(g, sub, res) for sub in range(ATT_TILE // (BLK * d)) for res in range(d)]
        for n in range(0, len(blocks), ATT_UNROLL[g]):
            yield from process(blocks[n:n + ATT_UNROLL[g]])

    l0, l1, l2 = l_refs[0][...], l_refs[1][...], l_refs[2][...]
    m = jnp.maximum(jnp.maximum(l0, l1), l2)
    w0, w1, w2 = jnp.exp2(l0 - m), jnp.exp2(l1 - m), jnp.exp2(l2 - m)
    y = (w0 * o_refs[0][...] + w1 * o_refs[1][...] + w2 * o_refs[2][...]) / (w0 + w1 + w2)
    y_ref[0] = (y * _silu(gb_ref[0].astype(F32))).astype(BF16)
    yield


N_SCAN_REFS = 9
N_ATTN_REFS = 17
MIXERS_VMEM_LIMIT = 62 * 1024 * 1024


def _mixers_kernel(*refs, nb, tt, tiles_per_seq, n_tiles):
    scan_in = refs[:N_SCAN_REFS]
    attn_in = refs[N_SCAN_REFS:N_SCAN_REFS + N_ATTN_REFS]
    ya_ref, yb_ref, s_ref = refs[N_SCAN_REFS + N_ATTN_REFS:N_SCAN_REFS + N_ATTN_REFS + 3]
    scratch = refs[N_SCAN_REFS + N_ATTN_REFS + 3:]
    is_first = ((pl.program_id(0) % n_tiles) % tiles_per_seq) == 0
    scan = _scan_stages(*scan_in, ya_ref, s_ref, nb=nb, tt=tt)
    attn = _attn_stages(attn_in[0:3], attn_in[3:6], attn_in[6:9], attn_in[9:12], attn_in[12:15],
                        attn_in[15], attn_in[16], yb_ref, scratch[0:3], scratch[3:6], is_first)
    for _ in _interleave(scan, attn):
        pass


def _mixers(r, cum, k, v, a, b, main, groups, vec, ones_bd, bias5, layer):
    B, S, W = r.shape
    n_pairs = HEADS_PER_GROUP // 2
    tiles_per_seq = S // ATT_TILE
    n_tiles = B * tiles_per_seq
    n_steps = n_pairs * n_tiles
    tt = S // n_steps
    assert tt * n_steps == S and tt % (CHUNK * SCAN_CHUNKS) == 0

    scan_spec = pl.BlockSpec((B, tt, W), lambda t: (0, t, 0))

    def full(arr):
        return pl.BlockSpec(arr.shape, lambda t: (0,) * arr.ndim)

    scan_specs = [scan_spec] * 6 + [pl.BlockSpec((B, tt, W), lambda t: (0, t, COL_GA // W)),
                                    _layer_block(vec, layer), full(ones_bd)]

    arrays = [main.reshape(B, 1, S, MAIN_WIDTH)] + list(groups)
    col_base = [COL_A0 // LANES, 0, 0]

    def where(t):
        tile = t % n_tiles
        return t // n_tiles, tile // tiles_per_seq, tile % tiles_per_seq

    def cur(g, part):
        d = DILATIONS[g]
        c0 = col_base[g] + part * (A_OUT_WIDTH // LANES)

        def index(t):
            hp, bi, ti = where(t)
            return bi, 0, ti, c0 + hp
        return pl.BlockSpec((1, d, ATT_TILE // d, LANES), index)

    def prev(g, part):
        d = DILATIONS[g]
        c0 = col_base[g] + part * (A_OUT_WIDTH // LANES)
        rb = ATT_TILE // (BLK * d)

        def index(t):
            hp, bi, ti = where(t)
            return bi, 0, jnp.maximum(ti * rb - 1, 0), c0 + hp
        return pl.BlockSpec((1, d, BLK, LANES), index)

    def tile(col0):
        def index(t):
            hp, bi, ti = where(t)
            return bi, ti, col0 // LANES + hp
        return pl.BlockSpec((1, ATT_TILE, LANES), index)

    attn_specs = ([cur(g, 0) for g in range(N_GROUPS)] + [cur(g, 1) for g in range(N_GROUPS)]
                  + [cur(g, 2) for g in range(N_GROUPS)]
                  + [prev(g, 1) for g in range(N_GROUPS)] + [prev(g, 2) for g in range(N_GROUPS)]
                  + [tile(COL_GB),
                     pl.BlockSpec((N_GROUPS, 1, 2, BLK, 2 * BLK), lambda t: (0, t // n_tiles, 0, 0, 0))])
    assert len(scan_specs) == N_SCAN_REFS and len(attn_specs) == N_ATTN_REFS
    return pl.pallas_call(
        functools.partial(_mixers_kernel, nb=B, tt=tt, tiles_per_seq=tiles_per_seq, n_tiles=n_tiles),
        grid=(n_steps,),
        in_specs=scan_specs + attn_specs,
        out_specs=[scan_spec, tile(0)],
        out_shape=[jax.ShapeDtypeStruct((B, S, W), BF16), jax.ShapeDtypeStruct((B, S, A_OUT_WIDTH), BF16)],
        scratch_shapes=([pltpu.VMEM((B * N_COLGROUPS, HEAD_DIM, MXU_DIM), F32)]
                        + [pltpu.VMEM((ATT_TILE, LANES), F32)] * 6),
        compiler_params=_cparams(("arbitrary",), MIXERS_VMEM_LIMIT),
        name="mixers",
    )(r, cum, k, v, a, b, main, vec, ones_bd, *(arrays * 5), main, bias5)


def _merge_kernel(ya_ref, yb_ref, ma_ref, mb_ref, x_ref, mod_ref, wa_ref, wb_ref, wo_ref, fg_ref,
                  o_ref, *, final_norm):
    pa = _dot(ya_ref[0], wa_ref[...])
    pb = _dot(yb_ref[0], wb_ref[...])
    merged = _sigmoid(ma_ref[0].astype(F32)) * pa + _sigmoid(mb_ref[0].astype(F32)) * pb
    out = _dot(merged.astype(BF16), wo_ref[...])
    gate = mod_ref[0, :, 2 * D_MODEL:3 * D_MODEL]
    xn = x_ref[0] + gate * out
    if final_norm:
        ms = jnp.mean(xn * xn, axis=-1, keepdims=True)
        xn = xn * lax.rsqrt(ms + RMS_EPS) * fg_ref[...]
    o_ref[0] = xn


def _merge(ya, yb, proj, x, mod, wa, wb, wo, final_g, final_norm, layer, tm=1024):
    B, S, D = x.shape

    def rows(width, c):
        return pl.BlockSpec((1, tm, width), lambda b, i: (b, i, c))

    def full(arr):
        return pl.BlockSpec(arr.shape, lambda b, i: (0,) * arr.ndim)

    return pl.pallas_call(
        functools.partial(_merge_kernel, final_norm=final_norm),
        grid=(B, S // tm),
        in_specs=[rows(R_WIDTH, 0), rows(A_OUT_WIDTH, 0),
                  rows(D, COL_MA // D), rows(D, COL_MB // D), rows(D, 0),
                  pl.BlockSpec((1, 1, 3 * D), lambda b, i: (layer * MOD_ROWS + b, 0, 0)),
                  _layer_block(wa, layer), _layer_block(wb, layer), _layer_block(wo, layer), full(final_g)],
        out_specs=rows(D, 0),
        out_shape=jax.ShapeDtypeStruct((B, S, D), F32),
        compiler_params=_cparams(("parallel", "parallel")),
        name="merge",
    )(ya, yb, proj, proj, x, mod, wa, wb, wo, final_g)


def _segment_ones():
    idx = np.arange(MXU_DIM)
    return jnp.asarray(idx[:, None] // HEAD_DIM == idx[None, :] // HEAD_DIM, BF16)


def kernel(x, c, norm_g, ada_w, ada_b, w_in, rwkv_mu_rkv, rwkv_mu_wa, rwkv_w0, rwkv_w1, rwkv_w2, rwkv_a0, rwkv_a1, rwkv_a2, rwkv_k_k, rwkv_k_a, rwkv_r_k, rwkv_ln_g, rwkv_ln_b, rwkv_mu_v, rwkv_v0, rwkv_v1, rwkv_v2, w_branch_a, w_branch_b, w_out, rel_bias, final_g):
    B, S, D = x.shape
    assert D == D_MODEL and S % ATT_TILE == 0 and w_in.shape[2] == PROJ_WIDTH
    ones_bd = _segment_ones()
    mod = _adaln_mod(c, ada_w, ada_b).reshape(DEPTH * MOD_ROWS, 1, 3 * D)
    bias = _rel_bias(rel_bias).reshape(N_GROUPS, HEADS_PER_GROUP // 2, 2, BLK, 2 * BLK)

    def cols(start, width):
        return w_in[:, :, start:start + width]

    def group_cols(g):
        return [cols(W_AQ + A_OUT_WIDTH * g, A_OUT_WIDTH) * (LOG2E / math.sqrt(HEAD_DIM)),
                cols(W_AK + A_OUT_WIDTH * g, A_OUT_WIDTH), cols(W_AV + A_OUT_WIDTH * g, A_OUT_WIDTH)]

    w_main = jnp.concatenate(
        [cols(W_R, 4 * R_WIDTH), cols(W_MA, 2 * D_MODEL), cols(W_GB, A_OUT_WIDTH)] + group_cols(0),
        axis=2).astype(BF16)
    w_groups = [jnp.concatenate(group_cols(g), axis=2).astype(BF16) for g in range(1, N_GROUPS)]
    zeros_rows = jnp.zeros((DEPTH, D), F32)
    pvec = jnp.stack([rwkv_mu_rkv[:, 0], rwkv_mu_rkv[:, 1], rwkv_mu_rkv[:, 2], rwkv_w0, rwkv_a0, rwkv_k_k,
                      rwkv_k_a, _first_layer_blank(rwkv_v0)] + [zeros_rows] * (PV_ROWS - 8), axis=1)
    lora = _pack_lora([(rwkv_mu_wa[:, 0], rwkv_w1, rwkv_w2), (rwkv_mu_wa[:, 1], rwkv_a1, rwkv_a2),
                       (_first_layer_blank(rwkv_mu_v), _first_layer_blank(rwkv_v1),
                        _first_layer_blank(rwkv_v2))])
    vec = jnp.stack([rwkv_r_k.reshape(DEPTH, -1), rwkv_ln_g, rwkv_ln_b] + [zeros_rows] * 5, axis=1)
    norm_g3 = norm_g.reshape(DEPTH, 1, D)
    wa, wb, wo = w_branch_a.astype(BF16), w_branch_b.astype(BF16), w_out.astype(BF16)

    v_first = None
    for i in range(DEPTH):
        proj, h = _norm_proj(x, mod, norm_g3, w_main, i)
        (r, cum, k, v, a, b), groups = _rwkv_prep(h, proj, v_first, pvec, lora if i > 0 else lora[:3],
                                                  ones_bd, w_groups, i)
        if i == 0:
            v_first = v
        y_a, y_b = _mixers(r, cum, k, v, a, b, proj, groups, vec, ones_bd, bias, i)
        x = _merge(y_a, y_b, proj, x, mod, wa, wb, wo, final_g.reshape(1, D),
                   final_norm=(i == DEPTH - 1), layer=i)
    return x
```
